```python
import math
import jax
import jax.numpy as jnp
from jax import lax
import numpy as np

D_MODEL = 1024
BATCH = 4
SEQ = 4096
DEPTH = 4
DEC_BATCH = 16
DEC_SEQ = 64
PAST_LEN = 2048

CHUNK = 64
HEAD_DIM = 64
A_HEADS = 4
A_WIDTH = A_HEADS * HEAD_DIM
B_HEADS = 4
B_WIDTH = B_HEADS * HEAD_DIM
CONV_W = 4
C_HEADS = 4
C_KV_HEADS = 2
C_GROUP = C_HEADS // C_KV_HEADS
C_WIDTH = C_HEADS * HEAD_DIM
WINDOW = 128
NUM_BUCKETS = 32
MAX_DISTANCE = 128
D_HEADS = 4
D_WIDTH = D_HEADS * HEAD_DIM
D_DECAY_LORA = 64
D_AAA_LORA = 64
D_GATE_LORA = 128
D_COLS = 3 * D_WIDTH + D_DECAY_LORA + D_AAA_LORA + D_GATE_LORA
N_BRANCH = 4
D_FF = 2816
N_EXPERTS = 8
TOP_K = 2
D_FF_EXPERT = 1408
N_DENSE = (DEPTH + 1) // 2
N_MOE = DEPTH // 2
ALPHA = (2 * DEPTH) ** 0.25
BETA = (8 * DEPTH) ** -0.25
LN_EPS = 1e-5
HEAD_NORM_EPS = 1e-5
RWKV_GN_EPS = 64e-5
LB_FLOOR = 1e-30
LB_CEIL = 1.0 - 1e-6

IN_SPLITS = (
    ('a_q', A_WIDTH), ('a_f', A_WIDTH), ('a_i', A_WIDTH), ('a_g', A_WIDTH),
    ('b_u', B_WIDTH), ('b_v', B_WIDTH), ('b_o', B_WIDTH), ('b_i', B_HEADS), ('b_f', B_HEADS),
    ('c_q', C_HEADS * HEAD_DIM), ('c_k', C_KV_HEADS * HEAD_DIM), ('c_v', C_KV_HEADS * HEAD_DIM),
    ('d', D_COLS), ('gate', N_BRANCH * D_MODEL),
)
D_SPLITS = (('r', D_WIDTH), ('w', D_DECAY_LORA), ('k', D_WIDTH), ('v', D_WIDTH), ('a', D_AAA_LORA), ('g', D_GATE_LORA))
IN_COLS = sum(size for _, size in IN_SPLITS)
STATE_NAMES = ('swa_k', 'swa_v', 'hgrn', 'mlstm_c', 'mlstm_n', 'mlstm_m', 'mlstm_conv', 'rwkv', 'rwkv_shift')

kernel_name = 'hybrid_streaming_encoder_step'


def _split_cols(p, splits):
    out, off = {}, 0
    for name, size in splits:
        out[name] = p[..., off:off + size]
        off += size
    return out


def _layernorm(x, g, b):
    xf = x.astype(jnp.float32)
    mu = jnp.mean(xf, axis=-1, keepdims=True)
    var = jnp.mean(jnp.square(xf - mu), axis=-1, keepdims=True)
    return ((xf - mu) * lax.rsqrt(var + LN_EPS) * g + b).astype(x.dtype)


def _groupnorm(x, w, eps):
    mu = jnp.mean(x, axis=-1, keepdims=True)
    var = jnp.mean(jnp.square(x - mu), axis=-1, keepdims=True)
    return (x - mu) * lax.rsqrt(var + eps) * w.reshape(x.shape[-2], x.shape[-1])


def _rmsnorm_heads(x, w):
    ms = jnp.mean(jnp.square(x), axis=-1, keepdims=True)
    return x * lax.rsqrt(ms + HEAD_NORM_EPS) * w.reshape(x.shape[-2], x.shape[-1])


def _lower_bounds(lb_raw):
    sm = jax.nn.softmax(lb_raw.astype(jnp.float32), axis=0)
    lb = jnp.concatenate([jnp.zeros_like(sm[:1]), jnp.cumsum(sm[1:], axis=0)[:-1] if sm.shape[0] > 1 else sm[:0]], axis=0)
    return jnp.clip(lb, 0.0, LB_CEIL)


def _to_chunks(a, nc, L):
    return jnp.moveaxis(a.reshape(a.shape[0], nc, L, *a.shape[2:]), 1, 0)


def _from_chunks(a):
    a = jnp.moveaxis(a, 0, 1)
    return a.reshape(a.shape[0], a.shape[1] * a.shape[2], *a.shape[3:])


def _hgrn2_chunked(q, k, v, logf, s0):
    T = q.shape[1]
    L = min(CHUNK, T)
    nc = T // L
    mask = jnp.tril(jnp.ones((L, L), bool))[None, :, :, None, None]

    def step(S, inp):
        qc, kc, vc, lf = inp
        b = jnp.cumsum(lf, axis=1)
        o_inter = jnp.einsum('bthk,bhkv->bthv', qc * jnp.exp(b), S)
        diff = b[:, :, None] - b[:, None, :]
        decay = jnp.where(mask, jnp.exp(jnp.minimum(diff, 0.0)), 0.0)
        attn = jnp.einsum('bthk,bshk,btshk->bhts', qc, kc, decay)
        o = o_inter + jnp.einsum('bhts,bshv->bthv', attn, vc)
        b_last = b[:, -1]
        S_new = jnp.exp(b_last)[..., None] * S + jnp.einsum('bshk,bshv->bhkv', kc * jnp.exp(b_last[:, None] - b), vc)
        return S_new, o

    xs = (_to_chunks(q, nc, L), _to_chunks(k, nc, L), _to_chunks(v, nc, L), _to_chunks(logf, nc, L))
    S_fin, o = lax.scan(step, s0, xs)
    return _from_chunks(o), S_fin


def _mlstm_chunked(q, k, v, logi, logf, c0, n0, m0):
    T = q.shape[1]
    L = min(CHUNK, T)
    nc = T // L
    mask = jnp.tril(jnp.ones((L, L), bool))[None, :, :, None]

    def step(carry, inp):
        C, n, m = carry
        qc, kc, vc, li, lf = inp
        F = jnp.cumsum(lf, axis=1)
        g = F + m[:, None]
        D = jnp.where(mask, F[:, :, None] - F[:, None, :] + li[:, None], -jnp.inf)
        mt = jnp.maximum(g, jnp.max(D, axis=2))
        wD = jnp.exp(D - mt[:, :, None])
        wg = jnp.exp(g - mt)
        qk = jnp.einsum('bthk,bshk->btsh', qc, kc) * wD
        num = wg[..., None] * jnp.einsum('bthk,bhkv->bthv', qc, C) + jnp.einsum('btsh,bshv->bthv', qk, vc)
        den = wg * jnp.einsum('bthk,bhk->bth', qc, n) + jnp.sum(qk, axis=2)
        h = num / jnp.maximum(jnp.abs(den), jnp.exp(-mt))[..., None]
        wl, wgl = wD[:, -1], wg[:, -1]
        C_new = wgl[..., None, None] * C + jnp.einsum('bsh,bshk,bshv->bhkv', wl, kc, vc)
        n_new = wgl[..., None] * n + jnp.einsum('bsh,bshk->bhk', wl, kc)
        return (C_new, n_new, mt[:, -1]), h

    xs = tuple(_to_chunks(a, nc, L) for a in (q, k, v, logi, logf))
    (C, n, m), h = lax.scan(step, (c0, n0, m0), xs)
    return _from_chunks(h), C, n, m


def _rwkv7_scan(r, w, k, v, a, b, s0):
    def step(S, inp):
        rt, wt, kt, vt, at, bt = inp
        sa = jnp.einsum('bhvk,bhk->bhv', S, at)
        S = S * wt[:, :, None, :] + sa[..., None] * bt[:, :, None, :] + vt[..., None] * kt[:, :, None, :]
        return S, jnp.einsum('bhvk,bhk->bhv', S, rt)

    xs = tuple(jnp.moveaxis(t, 1, 0) for t in (r, w, k, v, a, b))
    S, y = lax.scan(step, s0, xs)
    return jnp.moveaxis(y, 0, 1), S


def _rel_bucket(rel):
    half = NUM_BUCKETS // 2
    exact = half // 2
    dist = jnp.abs(rel)
    far = exact + (jnp.log(jnp.maximum(dist, 1).astype(jnp.float32) / exact)
                   / math.log(MAX_DISTANCE / exact) * (half - exact)).astype(jnp.int32)
    far = jnp.minimum(far, half - 1)
    return jnp.where(rel > 0, half, 0) + jnp.where(dist < exact, dist, far)


def _swa_sinks(q, k, v, k_prev, v_prev, prev_valid, keep, sinks, rel_bias):
    bsz, T = q.shape[0], q.shape[1]
    P = k_prev.shape[1]
    L = min(CHUNK, T)
    nc = T // L
    kf = jnp.concatenate([k_prev.astype(k.dtype), k], axis=1)
    vf = jnp.concatenate([v_prev.astype(v.dtype), v], axis=1)
    idx = (jnp.arange(nc) * L)[:, None] + jnp.arange(P + L)[None, :]
    kb = kf[:, idx]
    vb = vf[:, idx]
    valid = jnp.concatenate([jnp.full((P,), prev_valid), jnp.ones((T,), bool)])[idx]
    qb = q.reshape(bsz, nc, L, C_KV_HEADS, C_GROUP, HEAD_DIM)
    s = jnp.einsum('bcigrd,bcjgd->bcgrij', qb, kb).astype(jnp.float32) * HEAD_DIM ** -0.5
    rel = jnp.arange(P + L)[None, :] - P - jnp.arange(L)[:, None]
    bias = jnp.transpose(rel_bias.astype(jnp.float32)[_rel_bucket(rel)], (2, 0, 1))
    s = s + bias.reshape(C_KV_HEADS, C_GROUP, L, P + L)
    s = jnp.where(valid[None, :, None, None, None, :], s, -jnp.inf)
    sink = sinks.astype(jnp.float32).reshape(C_KV_HEADS, C_GROUP)[:, :, None, None]
    m = jnp.maximum(jnp.max(s, axis=-1, keepdims=True), sink)
    p = jnp.exp(s - m)
    probs = p / (jnp.sum(p, axis=-1, keepdims=True) + jnp.exp(sink - m))
    o = jnp.einsum('bcgrij,bcjgd->bcigrd', probs.astype(vb.dtype), vb)
    return o.reshape(bsz, T, C_WIDTH), kf[:, -keep:], vf[:, -keep:]


def _causal_conv(u, prev, w, b):
    T = u.shape[1]
    full = jnp.concatenate([prev.astype(u.dtype), u], axis=1)
    out = b + sum(full[:, j:j + T] * w[j] for j in range(CONV_W))
    return out, full[:, -(CONV_W - 1):]


def _mixer(x, st, prev_valid, keep, lp, lb, rel_bias):
    f32 = jnp.float32
    bsz, T, _ = x.shape
    cols = _split_cols(jnp.einsum('btd,dc->btc', x, lp['w_in']), IN_SPLITS)

    def heads(a, h):
        return a.reshape(bsz, T, h, -1)

    lbh = lb.astype(f32).reshape(A_HEADS, HEAD_DIM)
    zf = heads(cols['a_f'], A_HEADS).astype(f32)
    logf_a = jnp.logaddexp(jnp.log(jnp.maximum(lbh, LB_FLOOR)), jnp.log1p(-lbh) + jax.nn.log_sigmoid(zf))
    k_a = (1.0 - lbh) * jax.nn.sigmoid(-zf)
    q_a = jax.nn.silu(heads(cols['a_q'], A_HEADS).astype(f32))
    i_a = heads(cols['a_i'], A_HEADS).astype(f32)
    o_a, s_hgrn = _hgrn2_chunked(q_a, k_a, i_a, logf_a, st['hgrn'].astype(f32))
    y_a = _rmsnorm_heads(o_a, lp['gn_a']) * jax.nn.silu(heads(cols['a_g'], A_HEADS).astype(f32))

    conv_out, conv_state = _causal_conv(cols['b_u'], st['mlstm_conv'], lp['conv_w'], lp['conv_b'])
    c_act = jax.nn.silu(conv_out)
    c_heads = heads(c_act, B_HEADS)
    q_b = jnp.einsum('bthd,hde->bthe', c_heads, lp['wq_b']).astype(f32)
    k_b = jnp.einsum('bthd,hde->bthe', c_heads, lp['wk_b']).astype(f32) * HEAD_DIM ** -0.5
    v_b = heads(cols['b_v'], B_HEADS).astype(f32)
    logi_b = (cols['b_i'] + lp['b_i']).astype(f32)
    logf_b = jax.nn.log_sigmoid((cols['b_f'] + lp['b_f']).astype(f32))
    h_b, mc, mn, mm = _mlstm_chunked(q_b, k_b, v_b, logi_b, logf_b, st['mlstm_c'].astype(f32),
                                     st['mlstm_n'].astype(f32), st['mlstm_m'].astype(f32))
    o_b = jax.nn.sigmoid(heads(cols['b_o'], B_HEADS).astype(f32))
    y_b = _groupnorm(o_b * h_b, lp['gn_b'], HEAD_NORM_EPS) + heads(lp['skip_b'] * c_act, B_HEADS)

    y_c, k_win, v_win = _swa_sinks(heads(cols['c_q'], C_HEADS), heads(cols['c_k'], C_KV_HEADS),
                                   heads(cols['c_v'], C_KV_HEADS), st['swa_k'], st['swa_v'],
                                   prev_valid, keep, lp['sinks'], rel_bias)

    pd = cols['d']
    pd_prev = jnp.concatenate([st['rwkv_shift'].astype(pd.dtype)[:, None], pd[:, :-1]], axis=1)
    dc = _split_cols(pd + (pd_prev - pd) * lp['mu_d'], D_SPLITS)
    shift_state = pd[:, -1]
    r_d = heads(dc['r'], D_HEADS).astype(f32)
    w_log = -jax.nn.softplus(-(lp['w0_d'] + jnp.tanh(dc['w']) @ lp['w_up_d']).astype(f32)) - 0.5
    decay = jnp.exp(-jnp.exp(heads(w_log, D_HEADS)))
    a_d = jax.nn.sigmoid((lp['a0_d'] + dc['a'] @ lp['a_up_d']).astype(f32))
    g_d = jax.nn.sigmoid(dc['g']) @ lp['g_up_d']
    k_raw = dc['k'].astype(f32)
    kk = heads(k_raw * lp['k_k_d'], D_HEADS)
    kk = kk * lax.rsqrt(jnp.maximum(jnp.sum(kk * kk, axis=-1, keepdims=True), 1e-24))
    k_d = heads(k_raw * (1.0 + (a_d - 1.0) * lp['k_a_d']), D_HEADS)
    v_d = heads(dc['v'], D_HEADS).astype(f32)
    o_d, s_rwkv = _rwkv7_scan(r_d, decay, k_d, v_d, -kk, kk * heads(a_d, D_HEADS), st['rwkv'].astype(f32))
    o_d = _groupnorm(o_d, lp['gn_w_d'], RWKV_GN_EPS) + lp['gn_b_d'].reshape(D_HEADS, HEAD_DIM)
    o_d = o_d + jnp.sum(r_d * k_d * lp['r_k_d'].reshape(D_HEADS, HEAD_DIM), axis=-1, keepdims=True) * v_d
    y_d = o_d.reshape(bsz, T, D_WIDTH) * g_d

    gate_logits = cols['gate'].reshape(bsz, T, N_BRANCH, D_MODEL)
    branches = (y_a, y_b, y_c, y_d)
    merged = None
    for n in range(N_BRANCH):
        yb = branches[n].reshape(bsz, T, -1).astype(x.dtype)
        term = jax.nn.sigmoid(gate_logits[:, :, n].astype(f32)) * jnp.einsum('btw,wd->btd', yb, lp['w_br'][n])
        merged = term if merged is None else merged + term
    out = jnp.einsum('btd,de->bte', merged.astype(x.dtype), lp['w_o'])
    new_st = {'swa_k': k_win, 'swa_v': v_win, 'hgrn': s_hgrn, 'mlstm_c': mc, 'mlstm_n': mn,
              'mlstm_m': mm, 'mlstm_conv': conv_state, 'rwkv': s_rwkv, 'rwkv_shift': shift_state}
    return out, new_st


def _swiglu(x, w_gate, w_up, w_down):
    h = jax.nn.silu(jnp.einsum('btd,df->btf', x, w_gate)) * jnp.einsum('btd,df->btf', x, w_up)
    return jnp.einsum('btf,fd->btd', h, w_down)


def _moe(x, router_w, router_b, w_gate, w_up, w_down):
    logits = jnp.einsum('btd,de->bte', x, router_w).astype(jnp.float32) + router_b
    top_v, top_i = lax.top_k(logits, TOP_K)
    probs = jax.nn.softmax(top_v, axis=-1)
    gates = jnp.einsum('btk,btke->bte', probs, jax.nn.one_hot(top_i, N_EXPERTS, dtype=jnp.float32))
    y = None
    for e in range(N_EXPERTS):
        ye = gates[..., e:e + 1] * _swiglu(x, w_gate[e], w_up[e], w_down[e])
        y = ye if y is None else y + ye
    return y.astype(x.dtype)


def _trunk(x, states, prev_valid, keep, layer_params, lower_bounds, rel_bias, dense, moe):
    collected = {name: [] for name in STATE_NAMES}
    for l in range(DEPTH):
        lp = {name: arr[l] for name, arr in layer_params.items()}
        st = {name: states[name][l] for name in STATE_NAMES}
        mix, new_st = _mixer(x, st, prev_valid, keep, lp, lower_bounds[l], rel_bias)
        x = _layernorm(ALPHA * x + mix, lp['ln1_g'], lp['ln1_b'])
        j = l // 2
        if l % 2 == 0:
            ffn = _swiglu(x, dense[0][j], dense[1][j], dense[2][j])
        else:
            ffn = _moe(x, moe[0][j], moe[1][j], moe[2][j], moe[3][j], moe[4][j])
        x = _layernorm(ALPHA * x + ffn, lp['ln2_g'], lp['ln2_b'])
        for name in STATE_NAMES:
            collected[name].append(new_st[name])
    return x, {name: jnp.stack(collected[name]) for name in STATE_NAMES}


def setup_inputs(seed: int = 0) -> dict:
    key = jax.random.key(seed)
    keys = iter(jax.random.split(key, 64))

    def nrm(shape, scale):
        return jax.random.normal(next(keys), shape, jnp.float32) * scale

    def gain(shape):
        return 1.0 + nrm(shape, 0.01)

    w_cache = min(WINDOW, PAST_LEN)
    hd = HEAD_DIM
    return {
        'x_prompt': nrm((BATCH, SEQ, D_MODEL), 1.0),
        'x_sample': nrm((DEC_BATCH, DEC_SEQ, D_MODEL), 1.0),
        'cache_swa_k': nrm((DEPTH, DEC_BATCH, w_cache, C_KV_HEADS, hd), 1.0),
        'cache_swa_v': nrm((DEPTH, DEC_BATCH, w_cache, C_KV_HEADS, hd), 1.0),
        'state_hgrn': nrm((DEPTH, DEC_BATCH, A_HEADS, hd, hd), 0.5),
        'state_mlstm_c': nrm((DEPTH, DEC_BATCH, B_HEADS, hd, hd), 0.3),
        'state_mlstm_n': nrm((DEPTH, DEC_BATCH, B_HEADS, hd), 0.3),
        'state_mlstm_m': nrm((DEPTH, DEC_BATCH, B_HEADS), 1.0),
        'state_mlstm_conv': nrm((DEPTH, DEC_BATCH, CONV_W - 1, B_WIDTH), 1.0),
        'state_rwkv': nrm((DEPTH, DEC_BATCH, D_HEADS, hd, hd), 0.3),
        'state_rwkv_shift': nrm((DEPTH, DEC_BATCH, D_COLS), 1.0),
        'w_in': nrm((DEPTH, D_MODEL, IN_COLS), D_MODEL ** -0.5),
        'lb_raw': nrm((DEPTH, A_WIDTH), 0.5),
        'gn_a': gain((DEPTH, A_WIDTH)),
        'conv_w': nrm((DEPTH, CONV_W, B_WIDTH), CONV_W ** -0.5),
        'conv_b': nrm((DEPTH, B_WIDTH), 0.02),
        'wq_b': nrm((DEPTH, B_HEADS, hd, hd), hd ** -0.5),
        'wk_b': nrm((DEPTH, B_HEADS, hd, hd), hd ** -0.5),
        'b_i': nrm((DEPTH, B_HEADS), 0.1),
        'b_f': jnp.linspace(3.0, 6.0, B_HEADS, dtype=jnp.float32)[None, :] + nrm((DEPTH, B_HEADS), 0.1),
        'gn_b': gain((DEPTH, B_WIDTH)),
        'skip_b': gain((DEPTH, B_WIDTH)),
        'sinks': nrm((DEPTH, C_HEADS), 0.5),
        'rel_bias': nrm((NUM_BUCKETS, C_HEADS), 0.2),
        'mu_d': jax.random.uniform(next(keys), (DEPTH, D_COLS), jnp.float32),
        'w0_d': nrm((DEPTH, D_WIDTH), 0.5),
        'w_up_d': nrm((DEPTH, D_DECAY_LORA, D_WIDTH), 0.1),
        'a0_d': nrm((DEPTH, D_WIDTH), 0.1),
        'a_up_d': nrm((DEPTH, D_AAA_LORA, D_WIDTH), 0.1),
        'g_up_d': nrm((DEPTH, D_GATE_LORA, D_WIDTH), D_GATE_LORA ** -0.5),
        'k_k_d': 0.85 + nrm((DEPTH, D_WIDTH), 0.02),
        'k_a_d': gain((DEPTH, D_WIDTH)),
        'r_k_d': nrm((DEPTH, D_WIDTH), 0.1),
        'gn_w_d': gain((DEPTH, D_WIDTH)),
        'gn_b_d': nrm((DEPTH, D_WIDTH), 0.02),
        'w_br': nrm((DEPTH, N_BRANCH, A_WIDTH, D_MODEL), A_WIDTH ** -0.5),
        'w_o': nrm((DEPTH, D_MODEL, D_MODEL), BETA * D_MODEL ** -0.5),
        'ln1_g': gain((DEPTH, D_MODEL)),
        'ln1_b': nrm((DEPTH, D_MODEL), 0.02),
        'ln2_g': gain((DEPTH, D_MODEL)),
        'ln2_b': nrm((DEPTH, D_MODEL), 0.02),
        'ffn_w_gate': nrm((N_DENSE, D_MODEL, D_FF), D_MODEL ** -0.5),
        'ffn_w_up': nrm((N_DENSE, D_MODEL, D_FF), D_MODEL ** -0.5),
        'ffn_w_down': nrm((N_DENSE, D_FF, D_MODEL), BETA * D_FF ** -0.5),
        'router_w': nrm((N_MOE, D_MODEL, N_EXPERTS), D_MODEL ** -0.5),
        'router_b': nrm((N_MOE, N_EXPERTS), 0.01),
        'exp_w_gate': nrm((N_MOE, N_EXPERTS, D_MODEL, D_FF_EXPERT), D_MODEL ** -0.5),
        'exp_w_up': nrm((N_MOE, N_EXPERTS, D_MODEL, D_FF_EXPERT), D_MODEL ** -0.5),
        'exp_w_down': nrm((N_MOE, N_EXPERTS, D_FF_EXPERT, D_MODEL), BETA * D_FF_EXPERT ** -0.5),
    }


def reference(x_prompt, x_sample, cache_swa_k, cache_swa_v, state_hgrn, state_mlstm_c, state_mlstm_n,
              state_mlstm_m, state_mlstm_conv, state_rwkv, state_rwkv_shift,
              w_in, lb_raw, gn_a, conv_w, conv_b, wq_b, wk_b, b_i, b_f, gn_b, skip_b, sinks, rel_bias,
              mu_d, w0_d, w_up_d, a0_d, a_up_d, g_up_d, k_k_d, k_a_d, r_k_d, gn_w_d, gn_b_d,
              w_br, w_o, ln1_g, ln1_b, ln2_g, ln2_b, ffn_w_gate, ffn_w_up, ffn_w_down,
              router_w, router_b, exp_w_gate, exp_w_up, exp_w_down):
    layer_params = {
        'w_in': w_in, 'gn_a': gn_a, 'conv_w': conv_w, 'conv_b': conv_b, 'wq_b': wq_b, 'wk_b': wk_b,
        'b_i': b_i, 'b_f': b_f, 'gn_b': gn_b, 'skip_b': skip_b, 'sinks': sinks, 'mu_d': mu_d,
        'w0_d': w0_d, 'w_up_d': w_up_d, 'a0_d': a0_d, 'a_up_d': a_up_d, 'g_up_d': g_up_d,
        'k_k_d': k_k_d, 'k_a_d': k_a_d, 'r_k_d': r_k_d, 'gn_w_d': gn_w_d, 'gn_b_d': gn_b_d,
        'w_br': w_br, 'w_o': w_o, 'ln1_g': ln1_g, 'ln1_b': ln1_b, 'ln2_g': ln2_g, 'ln2_b': ln2_b,
    }
    lower_bounds = _lower_bounds(lb_raw)
    dense = (ffn_w_gate, ffn_w_up, ffn_w_down)
    moe = (router_w, router_b, exp_w_gate, exp_w_up, exp_w_down)
    sample_states = {
        'swa_k': cache_swa_k, 'swa_v': cache_swa_v, 'hgrn': state_hgrn, 'mlstm_c': state_mlstm_c,
        'mlstm_n': state_mlstm_n, 'mlstm_m': state_mlstm_m, 'mlstm_conv': state_mlstm_conv,
        'rwkv': state_rwkv, 'rwkv_shift': state_rwkv_shift,
    }
    keep = cache_swa_k.shape[2]
    bp = x_prompt.shape[0]
    prompt_states = {}
    for name in STATE_NAMES:
        arr = sample_states[name]
        rows = (WINDOW,) + arr.shape[3:] if name in ('swa_k', 'swa_v') else arr.shape[2:]
        prompt_states[name] = jnp.zeros((DEPTH, bp) + tuple(rows), arr.dtype)

    y_prompt, pst = _trunk(x_prompt, prompt_states, False, keep, layer_params, lower_bounds, rel_bias, dense, moe)
    y_sample, sst = _trunk(x_sample, sample_states, True, keep, layer_params, lower_bounds, rel_bias, dense, moe)
    p = {name: pst[name].astype(sample_states[name].dtype) for name in STATE_NAMES}
    s = {name: sst[name].astype(sample_states[name].dtype) for name in STATE_NAMES}
    return (y_prompt, y_sample,
            p['swa_k'], p['swa_v'], p['hgrn'], p['mlstm_c'], p['mlstm_n'], p['mlstm_m'], p['mlstm_conv'], p['rwkv'], p['rwkv_shift'],
            s['swa_k'], s['swa_v'], s['hgrn'], s['mlstm_c'], s['mlstm_n'], s['mlstm_m'], s['mlstm_conv'], s['rwkv'], s['rwkv_shift'])
```

```python
import functools
import math

import jax
import jax.numpy as jnp
from jax import lax
from jax.experimental import pallas as pl
from jax.experimental.pallas import tpu as pltpu

F32 = jnp.float32
BF16 = jnp.bfloat16
HIGHEST = lax.Precision.HIGHEST

D_MODEL = 1024
DEPTH = 4
CHUNK = 64
HEAD_DIM = 64
N_HEADS = 4
WIDTH = N_HEADS * HEAD_DIM
C_KV_HEADS = 2
KV_WIDTH = C_KV_HEADS * HEAD_DIM
CONV_W = 4
WINDOW = 128
NUM_BUCKETS = 32
MAX_DISTANCE = 128
D_DECAY_LORA = 64
D_AAA_LORA = 64
D_GATE_LORA = 128
D_COLS = 3 * WIDTH + D_DECAY_LORA + D_AAA_LORA + D_GATE_LORA
N_BRANCH = 4
D_FF = 2816
N_EXPERTS = 8
TOP_K = 2
D_FF_EXPERT = 1408
ALPHA = (2 * DEPTH) ** 0.25
LN_EPS = 1e-5
HEAD_NORM_EPS = 1e-5
RWKV_GN_EPS = 64e-5
LB_FLOOR = 1e-30
LB_CEIL = 1.0 - 1e-6

GATE_COLS = N_BRANCH * D_MODEL
COL_A = GATE_COLS
COL_B = COL_A + 4 * WIDTH
COL_C = COL_B + 3 * WIDTH
COL_D = COL_C + WIDTH + 2 * KV_WIDTH
MAIN_COLS = COL_D + D_COLS
SUB = 16
V7X_VMEM_LIMIT = 48 * 1024 * 1024


def _cparams(*sem):
    return pltpu.CompilerParams(dimension_semantics=sem, vmem_limit_bytes=V7X_VMEM_LIMIT)


def _sigmoid(x):
    return 1.0 / (1.0 + jnp.exp(-x))


def _silu(x):
    return x * _sigmoid(x)


def _log_sigmoid(x):
    return jnp.minimum(x, 0.0) - jnp.log1p(jnp.exp(-jnp.abs(x)))


def _layernorm_rows(z, g, b):
    mu = jnp.mean(z, axis=-1, keepdims=True)
    zc = z - mu
    var = jnp.mean(zc * zc, axis=-1, keepdims=True)
    return zc * lax.rsqrt(var + LN_EPS) * g + b


def _matmul_kernel(x_ref, w_ref, o_ref, *, precision):
    x = x_ref[...]
    w = w_ref[...]
    if precision is None:
        x = x.astype(w.dtype)
    o_ref[...] = jnp.dot(x, w, preferred_element_type=F32, precision=precision).astype(o_ref.dtype)


def _matmul(x, w, *, tm, tn, out_dtype=F32, precision=None):
    m, k = x.shape
    n = w.shape[1]
    tm = min(tm, m)
    tn = min(tn, n)
    return pl.pallas_call(
        functools.partial(_matmul_kernel, precision=precision),
        grid=(m // tm, n // tn),
        in_specs=[pl.BlockSpec((tm, k), lambda i, j: (i, 0)),
                  pl.BlockSpec((k, tn), lambda i, j: (0, j))],
        out_specs=pl.BlockSpec((tm, tn), lambda i, j: (i, j)),
        out_shape=jax.ShapeDtypeStruct((m, n), out_dtype),
        compiler_params=_cparams("parallel", "parallel"),
        name="matmul",
    )(x, w)


def _head(a, h):
    return a[:, h * HEAD_DIM:(h + 1) * HEAD_DIM]


def _hgrn_kernel(q_ref, f_ref, i_ref, g_ref, lbp_ref, gn_ref, s0_ref, y_ref, s_out_ref, st_ref, *, nc):
    c = pl.program_id(1)
    L = CHUNK

    @pl.when(c == 0)
    def _():
        for h in range(N_HEADS):
            st_ref[h] = s0_ref[0, h].T

    log_lb = lbp_ref[0:1, :]
    log1m_lb = lbp_ref[1:2, :]
    one_m_lb = lbp_ref[2:3, :]
    zf = f_ref[0]
    lsz = _log_sigmoid(zf)
    u = log_lb
    w = log1m_lb + lsz
    logf = jnp.maximum(u, w) + jnp.log1p(jnp.exp(-jnp.abs(u - w)))
    k = one_m_lb * _sigmoid(-zf)
    q = _silu(q_ref[0])
    v = i_ref[0]

    row = lax.broadcasted_iota(jnp.int32, (L, L), 0)
    col = lax.broadcasted_iota(jnp.int32, (L, L), 1)
    tri = (row >= col).astype(F32)
    b = jnp.dot(tri, logf, preferred_element_type=F32, precision=HIGHEST)
    b_last = b[L - 1:L]
    qe = q * jnp.exp(b)
    kdec = k * jnp.exp(b_last - b)
    sub_row = lax.broadcasted_iota(jnp.int32, (SUB, WIDTH), 0)

    o_heads = [[] for _ in range(N_HEADS)]
    for blk in range(L // SUB):
        r0 = blk * SUB
        b_i = b[r0:r0 + SUB]
        q_i = q[r0:r0 + SUB]
        acc = [jnp.zeros((SUB, HEAD_DIM), F32) for _ in range(N_HEADS)]
        for j in range(SUB):
            s = r0 + j
            p = q_i * k[s:s + 1] * jnp.exp(jnp.minimum(b_i - b[s:s + 1], 0.0))
            p = jnp.where(sub_row >= j, p, 0.0)
            for h in range(N_HEADS):
                a = jnp.sum(_head(p, h), axis=1, keepdims=True)
                acc[h] = acc[h] + a * _head(v[s:s + 1], h)
        if blk > 0:
            ref_row = b[r0 - 1:r0]
            q_s = q_i * jnp.exp(b_i - ref_row)
            k_s = k[:r0] * jnp.exp(ref_row - b[:r0])
            for h in range(N_HEADS):
                att = lax.dot_general(_head(q_s, h), _head(k_s, h), (((1,), (1,)), ((), ())),
                                      preferred_element_type=F32)
                acc[h] = acc[h] + jnp.dot(att, _head(v[:r0], h), preferred_element_type=F32)
        for h in range(N_HEADS):
            o_heads[h].append(acc[h])

    gn = gn_ref[...]
    gact = _silu(g_ref[0])
    for h in range(N_HEADS):
        st = st_ref[h]
        o_h = jnp.concatenate(o_heads[h], axis=0)
        o_h = o_h + lax.dot_general(_head(qe, h), st, (((1,), (1,)), ((), ())), preferred_element_type=F32)
        ms = jnp.mean(o_h * o_h, axis=1, keepdims=True)
        y_h = o_h * lax.rsqrt(ms + HEAD_NORM_EPS) * _head(gn, h) * _head(gact, h)
        y_ref[0, :, h * HEAD_DIM:(h + 1) * HEAD_DIM] = y_h.astype(y_ref.dtype)
        upd = lax.dot_general(_head(v, h), _head(kdec, h), (((0,), (0,)), ((), ())), preferred_element_type=F32)
        st_ref[h] = st * jnp.exp(_head(b_last, h)) + upd

    @pl.when(c == nc - 1)
    def _():
        for h in range(N_HEADS):
            s_out_ref[0, h] = st_ref[h].T


def _hgrn(h3, lbp, gn, s0):
    bsz, t, _ = h3.shape
    nc = t // CHUNK
    cb = COL_A // WIDTH

    def col(j):
        return pl.BlockSpec((1, CHUNK, WIDTH), lambda b, c, j=j: (b, c, cb + j))

    return pl.pallas_call(
        functools.partial(_hgrn_kernel, nc=nc),
        grid=(bsz, nc),
        in_specs=[col(0), col(1), col(2), col(3),
                  pl.BlockSpec((8, WIDTH), lambda b, c: (0, 0)),
                  pl.BlockSpec((1, WIDTH), lambda b, c: (0, 0)),
                  pl.BlockSpec((1, N_HEADS, HEAD_DIM, HEAD_DIM), lambda b, c: (b, 0, 0, 0))],
        out_specs=[pl.BlockSpec((1, CHUNK, WIDTH), lambda b, c: (b, c, 0)),
                   pl.BlockSpec((1, N_HEADS, HEAD_DIM, HEAD_DIM), lambda b, c: (b, 0, 0, 0))],
        out_shape=[jax.ShapeDtypeStruct((bsz, t, WIDTH), BF16),
                   jax.ShapeDtypeStruct((bsz, N_HEADS, HEAD_DIM, HEAD_DIM), F32)],
        scratch_shapes=[pltpu.VMEM((N_HEADS, HEAD_DIM, HEAD_DIM), F32)],
        compiler_params=_cparams("parallel", "arbitrary"),
        name="hgrn2",
    )(h3, h3, h3, h3, lbp, gn, s0)


def _mlstm_kernel(qk_ref, v_ref, o_ref, cact_ref, gcol_ref, grow_ref, gn_ref, skip_ref, c0_ref, n0_ref, m0_ref,
                  y_ref, c_out_ref, n_out_ref, m_out_ref, c_ref, n_ref, m_ref, *, nc):
    c = pl.program_id(1)
    L = CHUNK

    @pl.when(c == 0)
    def _():
        c_ref[...] = c0_ref[0]
        n_ref[...] = n0_ref[0]
        m_ref[...] = m0_ref[0]

    row = lax.broadcasted_iota(jnp.int32, (L, L), 0)
    col = lax.broadcasted_iota(jnp.int32, (L, L), 1)
    lower = row >= col
    gcol = gcol_ref[0]
    grow = grow_ref[0, 0]
    f_col = jnp.dot(lower.astype(F32), gcol, preferred_element_type=F32, precision=HIGHEST)
    f_row = jnp.dot(grow, (row <= col).astype(F32), preferred_element_type=F32, precision=HIGHEST)
    qk_all = qk_ref[0]
    v_all = v_ref[0]
    o_all = _sigmoid(o_ref[0])
    cact = cact_ref[0]
    gn = gn_ref[...]
    skip = skip_ref[...]

    for h in range(N_HEADS):
        fc = f_col[:, N_HEADS + h:N_HEADS + h + 1]
        fr = f_row[N_HEADS + h:N_HEADS + h + 1, :]
        li_r = grow[h:h + 1, :]
        li_c = gcol[:, h:h + 1]
        m_prev = m_ref[0:1, h:h + 1]
        q_h = _head(qk_all, h)
        k_h = _head(qk_all, N_HEADS + h)
        v_h = _head(v_all, h)
        cm = c_ref[h]
        n_h = n_ref[h:h + 1, :]

        g = fc + m_prev
        d = jnp.where(lower, fc - fr + li_r, -jnp.inf)
        mt = jnp.maximum(g, jnp.max(d, axis=1, keepdims=True))
        wd = jnp.exp(d - mt)
        wg = jnp.exp(g - mt)
        qk = lax.dot_general(q_h, k_h, (((1,), (1,)), ((), ())), preferred_element_type=F32) * wd
        num = wg * jnp.dot(q_h, cm, preferred_element_type=F32) + jnp.dot(qk, v_h, preferred_element_type=F32)
        den = wg * jnp.sum(q_h * n_h, axis=1, keepdims=True) + jnp.sum(qk, axis=1, keepdims=True)
        hh = num / jnp.maximum(jnp.abs(den), jnp.exp(-mt))

        mt_last = mt[L - 1:L]
        wl = jnp.exp(fc[L - 1:L] - fc + li_c - mt_last)
        wgl = wg[L - 1:L]
        kw = k_h * wl
        c_ref[h] = wgl * cm + lax.dot_general(kw, v_h, (((0,), (0,)), ((), ())), preferred_element_type=F32)
        n_ref[h:h + 1, :] = wgl * n_h + jnp.sum(kw, axis=0, keepdims=True)
        m_ref[0:1, h:h + 1] = mt_last

        z = _head(o_all, h) * hh
        mu = jnp.mean(z, axis=1, keepdims=True)
        zc = z - mu
        var = jnp.mean(zc * zc, axis=1, keepdims=True)
        y_h = zc * lax.rsqrt(var + HEAD_NORM_EPS) * _head(gn, h) + _head(skip, h) * _head(cact, h)
        y_ref[0, :, h * HEAD_DIM:(h + 1) * HEAD_DIM] = y_h.astype(y_ref.dtype)

    @pl.when(c == nc - 1)
    def _():
        c_out_ref[0] = c_ref[...]
        n_out_ref[0] = n_ref[...]
        m_out_ref[0] = m_ref[...]


def _mlstm(qk3, h3, cact3, gcol, grow, gn, skip, c0, n0, m0):
    bsz, t, _ = h3.shape
    nc = t // CHUNK
    cb = COL_B // WIDTH
    state4 = pl.BlockSpec((1, N_HEADS, HEAD_DIM, HEAD_DIM), lambda b, c: (b, 0, 0, 0))
    state_n = pl.BlockSpec((1, N_HEADS, HEAD_DIM), lambda b, c: (b, 0, 0))
    state_m = pl.BlockSpec((1, 1, N_HEADS), lambda b, c: (b, 0, 0))
    vec = pl.BlockSpec((1, WIDTH), lambda b, c: (0, 0))
    return pl.pallas_call(
        functools.partial(_mlstm_kernel, nc=nc),
        grid=(bsz, nc),
        in_specs=[pl.BlockSpec((1, CHUNK, 2 * WIDTH), lambda b, c: (b, c, 0)),
                  pl.BlockSpec((1, CHUNK, WIDTH), lambda b, c: (b, c, cb + 1)),
                  pl.BlockSpec((1, CHUNK, WIDTH), lambda b, c: (b, c, cb + 2)),
                  pl.BlockSpec((1, CHUNK, WIDTH), lambda b, c: (b, c, 0)),
                  pl.BlockSpec((1, CHUNK, 2 * N_HEADS), lambda b, c: (b, c, 0)),
                  pl.BlockSpec((1, 1, 2 * N_HEADS, CHUNK), lambda b, c: (b, c, 0, 0)),
                  vec, vec, state4, state_n, state_m],
        out_specs=[pl.BlockSpec((1, CHUNK, WIDTH), lambda b, c: (b, c, 0)), state4, state_n, state_m],
        out_shape=[jax.ShapeDtypeStruct((bsz, t, WIDTH), BF16),
                   jax.ShapeDtypeStruct((bsz, N_HEADS, HEAD_DIM, HEAD_DIM), F32),
                   jax.ShapeDtypeStruct((bsz, N_HEADS, HEAD_DIM), F32),
                   jax.ShapeDtypeStruct((bsz, 1, N_HEADS), F32)],
        scratch_shapes=[pltpu.VMEM((N_HEADS, HEAD_DIM, HEAD_DIM), F32),
                        pltpu.VMEM((N_HEADS, HEAD_DIM), F32),
                        pltpu.VMEM((1, N_HEADS), F32)],
        compiler_params=_cparams("parallel", "arbitrary"),
        name="mlstm",
    )(qk3, h3, h3, cact3, gcol, grow, gn, skip, c0, n0, m0)


def _swa_kernel(q_ref, k0_ref, k1_ref, k2_ref, v0_ref, v1_ref, v2_ref, bias_ref, sink_ref, y_ref, *, prev_valid):
    c = pl.program_id(1)
    L = CHUNK
    span = WINDOW + L
    q = q_ref[0]
    kcat = jnp.concatenate([k0_ref[0], k1_ref[0], k2_ref[0]], axis=0).astype(BF16)
    vcat = jnp.concatenate([v0_ref[0], v1_ref[0], v2_ref[0]], axis=0).astype(BF16)
    key_pos = lax.broadcasted_iota(jnp.int32, (L, span), 1) + c * L
    for h in range(N_HEADS):
        g = h // (N_HEADS // C_KV_HEADS)
        q_h = _head(q, h).astype(BF16)
        s = lax.dot_general(q_h, _head(kcat, g), (((1,), (1,)), ((), ())), preferred_element_type=F32)
        s = s * HEAD_DIM ** -0.5 + bias_ref[h]
        if not prev_valid:
            s = jnp.where(key_pos >= WINDOW, s, -jnp.inf)
        sink = sink_ref[0:1, h:h + 1]
        m = jnp.maximum(jnp.max(s, axis=1, keepdims=True), sink)
        p = jnp.exp(s - m)
        probs = p / (jnp.sum(p, axis=1, keepdims=True) + jnp.exp(sink - m))
        o = jnp.dot(probs.astype(BF16), _head(vcat, g), preferred_element_type=F32)
        y_ref[0, :, h * HEAD_DIM:(h + 1) * HEAD_DIM] = o.astype(y_ref.dtype)


def _swa(h3, kf, vf, bias, sinks, prev_valid):
    bsz, t, _ = h3.shape
    nc = t // CHUNK
    qb = COL_C // WIDTH

    def kv(j):
        return pl.BlockSpec((1, CHUNK, KV_WIDTH), lambda b, c, j=j: (b, c + j, 0))

    return pl.pallas_call(
        functools.partial(_swa_kernel, prev_valid=prev_valid),
        grid=(bsz, nc),
        in_specs=[pl.BlockSpec((1, CHUNK, WIDTH), lambda b, c: (b, c, qb)),
                  kv(0), kv(1), kv(2), kv(0), kv(1), kv(2),
                  pl.BlockSpec((N_HEADS, CHUNK, WINDOW + CHUNK), lambda b, c: (0, 0, 0)),
                  pl.BlockSpec((1, N_HEADS), lambda b, c: (0, 0))],
        out_specs=pl.BlockSpec((1, CHUNK, WIDTH), lambda b, c: (b, c, 0)),
        out_shape=jax.ShapeDtypeStruct((bsz, t, WIDTH), BF16),
        compiler_params=_cparams("parallel", "parallel"),
        name="swa",
    )(h3, kf, kf, kf, vf, vf, vf, bias, sinks)


RWKV_BB = 4
RWKV_UNROLL = 8


def _rwkv_kernel(r_ref, w_ref, k_ref, v_ref, a_ref, b_ref, s0_ref, y_ref, s_out_ref, s_ref, *, nt, tc):
    ti = pl.program_id(1)
    half = HEAD_DIM
    n_pairs = N_HEADS // 2

    @pl.when(ti == 0)
    def _():
        for bi in range(RWKV_BB):
            for p in range(n_pairs):
                s_ref[bi, p, :, 0:half] = s0_ref[bi, 2 * p]
                s_ref[bi, p, :, half:2 * half] = s0_ref[bi, 2 * p + 1]

    lane = lax.broadcasted_iota(jnp.int32, (HEAD_DIM, 2 * half), 1)
    sub = lax.broadcasted_iota(jnp.int32, (HEAD_DIM, 2 * half), 0)
    lo = lane < half
    lane1 = lax.broadcasted_iota(jnp.int32, (1, 2 * half), 1)
    lo1 = lane1 < half
    diag = (lane % half) == sub

    def seg_sum(x, row_vec):
        s_lo = jnp.sum(x * jnp.where(lo1, row_vec, 0.0), axis=1, keepdims=True)
        s_hi = jnp.sum(x * jnp.where(lo1, 0.0, row_vec), axis=1, keepdims=True)
        return jnp.where(lo, s_lo, s_hi)

    def block(i, carry):
        t0 = pl.multiple_of(i * RWKV_UNROLL, RWKV_UNROLL)
        for bi in range(RWKV_BB):
            for p in range(n_pairs):
                ls = slice(p * 2 * half, (p + 1) * 2 * half)
                r8 = r_ref[bi, pl.ds(t0, RWKV_UNROLL), ls]
                w8 = w_ref[bi, pl.ds(t0, RWKV_UNROLL), ls]
                k8 = k_ref[bi, pl.ds(t0, RWKV_UNROLL), ls]
                v8 = v_ref[bi, pl.ds(t0, RWKV_UNROLL), ls]
                a8 = a_ref[bi, pl.ds(t0, RWKV_UNROLL), ls]
                b8 = b_ref[bi, pl.ds(t0, RWKV_UNROLL), ls]
                st = s_ref[bi, p]
                ys = []
                for j in range(RWKV_UNROLL):
                    sa = seg_sum(st, a8[j:j + 1])
                    vcol = seg_sum(jnp.where(diag, 1.0, 0.0), v8[j:j + 1])
                    st = st * w8[j:j + 1] + sa * b8[j:j + 1] + vcol * k8[j:j + 1]
                    ycol = seg_sum(st, r8[j:j + 1])
                    ys.append(jnp.sum(jnp.where(diag, ycol, 0.0), axis=0, keepdims=True))
                s_ref[bi, p] = st
                y_ref[bi, pl.ds(t0, RWKV_UNROLL), ls] = jnp.concatenate(ys, axis=0)
        return carry

    lax.fori_loop(0, tc // RWKV_UNROLL, block, 0)

    @pl.when(ti == nt - 1)
    def _():
        for bi in range(RWKV_BB):
            for p in range(n_pairs):
                s_out_ref[bi, 2 * p] = s_ref[bi, p, :, 0:half]
                s_out_ref[bi, 2 * p + 1] = s_ref[bi, p, :, half:2 * half]


def _rwkv(r, w, k, v, a, b, s0):
    bsz, t, _ = r.shape
    tc = min(t, 256)
    nt = t // tc
    seq = pl.BlockSpec((RWKV_BB, tc, WIDTH), lambda g, i: (g, i, 0))
    state = pl.BlockSpec((RWKV_BB, N_HEADS, HEAD_DIM, HEAD_DIM), lambda g, i: (g, 0, 0, 0))
    return pl.pallas_call(
        functools.partial(_rwkv_kernel, nt=nt, tc=tc),
        grid=(bsz // RWKV_BB, nt),
        in_specs=[seq] * 6 + [state],
        out_specs=[seq, state],
        out_shape=[jax.ShapeDtypeStruct((bsz, t, WIDTH), F32),
                   jax.ShapeDtypeStruct((bsz, N_HEADS, HEAD_DIM, HEAD_DIM), F32)],
        scratch_shapes=[pltpu.VMEM((RWKV_BB, N_HEADS // 2, HEAD_DIM, 2 * HEAD_DIM), F32)],
        compiler_params=_cparams("parallel", "arbitrary"),
        name="rwkv7",
    )(r, w, k, v, a, b, s0)


def _merge_kernel(ya_ref, yb_ref, yc_ref, yd_ref, g0_ref, g1_ref, g2_ref, g3_ref, wbr_ref, wo_ref, x_ref,
                  lng_ref, lnb_ref, xo_ref, xob_ref):
    ys = (ya_ref, yb_ref, yc_ref, yd_ref)
    gs = (g0_ref, g1_ref, g2_ref, g3_ref)
    merged = None
    for n in range(N_BRANCH):
        term = _sigmoid(gs[n][...]) * jnp.dot(ys[n][...], wbr_ref[n], preferred_element_type=F32)
        merged = term if merged is None else merged + term
    out = jnp.dot(merged.astype(BF16), wo_ref[...], preferred_element_type=F32)
    xn = _layernorm_rows(ALPHA * x_ref[...] + out, lng_ref[...], lnb_ref[...])
    xo_ref[...] = xn
    xob_ref[...] = xn.astype(BF16)


def _merge(ys, h2, wbr, wo, x2, lng, lnb, *, tm=256):
    m = x2.shape[0]
    tm = min(tm, m)
    ysp = pl.BlockSpec((tm, WIDTH), lambda i: (i, 0))
    row = pl.BlockSpec((tm, D_MODEL), lambda i: (i, 0))
    vec = pl.BlockSpec((1, D_MODEL), lambda i: (0, 0))

    def gate(n):
        return pl.BlockSpec((tm, D_MODEL), lambda i, n=n: (i, n))

    return pl.pallas_call(
        _merge_kernel,
        grid=(m // tm,),
        in_specs=[ysp, ysp, ysp, ysp, gate(0), gate(1), gate(2), gate(3),
                  pl.BlockSpec((N_BRANCH, WIDTH, D_MODEL), lambda i: (0, 0, 0)),
                  pl.BlockSpec((D_MODEL, D_MODEL), lambda i: (0, 0)),
                  row, vec, vec],
        out_specs=[row, row],
        out_shape=[jax.ShapeDtypeStruct((m, D_MODEL), F32), jax.ShapeDtypeStruct((m, D_MODEL), BF16)],
        compiler_params=_cparams("parallel"),
        name="merge",
    )(*ys, h2, h2, h2, h2, wbr, wo, x2, lng, lnb)


def _ffn_kernel(*refs, n_steps, gated):
    if gated:
        xb_ref, x_ref, gates_ref, wg_ref, wu_ref, wd_ref, lng_ref, lnb_ref, xo_ref, xob_ref, acc_ref = refs
    else:
        xb_ref, x_ref, wg_ref, wu_ref, wd_ref, lng_ref, lnb_ref, xo_ref, xob_ref, acc_ref = refs
    j = pl.program_id(1)

    @pl.when(j == 0)
    def _():
        acc_ref[...] = jnp.zeros_like(acc_ref)

    xb = xb_ref[...]
    hg = jnp.dot(xb, wg_ref[0], preferred_element_type=F32)
    hu = jnp.dot(xb, wu_ref[0], preferred_element_type=F32)
    part = jnp.dot((_silu(hg) * hu).astype(BF16), wd_ref[0], preferred_element_type=F32)
    if gated:
        gates = gates_ref[...]
        lane = lax.broadcasted_iota(jnp.int32, gates.shape, 1)
        part = jnp.sum(jnp.where(lane == j, gates, 0.0), axis=1, keepdims=True) * part
    acc_ref[...] += part

    @pl.when(j == n_steps - 1)
    def _():
        xn = _layernorm_rows(ALPHA * x_ref[...] + acc_ref[...], lng_ref[...], lnb_ref[...])
        xo_ref[...] = xn
        xob_ref[...] = xn.astype(BF16)


def _ffn(xb, x2, wg, wu, wd, lng, lnb, gates=None, *, tm=512):
    m = x2.shape[0]
    tm = min(tm, m)
    n_steps, _, tf = wg.shape
    row = pl.BlockSpec((tm, D_MODEL), lambda i, j: (i, 0))
    vec = pl.BlockSpec((1, D_MODEL), lambda i, j: (0, 0))
    w_in_spec = pl.BlockSpec((1, D_MODEL, tf), lambda i, j: (j, 0, 0))
    w_out_spec = pl.BlockSpec((1, tf, D_MODEL), lambda i, j: (j, 0, 0))
    in_specs = [row, row]
    args = [xb, x2]
    if gates is not None:
        in_specs.append(pl.BlockSpec((tm, N_EXPERTS), lambda i, j: (i, 0)))
        args.append(gates)
    in_specs += [w_in_spec, w_in_spec, w_out_spec, vec, vec]
    args += [wg, wu, wd, lng, lnb]
    return pl.pallas_call(
        functools.partial(_ffn_kernel, n_steps=n_steps, gated=gates is not None),
        grid=(m // tm, n_steps),
        in_specs=in_specs,
        out_specs=[row, row],
        out_shape=[jax.ShapeDtypeStruct((m, D_MODEL), F32), jax.ShapeDtypeStruct((m, D_MODEL), BF16)],
        scratch_shapes=[pltpu.VMEM((tm, D_MODEL), F32)],
        compiler_params=_cparams("parallel", "arbitrary"),
        name="moe" if gates is not None else "ffn",
    )(*args)


def _rel_bucket(rel):
    half = NUM_BUCKETS // 2
    exact = half // 2
    dist = jnp.abs(rel)
    far = exact + (jnp.log(jnp.maximum(dist, 1).astype(F32) / exact)
                   / math.log(MAX_DISTANCE / exact) * (half - exact)).astype(jnp.int32)
    far = jnp.minimum(far, half - 1)
    return jnp.where(rel > 0, half, 0) + jnp.where(dist < exact, dist, far)


def _lower_bounds(lb_raw):
    sm = jax.nn.softmax(lb_raw.astype(F32), axis=0)
    lb = jnp.concatenate([jnp.zeros_like(sm[:1]), jnp.cumsum(sm[1:], axis=0)[:-1]], axis=0)
    return jnp.clip(lb, 0.0, LB_CEIL)


def _block_diag(w):
    out = jnp.zeros((WIDTH, WIDTH), w.dtype)
    for h in range(N_HEADS):
        out = out.at[h * HEAD_DIM:(h + 1) * HEAD_DIM, h * HEAD_DIM:(h + 1) * HEAD_DIM].set(w[h])
    return out


def _split(p, sizes):
    out, off = [], 0
    for size in sizes:
        out.append(p[..., off:off + size])
        off += size
    return out


def _mixer(x3, xb2, st, prev_valid, lp, l):
    bsz, t, _ = x3.shape
    m = bsz * t
    x2 = x3.reshape(m, D_MODEL)
    h2 = _matmul(xb2, lp['w_main'][l], tm=1024, tn=256)
    gif = _matmul(x2, lp['w_if'][l], tm=1024, tn=2 * N_HEADS, precision=HIGHEST)
    h3 = h2.reshape(bsz, t, MAIN_COLS)

    y_a, s_hgrn = _hgrn(h3, lp['lbp'][l], lp['gn_a'][l][None], st['hgrn'])

    u = h3[..., COL_B:COL_B + WIDTH]
    full = jnp.concatenate([st['mlstm_conv'], u], axis=1)
    conv_out = lp['conv_b'][l] + sum(full[:, j:j + t] * lp['conv_w'][l, j] for j in range(CONV_W))
    conv_state = full[:, -(CONV_W - 1):]
    c_act = jax.nn.silu(conv_out)
    qk3 = _matmul(c_act.reshape(m, WIDTH), lp['w_qk'][l], tm=1024, tn=2 * WIDTH).reshape(bsz, t, 2 * WIDTH)
    gi3 = gif.reshape(bsz, t, 2 * N_HEADS)
    logi = gi3[..., :N_HEADS] + lp['b_i'][l]
    logf = jax.nn.log_sigmoid(gi3[..., N_HEADS:] + lp['b_f'][l])
    gcol = jnp.concatenate([logi, logf], axis=-1)
    grow = jnp.swapaxes(gcol.reshape(bsz, t // CHUNK, CHUNK, 2 * N_HEADS), 2, 3)
    y_b, mc, mn, mm = _mlstm(qk3, h3, c_act, gcol, grow, lp['gn_b'][l][None], lp['skip_b'][l][None],
                             st['mlstm_c'], st['mlstm_n'], st['mlstm_m'][:, None, :])

    kf = jnp.concatenate([st['swa_k'].reshape(bsz, WINDOW, KV_WIDTH), h3[..., COL_C + WIDTH:COL_C + WIDTH + KV_WIDTH]], axis=1)
    vf = jnp.concatenate([st['swa_v'].reshape(bsz, WINDOW, KV_WIDTH), h3[..., COL_C + WIDTH + KV_WIDTH:COL_D]], axis=1)
    y_c = _swa(h3, kf, vf, lp['bias'], lp['sinks'][l][None], prev_valid)
    keep = st['keep']
    k_win = kf[:, -keep:].reshape(bsz, keep, C_KV_HEADS, HEAD_DIM)
    v_win = vf[:, -keep:].reshape(bsz, keep, C_KV_HEADS, HEAD_DIM)

    pd = h3[..., COL_D:COL_D + D_COLS]
    pd_prev = jnp.concatenate([st['rwkv_shift'][:, None], pd[:, :-1]], axis=1)
    r_d, w_d, k_raw, v_d, a_in, g_in = _split(pd + (pd_prev - pd) * lp['mu_d'][l],
                                              (WIDTH, D_DECAY_LORA, WIDTH, WIDTH, D_AAA_LORA, D_GATE_LORA))
    shift_state = pd[:, -1]
    w_lora = _matmul(jnp.tanh(w_d).reshape(m, D_DECAY_LORA), lp['w_up_d'][l], tm=1024, tn=WIDTH).reshape(bsz, t, WIDTH)
    a_lora = _matmul(a_in.reshape(m, D_AAA_LORA), lp['a_up_d'][l], tm=1024, tn=WIDTH).reshape(bsz, t, WIDTH)
    g_d = _matmul(jax.nn.sigmoid(g_in).reshape(m, D_GATE_LORA), lp['g_up_d'][l], tm=1024, tn=WIDTH).reshape(bsz, t, WIDTH)
    w_log = -jax.nn.softplus(-(lp['w0_d'][l] + w_lora)) - 0.5
    decay = jnp.exp(-jnp.exp(w_log))
    a_d = jax.nn.sigmoid(lp['a0_d'][l] + a_lora)

    def heads(a):
        return a.reshape(bsz, t, N_HEADS, HEAD_DIM)

    kk = heads(k_raw * lp['k_k_d'][l])
    kk = (kk * lax.rsqrt(jnp.maximum(jnp.sum(kk * kk, axis=-1, keepdims=True), 1e-24))).reshape(bsz, t, WIDTH)
    k_d = k_raw * (1.0 + (a_d - 1.0) * lp['k_a_d'][l])
    o_d, s_rwkv = _rwkv(r_d, decay, k_d, v_d, -kk, kk * a_d, st['rwkv'])
    o_h = heads(o_d)
    mu = jnp.mean(o_h, axis=-1, keepdims=True)
    var = jnp.mean(jnp.square(o_h - mu), axis=-1, keepdims=True)
    o_h = (o_h - mu) * lax.rsqrt(var + RWKV_GN_EPS) * lp['gn_w_d'][l].reshape(N_HEADS, HEAD_DIM)
    o_h = o_h + lp['gn_b_d'][l].reshape(N_HEADS, HEAD_DIM)
    o_h = o_h + jnp.sum(heads(r_d) * heads(k_d) * lp['r_k_d'][l].reshape(N_HEADS, HEAD_DIM),
                        axis=-1, keepdims=True) * heads(v_d)
    y_d = (o_h.reshape(bsz, t, WIDTH) * g_d).astype(BF16)

    ys = [y.reshape(m, WIDTH) for y in (y_a, y_b, y_c, y_d)]
    x1, x1b = _merge(ys, h2, lp['w_br'][l], lp['w_o'][l], x2, lp['ln1_g'][l][None], lp['ln1_b'][l][None])
    new_st = {'swa_k': k_win, 'swa_v': v_win, 'hgrn': s_hgrn, 'mlstm_c': mc, 'mlstm_n': mn,
              'mlstm_m': mm[:, 0, :], 'mlstm_conv': conv_state, 'rwkv': s_rwkv, 'rwkv_shift': shift_state}
    return x1, x1b, new_st


STATE_NAMES = ('swa_k', 'swa_v', 'hgrn', 'mlstm_c', 'mlstm_n', 'mlstm_m', 'mlstm_conv', 'rwkv', 'rwkv_shift')


def _trunk(x3, states, prev_valid, keep, lp):
    bsz, t, _ = x3.shape
    m = bsz * t
    xb2 = x3.reshape(m, D_MODEL).astype(BF16)
    collected = {name: [] for name in STATE_NAMES}
    for l in range(DEPTH):
        st = {name: states[name][l] for name in STATE_NAMES}
        st['keep'] = keep
        x1, x1b, new_st = _mixer(x3, xb2, st, prev_valid, lp, l)
        j = l // 2
        ln_g, ln_b = lp['ln2_g'][l][None], lp['ln2_b'][l][None]
        if l % 2 == 0:
            x2, xb2 = _ffn(x1b, x1, lp['ffn_wg'][j], lp['ffn_wu'][j], lp['ffn_wd'][j], ln_g, ln_b)
        else:
            logits = _matmul(x1, lp['router_w'][j], tm=1024, tn=N_EXPERTS, precision=HIGHEST) + lp['router_b'][j]
            top_v, top_i = lax.top_k(logits, TOP_K)
            probs = jax.nn.softmax(top_v, axis=-1)
            gates = jnp.einsum('mk,mke->me', probs, jax.nn.one_hot(top_i, N_EXPERTS, dtype=F32))
            x2, xb2 = _ffn(x1b, x1, lp['exp_wg'][j], lp['exp_wu'][j], lp['exp_wd'][j], ln_g, ln_b, gates)
        x3 = x2.reshape(bsz, t, D_MODEL)
        for name in STATE_NAMES:
            collected[name].append(new_st[name])
    return x3, {name: jnp.stack(collected[name]) for name in STATE_NAMES}


def kernel(x_prompt, x_sample, cache_swa_k, cache_swa_v, state_hgrn, state_mlstm_c, state_mlstm_n, state_mlstm_m, state_mlstm_conv, state_rwkv, state_rwkv_shift, w_in, lb_raw, gn_a, conv_w, conv_b, wq_b, wk_b, b_i, b_f, gn_b, skip_b, sinks, rel_bias, mu_d, w0_d, w_up_d, a0_d, a_up_d, g_up_d, k_k_d, k_a_d, r_k_d, gn_w_d, gn_b_d, w_br, w_o, ln1_g, ln1_b, ln2_g, ln2_b, ffn_w_gate, ffn_w_up, ffn_w_down, router_w, router_b, exp_w_gate, exp_w_up, exp_w_down):
    off_if = 4 * WIDTH + 3 * WIDTH
    off_c = off_if + 2 * N_HEADS
    off_gate = off_c + (WIDTH + 2 * KV_WIDTH) + D_COLS
    w_main = jnp.concatenate([w_in[:, :, off_gate:], w_in[:, :, :off_if], w_in[:, :, off_c:off_gate]], axis=-1).astype(BF16)
    w_if = w_in[:, :, off_if:off_c]

    lb = _lower_bounds(lb_raw)
    lb = lb[jnp.minimum(jnp.arange(DEPTH), lb.shape[0] - 1)]
    lbp = jnp.stack([jnp.log(jnp.maximum(lb, LB_FLOOR)), jnp.log1p(-lb), 1.0 - lb], axis=1)
    lbp = jnp.concatenate([lbp, jnp.zeros((DEPTH, 5, WIDTH), F32)], axis=1)

    w_qk = jnp.stack([jnp.concatenate([_block_diag(wq_b[l]), _block_diag(wk_b[l]) * HEAD_DIM ** -0.5], axis=1)
                      for l in range(DEPTH)]).astype(BF16)

    span = WINDOW + CHUNK
    rel = jnp.arange(span)[None, :] - WINDOW - jnp.arange(CHUNK)[:, None]
    bias = jnp.transpose(rel_bias.astype(F32)[_rel_bucket(rel)], (2, 0, 1))

    n_dense = ffn_w_gate.shape[0]
    ff_steps = D_FF // D_FF_EXPERT
    lp = {
        'w_main': w_main, 'w_if': w_if, 'lbp': lbp, 'gn_a': gn_a, 'conv_w': conv_w, 'conv_b': conv_b, 'w_qk': w_qk,
        'b_i': b_i, 'b_f': b_f, 'gn_b': gn_b, 'skip_b': skip_b, 'sinks': sinks, 'bias': bias, 'mu_d': mu_d,
        'w0_d': w0_d, 'w_up_d': w_up_d.astype(BF16), 'a0_d': a0_d, 'a_up_d': a_up_d.astype(BF16),
        'g_up_d': g_up_d.astype(BF16), 'k_k_d': k_k_d, 'k_a_d': k_a_d, 'r_k_d': r_k_d, 'gn_w_d': gn_w_d,
        'gn_b_d': gn_b_d, 'w_br': w_br.astype(BF16), 'w_o': w_o.astype(BF16),
        'ln1_g': ln1_g, 'ln1_b': ln1_b, 'ln2_g': ln2_g, 'ln2_b': ln2_b,
        'ffn_wg': jnp.swapaxes(ffn_w_gate.astype(BF16).reshape(n_dense, D_MODEL, ff_steps, D_FF_EXPERT), 1, 2),
        'ffn_wu': jnp.swapaxes(ffn_w_up.astype(BF16).reshape(n_dense, D_MODEL, ff_steps, D_FF_EXPERT), 1, 2),
        'ffn_wd': ffn_w_down.astype(BF16).reshape(n_dense, ff_steps, D_FF_EXPERT, D_MODEL),
        'router_w': router_w, 'router_b': router_b,
        'exp_wg': exp_w_gate.astype(BF16), 'exp_wu': exp_w_up.astype(BF16), 'exp_wd': exp_w_down.astype(BF16),
    }

    sample_states = {
        'swa_k': cache_swa_k, 'swa_v': cache_swa_v, 'hgrn': state_hgrn, 'mlstm_c': state_mlstm_c,
        'mlstm_n': state_mlstm_n, 'mlstm_m': state_mlstm_m, 'mlstm_conv': state_mlstm_conv,
        'rwkv': state_rwkv, 'rwkv_shift': state_rwkv_shift,
    }
    keep = cache_swa_k.shape[2]
    bp = x_prompt.shape[0]
    prompt_states = {}
    for name in STATE_NAMES:
        arr = sample_states[name]
        rows = (WINDOW,) + arr.shape[3:] if name in ('swa_k', 'swa_v') else arr.shape[2:]
        prompt_states[name] = jnp.zeros((DEPTH, bp) + tuple(rows), arr.dtype)

    y_prompt, pst = _trunk(x_prompt, prompt_states, False, keep, lp)
    y_sample, sst = _trunk(x_sample, sample_states, True, keep, lp)
    return (y_prompt, y_sample) + tuple(pst[n] for n in STATE_NAMES) + tuple(sst[n] for n in STATE_NAMES)
```

```python
import functools
import math

import jax
import jax.numpy as jnp
from jax import lax
from jax.experimental import pallas as pl
from jax.experimental.pallas import tpu as pltpu

F32 = jnp.float32
BF16 = jnp.bfloat16
HIGHEST = lax.Precision.HIGHEST

D_MODEL = 1024
DEPTH = 4
CHUNK = 64
HEAD_DIM = 64
N_HEADS = 4
WIDTH = N_HEADS * HEAD_DIM
C_KV_HEADS = 2
KV_WIDTH = C_KV_HEADS * HEAD_DIM
CONV_W = 4
WINDOW = 128
NUM_BUCKETS = 32
MAX_DISTANCE = 128
D_DECAY_LORA = 64
D_AAA_LORA = 64
D_GATE_LORA = 128
D_COLS = 3 * WIDTH + D_DECAY_LORA + D_AAA_LORA + D_GATE_LORA
N_BRANCH = 4
D_FF = 2816
N_EXPERTS = 8
TOP_K = 2
D_FF_EXPERT = 1408
ALPHA = (2 * DEPTH) ** 0.25
LN_EPS = 1e-5
HEAD_NORM_EPS = 1e-5
RWKV_GN_EPS = 64e-5
LB_FLOOR = 1e-30
LB_CEIL = 1.0 - 1e-6

GATE_COLS = N_BRANCH * D_MODEL
COL_A = GATE_COLS
COL_B = COL_A + 4 * WIDTH
COL_C = COL_B + 3 * WIDTH
COL_D = COL_C + WIDTH + 2 * KV_WIDTH
MAIN_COLS = COL_D + D_COLS
SUB = 16
V7X_VMEM_LIMIT = 48 * 1024 * 1024


def _cparams(*sem):
    return pltpu.CompilerParams(dimension_semantics=sem, vmem_limit_bytes=V7X_VMEM_LIMIT)


def _sigmoid(x):
    return 1.0 / (1.0 + jnp.exp(-x))


def _silu(x):
    return x * _sigmoid(x)


def _log_sigmoid(x):
    return jnp.minimum(x, 0.0) - jnp.log1p(jnp.exp(-jnp.abs(x)))


def _layernorm_rows(z, g, b):
    mu = jnp.mean(z, axis=-1, keepdims=True)
    zc = z - mu
    var = jnp.mean(zc * zc, axis=-1, keepdims=True)
    return zc * lax.rsqrt(var + LN_EPS) * g + b


def _matmul_kernel(x_ref, w_ref, o_ref, *, precision):
    x = x_ref[...]
    w = w_ref[...]
    if precision is None:
        x = x.astype(w.dtype)
    o_ref[...] = jnp.dot(x, w, preferred_element_type=F32, precision=precision).astype(o_ref.dtype)


def _matmul(x, w, *, tm, tn, out_dtype=F32, precision=None):
    m, k = x.shape
    n = w.shape[1]
    tm = min(tm, m)
    tn = min(tn, n)
    return pl.pallas_call(
        functools.partial(_matmul_kernel, precision=precision),
        grid=(m // tm, n // tn),
        in_specs=[pl.BlockSpec((tm, k), lambda i, j: (i, 0)),
                  pl.BlockSpec((k, tn), lambda i, j: (0, j))],
        out_specs=pl.BlockSpec((tm, tn), lambda i, j: (i, j)),
        out_shape=jax.ShapeDtypeStruct((m, n), out_dtype),
        compiler_params=_cparams("parallel", "parallel"),
        name="matmul",
    )(x, w)


def _head(a, h):
    return a[:, h * HEAD_DIM:(h + 1) * HEAD_DIM]


def _split_bf16(x):
    hi = x.astype(BF16)
    lo = (x - hi.astype(F32)).astype(BF16)
    return hi, lo


def _head_sums(x, ones_bd):
    hi, lo = _split_bf16(x)
    return (jnp.dot(hi, ones_bd, preferred_element_type=F32)
            + jnp.dot(lo, ones_bd, preferred_element_type=F32))


def _hgrn_kernel(q_ref, f_ref, i_ref, g_ref, lbp_ref, gn_ref, ones_ref, mask_ref, s0_ref, y_ref, s_out_ref,
                 st_ref, phi_ref, plo_ref, *, nc):
    c = pl.program_id(1)
    L = CHUNK

    @pl.when(c == 0)
    def _():
        st_ref[...] = jnp.zeros_like(st_ref)
        for h in range(N_HEADS):
            st_ref[h * HEAD_DIM:(h + 1) * HEAD_DIM, h * HEAD_DIM:(h + 1) * HEAD_DIM] = s0_ref[0, h].T

    ones_bd = ones_ref[...]
    mask_bd = mask_ref[...]
    lane_head = lax.broadcasted_iota(jnp.int32, (1, WIDTH), 1) // HEAD_DIM
    log_lb = lbp_ref[0:1, :]
    log1m_lb = lbp_ref[1:2, :]
    one_m_lb = lbp_ref[2:3, :]
    zf = f_ref[0]
    lsz = _log_sigmoid(zf)
    u = log_lb
    w = log1m_lb + lsz
    logf = jnp.maximum(u, w) + jnp.log1p(jnp.exp(-jnp.abs(u - w)))
    k = one_m_lb * _sigmoid(-zf)
    q = _silu(q_ref[0])
    v = i_ref[0]

    row = lax.broadcasted_iota(jnp.int32, (L, L), 0)
    col = lax.broadcasted_iota(jnp.int32, (L, L), 1)
    tri = (row >= col).astype(F32)
    b = jnp.dot(tri, logf, preferred_element_type=F32, precision=HIGHEST)
    b_last = b[L - 1:L]
    qe = q * jnp.exp(b)
    kdec = k * jnp.exp(b_last - b)
    sub_row = lax.broadcasted_iota(jnp.int32, (SUB, WIDTH), 0)

    for s in range(L):
        r0 = (s // SUB) * SUB
        p = q[r0:r0 + SUB] * k[s:s + 1] * jnp.exp(jnp.minimum(b[r0:r0 + SUB] - b[s:s + 1], 0.0))
        hi, lo = _split_bf16(jnp.where(sub_row >= s - r0, p, 0.0))
        phi_ref[s * SUB:(s + 1) * SUB, :] = hi
        plo_ref[s * SUB:(s + 1) * SUB, :] = lo
    att = (jnp.dot(phi_ref[...], ones_bd, preferred_element_type=F32)
           + jnp.dot(plo_ref[...], ones_bd, preferred_element_type=F32))

    o_blocks = []
    for blk in range(L // SUB):
        r0 = blk * SUB
        acc = jnp.zeros((SUB, WIDTH), F32)
        for j in range(SUB):
            s = r0 + j
            acc = acc + att[s * SUB:(s + 1) * SUB] * v[s:s + 1]
        if blk > 0:
            ref_row = b[r0 - 1:r0]
            q_s = q[r0:r0 + SUB] * jnp.exp(b[r0:r0 + SUB] - ref_row)
            k_s = k[:r0] * jnp.exp(ref_row - b[:r0])
            k_stack = jnp.concatenate([jnp.where(lane_head == h, k_s, 0.0) for h in range(N_HEADS)], axis=0)
            v_stack = jnp.concatenate([jnp.where(lane_head == h, v[:r0], 0.0) for h in range(N_HEADS)], axis=0)
            scores = lax.dot_general(q_s, k_stack, (((1,), (1,)), ((), ())), preferred_element_type=F32)
            acc = acc + jnp.dot(scores, v_stack, preferred_element_type=F32)
        o_blocks.append(acc)

    st = st_ref[...]
    o = jnp.concatenate(o_blocks, axis=0)
    o = o + lax.dot_general(qe, st, (((1,), (1,)), ((), ())), preferred_element_type=F32)
    ms = _head_sums(o * o, ones_bd) * (1.0 / HEAD_DIM)
    y_ref[0] = (o * lax.rsqrt(ms + HEAD_NORM_EPS) * gn_ref[...] * _silu(g_ref[0])).astype(y_ref.dtype)
    upd = lax.dot_general(v, kdec, (((0,), (0,)), ((), ())), preferred_element_type=F32)
    st_ref[...] = st * jnp.exp(b_last) + upd * mask_bd

    @pl.when(c == nc - 1)
    def _():
        for h in range(N_HEADS):
            s_out_ref[0, h] = st_ref[h * HEAD_DIM:(h + 1) * HEAD_DIM, h * HEAD_DIM:(h + 1) * HEAD_DIM].T


def _hgrn_kernel_v1(q_ref, f_ref, i_ref, g_ref, lbp_ref, gn_ref, s0_ref, y_ref, s_out_ref, st_ref, *, nc):
    c = pl.program_id(1)
    L = CHUNK

    @pl.when(c == 0)
    def _():
        for h in range(N_HEADS):
            st_ref[h] = s0_ref[0, h].T

    log_lb = lbp_ref[0:1, :]
    log1m_lb = lbp_ref[1:2, :]
    one_m_lb = lbp_ref[2:3, :]
    zf = f_ref[0]
    lsz = _log_sigmoid(zf)
    u = log_lb
    w = log1m_lb + lsz
    logf = jnp.maximum(u, w) + jnp.log1p(jnp.exp(-jnp.abs(u - w)))
    k = one_m_lb * _sigmoid(-zf)
    q = _silu(q_ref[0])
    v = i_ref[0]

    row = lax.broadcasted_iota(jnp.int32, (L, L), 0)
    col = lax.broadcasted_iota(jnp.int32, (L, L), 1)
    tri = (row >= col).astype(F32)
    b = jnp.dot(tri, logf, preferred_element_type=F32, precision=HIGHEST)
    b_last = b[L - 1:L]
    qe = q * jnp.exp(b)
    kdec = k * jnp.exp(b_last - b)
    sub_row = lax.broadcasted_iota(jnp.int32, (SUB, WIDTH), 0)

    o_heads = [[] for _ in range(N_HEADS)]
    for blk in range(L // SUB):
        r0 = blk * SUB
        b_i = b[r0:r0 + SUB]
        q_i = q[r0:r0 + SUB]
        acc = [jnp.zeros((SUB, HEAD_DIM), F32) for _ in range(N_HEADS)]
        for j in range(SUB):
            s = r0 + j
            p = q_i * k[s:s + 1] * jnp.exp(jnp.minimum(b_i - b[s:s + 1], 0.0))
            p = jnp.where(sub_row >= j, p, 0.0)
            for h in range(N_HEADS):
                a = jnp.sum(_head(p, h), axis=1, keepdims=True)
                acc[h] = acc[h] + a * _head(v[s:s + 1], h)
        if blk > 0:
            ref_row = b[r0 - 1:r0]
            q_s = q_i * jnp.exp(b_i - ref_row)
            k_s = k[:r0] * jnp.exp(ref_row - b[:r0])
            for h in range(N_HEADS):
                att = lax.dot_general(_head(q_s, h), _head(k_s, h), (((1,), (1,)), ((), ())),
                                      preferred_element_type=F32)
                acc[h] = acc[h] + jnp.dot(att, _head(v[:r0], h), preferred_element_type=F32)
        for h in range(N_HEADS):
            o_heads[h].append(acc[h])

    gn = gn_ref[...]
    gact = _silu(g_ref[0])
    for h in range(N_HEADS):
        st = st_ref[h]
        o_h = jnp.concatenate(o_heads[h], axis=0)
        o_h = o_h + lax.dot_general(_head(qe, h), st, (((1,), (1,)), ((), ())), preferred_element_type=F32)
        ms = jnp.mean(o_h * o_h, axis=1, keepdims=True)
        y_h = o_h * lax.rsqrt(ms + HEAD_NORM_EPS) * _head(gn, h) * _head(gact, h)
        y_ref[0, :, h * HEAD_DIM:(h + 1) * HEAD_DIM] = y_h.astype(y_ref.dtype)
        upd = lax.dot_general(_head(v, h), _head(kdec, h), (((0,), (0,)), ((), ())), preferred_element_type=F32)
        st_ref[h] = st * jnp.exp(_head(b_last, h)) + upd

    @pl.when(c == nc - 1)
    def _():
        for h in range(N_HEADS):
            s_out_ref[0, h] = st_ref[h].T


def _hgrn(h3, lbp, gn, s0):
    bsz, t, _ = h3.shape
    nc = t // CHUNK
    cb = COL_A // WIDTH

    def col(j):
        return pl.BlockSpec((1, CHUNK, WIDTH), lambda b, c, j=j: (b, c, cb + j))

    head_of = jnp.arange(WIDTH) // HEAD_DIM
    mask_bd = (head_of[:, None] == head_of[None, :]).astype(F32)
    square = pl.BlockSpec((WIDTH, WIDTH), lambda b, c: (0, 0))
    return pl.pallas_call(
        functools.partial(_hgrn_kernel, nc=nc),
        grid=(bsz, nc),
        in_specs=[col(0), col(1), col(2), col(3),
                  pl.BlockSpec((8, WIDTH), lambda b, c: (0, 0)),
                  pl.BlockSpec((1, WIDTH), lambda b, c: (0, 0)),
                  square, square,
                  pl.BlockSpec((1, N_HEADS, HEAD_DIM, HEAD_DIM), lambda b, c: (b, 0, 0, 0))],
        out_specs=[pl.BlockSpec((1, CHUNK, WIDTH), lambda b, c: (b, c, 0)),
                   pl.BlockSpec((1, N_HEADS, HEAD_DIM, HEAD_DIM), lambda b, c: (b, 0, 0, 0))],
        out_shape=[jax.ShapeDtypeStruct((bsz, t, WIDTH), BF16),
                   jax.ShapeDtypeStruct((bsz, N_HEADS, HEAD_DIM, HEAD_DIM), F32)],
        scratch_shapes=[pltpu.VMEM((WIDTH, WIDTH), F32),
                        pltpu.VMEM((CHUNK * SUB, WIDTH), BF16),
                        pltpu.VMEM((CHUNK * SUB, WIDTH), BF16)],
        compiler_params=_cparams("parallel", "arbitrary"),
        name="hgrn2",
    )(h3, h3, h3, h3, lbp, gn, mask_bd.astype(BF16), mask_bd, s0)


def _mlstm_kernel(qk_ref, v_ref, o_ref, cact_ref, gcol_ref, grow_ref, gn_ref, skip_ref, c0_ref, n0_ref, m0_ref,
                  y_ref, c_out_ref, n_out_ref, m_out_ref, c_ref, n_ref, m_ref, *, nc):
    c = pl.program_id(1)
    L = CHUNK

    @pl.when(c == 0)
    def _():
        c_ref[...] = c0_ref[0]
        n_ref[...] = n0_ref[0]
        m_ref[...] = m0_ref[0]

    row = lax.broadcasted_iota(jnp.int32, (L, L), 0)
    col = lax.broadcasted_iota(jnp.int32, (L, L), 1)
    lower = row >= col
    gcol = gcol_ref[0]
    grow = grow_ref[0, 0]
    f_col = jnp.dot(lower.astype(F32), gcol, preferred_element_type=F32, precision=HIGHEST)
    f_row = jnp.dot(grow, (row <= col).astype(F32), preferred_element_type=F32, precision=HIGHEST)
    qk_all = qk_ref[0]
    v_all = v_ref[0]
    o_all = _sigmoid(o_ref[0])
    cact = cact_ref[0]
    gn = gn_ref[...]
    skip = skip_ref[...]

    for h in range(N_HEADS):
        fc = f_col[:, N_HEADS + h:N_HEADS + h + 1]
        fr = f_row[N_HEADS + h:N_HEADS + h + 1, :]
        li_r = grow[h:h + 1, :]
        li_c = gcol[:, h:h + 1]
        m_prev = m_ref[0:1, h:h + 1]
        q_h = _head(qk_all, h)
        k_h = _head(qk_all, N_HEADS + h)
        v_h = _head(v_all, h)
        cm = c_ref[h]
        n_h = n_ref[h:h + 1, :]

        g = fc + m_prev
        d = jnp.where(lower, fc - fr + li_r, -jnp.inf)
        mt = jnp.maximum(g, jnp.max(d, axis=1, keepdims=True))
        wd = jnp.exp(d - mt)
        wg = jnp.exp(g - mt)
        qk = lax.dot_general(q_h, k_h, (((1,), (1,)), ((), ())), preferred_element_type=F32) * wd
        num = wg * jnp.dot(q_h, cm, preferred_element_type=F32) + jnp.dot(qk, v_h, preferred_element_type=F32)
        den = wg * jnp.sum(q_h * n_h, axis=1, keepdims=True) + jnp.sum(qk, axis=1, keepdims=True)
        hh = num / jnp.maximum(jnp.abs(den), jnp.exp(-mt))

        mt_last = mt[L - 1:L]
        wl = jnp.exp(fc[L - 1:L] - fc + li_c - mt_last)
        wgl = wg[L - 1:L]
        kw = k_h * wl
        c_ref[h] = wgl * cm + lax.dot_general(kw, v_h, (((0,), (0,)), ((), ())), preferred_element_type=F32)
        n_ref[h:h + 1, :] = wgl * n_h + jnp.sum(kw, axis=0, keepdims=True)
        m_ref[0:1, h:h + 1] = mt_last

        z = _head(o_all, h) * hh
        mu = jnp.mean(z, axis=1, keepdims=True)
        zc = z - mu
        var = jnp.mean(zc * zc, axis=1, keepdims=True)
        y_h = zc * lax.rsqrt(var + HEAD_NORM_EPS) * _head(gn, h) + _head(skip, h) * _head(cact, h)
        y_ref[0, :, h * HEAD_DIM:(h + 1) * HEAD_DIM] = y_h.astype(y_ref.dtype)

    @pl.when(c == nc - 1)
    def _():
        c_out_ref[0] = c_ref[...]
        n_out_ref[0] = n_ref[...]
        m_out_ref[0] = m_ref[...]


def _mlstm(qk3, h3, cact3, gcol, grow, gn, skip, c0, n0, m0):
    bsz, t, _ = h3.shape
    nc = t // CHUNK
    cb = COL_B // WIDTH
    state4 = pl.BlockSpec((1, N_HEADS, HEAD_DIM, HEAD_DIM), lambda b, c: (b, 0, 0, 0))
    state_n = pl.BlockSpec((1, N_HEADS, HEAD_DIM), lambda b, c: (b, 0, 0))
    state_m = pl.BlockSpec((1, 1, N_HEADS), lambda b, c: (b, 0, 0))
    vec = pl.BlockSpec((1, WIDTH), lambda b, c: (0, 0))
    return pl.pallas_call(
        functools.partial(_mlstm_kernel, nc=nc),
        grid=(bsz, nc),
        in_specs=[pl.BlockSpec((1, CHUNK, 2 * WIDTH), lambda b, c: (b, c, 0)),
                  pl.BlockSpec((1, CHUNK, WIDTH), lambda b, c: (b, c, cb + 1)),
                  pl.BlockSpec((1, CHUNK, WIDTH), lambda b, c: (b, c, cb + 2)),
                  pl.BlockSpec((1, CHUNK, WIDTH), lambda b, c: (b, c, 0)),
                  pl.BlockSpec((1, CHUNK, 2 * N_HEADS), lambda b, c: (b, c, 0)),
                  pl.BlockSpec((1, 1, 2 * N_HEADS, CHUNK), lambda b, c: (b, c, 0, 0)),
                  vec, vec, state4, state_n, state_m],
        out_specs=[pl.BlockSpec((1, CHUNK, WIDTH), lambda b, c: (b, c, 0)), state4, state_n, state_m],
        out_shape=[jax.ShapeDtypeStruct((bsz, t, WIDTH), BF16),
                   jax.ShapeDtypeStruct((bsz, N_HEADS, HEAD_DIM, HEAD_DIM), F32),
                   jax.ShapeDtypeStruct((bsz, N_HEADS, HEAD_DIM), F32),
                   jax.ShapeDtypeStruct((bsz, 1, N_HEADS), F32)],
        scratch_shapes=[pltpu.VMEM((N_HEADS, HEAD_DIM, HEAD_DIM), F32),
                        pltpu.VMEM((N_HEADS, HEAD_DIM), F32),
                        pltpu.VMEM((1, N_HEADS), F32)],
        compiler_params=_cparams("parallel", "arbitrary"),
        name="mlstm",
    )(qk3, h3, h3, cact3, gcol, grow, gn, skip, c0, n0, m0)


def _swa_kernel(q_ref, k0_ref, k1_ref, k2_ref, v0_ref, v1_ref, v2_ref, bias_ref, sink_ref, y_ref, *, prev_valid):
    c = pl.program_id(1)
    L = CHUNK
    span = WINDOW + L
    q = q_ref[0]
    kcat = jnp.concatenate([k0_ref[0], k1_ref[0], k2_ref[0]], axis=0).astype(BF16)
    vcat = jnp.concatenate([v0_ref[0], v1_ref[0], v2_ref[0]], axis=0).astype(BF16)
    key_pos = lax.broadcasted_iota(jnp.int32, (L, span), 1) + c * L
    for h in range(N_HEADS):
        g = h // (N_HEADS // C_KV_HEADS)
        q_h = _head(q, h).astype(BF16)
        s = lax.dot_general(q_h, _head(kcat, g), (((1,), (1,)), ((), ())), preferred_element_type=F32)
        s = s * HEAD_DIM ** -0.5 + bias_ref[h]
        if not prev_valid:
            s = jnp.where(key_pos >= WINDOW, s, -jnp.inf)
        sink = sink_ref[0:1, h:h + 1]
        m = jnp.maximum(jnp.max(s, axis=1, keepdims=True), sink)
        p = jnp.exp(s - m)
        probs = p / (jnp.sum(p, axis=1, keepdims=True) + jnp.exp(sink - m))
        o = jnp.dot(probs.astype(BF16), _head(vcat, g), preferred_element_type=F32)
        y_ref[0, :, h * HEAD_DIM:(h + 1) * HEAD_DIM] = o.astype(y_ref.dtype)


def _swa(h3, kf, vf, bias, sinks, prev_valid):
    bsz, t, _ = h3.shape
    nc = t // CHUNK
    qb = COL_C // WIDTH

    def kv(j):
        return pl.BlockSpec((1, CHUNK, KV_WIDTH), lambda b, c, j=j: (b, c + j, 0))

    return pl.pallas_call(
        functools.partial(_swa_kernel, prev_valid=prev_valid),
        grid=(bsz, nc),
        in_specs=[pl.BlockSpec((1, CHUNK, WIDTH), lambda b, c: (b, c, qb)),
                  kv(0), kv(1), kv(2), kv(0), kv(1), kv(2),
                  pl.BlockSpec((N_HEADS, CHUNK, WINDOW + CHUNK), lambda b, c: (0, 0, 0)),
                  pl.BlockSpec((1, N_HEADS), lambda b, c: (0, 0))],
        out_specs=pl.BlockSpec((1, CHUNK, WIDTH), lambda b, c: (b, c, 0)),
        out_shape=jax.ShapeDtypeStruct((bsz, t, WIDTH), BF16),
        compiler_params=_cparams("parallel", "parallel"),
        name="swa",
    )(h3, kf, kf, kf, vf, vf, vf, bias, sinks)


RWKV_BB = 4
RWKV_UNROLL = 8


def _rwkv_kernel(r_ref, w_ref, k_ref, v_ref, a_ref, b_ref, s0_ref, y_ref, s_out_ref, s_ref, *, nt, tc):
    ti = pl.program_id(1)
    half = HEAD_DIM
    n_pairs = N_HEADS // 2

    @pl.when(ti == 0)
    def _():
        for bi in range(RWKV_BB):
            for p in range(n_pairs):
                s_ref[bi, p, :, 0:half] = s0_ref[bi, 2 * p]
                s_ref[bi, p, :, half:2 * half] = s0_ref[bi, 2 * p + 1]

    lane = lax.broadcasted_iota(jnp.int32, (HEAD_DIM, 2 * half), 1)
    sub = lax.broadcasted_iota(jnp.int32, (HEAD_DIM, 2 * half), 0)
    lo = lane < half
    lane1 = lax.broadcasted_iota(jnp.int32, (1, 2 * half), 1)
    lo1 = lane1 < half
    diag = (lane % half) == sub

    def seg_sum(x, row_vec):
        s_lo = jnp.sum(x * jnp.where(lo1, row_vec, 0.0), axis=1, keepdims=True)
        s_hi = jnp.sum(x * jnp.where(lo1, 0.0, row_vec), axis=1, keepdims=True)
        return jnp.where(lo, s_lo, s_hi)

    def block(i, carry):
        t0 = pl.multiple_of(i * RWKV_UNROLL, RWKV_UNROLL)
        for bi in range(RWKV_BB):
            for p in range(n_pairs):
                ls = slice(p * 2 * half, (p + 1) * 2 * half)
                r8 = r_ref[bi, pl.ds(t0, RWKV_UNROLL), ls]
                w8 = w_ref[bi, pl.ds(t0, RWKV_UNROLL), ls]
                k8 = k_ref[bi, pl.ds(t0, RWKV_UNROLL), ls]
                v8 = v_ref[bi, pl.ds(t0, RWKV_UNROLL), ls]
                a8 = a_ref[bi, pl.ds(t0, RWKV_UNROLL), ls]
                b8 = b_ref[bi, pl.ds(t0, RWKV_UNROLL), ls]
                st = s_ref[bi, p]
                ys = []
                for j in range(RWKV_UNROLL):
                    sa = seg_sum(st, a8[j:j + 1])
                    vcol = seg_sum(jnp.where(diag, 1.0, 0.0), v8[j:j + 1])
                    st = st * w8[j:j + 1] + sa * b8[j:j + 1] + vcol * k8[j:j + 1]
                    ycol = seg_sum(st, r8[j:j + 1])
                    ys.append(jnp.sum(jnp.where(diag, ycol, 0.0), axis=0, keepdims=True))
                s_ref[bi, p] = st
                y_ref[bi, pl.ds(t0, RWKV_UNROLL), ls] = jnp.concatenate(ys, axis=0)
        return carry

    lax.fori_loop(0, tc // RWKV_UNROLL, block, 0)

    @pl.when(ti == nt - 1)
    def _():
        for bi in range(RWKV_BB):
            for p in range(n_pairs):
                s_out_ref[bi, 2 * p] = s_ref[bi, p, :, 0:half]
                s_out_ref[bi, 2 * p + 1] = s_ref[bi, p, :, half:2 * half]


def _stack_heads(x):
    lane_head = lax.broadcasted_iota(jnp.int32, (1, WIDTH), 1) // HEAD_DIM
    return jnp.concatenate([jnp.where(lane_head == h, x, 0.0) for h in range(N_HEADS)], axis=0)


def _dot_hi(a, b):
    return jnp.dot(a, b, preferred_element_type=F32, precision=HIGHEST)


def _dot_hi_nt(a, b):
    return lax.dot_general(a, b, (((1,), (1,)), ((), ())), preferred_element_type=F32, precision=HIGHEST)


def _rwkv_chunk_kernel(r_ref, lw_ref, k_ref, v_ref, a_ref, b_ref, mask_ref, s0_ref, y_ref, s_out_ref, st_ref, *, nc):
    c = pl.program_id(1)
    L = CHUNK

    @pl.when(c == 0)
    def _():
        st_ref[...] = jnp.zeros_like(st_ref)
        for h in range(N_HEADS):
            st_ref[h * HEAD_DIM:(h + 1) * HEAD_DIM, h * HEAD_DIM:(h + 1) * HEAD_DIM] = s0_ref[0, h]

    r = r_ref[0]
    lw = lw_ref[0]
    k = k_ref[0]
    v = v_ref[0]
    a = a_ref[0]
    b = b_ref[0]
    row = lax.broadcasted_iota(jnp.int32, (L, L), 0)
    col = lax.broadcasted_iota(jnp.int32, (L, L), 1)
    cw = _dot_hi((row >= col).astype(F32), lw)
    g_in = jnp.exp(cw)
    g_inv = jnp.exp(-cw)
    at = a * jnp.exp(cw - lw)
    rt = r * g_in
    bt = b * g_inv
    kt = k * g_inv
    x = jnp.concatenate([at, rt], axis=0)
    gram = _dot_hi_nt(x, jnp.concatenate([_stack_heads(bt), _stack_heads(kt)], axis=0))

    t_idx = lax.broadcasted_iota(jnp.int32, (L, WIDTH), 0)
    i_idx = lax.broadcasted_iota(jnp.int32, (L, WIDTH), 1) % L
    strict = i_idx < t_idx
    incl = i_idx <= t_idx
    same_blk = (i_idx // SUB) == (t_idx // SUB)
    eye = (i_idx == t_idx).astype(F32)
    n_all = jnp.where(strict, gram[0:L, 0:WIDTH], 0.0)
    m_all = jnp.where(strict, gram[0:L, WIDTH:2 * WIDTH], 0.0)
    rb_all = jnp.where(incl, gram[L:2 * L, 0:WIDTH], 0.0)
    rk_all = jnp.where(incl, gram[L:2 * L, WIDTH:2 * WIDTH], 0.0)
    n_d = jnp.where(same_blk, n_all, 0.0)
    n_off = jnp.where(same_blk, 0.0, n_all)

    x2 = _dot_hi(n_d, _stack_heads(n_d))
    x4 = _dot_hi(x2, _stack_heads(x2))
    x8 = _dot_hi(x4, _stack_heads(x4))
    t_d = eye + n_d
    t_d = t_d + _dot_hi(t_d, _stack_heads(x2))
    t_d = t_d + _dot_hi(t_d, _stack_heads(x4))
    t_d = t_d + _dot_hi(t_d, _stack_heads(x8))
    n1 = _dot_hi(t_d, _stack_heads(n_off))
    n2 = _dot_hi(n1, _stack_heads(n1))
    a1 = eye + n1
    a2 = a1 + _dot_hi(a1, _stack_heads(n2))
    t_full = _dot_hi(a2, _stack_heads(t_d))

    st = st_ref[...]
    u = _dot_hi_nt(x, st)
    v_stack = _stack_heads(v)
    rhs = u[0:L] + _dot_hi(m_all, v_stack)
    sa = _dot_hi(t_full, _stack_heads(rhs))
    y_ref[0] = u[L:2 * L] + _dot_hi(rb_all, _stack_heads(sa)) + _dot_hi(rk_all, v_stack)
    upd = lax.dot_general(jnp.concatenate([sa, v], axis=0), jnp.concatenate([bt, kt], axis=0),
                          (((0,), (0,)), ((), ())), preferred_element_type=F32, precision=HIGHEST)
    st_ref[...] = (st + upd * mask_ref[...]) * g_in[L - 1:L]

    @pl.when(c == nc - 1)
    def _():
        for h in range(N_HEADS):
            s_out_ref[0, h] = st_ref[h * HEAD_DIM:(h + 1) * HEAD_DIM, h * HEAD_DIM:(h + 1) * HEAD_DIM]


def _rwkv(r, lw, k, v, a, b, s0):
    bsz, t, _ = r.shape
    nc = t // CHUNK
    head_of = jnp.arange(WIDTH) // HEAD_DIM
    mask_bd = (head_of[:, None] == head_of[None, :]).astype(F32)
    seq = pl.BlockSpec((1, CHUNK, WIDTH), lambda g, i: (g, i, 0))
    state = pl.BlockSpec((1, N_HEADS, HEAD_DIM, HEAD_DIM), lambda g, i: (g, 0, 0, 0))
    return pl.pallas_call(
        functools.partial(_rwkv_chunk_kernel, nc=nc),
        grid=(bsz, nc),
        in_specs=[seq] * 6 + [pl.BlockSpec((WIDTH, WIDTH), lambda g, i: (0, 0)), state],
        out_specs=[seq, state],
        out_shape=[jax.ShapeDtypeStruct((bsz, t, WIDTH), F32),
                   jax.ShapeDtypeStruct((bsz, N_HEADS, HEAD_DIM, HEAD_DIM), F32)],
        scratch_shapes=[pltpu.VMEM((WIDTH, WIDTH), F32)],
        compiler_params=_cparams("parallel", "arbitrary"),
        name="rwkv7",
    )(r, lw, k, v, a, b, mask_bd, s0)


def _rwkv_seq(r, w, k, v, a, b, s0):
    bsz, t, _ = r.shape
    tc = min(t, 256)
    nt = t // tc
    seq = pl.BlockSpec((RWKV_BB, tc, WIDTH), lambda g, i: (g, i, 0))
    state = pl.BlockSpec((RWKV_BB, N_HEADS, HEAD_DIM, HEAD_DIM), lambda g, i: (g, 0, 0, 0))
    return pl.pallas_call(
        functools.partial(_rwkv_kernel, nt=nt, tc=tc),
        grid=(bsz // RWKV_BB, nt),
        in_specs=[seq] * 6 + [state],
        out_specs=[seq, state],
        out_shape=[jax.ShapeDtypeStruct((bsz, t, WIDTH), F32),
                   jax.ShapeDtypeStruct((bsz, N_HEADS, HEAD_DIM, HEAD_DIM), F32)],
        scratch_shapes=[pltpu.VMEM((RWKV_BB, N_HEADS // 2, HEAD_DIM, 2 * HEAD_DIM), F32)],
        compiler_params=_cparams("parallel", "arbitrary"),
        name="rwkv7",
    )(r, w, k, v, a, b, s0)


def _merge_kernel(ya_ref, yb_ref, yc_ref, yd_ref, g0_ref, g1_ref, g2_ref, g3_ref, wbr_ref, wo_ref, x_ref,
                  lng_ref, lnb_ref, xo_ref, xob_ref):
    ys = (ya_ref, yb_ref, yc_ref, yd_ref)
    gs = (g0_ref, g1_ref, g2_ref, g3_ref)
    merged = None
    for n in range(N_BRANCH):
        term = _sigmoid(gs[n][...]) * jnp.dot(ys[n][...], wbr_ref[n], preferred_element_type=F32)
        merged = term if merged is None else merged + term
    out = jnp.dot(merged.astype(BF16), wo_ref[...], preferred_element_type=F32)
    xn = _layernorm_rows(ALPHA * x_ref[...] + out, lng_ref[...], lnb_ref[...])
    xo_ref[...] = xn
    xob_ref[...] = xn.astype(BF16)


def _merge(ys, h2, wbr, wo, x2, lng, lnb, *, tm=256):
    m = x2.shape[0]
    tm = min(tm, m)
    ysp = pl.BlockSpec((tm, WIDTH), lambda i: (i, 0))
    row = pl.BlockSpec((tm, D_MODEL), lambda i: (i, 0))
    vec = pl.BlockSpec((1, D_MODEL), lambda i: (0, 0))

    def gate(n):
        return pl.BlockSpec((tm, D_MODEL), lambda i, n=n: (i, n))

    return pl.pallas_call(
        _merge_kernel,
        grid=(m // tm,),
        in_specs=[ysp, ysp, ysp, ysp, gate(0), gate(1), gate(2), gate(3),
                  pl.BlockSpec((N_BRANCH, WIDTH, D_MODEL), lambda i: (0, 0, 0)),
                  pl.BlockSpec((D_MODEL, D_MODEL), lambda i: (0, 0)),
                  row, vec, vec],
        out_specs=[row, row],
        out_shape=[jax.ShapeDtypeStruct((m, D_MODEL), F32), jax.ShapeDtypeStruct((m, D_MODEL), BF16)],
        compiler_params=_cparams("parallel"),
        name="merge",
    )(*ys, h2, h2, h2, h2, wbr, wo, x2, lng, lnb)


def _ffn_kernel(*refs, n_steps, gated):
    if gated:
        xb_ref, x_ref, gates_ref, wg_ref, wu_ref, wd_ref, lng_ref, lnb_ref, xo_ref, xob_ref, acc_ref = refs
    else:
        xb_ref, x_ref, wg_ref, wu_ref, wd_ref, lng_ref, lnb_ref, xo_ref, xob_ref, acc_ref = refs
    j = pl.program_id(1)

    @pl.when(j == 0)
    def _():
        acc_ref[...] = jnp.zeros_like(acc_ref)

    xb = xb_ref[...]
    hg = jnp.dot(xb, wg_ref[0], preferred_element_type=F32)
    hu = jnp.dot(xb, wu_ref[0], preferred_element_type=F32)
    part = jnp.dot((_silu(hg) * hu).astype(BF16), wd_ref[0], preferred_element_type=F32)
    if gated:
        gates = gates_ref[...]
        lane = lax.broadcasted_iota(jnp.int32, gates.shape, 1)
        part = jnp.sum(jnp.where(lane == j, gates, 0.0), axis=1, keepdims=True) * part
    acc_ref[...] += part

    @pl.when(j == n_steps - 1)
    def _():
        xn = _layernorm_rows(ALPHA * x_ref[...] + acc_ref[...], lng_ref[...], lnb_ref[...])
        xo_ref[...] = xn
        xob_ref[...] = xn.astype(BF16)


def _ffn(xb, x2, wg, wu, wd, lng, lnb, gates=None, *, tm=512):
    m = x2.shape[0]
    tm = min(tm, m)
    n_steps, _, tf = wg.shape
    row = pl.BlockSpec((tm, D_MODEL), lambda i, j: (i, 0))
    vec = pl.BlockSpec((1, D_MODEL), lambda i, j: (0, 0))
    w_in_spec = pl.BlockSpec((1, D_MODEL, tf), lambda i, j: (j, 0, 0))
    w_out_spec = pl.BlockSpec((1, tf, D_MODEL), lambda i, j: (j, 0, 0))
    in_specs = [row, row]
    args = [xb, x2]
    if gates is not None:
        in_specs.append(pl.BlockSpec((tm, N_EXPERTS), lambda i, j: (i, 0)))
        args.append(gates)
    in_specs += [w_in_spec, w_in_spec, w_out_spec, vec, vec]
    args += [wg, wu, wd, lng, lnb]
    return pl.pallas_call(
        functools.partial(_ffn_kernel, n_steps=n_steps, gated=gates is not None),
        grid=(m // tm, n_steps),
        in_specs=in_specs,
        out_specs=[row, row],
        out_shape=[jax.ShapeDtypeStruct((m, D_MODEL), F32), jax.ShapeDtypeStruct((m, D_MODEL), BF16)],
        scratch_shapes=[pltpu.VMEM((tm, D_MODEL), F32)],
        compiler_params=_cparams("parallel", "arbitrary"),
        name="moe" if gates is not None else "ffn",
    )(*args)


def _rel_bucket(rel):
    half = NUM_BUCKETS // 2
    exact = half // 2
    dist = jnp.abs(rel)
    far = exact + (jnp.log(jnp.maximum(dist, 1).astype(F32) / exact)
                   / math.log(MAX_DISTANCE / exact) * (half - exact)).astype(jnp.int32)
    far = jnp.minimum(far, half - 1)
    return jnp.where(rel > 0, half, 0) + jnp.where(dist < exact, dist, far)


def _lower_bounds(lb_raw):
    sm = jax.nn.softmax(lb_raw.astype(F32), axis=0)
    lb = jnp.concatenate([jnp.zeros_like(sm[:1]), jnp.cumsum(sm[1:], axis=0)[:-1]], axis=0)
    return jnp.clip(lb, 0.0, LB_CEIL)


def _block_diag(w):
    out = jnp.zeros((WIDTH, WIDTH), w.dtype)
    for h in range(N_HEADS):
        out = out.at[h * HEAD_DIM:(h + 1) * HEAD_DIM, h * HEAD_DIM:(h + 1) * HEAD_DIM].set(w[h])
    return out


def _split(p, sizes):
    out, off = [], 0
    for size in sizes:
        out.append(p[..., off:off + size])
        off += size
    return out


def _mixer(x3, xb2, st, prev_valid, lp, l):
    bsz, t, _ = x3.shape
    m = bsz * t
    x2 = x3.reshape(m, D_MODEL)
    h2 = _matmul(xb2, lp['w_main'][l], tm=1024, tn=256)
    gif = _matmul(x2, lp['w_if'][l], tm=1024, tn=2 * N_HEADS, precision=HIGHEST)
    h3 = h2.reshape(bsz, t, MAIN_COLS)

    y_a, s_hgrn = _hgrn(h3, lp['lbp'][l], lp['gn_a'][l][None], st['hgrn'])

    u = h3[..., COL_B:COL_B + WIDTH]
    full = jnp.concatenate([st['mlstm_conv'], u], axis=1)
    conv_out = lp['conv_b'][l] + sum(full[:, j:j + t] * lp['conv_w'][l, j] for j in range(CONV_W))
    conv_state = full[:, -(CONV_W - 1):]
    c_act = jax.nn.silu(conv_out)
    qk3 = _matmul(c_act.reshape(m, WIDTH), lp['w_qk'][l], tm=1024, tn=2 * WIDTH).reshape(bsz, t, 2 * WIDTH)
    gi3 = gif.reshape(bsz, t, 2 * N_HEADS)
    logi = gi3[..., :N_HEADS] + lp['b_i'][l]
    logf = jax.nn.log_sigmoid(gi3[..., N_HEADS:] + lp['b_f'][l])
    gcol = jnp.concatenate([logi, logf], axis=-1)
    grow = jnp.swapaxes(gcol.reshape(bsz, t // CHUNK, CHUNK, 2 * N_HEADS), 2, 3)
    y_b, mc, mn, mm = _mlstm(qk3, h3, c_act, gcol, grow, lp['gn_b'][l][None], lp['skip_b'][l][None],
                             st['mlstm_c'], st['mlstm_n'], st['mlstm_m'][:, None, :])

    kf = jnp.concatenate([st['swa_k'].reshape(bsz, WINDOW, KV_WIDTH), h3[..., COL_C + WIDTH:COL_C + WIDTH + KV_WIDTH]], axis=1)
    vf = jnp.concatenate([st['swa_v'].reshape(bsz, WINDOW, KV_WIDTH), h3[..., COL_C + WIDTH + KV_WIDTH:COL_D]], axis=1)
    y_c = _swa(h3, kf, vf, lp['bias'], lp['sinks'][l][None], prev_valid)
    keep = st['keep']
    k_win = kf[:, -keep:].reshape(bsz, keep, C_KV_HEADS, HEAD_DIM)
    v_win = vf[:, -keep:].reshape(bsz, keep, C_KV_HEADS, HEAD_DIM)

    pd = h3[..., COL_D:COL_D + D_COLS]
    pd_prev = jnp.concatenate([st['rwkv_shift'][:, None], pd[:, :-1]], axis=1)
    r_d, w_d, k_raw, v_d, a_in, g_in = _split(pd + (pd_prev - pd) * lp['mu_d'][l],
                                              (WIDTH, D_DECAY_LORA, WIDTH, WIDTH, D_AAA_LORA, D_GATE_LORA))
    shift_state = pd[:, -1]
    w_lora = _matmul(jnp.tanh(w_d).reshape(m, D_DECAY_LORA), lp['w_up_d'][l], tm=1024, tn=WIDTH).reshape(bsz, t, WIDTH)
    a_lora = _matmul(a_in.reshape(m, D_AAA_LORA), lp['a_up_d'][l], tm=1024, tn=WIDTH).reshape(bsz, t, WIDTH)
    g_d = _matmul(jax.nn.sigmoid(g_in).reshape(m, D_GATE_LORA), lp['g_up_d'][l], tm=1024, tn=WIDTH).reshape(bsz, t, WIDTH)
    w_log = -jax.nn.softplus(-(lp['w0_d'][l] + w_lora)) - 0.5
    log_decay = -jnp.exp(w_log)
    a_d = jax.nn.sigmoid(lp['a0_d'][l] + a_lora)

    def heads(a):
        return a.reshape(bsz, t, N_HEADS, HEAD_DIM)

    kk = heads(k_raw * lp['k_k_d'][l])
    kk = (kk * lax.rsqrt(jnp.maximum(jnp.sum(kk * kk, axis=-1, keepdims=True), 1e-24))).reshape(bsz, t, WIDTH)
    k_d = k_raw * (1.0 + (a_d - 1.0) * lp['k_a_d'][l])
    o_d, s_rwkv = _rwkv(r_d, log_decay, k_d, v_d, -kk, kk * a_d, st['rwkv'])
    o_h = heads(o_d)
    mu = jnp.mean(o_h, axis=-1, keepdims=True)
    var = jnp.mean(jnp.square(o_h - mu), axis=-1, keepdims=True)
    o_h = (o_h - mu) * lax.rsqrt(var + RWKV_GN_EPS) * lp['gn_w_d'][l].reshape(N_HEADS, HEAD_DIM)
    o_h = o_h + lp['gn_b_d'][l].reshape(N_HEADS, HEAD_DIM)
    o_h = o_h + jnp.sum(heads(r_d) * heads(k_d) * lp['r_k_d'][l].reshape(N_HEADS, HEAD_DIM),
                        axis=-1, keepdims=True) * heads(v_d)
    y_d = (o_h.reshape(bsz, t, WIDTH) * g_d).astype(BF16)

    ys = [y.reshape(m, WIDTH) for y in (y_a, y_b, y_c, y_d)]
    x1, x1b = _merge(ys, h2, lp['w_br'][l], lp['w_o'][l], x2, lp['ln1_g'][l][None], lp['ln1_b'][l][None])
    new_st = {'swa_k': k_win, 'swa_v': v_win, 'hgrn': s_hgrn, 'mlstm_c': mc, 'mlstm_n': mn,
              'mlstm_m': mm[:, 0, :], 'mlstm_conv': conv_state, 'rwkv': s_rwkv, 'rwkv_shift': shift_state}
    return x1, x1b, new_st


STATE_NAMES = ('swa_k', 'swa_v', 'hgrn', 'mlstm_c', 'mlstm_n', 'mlstm_m', 'mlstm_conv', 'rwkv', 'rwkv_shift')


def _trunk(x3, states, prev_valid, keep, lp):
    bsz, t, _ = x3.shape
    m = bsz * t
    xb2 = x3.reshape(m, D_MODEL).astype(BF16)
    collected = {name: [] for name in STATE_NAMES}
    for l in range(DEPTH):
        st = {name: states[name][l] for name in STATE_NAMES}
        st['keep'] = keep
        x1, x1b, new_st = _mixer(x3, xb2, st, prev_valid, lp, l)
        j = l // 2
        ln_g, ln_b = lp['ln2_g'][l][None], lp['ln2_b'][l][None]
        if l % 2 == 0:
            x2, xb2 = _ffn(x1b, x1, lp['ffn_wg'][j], lp['ffn_wu'][j], lp['ffn_wd'][j], ln_g, ln_b)
        else:
            logits = _matmul(x1, lp['router_w'][j], tm=1024, tn=N_EXPERTS, precision=HIGHEST) + lp['router_b'][j]
            top_v, top_i = lax.top_k(logits, TOP_K)
            probs = jax.nn.softmax(top_v, axis=-1)
            gates = jnp.einsum('mk,mke->me', probs, jax.nn.one_hot(top_i, N_EXPERTS, dtype=F32))
            x2, xb2 = _ffn(x1b, x1, lp['exp_wg'][j], lp['exp_wu'][j], lp['exp_wd'][j], ln_g, ln_b, gates)
        x3 = x2.reshape(bsz, t, D_MODEL)
        for name in STATE_NAMES:
            collected[name].append(new_st[name])
    return x3, {name: jnp.stack(collected[name]) for name in STATE_NAMES}


def kernel(x_prompt, x_sample, cache_swa_k, cache_swa_v, state_hgrn, state_mlstm_c, state_mlstm_n, state_mlstm_m, state_mlstm_conv, state_rwkv, state_rwkv_shift, w_in, lb_raw, gn_a, conv_w, conv_b, wq_b, wk_b, b_i, b_f, gn_b, skip_b, sinks, rel_bias, mu_d, w0_d, w_up_d, a0_d, a_up_d, g_up_d, k_k_d, k_a_d, r_k_d, gn_w_d, gn_b_d, w_br, w_o, ln1_g, ln1_b, ln2_g, ln2_b, ffn_w_gate, ffn_w_up, ffn_w_down, router_w, router_b, exp_w_gate, exp_w_up, exp_w_down):
    off_if = 4 * WIDTH + 3 * WIDTH
    off_c = off_if + 2 * N_HEADS
    off_gate = off_c + (WIDTH + 2 * KV_WIDTH) + D_COLS
    w_main = jnp.concatenate([w_in[:, :, off_gate:], w_in[:, :, :off_if], w_in[:, :, off_c:off_gate]], axis=-1).astype(BF16)
    w_if = w_in[:, :, off_if:off_c]

    lb = _lower_bounds(lb_raw)
    lb = lb[jnp.minimum(jnp.arange(DEPTH), lb.shape[0] - 1)]
    lbp = jnp.stack([jnp.log(jnp.maximum(lb, LB_FLOOR)), jnp.log1p(-lb), 1.0 - lb], axis=1)
    lbp = jnp.concatenate([lbp, jnp.zeros((DEPTH, 5, WIDTH), F32)], axis=1)

    w_qk = jnp.stack([jnp.concatenate([_block_diag(wq_b[l]), _block_diag(wk_b[l]) * HEAD_DIM ** -0.5], axis=1)
                      for l in range(DEPTH)]).astype(BF16)

    span = WINDOW + CHUNK
    rel = jnp.arange(span)[None, :] - WINDOW - jnp.arange(CHUNK)[:, None]
    bias = jnp.transpose(rel_bias.astype(F32)[_rel_bucket(rel)], (2, 0, 1))

    n_dense = ffn_w_gate.shape[0]
    ff_steps = D_FF // D_FF_EXPERT
    lp = {
        'w_main': w_main, 'w_if': w_if, 'lbp': lbp, 'gn_a': gn_a, 'conv_w': conv_w, 'conv_b': conv_b, 'w_qk': w_qk,
        'b_i': b_i, 'b_f': b_f, 'gn_b': gn_b, 'skip_b': skip_b, 'sinks': sinks, 'bias': bias, 'mu_d': mu_d,
        'w0_d': w0_d, 'w_up_d': w_up_d.astype(BF16), 'a0_d': a0_d, 'a_up_d': a_up_d.astype(BF16),
        'g_up_d': g_up_d.astype(BF16), 'k_k_d': k_k_d, 'k_a_d': k_a_d, 'r_k_d': r_k_d, 'gn_w_d': gn_w_d,
        'gn_b_d': gn_b_d, 'w_br': w_br.astype(BF16), 'w_o': w_o.astype(BF16),
        'ln1_g': ln1_g, 'ln1_b': ln1_b, 'ln2_g': ln2_g, 'ln2_b': ln2_b,
        'ffn_wg': jnp.swapaxes(ffn_w_gate.astype(BF16).reshape(n_dense, D_MODEL, ff_steps, D_FF_EXPERT), 1, 2),
        'ffn_wu': jnp.swapaxes(ffn_w_up.astype(BF16).reshape(n_dense, D_MODEL, ff_steps, D_FF_EXPERT), 1, 2),
        'ffn_wd': ffn_w_down.astype(BF16).reshape(n_dense, ff_steps, D_FF_EXPERT, D_MODEL),
        'router_w': router_w, 'router_b': router_b,
        'exp_wg': exp_w_gate.astype(BF16), 'exp_wu': exp_w_up.astype(BF16), 'exp_wd': exp_w_down.astype(BF16),
    }

    sample_states = {
        'swa_k': cache_swa_k, 'swa_v': cache_swa_v, 'hgrn': state_hgrn, 'mlstm_c': state_mlstm_c,
        'mlstm_n': state_mlstm_n, 'mlstm_m': state_mlstm_m, 'mlstm_conv': state_mlstm_conv,
        'rwkv': state_rwkv, 'rwkv_shift': state_rwkv_shift,
    }
    keep = cache_swa_k.shape[2]
    bp = x_prompt.shape[0]
    prompt_states = {}
    for name in STATE_NAMES:
        arr = sample_states[name]
        rows = (WINDOW,) + arr.shape[3:] if name in ('swa_k', 'swa_v') else arr.shape[2:]
        prompt_states[name] = jnp.zeros((DEPTH, bp) + tuple(rows), arr.dtype)

    y_prompt, pst = _trunk(x_prompt, prompt_states, False, keep, lp)
    y_sample, sst = _trunk(x_sample, sample_states, True, keep, lp)
    return (y_prompt, y_sample) + tuple(pst[n] for n in STATE_NAMES) + tuple(sst[n] for n in STATE_NAMES)
```

```python
import functools
import math

import jax
import jax.numpy as jnp
from jax import lax
from jax.experimental import pallas as pl
from jax.experimental.pallas import tpu as pltpu

F32 = jnp.float32
BF16 = jnp.bfloat16
HIGHEST = lax.Precision.HIGHEST

D_MODEL = 1024
DEPTH = 4
CHUNK = 64
HEAD_DIM = 64
N_HEADS = 4
WIDTH = N_HEADS * HEAD_DIM
C_KV_HEADS = 2
KV_WIDTH = C_KV_HEADS * HEAD_DIM
CONV_W = 4
WINDOW = 128
NUM_BUCKETS = 32
MAX_DISTANCE = 128
D_DECAY_LORA = 64
D_AAA_LORA = 64
D_GATE_LORA = 128
D_COLS = 3 * WIDTH + D_DECAY_LORA + D_AAA_LORA + D_GATE_LORA
N_BRANCH = 4
D_FF = 2816
N_EXPERTS = 8
TOP_K = 2
D_FF_EXPERT = 1408
ALPHA = (2 * DEPTH) ** 0.25
LN_EPS = 1e-5
HEAD_NORM_EPS = 1e-5
RWKV_GN_EPS = 64e-5
LB_FLOOR = 1e-30
LB_CEIL = 1.0 - 1e-6

GATE_COLS = N_BRANCH * D_MODEL
COL_A = GATE_COLS
COL_D = COL_A + 4 * WIDTH
COL_B = COL_D + D_COLS
COL_C = COL_B + 3 * WIDTH
MAIN_COLS = COL_C + WIDTH + 2 * KV_WIDTH
SUB = 16
V7X_VMEM_LIMIT = 48 * 1024 * 1024

NN = (((1,), (0,)), ((), ()))
NT = (((1,), (1,)), ((), ()))
TN = (((0,), (0,)), ((), ()))


def _cparams(*sem):
    return pltpu.CompilerParams(dimension_semantics=sem, vmem_limit_bytes=V7X_VMEM_LIMIT)


def _sigmoid(x):
    return 1.0 / (1.0 + jnp.exp(-x))


def _silu(x):
    return x * _sigmoid(x)


def _log_sigmoid(x):
    return jnp.minimum(x, 0.0) - jnp.log1p(jnp.exp(-jnp.abs(x)))


def _layernorm_rows(z, g, b):
    mu = jnp.mean(z, axis=-1, keepdims=True)
    zc = z - mu
    var = jnp.mean(zc * zc, axis=-1, keepdims=True)
    return zc * lax.rsqrt(var + LN_EPS) * g + b


def _split2(x):
    hi = x.astype(BF16)
    lo = (x - hi.astype(F32)).astype(BF16)
    return hi, lo


def _split3(x):
    hi = x.astype(BF16)
    r1 = x - hi.astype(F32)
    mid = r1.astype(BF16)
    lo = (r1 - mid.astype(F32)).astype(BF16)
    return hi, mid, lo


def _dot(a, b, dims=NN):
    return lax.dot_general(a, b, dims, preferred_element_type=F32)


def _mm2(a, b_parts, dims=NN):
    a_hi, a_lo = _split2(a)
    n = a.shape[0]
    both = _dot(jnp.concatenate([a_hi, a_lo], axis=0), b_parts[0], dims)
    return both[:n] + both[n:] + _dot(a_hi, b_parts[1], dims)


def _cumsum_rows(x):
    n = x.shape[0]
    row = lax.broadcasted_iota(jnp.int32, (n, n), 0)
    col = lax.broadcasted_iota(jnp.int32, (n, n), 1)
    tri = (row >= col).astype(BF16)
    hi, mid, lo = _split3(x)
    return _dot(tri, hi) + _dot(tri, mid) + _dot(tri, lo)


def _head_masks():
    lane_head = lax.broadcasted_iota(jnp.int32, (1, WIDTH), 1) // HEAD_DIM
    return [(lane_head == h) for h in range(N_HEADS)]


def _stack_heads(x, masks):
    return jnp.concatenate([jnp.where(mk, x, 0.0) for mk in masks], axis=0)


def _stack_parts(x, masks):
    parts = _split2(x)
    rows = [mk.astype(BF16) for mk in masks]
    return tuple(jnp.concatenate([p * r for r in rows], axis=0) for p in parts)


def _head_sums(x, ones_bd):
    hi, lo = _split2(x)
    return _dot(hi, ones_bd) + _dot(lo, ones_bd)


def _matmul_kernel(x_ref, w_ref, o_ref, *, precision, dims):
    x = x_ref[...]
    w = w_ref[...]
    if precision is None:
        x = x.astype(w.dtype)
    o_ref[...] = lax.dot_general(x, w, dims, preferred_element_type=F32, precision=precision).astype(o_ref.dtype)


def _matmul(x, w, *, tm, tn, out_dtype=F32, precision=None):
    m, k = x.shape
    n = w.shape[1]
    tm = min(tm, m)
    tn = min(tn, n)
    return pl.pallas_call(
        functools.partial(_matmul_kernel, precision=precision, dims=NN),
        grid=(m // tm, n // tn),
        in_specs=[pl.BlockSpec((tm, k), lambda i, j: (i, 0)),
                  pl.BlockSpec((k, tn), lambda i, j: (0, j))],
        out_specs=pl.BlockSpec((tm, tn), lambda i, j: (i, j)),
        out_shape=jax.ShapeDtypeStruct((m, n), out_dtype),
        compiler_params=_cparams("parallel", "parallel"),
        name="matmul",
    )(x, w)


def _matmul_t(wt, x, *, tm, precision):
    n, k = wt.shape
    m = x.shape[0]
    tm = min(tm, m)
    return pl.pallas_call(
        functools.partial(_matmul_kernel, precision=precision, dims=NT),
        grid=(m // tm,),
        in_specs=[pl.BlockSpec((n, k), lambda i: (0, 0)),
                  pl.BlockSpec((tm, k), lambda i: (i, 0))],
        out_specs=pl.BlockSpec((n, tm), lambda i: (0, i)),
        out_shape=jax.ShapeDtypeStruct((n, m), F32),
        compiler_params=_cparams("parallel"),
        name="matmul_t",
    )(wt, x)


def _head(a, h):
    return a[:, h * HEAD_DIM:(h + 1) * HEAD_DIM]


def _hgrn_kernel(q_ref, f_ref, i_ref, g_ref, lbp_ref, gn_ref, ones_ref, mask_ref, s0_ref, y_ref, s_out_ref,
                 st_ref, phi_ref, plo_ref, *, nc):
    c = pl.program_id(1)
    L = CHUNK

    @pl.when(c == 0)
    def _():
        st_ref[...] = jnp.zeros_like(st_ref)
        for h in range(N_HEADS):
            st_ref[h * HEAD_DIM:(h + 1) * HEAD_DIM, h * HEAD_DIM:(h + 1) * HEAD_DIM] = s0_ref[0, h].T

    ones_bd = ones_ref[...]
    masks = _head_masks()
    log_lb = lbp_ref[0:1, :]
    log1m_lb = lbp_ref[1:2, :]
    one_m_lb = lbp_ref[2:3, :]
    zf = f_ref[0]
    u = log_lb
    w = log1m_lb + _log_sigmoid(zf)
    logf = jnp.maximum(u, w) + jnp.log1p(jnp.exp(-jnp.abs(u - w)))
    k = one_m_lb * _sigmoid(-zf)
    q = _silu(q_ref[0])
    v = i_ref[0]

    b = _cumsum_rows(logf)
    b_last = b[L - 1:L]
    qe = q * jnp.exp(b)
    kdec = k * jnp.exp(b_last - b)
    sub_row = lax.broadcasted_iota(jnp.int32, (SUB, WIDTH), 0)

    for s in range(L):
        r0 = (s // SUB) * SUB
        p = q[r0:r0 + SUB] * k[s:s + 1] * jnp.exp(jnp.minimum(b[r0:r0 + SUB] - b[s:s + 1], 0.0))
        hi, lo = _split2(jnp.where(sub_row >= s - r0, p, 0.0))
        phi_ref[s * SUB:(s + 1) * SUB, :] = hi
        plo_ref[s * SUB:(s + 1) * SUB, :] = lo
    att = _dot(phi_ref[...], ones_bd) + _dot(plo_ref[...], ones_bd)

    o_blocks = []
    for blk in range(L // SUB):
        r0 = blk * SUB
        acc = jnp.zeros((SUB, WIDTH), F32)
        for j in range(SUB):
            s = r0 + j
            acc = acc + att[s * SUB:(s + 1) * SUB] * v[s:s + 1]
        if blk > 0:
            ref_row = b[r0 - 1:r0]
            q_s = q[r0:r0 + SUB] * jnp.exp(b[r0:r0 + SUB] - ref_row)
            k_s = k[:r0] * jnp.exp(ref_row - b[:r0])
            scores = _dot(q_s, _stack_heads(k_s, masks), NT)
            acc = acc + _dot(scores, _stack_heads(v[:r0], masks))
        o_blocks.append(acc)

    st = st_ref[...]
    o = jnp.concatenate(o_blocks, axis=0) + _dot(qe, st, NT)
    ms = _head_sums(o * o, ones_bd) * (1.0 / HEAD_DIM)
    y_ref[0] = (o * lax.rsqrt(ms + HEAD_NORM_EPS) * gn_ref[...] * _silu(g_ref[0])).astype(y_ref.dtype)
    st_ref[...] = st * jnp.exp(b_last) + _dot(v, kdec, TN) * mask_ref[...]

    @pl.when(c == nc - 1)
    def _():
        for h in range(N_HEADS):
            s_out_ref[0, h] = st_ref[h * HEAD_DIM:(h + 1) * HEAD_DIM, h * HEAD_DIM:(h + 1) * HEAD_DIM].T


def _hgrn(h3, lbp, gn, mask_bd, s0):
    bsz, t, _ = h3.shape
    nc = t // CHUNK
    cb = COL_A // WIDTH

    def col(j):
        return pl.BlockSpec((1, CHUNK, WIDTH), lambda b, c, j=j: (b, c, cb + j))

    square = pl.BlockSpec((WIDTH, WIDTH), lambda b, c: (0, 0))
    state = pl.BlockSpec((1, N_HEADS, HEAD_DIM, HEAD_DIM), lambda b, c: (b, 0, 0, 0))
    return pl.pallas_call(
        functools.partial(_hgrn_kernel, nc=nc),
        grid=(bsz, nc),
        in_specs=[col(0), col(1), col(2), col(3),
                  pl.BlockSpec((8, WIDTH), lambda b, c: (0, 0)),
                  pl.BlockSpec((1, WIDTH), lambda b, c: (0, 0)),
                  square, square, state],
        out_specs=[pl.BlockSpec((1, CHUNK, WIDTH), lambda b, c: (b, c, 0)), state],
        out_shape=[jax.ShapeDtypeStruct((bsz, t, WIDTH), BF16),
                   jax.ShapeDtypeStruct((bsz, N_HEADS, HEAD_DIM, HEAD_DIM), F32)],
        scratch_shapes=[pltpu.VMEM((WIDTH, WIDTH), F32),
                        pltpu.VMEM((CHUNK * SUB, WIDTH), BF16),
                        pltpu.VMEM((CHUNK * SUB, WIDTH), BF16)],
        compiler_params=_cparams("parallel", "arbitrary"),
        name="hgrn2",
    )(h3, h3, h3, h3, lbp, gn, mask_bd.astype(BF16), mask_bd, s0)


def _mlstm_kernel(u_ref, v_ref, o_ref, gcol_ref, grow_ref, bcol_ref, brow_ref, cw_ref, vec_ref, wqk_ref,
                  c0_ref, n0_ref, m0_ref, conv0_ref,
                  y_ref, c_out_ref, n_out_ref, m_out_ref, conv_out_ref,
                  c_ref, n_ref, m_ref, carry_ref, *, nc):
    c = pl.program_id(1)
    L = CHUNK

    @pl.when(c == 0)
    def _():
        c_ref[...] = c0_ref[0]
        n_ref[...] = n0_ref[0]
        m_ref[...] = m0_ref[0]
        carry_ref[...] = conv0_ref[0]

    u = u_ref[0]
    carry = carry_ref[...]
    row8 = lax.broadcasted_iota(jnp.int32, (8, WIDTH), 0)
    conv = vec_ref[0:1, :] + cw_ref[CONV_W - 1:CONV_W, :] * u
    for d in range(1, CONV_W):
        rolled = pltpu.roll(u, d, 0)
        top = jnp.where(row8 < d, pltpu.roll(carry, d, 0), rolled[0:8])
        conv = conv + cw_ref[CONV_W - 1 - d:CONV_W - d, :] * jnp.concatenate([top, rolled[8:]], axis=0)
    carry_ref[...] = u[L - 8:L]
    cact = _silu(conv)
    qk_all = _dot(cact.astype(BF16), wqk_ref[...])

    row = lax.broadcasted_iota(jnp.int32, (L, L), 0)
    col = lax.broadcasted_iota(jnp.int32, (L, L), 1)
    lower = row >= col
    raw_c = gcol_ref[0] + brow_ref[...]
    lane8 = lax.broadcasted_iota(jnp.int32, raw_c.shape, 1)
    gcol = jnp.where(lane8 < N_HEADS, raw_c, _log_sigmoid(raw_c))
    raw_r = grow_ref[0] + bcol_ref[...]
    sub8 = lax.broadcasted_iota(jnp.int32, raw_r.shape, 0)
    grow = jnp.where(sub8 < N_HEADS, raw_r, _log_sigmoid(raw_r))
    f_col = jnp.dot(lower.astype(F32), gcol, preferred_element_type=F32, precision=HIGHEST)
    f_row = jnp.dot(grow, (row <= col).astype(F32), preferred_element_type=F32, precision=HIGHEST)
    v_all = v_ref[0]
    o_all = _sigmoid(o_ref[0])
    gn = vec_ref[1:2, :]
    skip = vec_ref[2:3, :]

    for h in range(N_HEADS):
        fc = f_col[:, N_HEADS + h:N_HEADS + h + 1]
        fr = f_row[N_HEADS + h:N_HEADS + h + 1, :]
        li_r = grow[h:h + 1, :]
        li_c = gcol[:, h:h + 1]
        m_prev = m_ref[0:1, h:h + 1]
        q_h = _head(qk_all, h)
        k_h = _head(qk_all, N_HEADS + h)
        v_h = _head(v_all, h)
        cm = c_ref[h]
        n_h = n_ref[h:h + 1, :]

        g = fc + m_prev
        d = jnp.where(lower, fc - fr + li_r, -jnp.inf)
        mt = jnp.maximum(g, jnp.max(d, axis=1, keepdims=True))
        wd = jnp.exp(d - mt)
        wg = jnp.exp(g - mt)
        qk = _dot(q_h, k_h, NT) * wd
        num = wg * _dot(q_h, cm) + _dot(qk, v_h)
        den = wg * jnp.sum(q_h * n_h, axis=1, keepdims=True) + jnp.sum(qk, axis=1, keepdims=True)
        hh = num / jnp.maximum(jnp.abs(den), jnp.exp(-mt))

        mt_last = mt[L - 1:L]
        wl = jnp.exp(fc[L - 1:L] - fc + li_c - mt_last)
        wgl = wg[L - 1:L]
        kw = k_h * wl
        c_ref[h] = wgl * cm + _dot(kw, v_h, TN)
        n_ref[h:h + 1, :] = wgl * n_h + jnp.sum(kw, axis=0, keepdims=True)
        m_ref[0:1, h:h + 1] = mt_last

        z = _head(o_all, h) * hh
        mu = jnp.mean(z, axis=1, keepdims=True)
        zc = z - mu
        var = jnp.mean(zc * zc, axis=1, keepdims=True)
        y_h = zc * lax.rsqrt(var + HEAD_NORM_EPS) * _head(gn, h) + _head(skip, h) * _head(cact, h)
        y_ref[0, :, h * HEAD_DIM:(h + 1) * HEAD_DIM] = y_h.astype(y_ref.dtype)

    @pl.when(c == nc - 1)
    def _():
        c_out_ref[0] = c_ref[...]
        n_out_ref[0] = n_ref[...]
        m_out_ref[0] = m_ref[...]
        conv_out_ref[0] = carry_ref[...]


def _mlstm(h3, gcol, grow, b_col, b_row, conv_w, vecs, w_qk, c0, n0, m0, conv0):
    bsz, t, _ = h3.shape
    nc = t // CHUNK
    cb = COL_B // WIDTH
    state4 = pl.BlockSpec((1, N_HEADS, HEAD_DIM, HEAD_DIM), lambda b, c: (b, 0, 0, 0))
    state_n = pl.BlockSpec((1, N_HEADS, HEAD_DIM), lambda b, c: (b, 0, 0))
    state_m = pl.BlockSpec((1, 1, N_HEADS), lambda b, c: (b, 0, 0))
    state_conv = pl.BlockSpec((1, 8, WIDTH), lambda b, c: (b, 0, 0))

    def const(shape):
        return pl.BlockSpec(shape, lambda b, c: tuple(0 for _ in shape))

    def col(j):
        return pl.BlockSpec((1, CHUNK, WIDTH), lambda b, c, j=j: (b, c, cb + j))

    return pl.pallas_call(
        functools.partial(_mlstm_kernel, nc=nc),
        grid=(bsz, nc),
        in_specs=[col(0), col(1), col(2),
                  pl.BlockSpec((1, CHUNK, 2 * N_HEADS), lambda b, c: (b, c, 0)),
                  pl.BlockSpec((1, 2 * N_HEADS, CHUNK), lambda b, c, nc=nc: (b * nc + c, 0, 0)),
                  const((2 * N_HEADS, 1)), const((1, 2 * N_HEADS)), const((8, WIDTH)), const((8, WIDTH)),
                  const((WIDTH, 2 * WIDTH)),
                  state4, state_n, state_m, state_conv],
        out_specs=[pl.BlockSpec((1, CHUNK, WIDTH), lambda b, c: (b, c, 0)), state4, state_n, state_m, state_conv],
        out_shape=[jax.ShapeDtypeStruct((bsz, t, WIDTH), BF16),
                   jax.ShapeDtypeStruct((bsz, N_HEADS, HEAD_DIM, HEAD_DIM), F32),
                   jax.ShapeDtypeStruct((bsz, N_HEADS, HEAD_DIM), F32),
                   jax.ShapeDtypeStruct((bsz, 1, N_HEADS), F32),
                   jax.ShapeDtypeStruct((bsz, 8, WIDTH), F32)],
        scratch_shapes=[pltpu.VMEM((N_HEADS, HEAD_DIM, HEAD_DIM), F32),
                        pltpu.VMEM((N_HEADS, HEAD_DIM), F32),
                        pltpu.VMEM((1, N_HEADS), F32),
                        pltpu.VMEM((8, WIDTH), F32)],
        compiler_params=_cparams("parallel", "arbitrary"),
        name="mlstm",
    )(h3, h3, h3, gcol, grow, b_col, b_row, conv_w, vecs, w_qk, c0, n0, m0, conv0)


def _swa_kernel(q_ref, k0_ref, k1_ref, k2_ref, v0_ref, v1_ref, v2_ref, bias_ref, sink_ref, y_ref, *, prev_valid):
    c = pl.program_id(1)
    L = CHUNK
    span = WINDOW + L
    q = q_ref[0]
    kcat = jnp.concatenate([k0_ref[0], k1_ref[0], k2_ref[0]], axis=0).astype(BF16)
    vcat = jnp.concatenate([v0_ref[0], v1_ref[0], v2_ref[0]], axis=0).astype(BF16)
    key_pos = lax.broadcasted_iota(jnp.int32, (L, span), 1) + c * L
    for h in range(N_HEADS):
        g = h // (N_HEADS // C_KV_HEADS)
        q_h = _head(q, h).astype(BF16)
        s = _dot(q_h, _head(kcat, g), NT)
        s = s * HEAD_DIM ** -0.5 + bias_ref[h]
        if not prev_valid:
            s = jnp.where(key_pos >= WINDOW, s, -jnp.inf)
        sink = sink_ref[0:1, h:h + 1]
        m = jnp.maximum(jnp.max(s, axis=1, keepdims=True), sink)
        p = jnp.exp(s - m)
        probs = p / (jnp.sum(p, axis=1, keepdims=True) + jnp.exp(sink - m))
        o = _dot(probs.astype(BF16), _head(vcat, g))
        y_ref[0, :, h * HEAD_DIM:(h + 1) * HEAD_DIM] = o.astype(y_ref.dtype)


def _swa(h3, kf, vf, bias, sinks, prev_valid):
    bsz, t, _ = h3.shape
    nc = t // CHUNK
    qb = COL_C // WIDTH

    def kv(j):
        return pl.BlockSpec((1, CHUNK, KV_WIDTH), lambda b, c, j=j: (b, c + j, 0))

    return pl.pallas_call(
        functools.partial(_swa_kernel, prev_valid=prev_valid),
        grid=(bsz, nc),
        in_specs=[pl.BlockSpec((1, CHUNK, WIDTH), lambda b, c: (b, c, qb)),
                  kv(0), kv(1), kv(2), kv(0), kv(1), kv(2),
                  pl.BlockSpec((N_HEADS, CHUNK, WINDOW + CHUNK), lambda b, c: (0, 0, 0)),
                  pl.BlockSpec((1, N_HEADS), lambda b, c: (0, 0))],
        out_specs=pl.BlockSpec((1, CHUNK, WIDTH), lambda b, c: (b, c, 0)),
        out_shape=jax.ShapeDtypeStruct((bsz, t, WIDTH), BF16),
        compiler_params=_cparams("parallel", "parallel"),
        name="swa",
    )(h3, kf, kf, kf, vf, vf, vf, bias, sinks)


def _rwkv_kernel(pd_ref, mu_ref, vec_ref, wwa_ref, gup_ref, ones_ref, mask_ref, s0_ref, sh0_ref,
                 y_ref, s_out_ref, sh_out_ref, st_ref, carry_ref, *, nc):
    c = pl.program_id(1)
    L = CHUNK

    @pl.when(c == 0)
    def _():
        st_ref[...] = jnp.zeros_like(st_ref)
        for h in range(N_HEADS):
            st_ref[h * HEAD_DIM:(h + 1) * HEAD_DIM, h * HEAD_DIM:(h + 1) * HEAD_DIM] = s0_ref[0, h]
        carry_ref[7:8, :] = sh0_ref[0]

    ones_bd = ones_ref[...]
    masks = _head_masks()

    pd = pd_ref[0]
    rolled = pltpu.roll(pd, 1, 0)
    row8 = lax.broadcasted_iota(jnp.int32, (8, D_COLS), 0)
    top = jnp.where(row8 == 0, carry_ref[7:8, :], rolled[0:8])
    pd_prev = jnp.concatenate([top, rolled[8:]], axis=0)
    carry_ref[...] = pd[L - 8:L]
    mixed = pd + (pd_prev - pd) * mu_ref[...]
    r = mixed[:, 0:WIDTH]
    k_raw = mixed[:, WIDTH:2 * WIDTH]
    v = mixed[:, 2 * WIDTH:3 * WIDTH]
    wa = mixed[:, 3 * WIDTH:3 * WIDTH + 2 * D_DECAY_LORA]
    g_in = mixed[:, 3 * WIDTH + 2 * D_DECAY_LORA:D_COLS]
    lane_wa = lax.broadcasted_iota(jnp.int32, wa.shape, 1)
    wa_act = jnp.where(lane_wa < D_DECAY_LORA, jnp.tanh(wa), wa)
    lora = _dot(wa_act.astype(BF16), wwa_ref[...])
    g_d = _dot(_sigmoid(g_in).astype(BF16), gup_ref[...])
    z = -(vec_ref[0:1, :] + lora[:, 0:WIDTH])
    w_log = -(jnp.maximum(z, 0.0) + jnp.log1p(jnp.exp(-jnp.abs(z)))) - 0.5
    lw = -jnp.exp(w_log)
    a_d = _sigmoid(vec_ref[1:2, :] + lora[:, WIDTH:2 * WIDTH])
    kk = k_raw * vec_ref[2:3, :]
    kk = kk * lax.rsqrt(jnp.maximum(_head_sums(kk * kk, ones_bd), 1e-24))
    k = k_raw * (1.0 + (a_d - 1.0) * vec_ref[3:4, :])
    a = -kk
    b = kk * a_d

    cw = _cumsum_rows(lw)
    g_in_c = jnp.exp(cw)
    g_inv = jnp.exp(-cw)
    at = a * jnp.exp(cw - lw)
    rt = r * g_in_c
    bt = b * g_inv
    kt = k * g_inv
    x = jnp.concatenate([at, rt], axis=0)
    bt_st = _stack_parts(bt, masks)
    kt_st = _stack_parts(kt, masks)
    gram_b = _mm2(x, bt_st, NT)
    gram_k = _mm2(x, kt_st, NT)

    t_idx = lax.broadcasted_iota(jnp.int32, (L, WIDTH), 0)
    i_idx = lax.broadcasted_iota(jnp.int32, (L, WIDTH), 1) % L
    strict = i_idx < t_idx
    incl = i_idx <= t_idx
    same_blk = (i_idx // SUB) == (t_idx // SUB)
    eye = (i_idx == t_idx).astype(F32)
    n_all = jnp.where(strict, gram_b[0:L], 0.0)
    m_all = jnp.where(strict, gram_k[0:L], 0.0)
    rb_all = jnp.where(incl, gram_b[L:2 * L], 0.0)
    rk_all = jnp.where(incl, gram_k[L:2 * L], 0.0)
    n_d = jnp.where(same_blk, n_all, 0.0)
    n_off = jnp.where(same_blk, 0.0, n_all)

    x2 = _mm2(n_d, _stack_parts(n_d, masks))
    x2_st = _stack_parts(x2, masks)
    x4 = _mm2(x2, x2_st)
    x4_st = _stack_parts(x4, masks)
    x8_st = _stack_parts(_mm2(x4, x4_st), masks)
    t_d = eye + n_d
    t_d = t_d + _mm2(t_d, x2_st)
    t_d = t_d + _mm2(t_d, x4_st)
    t_d = t_d + _mm2(t_d, x8_st)
    n1 = _mm2(t_d, _stack_parts(n_off, masks))
    n1_st = _stack_parts(n1, masks)
    n2 = _mm2(n1, n1_st)
    a2 = eye + n1 + n2 + _mm2(n2, n1_st)
    t_full = _mm2(a2, _stack_parts(t_d, masks))

    st = st_ref[...]
    u = _mm2(x, _split2(st), NT)
    v_st = _stack_parts(v, masks)
    rhs = u[0:L] + _mm2(m_all, v_st)
    sa = _mm2(t_full, _stack_parts(rhs, masks))
    y = u[L:2 * L] + _mm2(rb_all, _stack_parts(sa, masks)) + _mm2(rk_all, v_st)
    sv_hi, sv_lo = _split2(jnp.concatenate([sa, v], axis=0))
    bk_hi, bk_lo = _split2(jnp.concatenate([bt, kt], axis=0))
    upd = _dot(sv_hi, bk_hi, TN) + _dot(sv_lo, bk_hi, TN) + _dot(sv_hi, bk_lo, TN)
    st_ref[...] = (st + upd * mask_ref[...]) * g_in_c[L - 1:L]

    mu = _head_sums(y, ones_bd) * (1.0 / HEAD_DIM)
    yc = y - mu
    var = _head_sums(yc * yc, ones_bd) * (1.0 / HEAD_DIM)
    o = yc * lax.rsqrt(var + RWKV_GN_EPS) * vec_ref[5:6, :] + vec_ref[6:7, :]
    o = o + _head_sums(r * k * vec_ref[4:5, :], ones_bd) * v
    y_ref[0] = (o * g_d).astype(y_ref.dtype)

    @pl.when(c == nc - 1)
    def _():
        for h in range(N_HEADS):
            s_out_ref[0, h] = st_ref[h * HEAD_DIM:(h + 1) * HEAD_DIM, h * HEAD_DIM:(h + 1) * HEAD_DIM]
        sh_out_ref[0] = pd[L - 1:L]


def _rwkv(h3, mu, vecs, w_wa, g_up, mask_bd, s0, shift0):
    bsz, t, _ = h3.shape
    nc = t // CHUNK
    state = pl.BlockSpec((1, N_HEADS, HEAD_DIM, HEAD_DIM), lambda g, i: (g, 0, 0, 0))
    shift = pl.BlockSpec((1, 1, D_COLS), lambda g, i: (g, 0, 0))

    def const(shape):
        return pl.BlockSpec(shape, lambda g, i: tuple(0 for _ in shape))

    return pl.pallas_call(
        functools.partial(_rwkv_kernel, nc=nc),
        grid=(bsz, nc),
        in_specs=[pl.BlockSpec((1, CHUNK, D_COLS), lambda g, i: (g, i, COL_D // D_COLS)),
                  const((1, D_COLS)), const((8, WIDTH)), const((2 * D_DECAY_LORA, 2 * WIDTH)),
                  const((D_GATE_LORA, WIDTH)), const((WIDTH, WIDTH)), const((WIDTH, WIDTH)), state, shift],
        out_specs=[pl.BlockSpec((1, CHUNK, WIDTH), lambda g, i: (g, i, 0)), state, shift],
        out_shape=[jax.ShapeDtypeStruct((bsz, t, WIDTH), BF16),
                   jax.ShapeDtypeStruct((bsz, N_HEADS, HEAD_DIM, HEAD_DIM), F32),
                   jax.ShapeDtypeStruct((bsz, 1, D_COLS), F32)],
        scratch_shapes=[pltpu.VMEM((WIDTH, WIDTH), F32), pltpu.VMEM((8, D_COLS), F32)],
        compiler_params=_cparams("parallel", "arbitrary"),
        name="rwkv7",
    )(h3, mu, vecs, w_wa, g_up, mask_bd.astype(BF16), mask_bd, s0, shift0)


def _merge_kernel(ya_ref, yb_ref, yc_ref, yd_ref, g0_ref, g1_ref, g2_ref, g3_ref, wbr_ref, wo_ref, x_ref,
                  lng_ref, lnb_ref, xo_ref, xob_ref):
    ys = (ya_ref, yb_ref, yc_ref, yd_ref)
    gs = (g0_ref, g1_ref, g2_ref, g3_ref)
    merged = None
    for n in range(N_BRANCH):
        term = _sigmoid(gs[n][...]) * _dot(ys[n][...], wbr_ref[n])
        merged = term if merged is None else merged + term
    out = _dot(merged.astype(BF16), wo_ref[...])
    xn = _layernorm_rows(ALPHA * x_ref[...] + out, lng_ref[...], lnb_ref[...])
    xo_ref[...] = xn
    xob_ref[...] = xn.astype(BF16)


def _merge(ys, h2, wbr, wo, x2, lng, lnb, *, tm=256):
    m = x2.shape[0]
    tm = min(tm, m)
    ysp = pl.BlockSpec((tm, WIDTH), lambda i: (i, 0))
    row = pl.BlockSpec((tm, D_MODEL), lambda i: (i, 0))
    vec = pl.BlockSpec((1, D_MODEL), lambda i: (0, 0))

    def gate(n):
        return pl.BlockSpec((tm, D_MODEL), lambda i, n=n: (i, n))

    return pl.pallas_call(
        _merge_kernel,
        grid=(m // tm,),
        in_specs=[ysp, ysp, ysp, ysp, gate(0), gate(1), gate(2), gate(3),
                  pl.BlockSpec((N_BRANCH, WIDTH, D_MODEL), lambda i: (0, 0, 0)),
                  pl.BlockSpec((D_MODEL, D_MODEL), lambda i: (0, 0)),
                  row, vec, vec],
        out_specs=[row, row],
        out_shape=[jax.ShapeDtypeStruct((m, D_MODEL), F32), jax.ShapeDtypeStruct((m, D_MODEL), BF16)],
        compiler_params=_cparams("parallel"),
        name="merge",
    )(*ys, h2, h2, h2, h2, wbr, wo, x2, lng, lnb)


def _ffn_kernel(*refs, n_steps, gated):
    if gated:
        xb_ref, x_ref, gates_ref, wg_ref, wu_ref, wd_ref, lng_ref, lnb_ref, xo_ref, xob_ref, acc_ref = refs
    else:
        xb_ref, x_ref, wg_ref, wu_ref, wd_ref, lng_ref, lnb_ref, xo_ref, xob_ref, acc_ref = refs
    j = pl.program_id(1)

    @pl.when(j == 0)
    def _():
        acc_ref[...] = jnp.zeros_like(acc_ref)

    xb = xb_ref[...]
    hg = _dot(xb, wg_ref[0])
    hu = _dot(xb, wu_ref[0])
    part = _dot((_silu(hg) * hu).astype(BF16), wd_ref[0])
    if gated:
        gates = gates_ref[...]
        lane = lax.broadcasted_iota(jnp.int32, gates.shape, 1)
        part = jnp.sum(jnp.where(lane == j, gates, 0.0), axis=1, keepdims=True) * part
    acc_ref[...] += part

    @pl.when(j == n_steps - 1)
    def _():
        xn = _layernorm_rows(ALPHA * x_ref[...] + acc_ref[...], lng_ref[...], lnb_ref[...])
        xo_ref[...] = xn
        xob_ref[...] = xn.astype(BF16)


def _ffn(xb, x2, wg, wu, wd, lng, lnb, gates=None, *, tm=512):
    m = x2.shape[0]
    tm = min(tm, m)
    n_steps, _, tf = wg.shape
    row = pl.BlockSpec((tm, D_MODEL), lambda i, j: (i, 0))
    vec = pl.BlockSpec((1, D_MODEL), lambda i, j: (0, 0))
    w_in_spec = pl.BlockSpec((1, D_MODEL, tf), lambda i, j: (j, 0, 0))
    w_out_spec = pl.BlockSpec((1, tf, D_MODEL), lambda i, j: (j, 0, 0))
    in_specs = [row, row]
    args = [xb, x2]
    if gates is not None:
        in_specs.append(pl.BlockSpec((tm, N_EXPERTS), lambda i, j: (i, 0)))
        args.append(gates)
    in_specs += [w_in_spec, w_in_spec, w_out_spec, vec, vec]
    args += [wg, wu, wd, lng, lnb]
    return pl.pallas_call(
        functools.partial(_ffn_kernel, n_steps=n_steps, gated=gates is not None),
        grid=(m // tm, n_steps),
        in_specs=in_specs,
        out_specs=[row, row],
        out_shape=[jax.ShapeDtypeStruct((m, D_MODEL), F32), jax.ShapeDtypeStruct((m, D_MODEL), BF16)],
        scratch_shapes=[pltpu.VMEM((tm, D_MODEL), F32)],
        compiler_params=_cparams("parallel", "arbitrary"),
        name="moe" if gates is not None else "ffn",
    )(*args)


STATE_NAMES = ('swa_k', 'swa_v', 'hgrn', 'mlstm_c', 'mlstm_n', 'mlstm_m', 'mlstm_conv', 'rwkv', 'rwkv_shift')

_D_ORIG = (('r', WIDTH), ('w', D_DECAY_LORA), ('k', WIDTH), ('v', WIDTH), ('a', D_AAA_LORA), ('g', D_GATE_LORA))
_D_KERNEL = ('r', 'k', 'v', 'w', 'a', 'g')


def _d_pieces(arr):
    out, off = {}, 0
    for name, size in _D_ORIG:
        out[name] = arr[..., off:off + size]
        off += size
    return out


def _d_to_kernel_order(arr):
    p = _d_pieces(arr)
    return jnp.concatenate([p[n] for n in _D_KERNEL], axis=-1)


def _d_to_original_order(arr):
    sizes = dict(_D_ORIG)
    p, off = {}, 0
    for name in _D_KERNEL:
        p[name] = arr[..., off:off + sizes[name]]
        off += sizes[name]
    return jnp.concatenate([p[n] for n, _ in _D_ORIG], axis=-1)


def _rel_bucket(rel):
    half = NUM_BUCKETS // 2
    exact = half // 2
    dist = jnp.abs(rel)
    far = exact + (jnp.log(jnp.maximum(dist, 1).astype(F32) / exact)
                   / math.log(MAX_DISTANCE / exact) * (half - exact)).astype(jnp.int32)
    far = jnp.minimum(far, half - 1)
    return jnp.where(rel > 0, half, 0) + jnp.where(dist < exact, dist, far)


def _lower_bounds(lb_raw):
    sm = jax.nn.softmax(lb_raw.astype(F32), axis=0)
    lb = jnp.concatenate([jnp.zeros_like(sm[:1]), jnp.cumsum(sm[1:], axis=0)[:-1]], axis=0)
    return jnp.clip(lb, 0.0, LB_CEIL)


def _block_diag(blocks):
    rows = sum(b.shape[0] for b in blocks)
    cols = sum(b.shape[1] for b in blocks)
    out = jnp.zeros((rows, cols), blocks[0].dtype)
    r = c = 0
    for b in blocks:
        out = out.at[r:r + b.shape[0], c:c + b.shape[1]].set(b)
        r += b.shape[0]
        c += b.shape[1]
    return out


def _pad_rows(a, rows):
    return jnp.concatenate([a, jnp.zeros((rows - a.shape[0],) + a.shape[1:], a.dtype)], axis=0)


def _mixer(x3, xb2, st, prev_valid, lp, l):
    bsz, t, _ = x3.shape
    m = bsz * t
    x2 = x3.reshape(m, D_MODEL)
    h2 = _matmul(xb2, lp['w_main'][l], tm=1024, tn=256)
    gif = _matmul(x2, lp['w_if'][l], tm=1024, tn=2 * N_HEADS, precision=HIGHEST)
    gif_t = _matmul_t(lp['w_if_t'][l], x2, tm=1024, precision=HIGHEST)
    h3 = h2.reshape(bsz, t, MAIN_COLS)

    y_a, s_hgrn = _hgrn(h3, lp['lbp'][l], lp['gn_a'][l][None], lp['mask_bd'], st['hgrn'])

    grow = jnp.transpose(gif_t.reshape(2 * N_HEADS, m // CHUNK, CHUNK), (1, 0, 2))
    conv0 = jnp.concatenate([jnp.zeros((bsz, 8 - (CONV_W - 1), WIDTH), F32), st['mlstm_conv']], axis=1)
    y_b, mc, mn, mm, conv8 = _mlstm(h3, gif.reshape(bsz, t, 2 * N_HEADS), grow, lp['b_if'][l][:, None],
                                    lp['b_if'][l][None, :], lp['conv_w8'][l], lp['vec_b'][l], lp['w_qk'][l],
                                    st['mlstm_c'], st['mlstm_n'], st['mlstm_m'][:, None, :], conv0)
    conv_state = conv8[:, 8 - (CONV_W - 1):]

    k_off = COL_C + WIDTH
    kf = jnp.concatenate([st['swa_k'].reshape(bsz, WINDOW, KV_WIDTH), h3[..., k_off:k_off + KV_WIDTH]], axis=1)
    vf = jnp.concatenate([st['swa_v'].reshape(bsz, WINDOW, KV_WIDTH), h3[..., k_off + KV_WIDTH:MAIN_COLS]], axis=1)
    y_c = _swa(h3, kf, vf, lp['bias'], lp['sinks'][l][None], prev_valid)
    keep = st['keep']
    k_win = kf[:, -keep:].reshape(bsz, keep, C_KV_HEADS, HEAD_DIM)
    v_win = vf[:, -keep:].reshape(bsz, keep, C_KV_HEADS, HEAD_DIM)

    y_d, s_rwkv, shift = _rwkv(h3, lp['mu_d'][l][None], lp['vec_d'][l], lp['w_wa'][l], lp['g_up_d'][l],
                               lp['mask_bd'], st['rwkv'], _d_to_kernel_order(st['rwkv_shift'])[:, None, :])
    shift_state = _d_to_original_order(shift[:, 0, :])

    ys = [y.reshape(m, WIDTH) for y in (y_a, y_b, y_c, y_d)]
    x1, x1b = _merge(ys, h2, lp['w_br'][l], lp['w_o'][l], x2, lp['ln1_g'][l][None], lp['ln1_b'][l][None])
    new_st = {'swa_k': k_win, 'swa_v': v_win, 'hgrn': s_hgrn, 'mlstm_c': mc, 'mlstm_n': mn,
              'mlstm_m': mm[:, 0, :], 'mlstm_conv': conv_state, 'rwkv': s_rwkv, 'rwkv_shift': shift_state}
    return x1, x1b, new_st


def _trunk(x3, states, prev_valid, keep, lp):
    bsz, t, _ = x3.shape
    m = bsz * t
    xb2 = x3.reshape(m, D_MODEL).astype(BF16)
    collected = {name: [] for name in STATE_NAMES}
    for l in range(DEPTH):
        st = {name: states[name][l] for name in STATE_NAMES}
        st['keep'] = keep
        x1, x1b, new_st = _mixer(x3, xb2, st, prev_valid, lp, l)
        j = l // 2
        ln_g, ln_b = lp['ln2_g'][l][None], lp['ln2_b'][l][None]
        if l % 2 == 0:
            x2, xb2 = _ffn(x1b, x1, lp['ffn_wg'][j], lp['ffn_wu'][j], lp['ffn_wd'][j], ln_g, ln_b)
        else:
            logits = _matmul(x1, lp['router_w'][j], tm=1024, tn=N_EXPERTS, precision=HIGHEST) + lp['router_b'][j]
            top_v, top_i = lax.top_k(logits, TOP_K)
            probs = jax.nn.softmax(top_v, axis=-1)
            gates = jnp.einsum('mk,mke->me', probs, jax.nn.one_hot(top_i, N_EXPERTS, dtype=F32))
            x2, xb2 = _ffn(x1b, x1, lp['exp_wg'][j], lp['exp_wu'][j], lp['exp_wd'][j], ln_g, ln_b, gates)
        x3 = x2.reshape(bsz, t, D_MODEL)
        for name in STATE_NAMES:
            collected[name].append(new_st[name])
    return x3, {name: jnp.stack(collected[name]) for name in STATE_NAMES}


def kernel(x_prompt, x_sample, cache_swa_k, cache_swa_v, state_hgrn, state_mlstm_c, state_mlstm_n, state_mlstm_m, state_mlstm_conv, state_rwkv, state_rwkv_shift, w_in, lb_raw, gn_a, conv_w, conv_b, wq_b, wk_b, b_i, b_f, gn_b, skip_b, sinks, rel_bias, mu_d, w0_d, w_up_d, a0_d, a_up_d, g_up_d, k_k_d, k_a_d, r_k_d, gn_w_d, gn_b_d, w_br, w_o, ln1_g, ln1_b, ln2_g, ln2_b, ffn_w_gate, ffn_w_up, ffn_w_down, router_w, router_b, exp_w_gate, exp_w_up, exp_w_down):
    off_if = 4 * WIDTH + 3 * WIDTH
    off_c = off_if + 2 * N_HEADS
    off_d = off_c + WIDTH + 2 * KV_WIDTH
    off_gate = off_d + D_COLS
    w_main = jnp.concatenate([w_in[:, :, off_gate:], w_in[:, :, :4 * WIDTH],
                              _d_to_kernel_order(w_in[:, :, off_d:off_gate]),
                              w_in[:, :, 4 * WIDTH:off_if], w_in[:, :, off_c:off_d]], axis=-1).astype(BF16)
    w_if = w_in[:, :, off_if:off_c]

    lb = _lower_bounds(lb_raw)
    lb = lb[jnp.minimum(jnp.arange(DEPTH), lb.shape[0] - 1)]
    lbp = jnp.stack([jnp.log(jnp.maximum(lb, LB_FLOOR)), jnp.log1p(-lb), 1.0 - lb], axis=1)
    lbp = jnp.concatenate([lbp, jnp.zeros((DEPTH, 5, WIDTH), F32)], axis=1)

    w_qk = jnp.stack([jnp.concatenate([_block_diag(list(wq_b[l])), _block_diag(list(wk_b[l])) * HEAD_DIM ** -0.5],
                                      axis=1) for l in range(DEPTH)]).astype(BF16)
    w_wa = jnp.stack([_block_diag([w_up_d[l], a_up_d[l]]) for l in range(DEPTH)]).astype(BF16)
    vec_b = jnp.stack([_pad_rows(jnp.stack([conv_b[l], gn_b[l], skip_b[l]]), 8) for l in range(DEPTH)])
    conv_w8 = jnp.stack([_pad_rows(conv_w[l], 8) for l in range(DEPTH)])
    vec_d = jnp.stack([_pad_rows(jnp.stack([w0_d[l], a0_d[l], k_k_d[l], k_a_d[l], r_k_d[l], gn_w_d[l], gn_b_d[l]]), 8)
                       for l in range(DEPTH)])
    head_of = jnp.arange(WIDTH) // HEAD_DIM
    mask_bd = (head_of[:, None] == head_of[None, :]).astype(F32)

    span = WINDOW + CHUNK
    rel = jnp.arange(span)[None, :] - WINDOW - jnp.arange(CHUNK)[:, None]
    bias = jnp.transpose(rel_bias.astype(F32)[_rel_bucket(rel)], (2, 0, 1))

    n_dense = ffn_w_gate.shape[0]
    ff_steps = D_FF // D_FF_EXPERT
    lp = {
        'w_main': w_main, 'w_if': w_if, 'w_if_t': jnp.swapaxes(w_if, 1, 2), 'lbp': lbp, 'gn_a': gn_a,
        'mask_bd': mask_bd, 'conv_w8': conv_w8, 'vec_b': vec_b, 'w_qk': w_qk,
        'b_if': jnp.concatenate([b_i, b_f], axis=-1), 'sinks': sinks, 'bias': bias,
        'mu_d': _d_to_kernel_order(mu_d), 'vec_d': vec_d, 'w_wa': w_wa, 'g_up_d': g_up_d.astype(BF16),
        'w_br': w_br.astype(BF16), 'w_o': w_o.astype(BF16),
        'ln1_g': ln1_g, 'ln1_b': ln1_b, 'ln2_g': ln2_g, 'ln2_b': ln2_b,
        'ffn_wg': jnp.swapaxes(ffn_w_gate.astype(BF16).reshape(n_dense, D_MODEL, ff_steps, D_FF_EXPERT), 1, 2),
        'ffn_wu': jnp.swapaxes(ffn_w_up.astype(BF16).reshape(n_dense, D_MODEL, ff_steps, D_FF_EXPERT), 1, 2),
        'ffn_wd': ffn_w_down.astype(BF16).reshape(n_dense, ff_steps, D_FF_EXPERT, D_MODEL),
        'router_w': router_w, 'router_b': router_b,
        'exp_wg': exp_w_gate.astype(BF16), 'exp_wu': exp_w_up.astype(BF16), 'exp_wd': exp_w_down.astype(BF16),
    }

    sample_states = {
        'swa_k': cache_swa_k, 'swa_v': cache_swa_v, 'hgrn': state_hgrn, 'mlstm_c': state_mlstm_c,
        'mlstm_n': state_mlstm_n, 'mlstm_m': state_mlstm_m, 'mlstm_conv': state_mlstm_conv,
        'rwkv': state_rwkv, 'rwkv_shift': state_rwkv_shift,
    }
    keep = cache_swa_k.shape[2]
    bp = x_prompt.shape[0]
    prompt_states = {}
    for name in STATE_NAMES:
        arr = sample_states[name]
        rows = (WINDOW,) + arr.shape[3:] if name in ('swa_k', 'swa_v') else arr.shape[2:]
        prompt_states[name] = jnp.zeros((DEPTH, bp) + tuple(rows), arr.dtype)

    y_prompt, pst = _trunk(x_prompt, prompt_states, False, keep, lp)
    y_sample, sst = _trunk(x_sample, sample_states, True, keep, lp)
    return (y_prompt, y_sample) + tuple(pst[n] for n in STATE_NAMES) + tuple(sst[n] for n in STATE_NAMES)
```

```python
import functools
import math

import jax
import jax.numpy as jnp
from jax import lax
from jax.experimental import pallas as pl
from jax.experimental.pallas import tpu as pltpu

F32 = jnp.float32
BF16 = jnp.bfloat16
HIGHEST = lax.Precision.HIGHEST

D_MODEL = 1024
DEPTH = 4
CHUNK = 64
HEAD_DIM = 64
N_HEADS = 4
WIDTH = N_HEADS * HEAD_DIM
C_KV_HEADS = 2
KV_WIDTH = C_KV_HEADS * HEAD_DIM
CONV_W = 4
WINDOW = 128
NUM_BUCKETS = 32
MAX_DISTANCE = 128
D_DECAY_LORA = 64
D_AAA_LORA = 64
D_GATE_LORA = 128
D_COLS = 3 * WIDTH + D_DECAY_LORA + D_AAA_LORA + D_GATE_LORA
N_BRANCH = 4
D_FF = 2816
N_EXPERTS = 8
TOP_K = 2
D_FF_EXPERT = 1408
ALPHA = (2 * DEPTH) ** 0.25
LN_EPS = 1e-5
HEAD_NORM_EPS = 1e-5
RWKV_GN_EPS = 64e-5
LB_FLOOR = 1e-30
LB_CEIL = 1.0 - 1e-6

GATE_COLS = N_BRANCH * D_MODEL
COL_A = GATE_COLS
COL_D = COL_A + 4 * WIDTH
COL_B = COL_D + D_COLS
COL_C = COL_B + 3 * WIDTH
MAIN_COLS = COL_C + WIDTH + 2 * KV_WIDTH
SUB = 16
V7X_VMEM_LIMIT = 48 * 1024 * 1024

NN = (((1,), (0,)), ((), ()))
NT = (((1,), (1,)), ((), ()))
TN = (((0,), (0,)), ((), ()))


def _cparams(*sem):
    return pltpu.CompilerParams(dimension_semantics=sem, vmem_limit_bytes=V7X_VMEM_LIMIT)


def _sigmoid(x):
    return 1.0 / (1.0 + jnp.exp(-x))


def _silu(x):
    return x * _sigmoid(x)


def _log_sigmoid(x):
    return jnp.minimum(x, 0.0) - jnp.log1p(jnp.exp(-jnp.abs(x)))


def _layernorm_rows(z, g, b):
    mu = jnp.mean(z, axis=-1, keepdims=True)
    zc = z - mu
    var = jnp.mean(zc * zc, axis=-1, keepdims=True)
    return zc * lax.rsqrt(var + LN_EPS) * g + b


def _split2(x):
    hi = x.astype(BF16)
    lo = (x - hi.astype(F32)).astype(BF16)
    return hi, lo


def _split3(x):
    hi = x.astype(BF16)
    r1 = x - hi.astype(F32)
    mid = r1.astype(BF16)
    lo = (r1 - mid.astype(F32)).astype(BF16)
    return hi, mid, lo


def _dot(a, b, dims=NN):
    return lax.dot_general(a, b, dims, preferred_element_type=F32)


def _mm2(a, b_parts, dims=NN):
    a_hi, a_lo = _split2(a)
    n = a.shape[0]
    both = _dot(jnp.concatenate([a_hi, a_lo], axis=0), b_parts[0], dims)
    return both[:n] + both[n:] + _dot(a_hi, b_parts[1], dims)


def _cumsum_rows(x):
    n = x.shape[0]
    row = lax.broadcasted_iota(jnp.int32, (n, n), 0)
    col = lax.broadcasted_iota(jnp.int32, (n, n), 1)
    tri = (row >= col).astype(BF16)
    hi, mid, lo = _split3(x)
    return _dot(tri, hi) + _dot(tri, mid) + _dot(tri, lo)


def _head_masks():
    lane_head = lax.broadcasted_iota(jnp.int32, (1, WIDTH), 1) // HEAD_DIM
    return [(lane_head == h) for h in range(N_HEADS)]


def _stack_heads(x, masks):
    return jnp.concatenate([jnp.where(mk, x, 0.0) for mk in masks], axis=0)


def _stack_parts(x, masks):
    parts = _split2(x)
    rows = [mk.astype(BF16) for mk in masks]
    return tuple(jnp.concatenate([p * r for r in rows], axis=0) for p in parts)


def _head_sums(x, ones_bd):
    hi, lo = _split2(x)
    return _dot(hi, ones_bd) + _dot(lo, ones_bd)


def _matmul_kernel(x_ref, w_ref, o_ref, *, precision, dims):
    x = x_ref[...]
    w = w_ref[...]
    if precision is None:
        x = x.astype(w.dtype)
    o_ref[...] = lax.dot_general(x, w, dims, preferred_element_type=F32, precision=precision).astype(o_ref.dtype)


def _matmul(x, w, *, tm, tn, out_dtype=F32, precision=None):
    m, k = x.shape
    n = w.shape[1]
    tm = min(tm, m)
    tn = min(tn, n)
    return pl.pallas_call(
        functools.partial(_matmul_kernel, precision=precision, dims=NN),
        grid=(m // tm, n // tn),
        in_specs=[pl.BlockSpec((tm, k), lambda i, j: (i, 0)),
                  pl.BlockSpec((k, tn), lambda i, j: (0, j))],
        out_specs=pl.BlockSpec((tm, tn), lambda i, j: (i, j)),
        out_shape=jax.ShapeDtypeStruct((m, n), out_dtype),
        compiler_params=_cparams("parallel", "parallel"),
        name="matmul",
    )(x, w)


PROJ_TN = 256


def _proj_kernel(x_ref, w_ref, o_ref):
    x = x_ref[...]
    for j in range(MAIN_COLS // PROJ_TN):
        o_ref[:, j * PROJ_TN:(j + 1) * PROJ_TN] = _dot(x, w_ref[:, j * PROJ_TN:(j + 1) * PROJ_TN])


def _proj(xb, w, *, tm=256):
    m, k = xb.shape
    n = w.shape[1]
    tm = min(tm, m)
    return pl.pallas_call(
        _proj_kernel,
        grid=(m // tm,),
        in_specs=[pl.BlockSpec((tm, k), lambda i: (i, 0)),
                  pl.BlockSpec((k, n), lambda i: (0, 0), pipeline_mode=pl.Buffered(1))],
        out_specs=pl.BlockSpec((tm, n), lambda i: (i, 0)),
        out_shape=jax.ShapeDtypeStruct((m, n), F32),
        compiler_params=_cparams("parallel"),
        name="proj",
    )(xb, w)


def _matmul_t(wt, x, *, tm, precision):
    n, k = wt.shape
    m = x.shape[0]
    tm = min(tm, m)
    return pl.pallas_call(
        functools.partial(_matmul_kernel, precision=precision, dims=NT),
        grid=(m // tm,),
        in_specs=[pl.BlockSpec((n, k), lambda i: (0, 0)),
                  pl.BlockSpec((tm, k), lambda i: (i, 0))],
        out_specs=pl.BlockSpec((n, tm), lambda i: (0, i)),
        out_shape=jax.ShapeDtypeStruct((n, m), F32),
        compiler_params=_cparams("parallel"),
        name="matmul_t",
    )(wt, x)


def _head(a, h):
    return a[:, h * HEAD_DIM:(h + 1) * HEAD_DIM]


def _hgrn_kernel(q_ref, f_ref, i_ref, g_ref, lbp_ref, gn_ref, ones_ref, mask_ref, s0_ref, y_ref, s_out_ref,
                 st_ref, phi_ref, plo_ref, *, nc):
    c = pl.program_id(1)
    L = CHUNK

    @pl.when(c == 0)
    def _():
        st_ref[...] = jnp.zeros_like(st_ref)
        for h in range(N_HEADS):
            st_ref[h * HEAD_DIM:(h + 1) * HEAD_DIM, h * HEAD_DIM:(h + 1) * HEAD_DIM] = s0_ref[0, h].T

    ones_bd = ones_ref[...]
    masks = _head_masks()
    log_lb = lbp_ref[0:1, :]
    log1m_lb = lbp_ref[1:2, :]
    one_m_lb = lbp_ref[2:3, :]
    zf = f_ref[0]
    u = log_lb
    w = log1m_lb + _log_sigmoid(zf)
    logf = jnp.maximum(u, w) + jnp.log1p(jnp.exp(-jnp.abs(u - w)))
    k = one_m_lb * _sigmoid(-zf)
    q = _silu(q_ref[0])
    v = i_ref[0]

    b = _cumsum_rows(logf)
    b_last = b[L - 1:L]
    qe = q * jnp.exp(b)
    kdec = k * jnp.exp(b_last - b)
    sub_row = lax.broadcasted_iota(jnp.int32, (SUB, WIDTH), 0)

    for s in range(L):
        r0 = (s // SUB) * SUB
        p = q[r0:r0 + SUB] * k[s:s + 1] * jnp.exp(jnp.minimum(b[r0:r0 + SUB] - b[s:s + 1], 0.0))
        hi, lo = _split2(jnp.where(sub_row >= s - r0, p, 0.0))
        phi_ref[s * SUB:(s + 1) * SUB, :] = hi
        plo_ref[s * SUB:(s + 1) * SUB, :] = lo
    att = _dot(phi_ref[...], ones_bd) + _dot(plo_ref[...], ones_bd)

    o_blocks = []
    for blk in range(L // SUB):
        r0 = blk * SUB
        acc = jnp.zeros((SUB, WIDTH), F32)
        for j in range(SUB):
            s = r0 + j
            acc = acc + att[s * SUB:(s + 1) * SUB] * v[s:s + 1]
        if blk > 0:
            ref_row = b[r0 - 1:r0]
            q_s = q[r0:r0 + SUB] * jnp.exp(b[r0:r0 + SUB] - ref_row)
            k_s = k[:r0] * jnp.exp(ref_row - b[:r0])
            scores = _dot(q_s, _stack_heads(k_s, masks), NT)
            acc = acc + _dot(scores, _stack_heads(v[:r0], masks))
        o_blocks.append(acc)

    st = st_ref[...]
    o = jnp.concatenate(o_blocks, axis=0) + _dot(qe, st, NT)
    ms = _head_sums(o * o, ones_bd) * (1.0 / HEAD_DIM)
    y_ref[0] = (o * lax.rsqrt(ms + HEAD_NORM_EPS) * gn_ref[...] * _silu(g_ref[0])).astype(y_ref.dtype)
    st_ref[...] = st * jnp.exp(b_last) + _dot(v, kdec, TN) * mask_ref[...]

    @pl.when(c == nc - 1)
    def _():
        for h in range(N_HEADS):
            s_out_ref[0, h] = st_ref[h * HEAD_DIM:(h + 1) * HEAD_DIM, h * HEAD_DIM:(h + 1) * HEAD_DIM].T


def _hgrn(h3, lbp, gn, mask_bd, s0):
    bsz, t, _ = h3.shape
    nc = t // CHUNK
    cb = COL_A // WIDTH

    def col(j):
        return pl.BlockSpec((1, CHUNK, WIDTH), lambda b, c, j=j: (b, c, cb + j))

    square = pl.BlockSpec((WIDTH, WIDTH), lambda b, c: (0, 0))
    state = pl.BlockSpec((1, N_HEADS, HEAD_DIM, HEAD_DIM), lambda b, c: (b, 0, 0, 0))
    return pl.pallas_call(
        functools.partial(_hgrn_kernel, nc=nc),
        grid=(bsz, nc),
        in_specs=[col(0), col(1), col(2), col(3),
                  pl.BlockSpec((8, WIDTH), lambda b, c: (0, 0)),
                  pl.BlockSpec((1, WIDTH), lambda b, c: (0, 0)),
                  square, square, state],
        out_specs=[pl.BlockSpec((1, CHUNK, WIDTH), lambda b, c: (b, c, 0)), state],
        out_shape=[jax.ShapeDtypeStruct((bsz, t, WIDTH), BF16),
                   jax.ShapeDtypeStruct((bsz, N_HEADS, HEAD_DIM, HEAD_DIM), F32)],
        scratch_shapes=[pltpu.VMEM((WIDTH, WIDTH), F32),
                        pltpu.VMEM((CHUNK * SUB, WIDTH), BF16),
                        pltpu.VMEM((CHUNK * SUB, WIDTH), BF16)],
        compiler_params=_cparams("parallel", "arbitrary"),
        name="hgrn2",
    )(h3, h3, h3, h3, lbp, gn, mask_bd.astype(BF16), mask_bd, s0)


def _mlstm_kernel(u_ref, v_ref, o_ref, gcol_ref, grow_ref, bcol_ref, brow_ref, cw_ref, vec_ref, wqk_ref,
                  c0_ref, n0_ref, m0_ref, conv0_ref,
                  y_ref, c_out_ref, n_out_ref, m_out_ref, conv_out_ref,
                  c_ref, n_ref, m_ref, carry_ref, *, nc):
    c = pl.program_id(1)
    L = CHUNK

    @pl.when(c == 0)
    def _():
        c_ref[...] = c0_ref[0]
        n_ref[...] = n0_ref[0]
        m_ref[...] = m0_ref[0]
        carry_ref[...] = conv0_ref[0]

    u = u_ref[0]
    carry = carry_ref[...]
    row8 = lax.broadcasted_iota(jnp.int32, (8, WIDTH), 0)
    conv = vec_ref[0:1, :] + cw_ref[CONV_W - 1:CONV_W, :] * u
    for d in range(1, CONV_W):
        rolled = pltpu.roll(u, d, 0)
        top = jnp.where(row8 < d, pltpu.roll(carry, d, 0), rolled[0:8])
        conv = conv + cw_ref[CONV_W - 1 - d:CONV_W - d, :] * jnp.concatenate([top, rolled[8:]], axis=0)
    carry_ref[...] = u[L - 8:L]
    cact = _silu(conv)
    qk_all = _dot(cact.astype(BF16), wqk_ref[...])

    row = lax.broadcasted_iota(jnp.int32, (L, L), 0)
    col = lax.broadcasted_iota(jnp.int32, (L, L), 1)
    lower = row >= col
    raw_c = gcol_ref[0] + brow_ref[...]
    lane8 = lax.broadcasted_iota(jnp.int32, raw_c.shape, 1)
    gcol = jnp.where(lane8 < N_HEADS, raw_c, _log_sigmoid(raw_c))
    raw_r = grow_ref[0] + bcol_ref[...]
    sub8 = lax.broadcasted_iota(jnp.int32, raw_r.shape, 0)
    grow = jnp.where(sub8 < N_HEADS, raw_r, _log_sigmoid(raw_r))
    f_col = jnp.dot(lower.astype(F32), gcol, preferred_element_type=F32, precision=HIGHEST)
    f_row = jnp.dot(grow, (row <= col).astype(F32), preferred_element_type=F32, precision=HIGHEST)
    v_all = v_ref[0]
    o_all = _sigmoid(o_ref[0])
    gn = vec_ref[1:2, :]
    skip = vec_ref[2:3, :]

    for h in range(N_HEADS):
        fc = f_col[:, N_HEADS + h:N_HEADS + h + 1]
        fr = f_row[N_HEADS + h:N_HEADS + h + 1, :]
        li_r = grow[h:h + 1, :]
        li_c = gcol[:, h:h + 1]
        m_prev = m_ref[0:1, h:h + 1]
        q_h = _head(qk_all, h)
        k_h = _head(qk_all, N_HEADS + h)
        v_h = _head(v_all, h)
        cm = c_ref[h]
        n_h = n_ref[h:h + 1, :]

        g = fc + m_prev
        d = jnp.where(lower, fc - fr + li_r, -jnp.inf)
        mt = jnp.maximum(g, jnp.max(d, axis=1, keepdims=True))
        wd = jnp.exp(d - mt)
        wg = jnp.exp(g - mt)
        qk = _dot(q_h, k_h, NT) * wd
        num = wg * _dot(q_h, cm) + _dot(qk, v_h)
        den = wg * jnp.sum(q_h * n_h, axis=1, keepdims=True) + jnp.sum(qk, axis=1, keepdims=True)
        hh = num / jnp.maximum(jnp.abs(den), jnp.exp(-mt))

        mt_last = mt[L - 1:L]
        wl = jnp.exp(fc[L - 1:L] - fc + li_c - mt_last)
        wgl = wg[L - 1:L]
        kw = k_h * wl
        c_ref[h] = wgl * cm + _dot(kw, v_h, TN)
        n_ref[h:h + 1, :] = wgl * n_h + jnp.sum(kw, axis=0, keepdims=True)
        m_ref[0:1, h:h + 1] = mt_last

        z = _head(o_all, h) * hh
        mu = jnp.mean(z, axis=1, keepdims=True)
        zc = z - mu
        var = jnp.mean(zc * zc, axis=1, keepdims=True)
        y_h = zc * lax.rsqrt(var + HEAD_NORM_EPS) * _head(gn, h) + _head(skip, h) * _head(cact, h)
        y_ref[0, :, h * HEAD_DIM:(h + 1) * HEAD_DIM] = y_h.astype(y_ref.dtype)

    @pl.when(c == nc - 1)
    def _():
        c_out_ref[0] = c_ref[...]
        n_out_ref[0] = n_ref[...]
        m_out_ref[0] = m_ref[...]
        conv_out_ref[0] = carry_ref[...]


def _mlstm(h3, gcol, grow, b_col, b_row, conv_w, vecs, w_qk, c0, n0, m0, conv0):
    bsz, t, _ = h3.shape
    nc = t // CHUNK
    cb = COL_B // WIDTH
    state4 = pl.BlockSpec((1, N_HEADS, HEAD_DIM, HEAD_DIM), lambda b, c: (b, 0, 0, 0))
    state_n = pl.BlockSpec((1, N_HEADS, HEAD_DIM), lambda b, c: (b, 0, 0))
    state_m = pl.BlockSpec((1, 1, N_HEADS), lambda b, c: (b, 0, 0))
    state_conv = pl.BlockSpec((1, 8, WIDTH), lambda b, c: (b, 0, 0))

    def const(shape):
        return pl.BlockSpec(shape, lambda b, c: tuple(0 for _ in shape))

    def col(j):
        return pl.BlockSpec((1, CHUNK, WIDTH), lambda b, c, j=j: (b, c, cb + j))

    return pl.pallas_call(
        functools.partial(_mlstm_kernel, nc=nc),
        grid=(bsz, nc),
        in_specs=[col(0), col(1), col(2),
                  pl.BlockSpec((1, CHUNK, 2 * N_HEADS), lambda b, c: (b, c, 0)),
                  pl.BlockSpec((1, 2 * N_HEADS, CHUNK), lambda b, c, nc=nc: (b * nc + c, 0, 0)),
                  const((2 * N_HEADS, 1)), const((1, 2 * N_HEADS)), const((8, WIDTH)), const((8, WIDTH)),
                  const((WIDTH, 2 * WIDTH)),
                  state4, state_n, state_m, state_conv],
        out_specs=[pl.BlockSpec((1, CHUNK, WIDTH), lambda b, c: (b, c, 0)), state4, state_n, state_m, state_conv],
        out_shape=[jax.ShapeDtypeStruct((bsz, t, WIDTH), BF16),
                   jax.ShapeDtypeStruct((bsz, N_HEADS, HEAD_DIM, HEAD_DIM), F32),
                   jax.ShapeDtypeStruct((bsz, N_HEADS, HEAD_DIM), F32),
                   jax.ShapeDtypeStruct((bsz, 1, N_HEADS), F32),
                   jax.ShapeDtypeStruct((bsz, 8, WIDTH), F32)],
        scratch_shapes=[pltpu.VMEM((N_HEADS, HEAD_DIM, HEAD_DIM), F32),
                        pltpu.VMEM((N_HEADS, HEAD_DIM), F32),
                        pltpu.VMEM((1, N_HEADS), F32),
                        pltpu.VMEM((8, WIDTH), F32)],
        compiler_params=_cparams("parallel", "arbitrary"),
        name="mlstm",
    )(h3, h3, h3, gcol, grow, b_col, b_row, conv_w, vecs, w_qk, c0, n0, m0, conv0)


SWA_CHUNKS_PER_STEP = 4


def _swa_kernel(*refs, prev_valid, cps, n_kv):
    q_ref = refs[0]
    k_refs = refs[1:1 + n_kv]
    v_refs = refs[1 + n_kv:1 + 2 * n_kv]
    bias_ref, sink_ref, y_ref, s_ref, p_ref = refs[1 + 2 * n_kv:]
    g = pl.program_id(1)
    L = CHUNK
    span = WINDOW + L
    kcat = jnp.concatenate([r[0] for r in k_refs], axis=0).astype(BF16)
    vcat = jnp.concatenate([r[0] for r in v_refs], axis=0).astype(BF16)
    kv_of = [h // (N_HEADS // C_KV_HEADS) for h in range(N_HEADS)]
    for j in range(cps):
        q = q_ref[0, j * L:(j + 1) * L, :].astype(BF16)
        k_j = kcat[j * L:j * L + span]
        for h in range(N_HEADS):
            s_ref[j * N_HEADS + h] = _dot(_head(q, h), _head(k_j, kv_of[h]), NT)
    for j in range(cps):
        key_pos = lax.broadcasted_iota(jnp.int32, (L, span), 1) + (g * cps + j) * L
        for h in range(N_HEADS):
            s = s_ref[j * N_HEADS + h] * HEAD_DIM ** -0.5 + bias_ref[h]
            if not prev_valid:
                s = jnp.where(key_pos >= WINDOW, s, -jnp.inf)
            sink = sink_ref[0:1, h:h + 1]
            m = jnp.maximum(jnp.max(s, axis=1, keepdims=True), sink)
            p = jnp.exp(s - m)
            inv = 1.0 / (jnp.sum(p, axis=1, keepdims=True) + jnp.exp(sink - m))
            p_ref[j * N_HEADS + h] = (p * inv).astype(BF16)
    for j in range(cps):
        v_j = vcat[j * L:j * L + span]
        o = [_dot(p_ref[j * N_HEADS + h], _head(v_j, kv_of[h])) for h in range(N_HEADS)]
        y_ref[0, j * L:(j + 1) * L, :] = jnp.concatenate(o, axis=1).astype(y_ref.dtype)


def _swa(h3, kf, vf, bias, sinks, prev_valid):
    bsz, t, _ = h3.shape
    nc = t // CHUNK
    cps = min(SWA_CHUNKS_PER_STEP, nc)
    rows = cps * CHUNK
    kv_rows = min(rows, WINDOW)
    n_kv = (WINDOW + rows) // kv_rows
    qb = COL_C // WIDTH
    kv = [pl.BlockSpec((1, kv_rows, KV_WIDTH), lambda b, g, j=j: (b, g * (rows // kv_rows) + j, 0))
          for j in range(n_kv)]

    return pl.pallas_call(
        functools.partial(_swa_kernel, prev_valid=prev_valid, cps=cps, n_kv=n_kv),
        grid=(bsz, nc // cps),
        in_specs=[pl.BlockSpec((1, rows, WIDTH), lambda b, g: (b, g, qb))] + kv + kv + [
                  pl.BlockSpec((N_HEADS, CHUNK, WINDOW + CHUNK), lambda b, g: (0, 0, 0)),
                  pl.BlockSpec((1, N_HEADS), lambda b, g: (0, 0))],
        out_specs=pl.BlockSpec((1, rows, WIDTH), lambda b, g: (b, g, 0)),
        out_shape=jax.ShapeDtypeStruct((bsz, t, WIDTH), BF16),
        scratch_shapes=[pltpu.VMEM((cps * N_HEADS, CHUNK, WINDOW + CHUNK), F32),
                        pltpu.VMEM((cps * N_HEADS, CHUNK, WINDOW + CHUNK), BF16)],
        compiler_params=_cparams("parallel", "parallel"),
        name="swa",
    )(h3, *([kf] * n_kv), *([vf] * n_kv), bias, sinks)


def _rwkv_kernel(pd_ref, mu_ref, vec_ref, wwa_ref, gup_ref, ones_ref, mask_ref, s0_ref, sh0_ref,
                 y_ref, s_out_ref, sh_out_ref, st_ref, carry_ref, *, nc):
    c = pl.program_id(1)
    L = CHUNK

    @pl.when(c == 0)
    def _():
        st_ref[...] = jnp.zeros_like(st_ref)
        for h in range(N_HEADS):
            st_ref[h * HEAD_DIM:(h + 1) * HEAD_DIM, h * HEAD_DIM:(h + 1) * HEAD_DIM] = s0_ref[0, h]
        carry_ref[7:8, :] = sh0_ref[0]

    ones_bd = ones_ref[...]
    masks = _head_masks()

    pd = pd_ref[0]
    rolled = pltpu.roll(pd, 1, 0)
    row8 = lax.broadcasted_iota(jnp.int32, (8, D_COLS), 0)
    top = jnp.where(row8 == 0, carry_ref[7:8, :], rolled[0:8])
    pd_prev = jnp.concatenate([top, rolled[8:]], axis=0)
    carry_ref[...] = pd[L - 8:L]
    mixed = pd + (pd_prev - pd) * mu_ref[...]
    r = mixed[:, 0:WIDTH]
    k_raw = mixed[:, WIDTH:2 * WIDTH]
    v = mixed[:, 2 * WIDTH:3 * WIDTH]
    wa = mixed[:, 3 * WIDTH:3 * WIDTH + 2 * D_DECAY_LORA]
    g_in = mixed[:, 3 * WIDTH + 2 * D_DECAY_LORA:D_COLS]
    lane_wa = lax.broadcasted_iota(jnp.int32, wa.shape, 1)
    wa_act = jnp.where(lane_wa < D_DECAY_LORA, jnp.tanh(wa), wa)
    lora = _dot(wa_act.astype(BF16), wwa_ref[...])
    g_d = _dot(_sigmoid(g_in).astype(BF16), gup_ref[...])
    z = -(vec_ref[0:1, :] + lora[:, 0:WIDTH])
    w_log = -(jnp.maximum(z, 0.0) + jnp.log1p(jnp.exp(-jnp.abs(z)))) - 0.5
    lw = -jnp.exp(w_log)
    a_d = _sigmoid(vec_ref[1:2, :] + lora[:, WIDTH:2 * WIDTH])
    kk = k_raw * vec_ref[2:3, :]
    kk = kk * lax.rsqrt(jnp.maximum(_head_sums(kk * kk, ones_bd), 1e-24))
    k = k_raw * (1.0 + (a_d - 1.0) * vec_ref[3:4, :])
    a = -kk
    b = kk * a_d

    cw = _cumsum_rows(lw)
    g_in_c = jnp.exp(cw)
    g_inv = jnp.exp(-cw)
    at = a * jnp.exp(cw - lw)
    rt = r * g_in_c
    bt = b * g_inv
    kt = k * g_inv
    x = jnp.concatenate([at, rt], axis=0)
    bt_st = _stack_parts(bt, masks)
    kt_st = _stack_parts(kt, masks)
    gram_b = _mm2(x, bt_st, NT)
    gram_k = _mm2(x, kt_st, NT)

    t_idx = lax.broadcasted_iota(jnp.int32, (L, WIDTH), 0)
    i_idx = lax.broadcasted_iota(jnp.int32, (L, WIDTH), 1) % L
    strict = i_idx < t_idx
    incl = i_idx <= t_idx
    same_blk = (i_idx // SUB) == (t_idx // SUB)
    eye = (i_idx == t_idx).astype(F32)
    n_all = jnp.where(strict, gram_b[0:L], 0.0)
    m_all = jnp.where(strict, gram_k[0:L], 0.0)
    rb_all = jnp.where(incl, gram_b[L:2 * L], 0.0)
    rk_all = jnp.where(incl, gram_k[L:2 * L], 0.0)
    n_d = jnp.where(same_blk, n_all, 0.0)
    n_off = jnp.where(same_blk, 0.0, n_all)

    x2 = _mm2(n_d, _stack_parts(n_d, masks))
    x2_st = _stack_parts(x2, masks)
    x4 = _mm2(x2, x2_st)
    x4_st = _stack_parts(x4, masks)
    x8_st = _stack_parts(_mm2(x4, x4_st), masks)
    t_d = eye + n_d
    t_d = t_d + _mm2(t_d, x2_st)
    t_d = t_d + _mm2(t_d, x4_st)
    t_d = t_d + _mm2(t_d, x8_st)
    n1 = _mm2(t_d, _stack_parts(n_off, masks))
    n1_st = _stack_parts(n1, masks)
    n2 = _mm2(n1, n1_st)
    a2 = eye + n1 + n2 + _mm2(n2, n1_st)
    t_full = _mm2(a2, _stack_parts(t_d, masks))

    st = st_ref[...]
    u = _mm2(x, _split2(st), NT)
    v_st = _stack_parts(v, masks)
    rhs = u[0:L] + _mm2(m_all, v_st)
    sa = _mm2(t_full, _stack_parts(rhs, masks))
    y = u[L:2 * L] + _mm2(rb_all, _stack_parts(sa, masks)) + _mm2(rk_all, v_st)
    sv_hi, sv_lo = _split2(jnp.concatenate([sa, v], axis=0))
    bk_hi, bk_lo = _split2(jnp.concatenate([bt, kt], axis=0))
    upd = _dot(sv_hi, bk_hi, TN) + _dot(sv_lo, bk_hi, TN) + _dot(sv_hi, bk_lo, TN)
    st_ref[...] = (st + upd * mask_ref[...]) * g_in_c[L - 1:L]

    mu = _head_sums(y, ones_bd) * (1.0 / HEAD_DIM)
    yc = y - mu
    var = _head_sums(yc * yc, ones_bd) * (1.0 / HEAD_DIM)
    o = yc * lax.rsqrt(var + RWKV_GN_EPS) * vec_ref[5:6, :] + vec_ref[6:7, :]
    o = o + _head_sums(r * k * vec_ref[4:5, :], ones_bd) * v
    y_ref[0] = (o * g_d).astype(y_ref.dtype)

    @pl.when(c == nc - 1)
    def _():
        for h in range(N_HEADS):
            s_out_ref[0, h] = st_ref[h * HEAD_DIM:(h + 1) * HEAD_DIM, h * HEAD_DIM:(h + 1) * HEAD_DIM]
        sh_out_ref[0] = pd[L - 1:L]


def _rwkv(h3, mu, vecs, w_wa, g_up, mask_bd, s0, shift0):
    bsz, t, _ = h3.shape
    nc = t // CHUNK
    state = pl.BlockSpec((1, N_HEADS, HEAD_DIM, HEAD_DIM), lambda g, i: (g, 0, 0, 0))
    shift = pl.BlockSpec((1, 1, D_COLS), lambda g, i: (g, 0, 0))

    def const(shape):
        return pl.BlockSpec(shape, lambda g, i: tuple(0 for _ in shape))

    return pl.pallas_call(
        functools.partial(_rwkv_kernel, nc=nc),
        grid=(bsz, nc),
        in_specs=[pl.BlockSpec((1, CHUNK, D_COLS), lambda g, i: (g, i, COL_D // D_COLS)),
                  const((1, D_COLS)), const((8, WIDTH)), const((2 * D_DECAY_LORA, 2 * WIDTH)),
                  const((D_GATE_LORA, WIDTH)), const((WIDTH, WIDTH)), const((WIDTH, WIDTH)), state, shift],
        out_specs=[pl.BlockSpec((1, CHUNK, WIDTH), lambda g, i: (g, i, 0)), state, shift],
        out_shape=[jax.ShapeDtypeStruct((bsz, t, WIDTH), BF16),
                   jax.ShapeDtypeStruct((bsz, N_HEADS, HEAD_DIM, HEAD_DIM), F32),
                   jax.ShapeDtypeStruct((bsz, 1, D_COLS), F32)],
        scratch_shapes=[pltpu.VMEM((WIDTH, WIDTH), F32), pltpu.VMEM((8, D_COLS), F32)],
        compiler_params=_cparams("parallel", "arbitrary"),
        name="rwkv7",
    )(h3, mu, vecs, w_wa, g_up, mask_bd.astype(BF16), mask_bd, s0, shift0)


def _merge_kernel(ya_ref, yb_ref, yc_ref, yd_ref, g0_ref, g1_ref, g2_ref, g3_ref, wbr_ref, wo_ref, x_ref,
                  lng_ref, lnb_ref, xo_ref, xob_ref):
    ys = (ya_ref, yb_ref, yc_ref, yd_ref)
    gs = (g0_ref, g1_ref, g2_ref, g3_ref)
    merged = None
    for n in range(N_BRANCH):
        term = _sigmoid(gs[n][...]) * _dot(ys[n][...], wbr_ref[n])
        merged = term if merged is None else merged + term
    out = _dot(merged.astype(BF16), wo_ref[...])
    xn = _layernorm_rows(ALPHA * x_ref[...] + out, lng_ref[...], lnb_ref[...])
    xo_ref[...] = xn
    xob_ref[...] = xn.astype(BF16)


def _merge(ys, h2, wbr, wo, x2, lng, lnb, *, tm=256):
    m = x2.shape[0]
    tm = min(tm, m)
    ysp = pl.BlockSpec((tm, WIDTH), lambda i: (i, 0))
    row = pl.BlockSpec((tm, D_MODEL), lambda i: (i, 0))
    vec = pl.BlockSpec((1, D_MODEL), lambda i: (0, 0))

    def gate(n):
        return pl.BlockSpec((tm, D_MODEL), lambda i, n=n: (i, n))

    return pl.pallas_call(
        _merge_kernel,
        grid=(m // tm,),
        in_specs=[ysp, ysp, ysp, ysp, gate(0), gate(1), gate(2), gate(3),
                  pl.BlockSpec((N_BRANCH, WIDTH, D_MODEL), lambda i: (0, 0, 0)),
                  pl.BlockSpec((D_MODEL, D_MODEL), lambda i: (0, 0)),
                  row, vec, vec],
        out_specs=[row, row],
        out_shape=[jax.ShapeDtypeStruct((m, D_MODEL), F32), jax.ShapeDtypeStruct((m, D_MODEL), BF16)],
        compiler_params=_cparams("parallel"),
        name="merge",
    )(*ys, h2, h2, h2, h2, wbr, wo, x2, lng, lnb)


def _ffn_kernel(*refs, n_steps, gated):
    if gated:
        xb_ref, x_ref, gates_ref, wg_ref, wu_ref, wd_ref, lng_ref, lnb_ref, xo_ref, xob_ref, acc_ref = refs
    else:
        xb_ref, x_ref, wg_ref, wu_ref, wd_ref, lng_ref, lnb_ref, xo_ref, xob_ref, acc_ref = refs
    j = pl.program_id(1)

    @pl.when(j == 0)
    def _():
        acc_ref[...] = jnp.zeros_like(acc_ref)

    xb = xb_ref[...]
    hg = _dot(xb, wg_ref[0])
    hu = _dot(xb, wu_ref[0])
    part = _dot((_silu(hg) * hu).astype(BF16), wd_ref[0])
    if gated:
        gates = gates_ref[...]
        lane = lax.broadcasted_iota(jnp.int32, gates.shape, 1)
        part = jnp.sum(jnp.where(lane == j, gates, 0.0), axis=1, keepdims=True) * part
    acc_ref[...] += part

    @pl.when(j == n_steps - 1)
    def _():
        xn = _layernorm_rows(ALPHA * x_ref[...] + acc_ref[...], lng_ref[...], lnb_ref[...])
        xo_ref[...] = xn
        xob_ref[...] = xn.astype(BF16)


def _ffn(xb, x2, wg, wu, wd, lng, lnb, gates=None, *, tm=512):
    m = x2.shape[0]
    tm = min(tm, m)
    n_steps, _, tf = wg.shape
    row = pl.BlockSpec((tm, D_MODEL), lambda i, j: (i, 0))
    vec = pl.BlockSpec((1, D_MODEL), lambda i, j: (0, 0))
    w_in_spec = pl.BlockSpec((1, D_MODEL, tf), lambda i, j: (j, 0, 0))
    w_out_spec = pl.BlockSpec((1, tf, D_MODEL), lambda i, j: (j, 0, 0))
    in_specs = [row, row]
    args = [xb, x2]
    if gates is not None:
        in_specs.append(pl.BlockSpec((tm, N_EXPERTS), lambda i, j: (i, 0)))
        args.append(gates)
    in_specs += [w_in_spec, w_in_spec, w_out_spec, vec, vec]
    args += [wg, wu, wd, lng, lnb]
    return pl.pallas_call(
        functools.partial(_ffn_kernel, n_steps=n_steps, gated=gates is not None),
        grid=(m // tm, n_steps),
        in_specs=in_specs,
        out_specs=[row, row],
        out_shape=[jax.ShapeDtypeStruct((m, D_MODEL), F32), jax.ShapeDtypeStruct((m, D_MODEL), BF16)],
        scratch_shapes=[pltpu.VMEM((tm, D_MODEL), F32)],
        compiler_params=_cparams("parallel", "arbitrary"),
        name="moe" if gates is not None else "ffn",
    )(*args)


STATE_NAMES = ('swa_k', 'swa_v', 'hgrn', 'mlstm_c', 'mlstm_n', 'mlstm_m', 'mlstm_conv', 'rwkv', 'rwkv_shift')

_D_ORIG = (('r', WIDTH), ('w', D_DECAY_LORA), ('k', WIDTH), ('v', WIDTH), ('a', D_AAA_LORA), ('g', D_GATE_LORA))
_D_KERNEL = ('r', 'k', 'v', 'w', 'a', 'g')


def _d_pieces(arr):
    out, off = {}, 0
    for name, size in _D_ORIG:
        out[name] = arr[..., off:off + size]
        off += size
    return out


def _d_to_kernel_order(arr):
    p = _d_pieces(arr)
    return jnp.concatenate([p[n] for n in _D_KERNEL], axis=-1)


def _d_to_original_order(arr):
    sizes = dict(_D_ORIG)
    p, off = {}, 0
    for name in _D_KERNEL:
        p[name] = arr[..., off:off + sizes[name]]
        off += sizes[name]
    return jnp.concatenate([p[n] for n, _ in _D_ORIG], axis=-1)


def _rel_bucket(rel):
    half = NUM_BUCKETS // 2
    exact = half // 2
    dist = jnp.abs(rel)
    far = exact + (jnp.log(jnp.maximum(dist, 1).astype(F32) / exact)
                   / math.log(MAX_DISTANCE / exact) * (half - exact)).astype(jnp.int32)
    far = jnp.minimum(far, half - 1)
    return jnp.where(rel > 0, half, 0) + jnp.where(dist < exact, dist, far)


def _lower_bounds(lb_raw):
    sm = jax.nn.softmax(lb_raw.astype(F32), axis=0)
    lb = jnp.concatenate([jnp.zeros_like(sm[:1]), jnp.cumsum(sm[1:], axis=0)[:-1]], axis=0)
    return jnp.clip(lb, 0.0, LB_CEIL)


def _block_diag(blocks):
    rows = sum(b.shape[0] for b in blocks)
    cols = sum(b.shape[1] for b in blocks)
    out = jnp.zeros((rows, cols), blocks[0].dtype)
    r = c = 0
    for b in blocks:
        out = out.at[r:r + b.shape[0], c:c + b.shape[1]].set(b)
        r += b.shape[0]
        c += b.shape[1]
    return out


def _pad_rows(a, rows):
    return jnp.concatenate([a, jnp.zeros((rows - a.shape[0],) + a.shape[1:], a.dtype)], axis=0)


def _mixer(x3, xb2, st, prev_valid, lp, l):
    bsz, t, _ = x3.shape
    m = bsz * t
    x2 = x3.reshape(m, D_MODEL)
    h2 = _proj(xb2, lp['w_main'][l])
    gif = _matmul(x2, lp['w_if'][l], tm=1024, tn=2 * N_HEADS, precision=HIGHEST)
    gif_t = _matmul_t(lp['w_if_t'][l], x2, tm=1024, precision=HIGHEST)
    h3 = h2.reshape(bsz, t, MAIN_COLS)

    y_a, s_hgrn = _hgrn(h3, lp['lbp'][l], lp['gn_a'][l][None], lp['mask_bd'], st['hgrn'])

    grow = jnp.transpose(gif_t.reshape(2 * N_HEADS, m // CHUNK, CHUNK), (1, 0, 2))
    conv0 = jnp.concatenate([jnp.zeros((bsz, 8 - (CONV_W - 1), WIDTH), F32), st['mlstm_conv']], axis=1)
    y_b, mc, mn, mm, conv8 = _mlstm(h3, gif.reshape(bsz, t, 2 * N_HEADS), grow, lp['b_if'][l][:, None],
                                    lp['b_if'][l][None, :], lp['conv_w8'][l], lp['vec_b'][l], lp['w_qk'][l],
                                    st['mlstm_c'], st['mlstm_n'], st['mlstm_m'][:, None, :], conv0)
    conv_state = conv8[:, 8 - (CONV_W - 1):]

    k_off = COL_C + WIDTH
    kf = jnp.concatenate([st['swa_k'].reshape(bsz, WINDOW, KV_WIDTH), h3[..., k_off:k_off + KV_WIDTH]], axis=1)
    vf = jnp.concatenate([st['swa_v'].reshape(bsz, WINDOW, KV_WIDTH), h3[..., k_off + KV_WIDTH:MAIN_COLS]], axis=1)
    y_c = _swa(h3, kf, vf, lp['bias'], lp['sinks'][l][None], prev_valid)
    keep = st['keep']
    k_win = kf[:, -keep:].reshape(bsz, keep, C_KV_HEADS, HEAD_DIM)
    v_win = vf[:, -keep:].reshape(bsz, keep, C_KV_HEADS, HEAD_DIM)

    y_d, s_rwkv, shift = _rwkv(h3, lp['mu_d'][l][None], lp['vec_d'][l], lp['w_wa'][l], lp['g_up_d'][l],
                               lp['mask_bd'], st['rwkv'], _d_to_kernel_order(st['rwkv_shift'])[:, None, :])
    shift_state = _d_to_original_order(shift[:, 0, :])

    ys = [y.reshape(m, WIDTH) for y in (y_a, y_b, y_c, y_d)]
    x1, x1b = _merge(ys, h2, lp['w_br'][l], lp['w_o'][l], x2, lp['ln1_g'][l][None], lp['ln1_b'][l][None])
    new_st = {'swa_k': k_win, 'swa_v': v_win, 'hgrn': s_hgrn, 'mlstm_c': mc, 'mlstm_n': mn,
              'mlstm_m': mm[:, 0, :], 'mlstm_conv': conv_state, 'rwkv': s_rwkv, 'rwkv_shift': shift_state}
    return x1, x1b, new_st


def _trunk(x3, states, prev_valid, keep, lp):
    bsz, t, _ = x3.shape
    m = bsz * t
    xb2 = x3.reshape(m, D_MODEL).astype(BF16)
    collected = {name: [] for name in STATE_NAMES}
    for l in range(DEPTH):
        st = {name: states[name][l] for name in STATE_NAMES}
        st['keep'] = keep
        x1, x1b, new_st = _mixer(x3, xb2, st, prev_valid, lp, l)
        j = l // 2
        ln_g, ln_b = lp['ln2_g'][l][None], lp['ln2_b'][l][None]
        if l % 2 == 0:
            x2, xb2 = _ffn(x1b, x1, lp['ffn_wg'][j], lp['ffn_wu'][j], lp['ffn_wd'][j], ln_g, ln_b)
        else:
            logits = _matmul(x1, lp['router_w'][j], tm=1024, tn=N_EXPERTS, precision=HIGHEST) + lp['router_b'][j]
            top_v, top_i = lax.top_k(logits, TOP_K)
            probs = jax.nn.softmax(top_v, axis=-1)
            gates = jnp.einsum('mk,mke->me', probs, jax.nn.one_hot(top_i, N_EXPERTS, dtype=F32))
            x2, xb2 = _ffn(x1b, x1, lp['exp_wg'][j], lp['exp_wu'][j], lp['exp_wd'][j], ln_g, ln_b, gates)
        x3 = x2.reshape(bsz, t, D_MODEL)
        for name in STATE_NAMES:
            collected[name].append(new_st[name])
    return x3, {name: jnp.stack(collected[name]) for name in STATE_NAMES}


def kernel(x_prompt, x_sample, cache_swa_k, cache_swa_v, state_hgrn, state_mlstm_c, state_mlstm_n, state_mlstm_m, state_mlstm_conv, state_rwkv, state_rwkv_shift, w_in, lb_raw, gn_a, conv_w, conv_b, wq_b, wk_b, b_i, b_f, gn_b, skip_b, sinks, rel_bias, mu_d, w0_d, w_up_d, a0_d, a_up_d, g_up_d, k_k_d, k_a_d, r_k_d, gn_w_d, gn_b_d, w_br, w_o, ln1_g, ln1_b, ln2_g, ln2_b, ffn_w_gate, ffn_w_up, ffn_w_down, router_w, router_b, exp_w_gate, exp_w_up, exp_w_down):
    off_if = 4 * WIDTH + 3 * WIDTH
    off_c = off_if + 2 * N_HEADS
    off_d = off_c + WIDTH + 2 * KV_WIDTH
    off_gate = off_d + D_COLS
    w_main = jnp.concatenate([w_in[:, :, off_gate:], w_in[:, :, :4 * WIDTH],
                              _d_to_kernel_order(w_in[:, :, off_d:off_gate]),
                              w_in[:, :, 4 * WIDTH:off_if], w_in[:, :, off_c:off_d]], axis=-1).astype(BF16)
    w_if = w_in[:, :, off_if:off_c]

    lb = _lower_bounds(lb_raw)
    lb = lb[jnp.minimum(jnp.arange(DEPTH), lb.shape[0] - 1)]
    lbp = jnp.stack([jnp.log(jnp.maximum(lb, LB_FLOOR)), jnp.log1p(-lb), 1.0 - lb], axis=1)
    lbp = jnp.concatenate([lbp, jnp.zeros((DEPTH, 5, WIDTH), F32)], axis=1)

    w_qk = jnp.stack([jnp.concatenate([_block_diag(list(wq_b[l])), _block_diag(list(wk_b[l])) * HEAD_DIM ** -0.5],
                                      axis=1) for l in range(DEPTH)]).astype(BF16)
    w_wa = jnp.stack([_block_diag([w_up_d[l], a_up_d[l]]) for l in range(DEPTH)]).astype(BF16)
    vec_b = jnp.stack([_pad_rows(jnp.stack([conv_b[l], gn_b[l], skip_b[l]]), 8) for l in range(DEPTH)])
    conv_w8 = jnp.stack([_pad_rows(conv_w[l], 8) for l in range(DEPTH)])
    vec_d = jnp.stack([_pad_rows(jnp.stack([w0_d[l], a0_d[l], k_k_d[l], k_a_d[l], r_k_d[l], gn_w_d[l], gn_b_d[l]]), 8)
                       for l in range(DEPTH)])
    head_of = jnp.arange(WIDTH) // HEAD_DIM
    mask_bd = (head_of[:, None] == head_of[None, :]).astype(F32)

    span = WINDOW + CHUNK
    rel = jnp.arange(span)[None, :] - WINDOW - jnp.arange(CHUNK)[:, None]
    bias = jnp.transpose(rel_bias.astype(F32)[_rel_bucket(rel)], (2, 0, 1))

    n_dense = ffn_w_gate.shape[0]
    ff_steps = D_FF // D_FF_EXPERT
    lp = {
        'w_main': w_main, 'w_if': w_if, 'w_if_t': jnp.swapaxes(w_if, 1, 2), 'lbp': lbp, 'gn_a': gn_a,
        'mask_bd': mask_bd, 'conv_w8': conv_w8, 'vec_b': vec_b, 'w_qk': w_qk,
        'b_if': jnp.concatenate([b_i, b_f], axis=-1), 'sinks': sinks, 'bias': bias,
        'mu_d': _d_to_kernel_order(mu_d), 'vec_d': vec_d, 'w_wa': w_wa, 'g_up_d': g_up_d.astype(BF16),
        'w_br': w_br.astype(BF16), 'w_o': w_o.astype(BF16),
        'ln1_g': ln1_g, 'ln1_b': ln1_b, 'ln2_g': ln2_g, 'ln2_b': ln2_b,
        'ffn_wg': jnp.swapaxes(ffn_w_gate.astype(BF16).reshape(n_dense, D_MODEL, ff_steps, D_FF_EXPERT), 1, 2),
        'ffn_wu': jnp.swapaxes(ffn_w_up.astype(BF16).reshape(n_dense, D_MODEL, ff_steps, D_FF_EXPERT), 1, 2),
        'ffn_wd': ffn_w_down.astype(BF16).reshape(n_dense, ff_steps, D_FF_EXPERT, D_MODEL),
        'router_w': router_w, 'router_b': router_b,
        'exp_wg': exp_w_gate.astype(BF16), 'exp_wu': exp_w_up.astype(BF16), 'exp_wd': exp_w_down.astype(BF16),
    }

    sample_states = {
        'swa_k': cache_swa_k, 'swa_v': cache_swa_v, 'hgrn': state_hgrn, 'mlstm_c': state_mlstm_c,
        'mlstm_n': state_mlstm_n, 'mlstm_m': state_mlstm_m, 'mlstm_conv': state_mlstm_conv,
        'rwkv': state_rwkv, 'rwkv_shift': state_rwkv_shift,
    }
    keep = cache_swa_k.shape[2]
    bp = x_prompt.shape[0]
    prompt_states = {}
    for name in STATE_NAMES:
        arr = sample_states[name]
        rows = (WINDOW,) + arr.shape[3:] if name in ('swa_k', 'swa_v') else arr.shape[2:]
        prompt_states[name] = jnp.zeros((DEPTH, bp) + tuple(rows), arr.dtype)

    y_prompt, pst = _trunk(x_prompt, prompt_states, False, keep, lp)
    y_sample, sst = _trunk(x_sample, sample_states, True, keep, lp)
    return (y_prompt, y_sample) + tuple(pst[n] for n in STATE_NAMES) + tuple(sst[n] for n in STATE_NAMES)
```

```python
import functools
import math

import jax
import jax.numpy as jnp
from jax import lax
from jax.experimental import pallas as pl
from jax.experimental.pallas import tpu as pltpu

F32 = jnp.float32
BF16 = jnp.bfloat16
HIGHEST = lax.Precision.HIGHEST

D_MODEL = 1024
DEPTH = 4
CHUNK = 64
HEAD_DIM = 64
N_HEADS = 4
WIDTH = N_HEADS * HEAD_DIM
C_KV_HEADS = 2
KV_WIDTH = C_KV_HEADS * HEAD_DIM
CONV_W = 4
WINDOW = 128
NUM_BUCKETS = 32
MAX_DISTANCE = 128
D_DECAY_LORA = 64
D_AAA_LORA = 64
D_GATE_LORA = 128
D_COLS = 3 * WIDTH + D_DECAY_LORA + D_AAA_LORA + D_GATE_LORA
N_BRANCH = 4
D_FF = 2816
N_EXPERTS = 8
TOP_K = 2
D_FF_EXPERT = 1408
ALPHA = (2 * DEPTH) ** 0.25
LN_EPS = 1e-5
HEAD_NORM_EPS = 1e-5
RWKV_GN_EPS = 64e-5
LB_FLOOR = 1e-30
LB_CEIL = 1.0 - 1e-6

GATE_COLS = N_BRANCH * D_MODEL
COL_A = GATE_COLS
COL_D = COL_A + 4 * WIDTH
COL_B = COL_D + D_COLS
COL_C = COL_B + 3 * WIDTH
MAIN_COLS = COL_C + WIDTH + 2 * KV_WIDTH
SUB = 16
STREAMS_PER_STEP = 4
V7X_VMEM_LIMIT = 48 * 1024 * 1024

NN = (((1,), (0,)), ((), ()))
NT = (((1,), (1,)), ((), ()))
TN = (((0,), (0,)), ((), ()))


def _cparams(*sem):
    return pltpu.CompilerParams(dimension_semantics=sem, vmem_limit_bytes=V7X_VMEM_LIMIT)


def _sigmoid(x):
    return 1.0 / (1.0 + jnp.exp(-x))


def _silu(x):
    return x * _sigmoid(x)


def _log_sigmoid(x):
    return jnp.minimum(x, 0.0) - jnp.log1p(jnp.exp(-jnp.abs(x)))


def _layernorm_rows(z, g, b):
    mu = jnp.mean(z, axis=-1, keepdims=True)
    zc = z - mu
    var = jnp.mean(zc * zc, axis=-1, keepdims=True)
    return zc * lax.rsqrt(var + LN_EPS) * g + b


def _split2(x):
    hi = x.astype(BF16)
    lo = (x - hi.astype(F32)).astype(BF16)
    return hi, lo


def _split3(x):
    hi = x.astype(BF16)
    r1 = x - hi.astype(F32)
    mid = r1.astype(BF16)
    lo = (r1 - mid.astype(F32)).astype(BF16)
    return hi, mid, lo


def _dot(a, b, dims=NN):
    return lax.dot_general(a, b, dims, preferred_element_type=F32)


def _mm2(a, b_parts, dims=NN):
    a_hi, a_lo = _split2(a)
    n = a.shape[0]
    both = _dot(jnp.concatenate([a_hi, a_lo], axis=0), b_parts[0], dims)
    return both[:n] + both[n:] + _dot(a_hi, b_parts[1], dims)


def _cumsum_rows(x):
    n = x.shape[0]
    row = lax.broadcasted_iota(jnp.int32, (n, n), 0)
    col = lax.broadcasted_iota(jnp.int32, (n, n), 1)
    tri = (row >= col).astype(BF16)
    hi, mid, lo = _split3(x)
    return _dot(tri, hi) + _dot(tri, mid) + _dot(tri, lo)


def _head_masks():
    lane_head = lax.broadcasted_iota(jnp.int32, (1, WIDTH), 1) // HEAD_DIM
    return [(lane_head == h) for h in range(N_HEADS)]


def _stack_heads(x, masks):
    return jnp.concatenate([jnp.where(mk, x, 0.0) for mk in masks], axis=0)


def _stack_parts(x, masks):
    parts = _split2(x)
    rows = [mk.astype(BF16) for mk in masks]
    return tuple(jnp.concatenate([p * r for r in rows], axis=0) for p in parts)


def _round_robin(stage_generators):
    live = list(stage_generators)
    while live:
        live = [g for g in live if next(g, _DONE) is not _DONE]


_DONE = object()


def _head_sums(x, ones_bd):
    hi, lo = _split2(x)
    return _dot(hi, ones_bd) + _dot(lo, ones_bd)


def _matmul_kernel(x_ref, w_ref, o_ref, *, precision, dims):
    x = x_ref[...]
    w = w_ref[...]
    if precision is None:
        x = x.astype(w.dtype)
    o_ref[...] = lax.dot_general(x, w, dims, preferred_element_type=F32, precision=precision).astype(o_ref.dtype)


def _matmul(x, w, *, tm, tn, out_dtype=F32, precision=None):
    m, k = x.shape
    n = w.shape[1]
    tm = min(tm, m)
    tn = min(tn, n)
    return pl.pallas_call(
        functools.partial(_matmul_kernel, precision=precision, dims=NN),
        grid=(m // tm, n // tn),
        in_specs=[pl.BlockSpec((tm, k), lambda i, j: (i, 0)),
                  pl.BlockSpec((k, tn), lambda i, j: (0, j))],
        out_specs=pl.BlockSpec((tm, tn), lambda i, j: (i, j)),
        out_shape=jax.ShapeDtypeStruct((m, n), out_dtype),
        compiler_params=_cparams("parallel", "parallel"),
        name="matmul",
    )(x, w)


PROJ_TN = 256


def _proj_kernel(x_ref, w_ref, o_ref):
    x = x_ref[...]
    for j in range(MAIN_COLS // PROJ_TN):
        o_ref[:, j * PROJ_TN:(j + 1) * PROJ_TN] = _dot(x, w_ref[:, j * PROJ_TN:(j + 1) * PROJ_TN])


def _proj(xb, w, *, tm=256):
    m, k = xb.shape
    n = w.shape[1]
    tm = min(tm, m)
    return pl.pallas_call(
        _proj_kernel,
        grid=(m // tm,),
        in_specs=[pl.BlockSpec((tm, k), lambda i: (i, 0)),
                  pl.BlockSpec((k, n), lambda i: (0, 0), pipeline_mode=pl.Buffered(1))],
        out_specs=pl.BlockSpec((tm, n), lambda i: (i, 0)),
        out_shape=jax.ShapeDtypeStruct((m, n), F32),
        compiler_params=_cparams("parallel"),
        name="proj",
    )(xb, w)


def _matmul_t(wt, x, *, tm, precision):
    n, k = wt.shape
    m = x.shape[0]
    tm = min(tm, m)
    return pl.pallas_call(
        functools.partial(_matmul_kernel, precision=precision, dims=NT),
        grid=(m // tm,),
        in_specs=[pl.BlockSpec((n, k), lambda i: (0, 0)),
                  pl.BlockSpec((tm, k), lambda i: (i, 0))],
        out_specs=pl.BlockSpec((n, tm), lambda i: (0, i)),
        out_shape=jax.ShapeDtypeStruct((n, m), F32),
        compiler_params=_cparams("parallel"),
        name="matmul_t",
    )(wt, x)


def _head(a, h):
    return a[:, h * HEAD_DIM:(h + 1) * HEAD_DIM]


def _hgrn_kernel(q_ref, f_ref, i_ref, g_ref, lbp_ref, gn_ref, ones_ref, mask_ref, s0_ref, y_ref, s_out_ref,
                 st_ref, phi_ref, plo_ref, *, nc, bb):
    c = pl.program_id(1)

    @pl.when(c == 0)
    def _():
        for bi in range(bb):
            st_ref[bi] = jnp.zeros(st_ref.shape[1:], F32)
            for h in range(N_HEADS):
                st_ref[bi, h * HEAD_DIM:(h + 1) * HEAD_DIM, h * HEAD_DIM:(h + 1) * HEAD_DIM] = s0_ref[bi, h].T

    _round_robin([_hgrn_chunk(q_ref.at[bi], f_ref.at[bi], i_ref.at[bi], g_ref.at[bi], lbp_ref, gn_ref, ones_ref,
                              mask_ref, y_ref.at[bi], st_ref.at[bi], phi_ref.at[bi], plo_ref.at[bi])
                  for bi in range(bb)])

    @pl.when(c == nc - 1)
    def _():
        for bi in range(bb):
            for h in range(N_HEADS):
                s_out_ref[bi, h] = st_ref[bi, h * HEAD_DIM:(h + 1) * HEAD_DIM, h * HEAD_DIM:(h + 1) * HEAD_DIM].T


def _hgrn_chunk(q_ref, f_ref, i_ref, g_ref, lbp_ref, gn_ref, ones_ref, mask_ref, y_ref, st_ref, phi_ref, plo_ref):
    L = CHUNK
    ones_bd = ones_ref[...]
    masks = _head_masks()
    log_lb = lbp_ref[0:1, :]
    log1m_lb = lbp_ref[1:2, :]
    one_m_lb = lbp_ref[2:3, :]
    zf = f_ref[...]
    u = log_lb
    w = log1m_lb + _log_sigmoid(zf)
    logf = jnp.maximum(u, w) + jnp.log1p(jnp.exp(-jnp.abs(u - w)))
    k = one_m_lb * _sigmoid(-zf)
    q = _silu(q_ref[...])
    v = i_ref[...]

    b = _cumsum_rows(logf)
    yield
    b_last = b[L - 1:L]
    qe = q * jnp.exp(b)
    kdec = k * jnp.exp(b_last - b)
    sub_row = lax.broadcasted_iota(jnp.int32, (SUB, WIDTH), 0)

    for s in range(L):
        r0 = (s // SUB) * SUB
        p = q[r0:r0 + SUB] * k[s:s + 1] * jnp.exp(jnp.minimum(b[r0:r0 + SUB] - b[s:s + 1], 0.0))
        hi, lo = _split2(jnp.where(sub_row >= s - r0, p, 0.0))
        phi_ref[s * SUB:(s + 1) * SUB, :] = hi
        plo_ref[s * SUB:(s + 1) * SUB, :] = lo
    yield
    att = _dot(phi_ref[...], ones_bd) + _dot(plo_ref[...], ones_bd)
    st = st_ref[...]
    o_inter = _dot(qe, st, NT)
    upd = _dot(v, kdec, TN)
    scores = []
    for blk in range(1, L // SUB):
        r0 = blk * SUB
        ref_row = b[r0 - 1:r0]
        q_s = q[r0:r0 + SUB] * jnp.exp(b[r0:r0 + SUB] - ref_row)
        k_s = k[:r0] * jnp.exp(ref_row - b[:r0])
        scores.append(_dot(q_s, _stack_heads(k_s, masks), NT))
    yield
    st_ref[...] = st * jnp.exp(b_last) + upd * mask_ref[...]
    o_blocks = []
    for blk in range(L // SUB):
        r0 = blk * SUB
        acc = jnp.zeros((SUB, WIDTH), F32)
        for j in range(SUB):
            s = r0 + j
            acc = acc + att[s * SUB:(s + 1) * SUB] * v[s:s + 1]
        if blk > 0:
            acc = acc + _dot(scores[blk - 1], _stack_heads(v[:r0], masks))
        o_blocks.append(acc)
    yield
    o = jnp.concatenate(o_blocks, axis=0) + o_inter
    ms = _head_sums(o * o, ones_bd) * (1.0 / HEAD_DIM)
    yield
    y_ref[...] = (o * lax.rsqrt(ms + HEAD_NORM_EPS) * gn_ref[...] * _silu(g_ref[...])).astype(y_ref.dtype)


def _hgrn(h3, lbp, gn, mask_bd, s0):
    bsz, t, _ = h3.shape
    nc = t // CHUNK
    cb = COL_A // WIDTH
    bb = STREAMS_PER_STEP

    def col(j):
        return pl.BlockSpec((bb, CHUNK, WIDTH), lambda b, c, j=j: (b, c, cb + j))

    square = pl.BlockSpec((WIDTH, WIDTH), lambda b, c: (0, 0))
    state = pl.BlockSpec((bb, N_HEADS, HEAD_DIM, HEAD_DIM), lambda b, c: (b, 0, 0, 0))
    return pl.pallas_call(
        functools.partial(_hgrn_kernel, nc=nc, bb=bb),
        grid=(bsz // bb, nc),
        in_specs=[col(0), col(1), col(2), col(3),
                  pl.BlockSpec((8, WIDTH), lambda b, c: (0, 0)),
                  pl.BlockSpec((1, WIDTH), lambda b, c: (0, 0)),
                  square, square, state],
        out_specs=[pl.BlockSpec((bb, CHUNK, WIDTH), lambda b, c: (b, c, 0)), state],
        out_shape=[jax.ShapeDtypeStruct((bsz, t, WIDTH), BF16),
                   jax.ShapeDtypeStruct((bsz, N_HEADS, HEAD_DIM, HEAD_DIM), F32)],
        scratch_shapes=[pltpu.VMEM((bb, WIDTH, WIDTH), F32),
                        pltpu.VMEM((bb, CHUNK * SUB, WIDTH), BF16),
                        pltpu.VMEM((bb, CHUNK * SUB, WIDTH), BF16)],
        compiler_params=_cparams("parallel", "arbitrary"),
        name="hgrn2",
    )(h3, h3, h3, h3, lbp, gn, mask_bd.astype(BF16), mask_bd, s0)


def _mlstm_kernel(u_ref, v_ref, o_ref, gcol_ref, grow_ref, bcol_ref, brow_ref, cw_ref, vec_ref, wqk_ref,
                  c0_ref, n0_ref, m0_ref, conv0_ref,
                  y_ref, c_out_ref, n_out_ref, m_out_ref, conv_out_ref,
                  c_ref, n_ref, m_ref, carry_ref, *, nc, bb):
    c = pl.program_id(1)

    @pl.when(c == 0)
    def _():
        c_ref[...] = c0_ref[...]
        n_ref[...] = n0_ref[...]
        m_ref[...] = m0_ref[...]
        carry_ref[...] = conv0_ref[...]

    _round_robin([_mlstm_chunk(u_ref.at[bi], v_ref.at[bi], o_ref.at[bi], gcol_ref.at[bi], grow_ref.at[bi, 0],
                               bcol_ref, brow_ref, cw_ref, vec_ref, wqk_ref, y_ref.at[bi],
                               c_ref.at[bi], n_ref.at[bi], m_ref.at[bi], carry_ref.at[bi]) for bi in range(bb)])

    @pl.when(c == nc - 1)
    def _():
        c_out_ref[...] = c_ref[...]
        n_out_ref[...] = n_ref[...]
        m_out_ref[...] = m_ref[...]
        conv_out_ref[...] = carry_ref[...]


def _mlstm_chunk(u_ref, v_ref, o_ref, gcol_ref, grow_ref, bcol_ref, brow_ref, cw_ref, vec_ref, wqk_ref,
                 y_ref, c_ref, n_ref, m_ref, carry_ref):
    L = CHUNK
    heads = range(N_HEADS)

    u = u_ref[...]
    carry = carry_ref[...]
    row8 = lax.broadcasted_iota(jnp.int32, (8, WIDTH), 0)
    conv = vec_ref[0:1, :] + cw_ref[CONV_W - 1:CONV_W, :] * u
    for d in range(1, CONV_W):
        rolled = pltpu.roll(u, d, 0)
        top = jnp.where(row8 < d, pltpu.roll(carry, d, 0), rolled[0:8])
        conv = conv + cw_ref[CONV_W - 1 - d:CONV_W - d, :] * jnp.concatenate([top, rolled[8:]], axis=0)
    carry_ref[...] = u[L - 8:L]
    cact = _silu(conv)
    qk_all = _dot(cact.astype(BF16), wqk_ref[...])

    row = lax.broadcasted_iota(jnp.int32, (L, L), 0)
    col = lax.broadcasted_iota(jnp.int32, (L, L), 1)
    lower = row >= col
    raw_c = gcol_ref[...] + brow_ref[...]
    lane8 = lax.broadcasted_iota(jnp.int32, raw_c.shape, 1)
    gcol = jnp.where(lane8 < N_HEADS, raw_c, _log_sigmoid(raw_c))
    raw_r = grow_ref[...] + bcol_ref[...]
    sub8 = lax.broadcasted_iota(jnp.int32, raw_r.shape, 0)
    grow = jnp.where(sub8 < N_HEADS, raw_r, _log_sigmoid(raw_r))
    f_col = jnp.dot(lower.astype(F32), gcol, preferred_element_type=F32, precision=HIGHEST)
    f_row = jnp.dot(grow, (row <= col).astype(F32), preferred_element_type=F32, precision=HIGHEST)
    yield
    v_all = v_ref[...]
    o_all = _sigmoid(o_ref[...])
    gn = vec_ref[1:2, :]
    skip = vec_ref[2:3, :]
    fc = [f_col[:, N_HEADS + h:N_HEADS + h + 1] for h in heads]
    li_c = [gcol[:, h:h + 1] for h in heads]
    q = [_head(qk_all, h) for h in heads]
    k = [_head(qk_all, N_HEADS + h) for h in heads]
    v = [_head(v_all, h) for h in heads]
    cm = [c_ref[h] for h in heads]
    n = [n_ref[h:h + 1, :] for h in heads]
    g = [fc[h] + m_ref[0:1, h:h + 1] for h in heads]
    d = [jnp.where(lower, fc[h] - f_row[N_HEADS + h:N_HEADS + h + 1, :] + grow[h:h + 1, :], -jnp.inf) for h in heads]
    qk_raw = [_dot(q[h], k[h], NT) for h in heads]
    q_c = [_dot(q[h], cm[h]) for h in heads]
    q_n = [jnp.sum(q[h] * n[h], axis=1, keepdims=True) for h in heads]
    mt = [jnp.maximum(g[h], jnp.max(d[h], axis=1, keepdims=True)) for h in heads]
    yield
    wg = [jnp.exp(g[h] - mt[h]) for h in heads]
    qk = [qk_raw[h] * jnp.exp(d[h] - mt[h]) for h in heads]
    qkv = [_dot(qk[h], v[h]) for h in heads]
    qk_sum = [jnp.sum(qk[h], axis=1, keepdims=True) for h in heads]
    mt_last = [mt[h][L - 1:L] for h in heads]
    kw = [k[h] * jnp.exp(fc[h][L - 1:L] - fc[h] + li_c[h] - mt_last[h]) for h in heads]
    c_upd = [_dot(kw[h], v[h], TN) for h in heads]
    yield
    for h in heads:
        wgl = wg[h][L - 1:L]
        c_ref[h] = wgl * cm[h] + c_upd[h]
        n_ref[h:h + 1, :] = wgl * n[h] + jnp.sum(kw[h], axis=0, keepdims=True)
        m_ref[0:1, h:h + 1] = mt_last[h]
    den = [wg[h] * q_n[h] + qk_sum[h] for h in heads]
    hh = [(wg[h] * q_c[h] + qkv[h]) / jnp.maximum(jnp.abs(den[h]), jnp.exp(-mt[h])) for h in heads]
    z = [_head(o_all, h) * hh[h] for h in heads]
    mu = [jnp.mean(z[h], axis=1, keepdims=True) for h in heads]
    yield
    zc = [z[h] - mu[h] for h in heads]
    var = [jnp.mean(zc[h] * zc[h], axis=1, keepdims=True) for h in heads]
    yield
    y = [zc[h] * lax.rsqrt(var[h] + HEAD_NORM_EPS) * _head(gn, h) + _head(skip, h) * _head(cact, h) for h in heads]
    y_ref[...] = jnp.concatenate(y, axis=1).astype(y_ref.dtype)


def _mlstm(h3, gcol, grow, b_col, b_row, conv_w, vecs, w_qk, c0, n0, m0, conv0):
    bsz, t, _ = h3.shape
    nc = t // CHUNK
    cb = COL_B // WIDTH
    bb = STREAMS_PER_STEP
    state4 = pl.BlockSpec((bb, N_HEADS, HEAD_DIM, HEAD_DIM), lambda b, c: (b, 0, 0, 0))
    state_n = pl.BlockSpec((bb, N_HEADS, HEAD_DIM), lambda b, c: (b, 0, 0))
    state_m = pl.BlockSpec((bb, 1, N_HEADS), lambda b, c: (b, 0, 0))
    state_conv = pl.BlockSpec((bb, 8, WIDTH), lambda b, c: (b, 0, 0))

    def const(shape):
        return pl.BlockSpec(shape, lambda b, c: tuple(0 for _ in shape))

    def col(j):
        return pl.BlockSpec((bb, CHUNK, WIDTH), lambda b, c, j=j: (b, c, cb + j))

    return pl.pallas_call(
        functools.partial(_mlstm_kernel, nc=nc, bb=bb),
        grid=(bsz // bb, nc),
        in_specs=[col(0), col(1), col(2),
                  pl.BlockSpec((bb, CHUNK, 2 * N_HEADS), lambda b, c: (b, c, 0)),
                  pl.BlockSpec((bb, 1, 2 * N_HEADS, CHUNK), lambda b, c: (b, c, 0, 0)),
                  const((2 * N_HEADS, 1)), const((1, 2 * N_HEADS)), const((8, WIDTH)), const((8, WIDTH)),
                  const((WIDTH, 2 * WIDTH)),
                  state4, state_n, state_m, state_conv],
        out_specs=[pl.BlockSpec((bb, CHUNK, WIDTH), lambda b, c: (b, c, 0)), state4, state_n, state_m, state_conv],
        out_shape=[jax.ShapeDtypeStruct((bsz, t, WIDTH), BF16),
                   jax.ShapeDtypeStruct((bsz, N_HEADS, HEAD_DIM, HEAD_DIM), F32),
                   jax.ShapeDtypeStruct((bsz, N_HEADS, HEAD_DIM), F32),
                   jax.ShapeDtypeStruct((bsz, 1, N_HEADS), F32),
                   jax.ShapeDtypeStruct((bsz, 8, WIDTH), F32)],
        scratch_shapes=[pltpu.VMEM((bb, N_HEADS, HEAD_DIM, HEAD_DIM), F32),
                        pltpu.VMEM((bb, N_HEADS, HEAD_DIM), F32),
                        pltpu.VMEM((bb, 1, N_HEADS), F32),
                        pltpu.VMEM((bb, 8, WIDTH), F32)],
        compiler_params=_cparams("parallel", "arbitrary"),
        name="mlstm",
    )(h3, h3, h3, gcol, grow, b_col, b_row, conv_w, vecs, w_qk, c0, n0, m0, conv0)


SWA_CHUNKS_PER_STEP = 4


def _swa_kernel(*refs, prev_valid, cps, n_kv):
    q_ref = refs[0]
    k_refs = refs[1:1 + n_kv]
    v_refs = refs[1 + n_kv:1 + 2 * n_kv]
    bias_ref, sink_ref, y_ref, s_ref, p_ref = refs[1 + 2 * n_kv:]
    g = pl.program_id(1)
    L = CHUNK
    span = WINDOW + L
    kcat = jnp.concatenate([r[0] for r in k_refs], axis=0).astype(BF16)
    vcat = jnp.concatenate([r[0] for r in v_refs], axis=0).astype(BF16)
    kv_of = [h // (N_HEADS // C_KV_HEADS) for h in range(N_HEADS)]
    for j in range(cps):
        q = q_ref[0, j * L:(j + 1) * L, :].astype(BF16)
        k_j = kcat[j * L:j * L + span]
        for h in range(N_HEADS):
            s_ref[j * N_HEADS + h] = _dot(_head(q, h), _head(k_j, kv_of[h]), NT)
    for j in range(cps):
        key_pos = lax.broadcasted_iota(jnp.int32, (L, span), 1) + (g * cps + j) * L
        for h in range(N_HEADS):
            s = s_ref[j * N_HEADS + h] * HEAD_DIM ** -0.5 + bias_ref[h]
            if not prev_valid:
                s = jnp.where(key_pos >= WINDOW, s, -jnp.inf)
            sink = sink_ref[0:1, h:h + 1]
            m = jnp.maximum(jnp.max(s, axis=1, keepdims=True), sink)
            p = jnp.exp(s - m)
            inv = 1.0 / (jnp.sum(p, axis=1, keepdims=True) + jnp.exp(sink - m))
            p_ref[j * N_HEADS + h] = (p * inv).astype(BF16)
    for j in range(cps):
        v_j = vcat[j * L:j * L + span]
        o = [_dot(p_ref[j * N_HEADS + h], _head(v_j, kv_of[h])) for h in range(N_HEADS)]
        y_ref[0, j * L:(j + 1) * L, :] = jnp.concatenate(o, axis=1).astype(y_ref.dtype)


def _swa(h3, kf, vf, bias, sinks, prev_valid):
    bsz, t, _ = h3.shape
    nc = t // CHUNK
    cps = min(SWA_CHUNKS_PER_STEP, nc)
    rows = cps * CHUNK
    kv_rows = min(rows, WINDOW)
    n_kv = (WINDOW + rows) // kv_rows
    qb = COL_C // WIDTH
    kv = [pl.BlockSpec((1, kv_rows, KV_WIDTH), lambda b, g, j=j: (b, g * (rows // kv_rows) + j, 0))
          for j in range(n_kv)]

    return pl.pallas_call(
        functools.partial(_swa_kernel, prev_valid=prev_valid, cps=cps, n_kv=n_kv),
        grid=(bsz, nc // cps),
        in_specs=[pl.BlockSpec((1, rows, WIDTH), lambda b, g: (b, g, qb))] + kv + kv + [
                  pl.BlockSpec((N_HEADS, CHUNK, WINDOW + CHUNK), lambda b, g: (0, 0, 0)),
                  pl.BlockSpec((1, N_HEADS), lambda b, g: (0, 0))],
        out_specs=pl.BlockSpec((1, rows, WIDTH), lambda b, g: (b, g, 0)),
        out_shape=jax.ShapeDtypeStruct((bsz, t, WIDTH), BF16),
        scratch_shapes=[pltpu.VMEM((cps * N_HEADS, CHUNK, WINDOW + CHUNK), F32),
                        pltpu.VMEM((cps * N_HEADS, CHUNK, WINDOW + CHUNK), BF16)],
        compiler_params=_cparams("parallel", "parallel"),
        name="swa",
    )(h3, *([kf] * n_kv), *([vf] * n_kv), bias, sinks)


def _rwkv_kernel(pd_ref, mu_ref, vec_ref, wwa_ref, gup_ref, ones_ref, mask_ref, s0_ref, sh0_ref,
                 y_ref, s_out_ref, sh_out_ref, st_ref, carry_ref, *, nc, bb):
    c = pl.program_id(1)

    @pl.when(c == 0)
    def _():
        for bi in range(bb):
            st_ref[bi] = jnp.zeros(st_ref.shape[1:], F32)
            for h in range(N_HEADS):
                st_ref[bi, h * HEAD_DIM:(h + 1) * HEAD_DIM, h * HEAD_DIM:(h + 1) * HEAD_DIM] = s0_ref[bi, h]
            carry_ref[bi, 7:8, :] = sh0_ref[bi]

    _round_robin([_rwkv_chunk(pd_ref.at[bi], mu_ref, vec_ref, wwa_ref, gup_ref, ones_ref, mask_ref,
                              y_ref.at[bi], st_ref.at[bi], carry_ref.at[bi]) for bi in range(bb)])

    @pl.when(c == nc - 1)
    def _():
        for bi in range(bb):
            for h in range(N_HEADS):
                s_out_ref[bi, h] = st_ref[bi, h * HEAD_DIM:(h + 1) * HEAD_DIM, h * HEAD_DIM:(h + 1) * HEAD_DIM]
            sh_out_ref[bi] = carry_ref[bi, 7:8, :]


def _rwkv_chunk(pd_ref, mu_ref, vec_ref, wwa_ref, gup_ref, ones_ref, mask_ref, y_ref, st_ref, carry_ref):
    L = CHUNK
    ones_bd = ones_ref[...]
    masks = _head_masks()

    pd = pd_ref[...]
    rolled = pltpu.roll(pd, 1, 0)
    row8 = lax.broadcasted_iota(jnp.int32, (8, D_COLS), 0)
    top = jnp.where(row8 == 0, carry_ref[7:8, :], rolled[0:8])
    pd_prev = jnp.concatenate([top, rolled[8:]], axis=0)
    carry_ref[...] = pd[L - 8:L]
    mixed = pd + (pd_prev - pd) * mu_ref[...]
    r = mixed[:, 0:WIDTH]
    k_raw = mixed[:, WIDTH:2 * WIDTH]
    v = mixed[:, 2 * WIDTH:3 * WIDTH]
    wa = mixed[:, 3 * WIDTH:3 * WIDTH + 2 * D_DECAY_LORA]
    g_in = mixed[:, 3 * WIDTH + 2 * D_DECAY_LORA:D_COLS]
    lane_wa = lax.broadcasted_iota(jnp.int32, wa.shape, 1)
    wa_act = jnp.where(lane_wa < D_DECAY_LORA, jnp.tanh(wa), wa)
    lora = _dot(wa_act.astype(BF16), wwa_ref[...])
    g_d = _dot(_sigmoid(g_in).astype(BF16), gup_ref[...])
    yield
    z = -(vec_ref[0:1, :] + lora[:, 0:WIDTH])
    w_log = -(jnp.maximum(z, 0.0) + jnp.log1p(jnp.exp(-jnp.abs(z)))) - 0.5
    lw = -jnp.exp(w_log)
    a_d = _sigmoid(vec_ref[1:2, :] + lora[:, WIDTH:2 * WIDTH])
    kk = k_raw * vec_ref[2:3, :]
    kk_norm = _head_sums(kk * kk, ones_bd)
    cw = _cumsum_rows(lw)
    yield
    kk = kk * lax.rsqrt(jnp.maximum(kk_norm, 1e-24))
    k = k_raw * (1.0 + (a_d - 1.0) * vec_ref[3:4, :])
    a = -kk
    b = kk * a_d
    g_in_c = jnp.exp(cw)
    g_inv = jnp.exp(-cw)
    at = a * jnp.exp(cw - lw)
    rt = r * g_in_c
    bt = b * g_inv
    kt = k * g_inv
    x = jnp.concatenate([at, rt], axis=0)
    bt_st = _stack_parts(bt, masks)
    kt_st = _stack_parts(kt, masks)
    gram_b = _mm2(x, bt_st, NT)
    gram_k = _mm2(x, kt_st, NT)
    st = st_ref[...]
    u = _mm2(x, _split2(st), NT)
    yield

    t_idx = lax.broadcasted_iota(jnp.int32, (L, WIDTH), 0)
    i_idx = lax.broadcasted_iota(jnp.int32, (L, WIDTH), 1) % L
    strict = i_idx < t_idx
    incl = i_idx <= t_idx
    same_blk = (i_idx // SUB) == (t_idx // SUB)
    eye = (i_idx == t_idx).astype(F32)
    n_all = jnp.where(strict, gram_b[0:L], 0.0)
    m_all = jnp.where(strict, gram_k[0:L], 0.0)
    rb_all = jnp.where(incl, gram_b[L:2 * L], 0.0)
    rk_all = jnp.where(incl, gram_k[L:2 * L], 0.0)
    n_d = jnp.where(same_blk, n_all, 0.0)
    n_off = jnp.where(same_blk, 0.0, n_all)

    v_st = _stack_parts(v, masks)
    x2 = _mm2(n_d, _stack_parts(n_d, masks))
    rhs = u[0:L] + _mm2(m_all, v_st)
    bonus = _head_sums(r * k * vec_ref[4:5, :], ones_bd) * v
    yield
    x2_st = _stack_parts(x2, masks)
    x4 = _mm2(x2, x2_st)
    t_d = eye + n_d
    t_d = t_d + _mm2(t_d, x2_st)
    yield
    x4_st = _stack_parts(x4, masks)
    x8 = _mm2(x4, x4_st)
    t_d = t_d + _mm2(t_d, x4_st)
    yield
    t_d = t_d + _mm2(t_d, _stack_parts(x8, masks))
    yield
    n1 = _mm2(t_d, _stack_parts(n_off, masks))
    yield
    n1_st = _stack_parts(n1, masks)
    n2 = _mm2(n1, n1_st)
    yield
    a2 = eye + n1 + n2 + _mm2(n2, n1_st)
    yield
    t_full = _mm2(a2, _stack_parts(t_d, masks))
    yield
    sa = _mm2(t_full, _stack_parts(rhs, masks))
    yield
    y = u[L:2 * L] + _mm2(rb_all, _stack_parts(sa, masks)) + _mm2(rk_all, v_st)
    sv_hi, sv_lo = _split2(jnp.concatenate([sa, v], axis=0))
    bk_hi, bk_lo = _split2(jnp.concatenate([bt, kt], axis=0))
    upd = _dot(sv_hi, bk_hi, TN) + _dot(sv_lo, bk_hi, TN) + _dot(sv_hi, bk_lo, TN)
    yield
    st_ref[...] = (st + upd * mask_ref[...]) * g_in_c[L - 1:L]

    mu = _head_sums(y, ones_bd) * (1.0 / HEAD_DIM)
    yield
    yc = y - mu
    var = _head_sums(yc * yc, ones_bd) * (1.0 / HEAD_DIM)
    yield
    o = yc * lax.rsqrt(var + RWKV_GN_EPS) * vec_ref[5:6, :] + vec_ref[6:7, :]
    y_ref[...] = ((o + bonus) * g_d).astype(y_ref.dtype)


def _rwkv(h3, mu, vecs, w_wa, g_up, mask_bd, s0, shift0):
    bsz, t, _ = h3.shape
    nc = t // CHUNK
    bb = STREAMS_PER_STEP
    state = pl.BlockSpec((bb, N_HEADS, HEAD_DIM, HEAD_DIM), lambda g, i: (g, 0, 0, 0))
    shift = pl.BlockSpec((bb, 1, D_COLS), lambda g, i: (g, 0, 0))

    def const(shape):
        return pl.BlockSpec(shape, lambda g, i: tuple(0 for _ in shape))

    return pl.pallas_call(
        functools.partial(_rwkv_kernel, nc=nc, bb=bb),
        grid=(bsz // bb, nc),
        in_specs=[pl.BlockSpec((bb, CHUNK, D_COLS), lambda g, i: (g, i, COL_D // D_COLS)),
                  const((1, D_COLS)), const((8, WIDTH)), const((2 * D_DECAY_LORA, 2 * WIDTH)),
                  const((D_GATE_LORA, WIDTH)), const((WIDTH, WIDTH)), const((WIDTH, WIDTH)), state, shift],
        out_specs=[pl.BlockSpec((bb, CHUNK, WIDTH), lambda g, i: (g, i, 0)), state, shift],
        out_shape=[jax.ShapeDtypeStruct((bsz, t, WIDTH), BF16),
                   jax.ShapeDtypeStruct((bsz, N_HEADS, HEAD_DIM, HEAD_DIM), F32),
                   jax.ShapeDtypeStruct((bsz, 1, D_COLS), F32)],
        scratch_shapes=[pltpu.VMEM((bb, WIDTH, WIDTH), F32), pltpu.VMEM((bb, 8, D_COLS), F32)],
        compiler_params=_cparams("parallel", "arbitrary"),
        name="rwkv7",
    )(h3, mu, vecs, w_wa, g_up, mask_bd.astype(BF16), mask_bd, s0, shift0)


def _merge_kernel(ya_ref, yb_ref, yc_ref, yd_ref, g0_ref, g1_ref, g2_ref, g3_ref, wbr_ref, wo_ref, x_ref,
                  lng_ref, lnb_ref, xo_ref, xob_ref):
    ys = (ya_ref, yb_ref, yc_ref, yd_ref)
    gs = (g0_ref, g1_ref, g2_ref, g3_ref)
    merged = None
    for n in range(N_BRANCH):
        term = _sigmoid(gs[n][...]) * _dot(ys[n][...], wbr_ref[n])
        merged = term if merged is None else merged + term
    out = _dot(merged.astype(BF16), wo_ref[...])
    xn = _layernorm_rows(ALPHA * x_ref[...] + out, lng_ref[...], lnb_ref[...])
    xo_ref[...] = xn
    xob_ref[...] = xn.astype(BF16)


def _merge(ys, h2, wbr, wo, x2, lng, lnb, *, tm=256):
    m = x2.shape[0]
    tm = min(tm, m)
    ysp = pl.BlockSpec((tm, WIDTH), lambda i: (i, 0))
    row = pl.BlockSpec((tm, D_MODEL), lambda i: (i, 0))
    vec = pl.BlockSpec((1, D_MODEL), lambda i: (0, 0))

    def gate(n):
        return pl.BlockSpec((tm, D_MODEL), lambda i, n=n: (i, n))

    return pl.pallas_call(
        _merge_kernel,
        grid=(m // tm,),
        in_specs=[ysp, ysp, ysp, ysp, gate(0), gate(1), gate(2), gate(3),
                  pl.BlockSpec((N_BRANCH, WIDTH, D_MODEL), lambda i: (0, 0, 0)),
                  pl.BlockSpec((D_MODEL, D_MODEL), lambda i: (0, 0)),
                  row, vec, vec],
        out_specs=[row, row],
        out_shape=[jax.ShapeDtypeStruct((m, D_MODEL), F32), jax.ShapeDtypeStruct((m, D_MODEL), BF16)],
        compiler_params=_cparams("parallel"),
        name="merge",
    )(*ys, h2, h2, h2, h2, wbr, wo, x2, lng, lnb)


def _ffn_kernel(*refs, n_steps, gated):
    if gated:
        xb_ref, x_ref, gates_ref, wg_ref, wu_ref, wd_ref, lng_ref, lnb_ref, xo_ref, xob_ref, acc_ref = refs
    else:
        xb_ref, x_ref, wg_ref, wu_ref, wd_ref, lng_ref, lnb_ref, xo_ref, xob_ref, acc_ref = refs
    j = pl.program_id(1)

    @pl.when(j == 0)
    def _():
        acc_ref[...] = jnp.zeros_like(acc_ref)

    xb = xb_ref[...]
    hg = _dot(xb, wg_ref[0])
    hu = _dot(xb, wu_ref[0])
    part = _dot((_silu(hg) * hu).astype(BF16), wd_ref[0])
    if gated:
        gates = gates_ref[...]
        lane = lax.broadcasted_iota(jnp.int32, gates.shape, 1)
        part = jnp.sum(jnp.where(lane == j, gates, 0.0), axis=1, keepdims=True) * part
    acc_ref[...] += part

    @pl.when(j == n_steps - 1)
    def _():
        xn = _layernorm_rows(ALPHA * x_ref[...] + acc_ref[...], lng_ref[...], lnb_ref[...])
        xo_ref[...] = xn
        xob_ref[...] = xn.astype(BF16)


def _ffn(xb, x2, wg, wu, wd, lng, lnb, gates=None, *, tm=512):
    m = x2.shape[0]
    tm = min(tm, m)
    n_steps, _, tf = wg.shape
    row = pl.BlockSpec((tm, D_MODEL), lambda i, j: (i, 0))
    vec = pl.BlockSpec((1, D_MODEL), lambda i, j: (0, 0))
    w_in_spec = pl.BlockSpec((1, D_MODEL, tf), lambda i, j: (j, 0, 0))
    w_out_spec = pl.BlockSpec((1, tf, D_MODEL), lambda i, j: (j, 0, 0))
    in_specs = [row, row]
    args = [xb, x2]
    if gates is not None:
        in_specs.append(pl.BlockSpec((tm, N_EXPERTS), lambda i, j: (i, 0)))
        args.append(gates)
    in_specs += [w_in_spec, w_in_spec, w_out_spec, vec, vec]
    args += [wg, wu, wd, lng, lnb]
    return pl.pallas_call(
        functools.partial(_ffn_kernel, n_steps=n_steps, gated=gates is not None),
        grid=(m // tm, n_steps),
        in_specs=in_specs,
        out_specs=[row, row],
        out_shape=[jax.ShapeDtypeStruct((m, D_MODEL), F32), jax.ShapeDtypeStruct((m, D_MODEL), BF16)],
        scratch_shapes=[pltpu.VMEM((tm, D_MODEL), F32)],
        compiler_params=_cparams("parallel", "arbitrary"),
        name="moe" if gates is not None else "ffn",
    )(*args)


STATE_NAMES = ('swa_k', 'swa_v', 'hgrn', 'mlstm_c', 'mlstm_n', 'mlstm_m', 'mlstm_conv', 'rwkv', 'rwkv_shift')

_D_ORIG = (('r', WIDTH), ('w', D_DECAY_LORA), ('k', WIDTH), ('v', WIDTH), ('a', D_AAA_LORA), ('g', D_GATE_LORA))
_D_KERNEL = ('r', 'k', 'v', 'w', 'a', 'g')


def _d_pieces(arr):
    out, off = {}, 0
    for name, size in _D_ORIG:
        out[name] = arr[..., off:off + size]
        off += size
    return out


def _d_to_kernel_order(arr):
    p = _d_pieces(arr)
    return jnp.concatenate([p[n] for n in _D_KERNEL], axis=-1)


def _d_to_original_order(arr):
    sizes = dict(_D_ORIG)
    p, off = {}, 0
    for name in _D_KERNEL:
        p[name] = arr[..., off:off + sizes[name]]
        off += sizes[name]
    return jnp.concatenate([p[n] for n, _ in _D_ORIG], axis=-1)


def _rel_bucket(rel):
    half = NUM_BUCKETS // 2
    exact = half // 2
    dist = jnp.abs(rel)
    far = exact + (jnp.log(jnp.maximum(dist, 1).astype(F32) / exact)
                   / math.log(MAX_DISTANCE / exact) * (half - exact)).astype(jnp.int32)
    far = jnp.minimum(far, half - 1)
    return jnp.where(rel > 0, half, 0) + jnp.where(dist < exact, dist, far)


def _lower_bounds(lb_raw):
    sm = jax.nn.softmax(lb_raw.astype(F32), axis=0)
    lb = jnp.concatenate([jnp.zeros_like(sm[:1]), jnp.cumsum(sm[1:], axis=0)[:-1]], axis=0)
    return jnp.clip(lb, 0.0, LB_CEIL)


def _block_diag(blocks):
    rows = sum(b.shape[0] for b in blocks)
    cols = sum(b.shape[1] for b in blocks)
    out = jnp.zeros((rows, cols), blocks[0].dtype)
    r = c = 0
    for b in blocks:
        out = out.at[r:r + b.shape[0], c:c + b.shape[1]].set(b)
        r += b.shape[0]
        c += b.shape[1]
    return out


def _pad_rows(a, rows):
    return jnp.concatenate([a, jnp.zeros((rows - a.shape[0],) + a.shape[1:], a.dtype)], axis=0)


def _mixer(x3, xb2, st, prev_valid, lp, l):
    bsz, t, _ = x3.shape
    m = bsz * t
    x2 = x3.reshape(m, D_MODEL)
    h2 = _proj(xb2, lp['w_main'][l])
    gif = _matmul(x2, lp['w_if'][l], tm=1024, tn=2 * N_HEADS, precision=HIGHEST)
    gif_t = _matmul_t(lp['w_if_t'][l], x2, tm=1024, precision=HIGHEST)
    h3 = h2.reshape(bsz, t, MAIN_COLS)

    y_a, s_hgrn = _hgrn(h3, lp['lbp'][l], lp['gn_a'][l][None], lp['mask_bd'], st['hgrn'])

    grow = jnp.transpose(gif_t.reshape(2 * N_HEADS, bsz, t // CHUNK, CHUNK), (1, 2, 0, 3))
    conv0 = jnp.concatenate([jnp.zeros((bsz, 8 - (CONV_W - 1), WIDTH), F32), st['mlstm_conv']], axis=1)
    y_b, mc, mn, mm, conv8 = _mlstm(h3, gif.reshape(bsz, t, 2 * N_HEADS), grow, lp['b_if'][l][:, None],
                                    lp['b_if'][l][None, :], lp['conv_w8'][l], lp['vec_b'][l], lp['w_qk'][l],
                                    st['mlstm_c'], st['mlstm_n'], st['mlstm_m'][:, None, :], conv0)
    conv_state = conv8[:, 8 - (CONV_W - 1):]

    k_off = COL_C + WIDTH
    kf = jnp.concatenate([st['swa_k'].reshape(bsz, WINDOW, KV_WIDTH), h3[..., k_off:k_off + KV_WIDTH]], axis=1)
    vf = jnp.concatenate([st['swa_v'].reshape(bsz, WINDOW, KV_WIDTH), h3[..., k_off + KV_WIDTH:MAIN_COLS]], axis=1)
    y_c = _swa(h3, kf, vf, lp['bias'], lp['sinks'][l][None], prev_valid)
    keep = st['keep']
    k_win = kf[:, -keep:].reshape(bsz, keep, C_KV_HEADS, HEAD_DIM)
    v_win = vf[:, -keep:].reshape(bsz, keep, C_KV_HEADS, HEAD_DIM)

    y_d, s_rwkv, shift = _rwkv(h3, lp['mu_d'][l][None], lp['vec_d'][l], lp['w_wa'][l], lp['g_up_d'][l],
                               lp['mask_bd'], st['rwkv'], _d_to_kernel_order(st['rwkv_shift'])[:, None, :])
    shift_state = _d_to_original_order(shift[:, 0, :])

    ys = [y.reshape(m, WIDTH) for y in (y_a, y_b, y_c, y_d)]
    x1, x1b = _merge(ys, h2, lp['w_br'][l], lp['w_o'][l], x2, lp['ln1_g'][l][None], lp['ln1_b'][l][None])
    new_st = {'swa_k': k_win, 'swa_v': v_win, 'hgrn': s_hgrn, 'mlstm_c': mc, 'mlstm_n': mn,
              'mlstm_m': mm[:, 0, :], 'mlstm_conv': conv_state, 'rwkv': s_rwkv, 'rwkv_shift': shift_state}
    return x1, x1b, new_st


def _trunk(x3, states, prev_valid, keep, lp):
    bsz, t, _ = x3.shape
    m = bsz * t
    xb2 = x3.reshape(m, D_MODEL).astype(BF16)
    collected = {name: [] for name in STATE_NAMES}
    for l in range(DEPTH):
        st = {name: states[name][l] for name in STATE_NAMES}
        st['keep'] = keep
        x1, x1b, new_st = _mixer(x3, xb2, st, prev_valid, lp, l)
        j = l // 2
        ln_g, ln_b = lp['ln2_g'][l][None], lp['ln2_b'][l][None]
        if l % 2 == 0:
            x2, xb2 = _ffn(x1b, x1, lp['ffn_wg'][j], lp['ffn_wu'][j], lp['ffn_wd'][j], ln_g, ln_b)
        else:
            logits = _matmul(x1, lp['router_w'][j], tm=1024, tn=N_EXPERTS, precision=HIGHEST) + lp['router_b'][j]
            top_v, top_i = lax.top_k(logits, TOP_K)
            probs = jax.nn.softmax(top_v, axis=-1)
            gates = jnp.einsum('mk,mke->me', probs, jax.nn.one_hot(top_i, N_EXPERTS, dtype=F32))
            x2, xb2 = _ffn(x1b, x1, lp['exp_wg'][j], lp['exp_wu'][j], lp['exp_wd'][j], ln_g, ln_b, gates)
        x3 = x2.reshape(bsz, t, D_MODEL)
        for name in STATE_NAMES:
            collected[name].append(new_st[name])
    return x3, {name: jnp.stack(collected[name]) for name in STATE_NAMES}


def kernel(x_prompt, x_sample, cache_swa_k, cache_swa_v, state_hgrn, state_mlstm_c, state_mlstm_n, state_mlstm_m, state_mlstm_conv, state_rwkv, state_rwkv_shift, w_in, lb_raw, gn_a, conv_w, conv_b, wq_b, wk_b, b_i, b_f, gn_b, skip_b, sinks, rel_bias, mu_d, w0_d, w_up_d, a0_d, a_up_d, g_up_d, k_k_d, k_a_d, r_k_d, gn_w_d, gn_b_d, w_br, w_o, ln1_g, ln1_b, ln2_g, ln2_b, ffn_w_gate, ffn_w_up, ffn_w_down, router_w, router_b, exp_w_gate, exp_w_up, exp_w_down):
    off_if = 4 * WIDTH + 3 * WIDTH
    off_c = off_if + 2 * N_HEADS
    off_d = off_c + WIDTH + 2 * KV_WIDTH
    off_gate = off_d + D_COLS
    w_main = jnp.concatenate([w_in[:, :, off_gate:], w_in[:, :, :4 * WIDTH],
                              _d_to_kernel_order(w_in[:, :, off_d:off_gate]),
                              w_in[:, :, 4 * WIDTH:off_if], w_in[:, :, off_c:off_d]], axis=-1).astype(BF16)
    w_if = w_in[:, :, off_if:off_c]

    lb = _lower_bounds(lb_raw)
    lb = lb[jnp.minimum(jnp.arange(DEPTH), lb.shape[0] - 1)]
    lbp = jnp.stack([jnp.log(jnp.maximum(lb, LB_FLOOR)), jnp.log1p(-lb), 1.0 - lb], axis=1)
    lbp = jnp.concatenate([lbp, jnp.zeros((DEPTH, 5, WIDTH), F32)], axis=1)

    w_qk = jnp.stack([jnp.concatenate([_block_diag(list(wq_b[l])), _block_diag(list(wk_b[l])) * HEAD_DIM ** -0.5],
                                      axis=1) for l in range(DEPTH)]).astype(BF16)
    w_wa = jnp.stack([_block_diag([w_up_d[l], a_up_d[l]]) for l in range(DEPTH)]).astype(BF16)
    vec_b = jnp.stack([_pad_rows(jnp.stack([conv_b[l], gn_b[l], skip_b[l]]), 8) for l in range(DEPTH)])
    conv_w8 = jnp.stack([_pad_rows(conv_w[l], 8) for l in range(DEPTH)])
    vec_d = jnp.stack([_pad_rows(jnp.stack([w0_d[l], a0_d[l], k_k_d[l], k_a_d[l], r_k_d[l], gn_w_d[l], gn_b_d[l]]), 8)
                       for l in range(DEPTH)])
    head_of = jnp.arange(WIDTH) // HEAD_DIM
    mask_bd = (head_of[:, None] == head_of[None, :]).astype(F32)

    span = WINDOW + CHUNK
    rel = jnp.arange(span)[None, :] - WINDOW - jnp.arange(CHUNK)[:, None]
    bias = jnp.transpose(rel_bias.astype(F32)[_rel_bucket(rel)], (2, 0, 1))

    n_dense = ffn_w_gate.shape[0]
    ff_steps = D_FF // D_FF_EXPERT
    lp = {
        'w_main': w_main, 'w_if': w_if, 'w_if_t': jnp.swapaxes(w_if, 1, 2), 'lbp': lbp, 'gn_a': gn_a,
        'mask_bd': mask_bd, 'conv_w8': conv_w8, 'vec_b': vec_b, 'w_qk': w_qk,
        'b_if': jnp.concatenate([b_i, b_f], axis=-1), 'sinks': sinks, 'bias': bias,
        'mu_d': _d_to_kernel_order(mu_d), 'vec_d': vec_d, 'w_wa': w_wa, 'g_up_d': g_up_d.astype(BF16),
        'w_br': w_br.astype(BF16), 'w_o': w_o.astype(BF16),
        'ln1_g': ln1_g, 'ln1_b': ln1_b, 'ln2_g': ln2_g, 'ln2_b': ln2_b,
        'ffn_wg': jnp.swapaxes(ffn_w_gate.astype(BF16).reshape(n_dense, D_MODEL, ff_steps, D_FF_EXPERT), 1, 2),
        'ffn_wu': jnp.swapaxes(ffn_w_up.astype(BF16).reshape(n_dense, D_MODEL, ff_steps, D_FF_EXPERT), 1, 2),
        'ffn_wd': ffn_w_down.astype(BF16).reshape(n_dense, ff_steps, D_FF_EXPERT, D_MODEL),
        'router_w': router_w, 'router_b': router_b,
        'exp_wg': exp_w_gate.astype(BF16), 'exp_wu': exp_w_up.astype(BF16), 'exp_wd': exp_w_down.astype(BF16),
    }

    sample_states = {
        'swa_k': cache_swa_k, 'swa_v': cache_swa_v, 'hgrn': state_hgrn, 'mlstm_c': state_mlstm_c,
        'mlstm_n': state_mlstm_n, 'mlstm_m': state_mlstm_m, 'mlstm_conv': state_mlstm_conv,
        'rwkv': state_rwkv, 'rwkv_shift': state_rwkv_shift,
    }
    keep = cache_swa_k.shape[2]
    bp = x_prompt.shape[0]
    prompt_states = {}
    for name in STATE_NAMES:
        arr = sample_states[name]
        rows = (WINDOW,) + arr.shape[3:] if name in ('swa_k', 'swa_v') else arr.shape[2:]
        prompt_states[name] = jnp.zeros((DEPTH, bp) + tuple(rows), arr.dtype)

    y_prompt, pst = _trunk(x_prompt, prompt_states, False, keep, lp)
    y_sample, sst = _trunk(x_sample, sample_states, True, keep, lp)
    return (y_prompt, y_sample) + tuple(pst[n] for n in STATE_NAMES) + tuple(sst[n] for n in STATE_NAMES)
```

```python
import functools
import math

import jax
import jax.numpy as jnp
from jax import lax
from jax.experimental import pallas as pl
from jax.experimental.pallas import tpu as pltpu

F32 = jnp.float32
BF16 = jnp.bfloat16
HIGHEST = lax.Precision.HIGHEST

D_MODEL = 1024
DEPTH = 4
CHUNK = 64
HEAD_DIM = 64
N_HEADS = 4
WIDTH = N_HEADS * HEAD_DIM
C_KV_HEADS = 2
KV_WIDTH = C_KV_HEADS * HEAD_DIM
CONV_W = 4
WINDOW = 128
NUM_BUCKETS = 32
MAX_DISTANCE = 128
D_DECAY_LORA = 64
D_AAA_LORA = 64
D_GATE_LORA = 128
D_COLS = 3 * WIDTH + D_DECAY_LORA + D_AAA_LORA + D_GATE_LORA
N_BRANCH = 4
D_FF = 2816
N_EXPERTS = 8
TOP_K = 2
D_FF_EXPERT = 1408
ALPHA = (2 * DEPTH) ** 0.25
LN_EPS = 1e-5
HEAD_NORM_EPS = 1e-5
RWKV_GN_EPS = 64e-5
LB_FLOOR = 1e-30
LB_CEIL = 1.0 - 1e-6

GATE_COLS = N_BRANCH * D_MODEL
COL_A = 0
COL_D = COL_A + 4 * WIDTH
COL_B = COL_D + D_COLS
COL_C = COL_B + 3 * WIDTH
MAIN_COLS = COL_C + WIDTH + 2 * KV_WIDTH
SUB = 16
STREAMS_PER_STEP = 4
V7X_VMEM_LIMIT = 48 * 1024 * 1024

NN = (((1,), (0,)), ((), ()))
NT = (((1,), (1,)), ((), ()))
TN = (((0,), (0,)), ((), ()))


def _cparams(*sem):
    return pltpu.CompilerParams(dimension_semantics=sem, vmem_limit_bytes=V7X_VMEM_LIMIT)


def _sigmoid(x):
    return 1.0 / (1.0 + jnp.exp(-x))


def _silu(x):
    return x * _sigmoid(x)


def _log_sigmoid(x):
    return jnp.minimum(x, 0.0) - jnp.log1p(jnp.exp(-jnp.abs(x)))


def _layernorm_rows(z, g, b):
    mu = jnp.mean(z, axis=-1, keepdims=True)
    zc = z - mu
    var = jnp.mean(zc * zc, axis=-1, keepdims=True)
    return zc * lax.rsqrt(var + LN_EPS) * g + b


def _split2(x):
    hi = x.astype(BF16)
    lo = (x - hi.astype(F32)).astype(BF16)
    return hi, lo


def _split3(x):
    hi = x.astype(BF16)
    r1 = x - hi.astype(F32)
    mid = r1.astype(BF16)
    lo = (r1 - mid.astype(F32)).astype(BF16)
    return hi, mid, lo


def _dot(a, b, dims=NN):
    return lax.dot_general(a, b, dims, preferred_element_type=F32)


def _mm2(a, b_parts, dims=NN):
    a_hi, a_lo = _split2(a)
    n = a.shape[0]
    both = _dot(jnp.concatenate([a_hi, a_lo], axis=0), b_parts[0], dims)
    return both[:n] + both[n:] + _dot(a_hi, b_parts[1], dims)


def _cumsum_rows(x):
    n = x.shape[0]
    row = lax.broadcasted_iota(jnp.int32, (n, n), 0)
    col = lax.broadcasted_iota(jnp.int32, (n, n), 1)
    tri = (row >= col).astype(BF16)
    hi, mid, lo = _split3(x)
    return _dot(tri, hi) + _dot(tri, mid) + _dot(tri, lo)


def _head_masks():
    lane_head = lax.broadcasted_iota(jnp.int32, (1, WIDTH), 1) // HEAD_DIM
    return [(lane_head == h) for h in range(N_HEADS)]


def _stack_heads(x, masks):
    return jnp.concatenate([jnp.where(mk, x, 0.0) for mk in masks], axis=0)


def _stack_parts(x, masks):
    parts = _split2(x)
    rows = [mk.astype(BF16) for mk in masks]
    return tuple(jnp.concatenate([p * r for r in rows], axis=0) for p in parts)


def _round_robin(stage_generators):
    live = list(stage_generators)
    while live:
        live = [g for g in live if next(g, _DONE) is not _DONE]


_DONE = object()


def _head_sums(x, ones_bd):
    hi, lo = _split2(x)
    return _dot(hi, ones_bd) + _dot(lo, ones_bd)


def _matmul_kernel(x_ref, w_ref, o_ref, *, precision, dims):
    x = x_ref[...]
    w = w_ref[...]
    if precision is None:
        x = x.astype(w.dtype)
    o_ref[...] = lax.dot_general(x, w, dims, preferred_element_type=F32, precision=precision).astype(o_ref.dtype)


def _matmul(x, w, *, tm, tn, out_dtype=F32, precision=None):
    m, k = x.shape
    n = w.shape[1]
    tm = min(tm, m)
    tn = min(tn, n)
    return pl.pallas_call(
        functools.partial(_matmul_kernel, precision=precision, dims=NN),
        grid=(m // tm, n // tn),
        in_specs=[pl.BlockSpec((tm, k), lambda i, j: (i, 0)),
                  pl.BlockSpec((k, tn), lambda i, j: (0, j))],
        out_specs=pl.BlockSpec((tm, tn), lambda i, j: (i, j)),
        out_shape=jax.ShapeDtypeStruct((m, n), out_dtype),
        compiler_params=_cparams("parallel", "parallel"),
        name="matmul",
    )(x, w)


PROJ_TN = 256


def _proj_kernel(x_ref, w_ref, o_ref):
    x = x_ref[...]
    for j in range(MAIN_COLS // PROJ_TN):
        o_ref[:, j * PROJ_TN:(j + 1) * PROJ_TN] = _dot(x, w_ref[:, j * PROJ_TN:(j + 1) * PROJ_TN])


def _proj(xb, w, *, tm=512):
    m, k = xb.shape
    n = w.shape[1]
    tm = min(tm, m)
    return pl.pallas_call(
        _proj_kernel,
        grid=(m // tm,),
        in_specs=[pl.BlockSpec((tm, k), lambda i: (i, 0)),
                  pl.BlockSpec((k, n), lambda i: (0, 0), pipeline_mode=pl.Buffered(1))],
        out_specs=pl.BlockSpec((tm, n), lambda i: (i, 0)),
        out_shape=jax.ShapeDtypeStruct((m, n), F32),
        compiler_params=_cparams("parallel"),
        name="proj",
    )(xb, w)


def _matmul_t(wt, x, *, tm, precision):
    n, k = wt.shape
    m = x.shape[0]
    tm = min(tm, m)
    return pl.pallas_call(
        functools.partial(_matmul_kernel, precision=precision, dims=NT),
        grid=(m // tm,),
        in_specs=[pl.BlockSpec((n, k), lambda i: (0, 0)),
                  pl.BlockSpec((tm, k), lambda i: (i, 0))],
        out_specs=pl.BlockSpec((n, tm), lambda i: (0, i)),
        out_shape=jax.ShapeDtypeStruct((n, m), F32),
        compiler_params=_cparams("parallel"),
        name="matmul_t",
    )(wt, x)


def _head(a, h):
    return a[:, h * HEAD_DIM:(h + 1) * HEAD_DIM]


def _hgrn_kernel(q_ref, f_ref, i_ref, g_ref, lbp_ref, gn_ref, ones_ref, mask_ref, s0_ref, y_ref, s_out_ref,
                 st_ref, phi_ref, plo_ref, *, nc, bb):
    c = pl.program_id(1)

    @pl.when(c == 0)
    def _():
        for bi in range(bb):
            st_ref[bi] = jnp.zeros(st_ref.shape[1:], F32)
            for h in range(N_HEADS):
                st_ref[bi, h * HEAD_DIM:(h + 1) * HEAD_DIM, h * HEAD_DIM:(h + 1) * HEAD_DIM] = s0_ref[bi, h].T

    _round_robin([_hgrn_chunk(q_ref.at[bi], f_ref.at[bi], i_ref.at[bi], g_ref.at[bi], lbp_ref, gn_ref, ones_ref,
                              mask_ref, y_ref.at[bi], st_ref.at[bi], phi_ref.at[bi], plo_ref.at[bi])
                  for bi in range(bb)])

    @pl.when(c == nc - 1)
    def _():
        for bi in range(bb):
            for h in range(N_HEADS):
                s_out_ref[bi, h] = st_ref[bi, h * HEAD_DIM:(h + 1) * HEAD_DIM, h * HEAD_DIM:(h + 1) * HEAD_DIM].T


def _hgrn_chunk(q_ref, f_ref, i_ref, g_ref, lbp_ref, gn_ref, ones_ref, mask_ref, y_ref, st_ref, phi_ref, plo_ref):
    L = CHUNK
    ones_bd = ones_ref[...]
    masks = _head_masks()
    log_lb = lbp_ref[0:1, :]
    log1m_lb = lbp_ref[1:2, :]
    one_m_lb = lbp_ref[2:3, :]
    zf = f_ref[...]
    u = log_lb
    w = log1m_lb + _log_sigmoid(zf)
    logf = jnp.maximum(u, w) + jnp.log1p(jnp.exp(-jnp.abs(u - w)))
    k = one_m_lb * _sigmoid(-zf)
    q = _silu(q_ref[...])
    v = i_ref[...]

    b = _cumsum_rows(logf)
    yield
    b_last = b[L - 1:L]
    qe = q * jnp.exp(b)
    kdec = k * jnp.exp(b_last - b)
    sub_row = lax.broadcasted_iota(jnp.int32, (SUB, WIDTH), 0)

    for s in range(L):
        r0 = (s // SUB) * SUB
        p = q[r0:r0 + SUB] * k[s:s + 1] * jnp.exp(jnp.minimum(b[r0:r0 + SUB] - b[s:s + 1], 0.0))
        hi, lo = _split2(jnp.where(sub_row >= s - r0, p, 0.0))
        phi_ref[s * SUB:(s + 1) * SUB, :] = hi
        plo_ref[s * SUB:(s + 1) * SUB, :] = lo
    yield
    att = _dot(phi_ref[...], ones_bd) + _dot(plo_ref[...], ones_bd)
    st = st_ref[...]
    o_inter = _dot(qe, st, NT)
    upd = _dot(v, kdec, TN)
    scores = []
    for blk in range(1, L // SUB):
        r0 = blk * SUB
        ref_row = b[r0 - 1:r0]
        q_s = q[r0:r0 + SUB] * jnp.exp(b[r0:r0 + SUB] - ref_row)
        k_s = k[:r0] * jnp.exp(ref_row - b[:r0])
        scores.append(_dot(q_s, _stack_heads(k_s, masks), NT))
    yield
    st_ref[...] = st * jnp.exp(b_last) + upd * mask_ref[...]
    o_blocks = []
    for blk in range(L // SUB):
        r0 = blk * SUB
        acc = jnp.zeros((SUB, WIDTH), F32)
        for j in range(SUB):
            s = r0 + j
            acc = acc + att[s * SUB:(s + 1) * SUB] * v[s:s + 1]
        if blk > 0:
            acc = acc + _dot(scores[blk - 1], _stack_heads(v[:r0], masks))
        o_blocks.append(acc)
    yield
    o = jnp.concatenate(o_blocks, axis=0) + o_inter
    ms = _head_sums(o * o, ones_bd) * (1.0 / HEAD_DIM)
    yield
    y_ref[...] = (o * lax.rsqrt(ms + HEAD_NORM_EPS) * gn_ref[...] * _silu(g_ref[...])).astype(y_ref.dtype)


def _hgrn(h3, lbp, gn, mask_bd, s0):
    bsz, t, _ = h3.shape
    nc = t // CHUNK
    cb = COL_A // WIDTH
    bb = STREAMS_PER_STEP

    def col(j):
        return pl.BlockSpec((bb, CHUNK, WIDTH), lambda b, c, j=j: (b, c, cb + j))

    square = pl.BlockSpec((WIDTH, WIDTH), lambda b, c: (0, 0))
    state = pl.BlockSpec((bb, N_HEADS, HEAD_DIM, HEAD_DIM), lambda b, c: (b, 0, 0, 0))
    return pl.pallas_call(
        functools.partial(_hgrn_kernel, nc=nc, bb=bb),
        grid=(bsz // bb, nc),
        in_specs=[col(0), col(1), col(2), col(3),
                  pl.BlockSpec((8, WIDTH), lambda b, c: (0, 0)),
                  pl.BlockSpec((1, WIDTH), lambda b, c: (0, 0)),
                  square, square, state],
        out_specs=[pl.BlockSpec((bb, CHUNK, WIDTH), lambda b, c: (b, c, 0)), state],
        out_shape=[jax.ShapeDtypeStruct((bsz, t, WIDTH), BF16),
                   jax.ShapeDtypeStruct((bsz, N_HEADS, HEAD_DIM, HEAD_DIM), F32)],
        scratch_shapes=[pltpu.VMEM((bb, WIDTH, WIDTH), F32),
                        pltpu.VMEM((bb, CHUNK * SUB, WIDTH), BF16),
                        pltpu.VMEM((bb, CHUNK * SUB, WIDTH), BF16)],
        compiler_params=_cparams("parallel", "arbitrary"),
        name="hgrn2",
    )(h3, h3, h3, h3, lbp, gn, mask_bd.astype(BF16), mask_bd, s0)


def _mlstm_kernel(u_ref, v_ref, o_ref, gcol_ref, grow_ref, bcol_ref, brow_ref, cw_ref, vec_ref, wqk_ref,
                  ones_ref, mask_ref, c0_ref, n0_ref, m0_ref, conv0_ref,
                  y_ref, c_out_ref, n_out_ref, m_out_ref, conv_out_ref,
                  c_ref, n_ref, m_ref, carry_ref, *, nc, bb):
    c = pl.program_id(1)

    @pl.when(c == 0)
    def _():
        for bi in range(bb):
            c_ref[bi] = jnp.zeros(c_ref.shape[1:], F32)
            for h in range(N_HEADS):
                c_ref[bi, h * HEAD_DIM:(h + 1) * HEAD_DIM, h * HEAD_DIM:(h + 1) * HEAD_DIM] = c0_ref[bi, h]
        n_ref[...] = n0_ref[...]
        m_ref[...] = m0_ref[...]
        carry_ref[...] = conv0_ref[...]

    _round_robin([_mlstm_chunk(u_ref.at[bi], v_ref.at[bi], o_ref.at[bi], gcol_ref.at[bi], grow_ref.at[bi, 0],
                               bcol_ref, brow_ref, cw_ref, vec_ref, wqk_ref, ones_ref, mask_ref, y_ref.at[bi],
                               c_ref.at[bi], n_ref.at[bi], m_ref.at[bi], carry_ref.at[bi]) for bi in range(bb)])

    @pl.when(c == nc - 1)
    def _():
        for bi in range(bb):
            for h in range(N_HEADS):
                c_out_ref[bi, h] = c_ref[bi, h * HEAD_DIM:(h + 1) * HEAD_DIM, h * HEAD_DIM:(h + 1) * HEAD_DIM]
        n_out_ref[...] = n_ref[...]
        m_out_ref[...] = m_ref[...]
        conv_out_ref[...] = carry_ref[...]


def _mlstm_chunk(u_ref, v_ref, o_ref, gcol_ref, grow_ref, bcol_ref, brow_ref, cw_ref, vec_ref, wqk_ref,
                 ones_ref, mask_ref, y_ref, c_ref, n_ref, m_ref, carry_ref):
    L = CHUNK
    heads = range(N_HEADS)

    u = u_ref[...]
    carry = carry_ref[...]
    row8 = lax.broadcasted_iota(jnp.int32, (8, WIDTH), 0)
    conv = vec_ref[0:1, :] + cw_ref[CONV_W - 1:CONV_W, :] * u
    for d in range(1, CONV_W):
        rolled = pltpu.roll(u, d, 0)
        top = jnp.where(row8 < d, pltpu.roll(carry, d, 0), rolled[0:8])
        conv = conv + cw_ref[CONV_W - 1 - d:CONV_W - d, :] * jnp.concatenate([top, rolled[8:]], axis=0)
    carry_ref[...] = u[L - 8:L]
    cact = _silu(conv)
    qk_all = _dot(cact.astype(BF16), wqk_ref[...])

    row = lax.broadcasted_iota(jnp.int32, (L, L), 0)
    col = lax.broadcasted_iota(jnp.int32, (L, L), 1)
    lower = row >= col
    raw_c = gcol_ref[...] + brow_ref[...]
    lane8 = lax.broadcasted_iota(jnp.int32, raw_c.shape, 1)
    gcol = jnp.where(lane8 < N_HEADS, raw_c, _log_sigmoid(raw_c))
    raw_r = grow_ref[...] + bcol_ref[...]
    sub8 = lax.broadcasted_iota(jnp.int32, raw_r.shape, 0)
    grow = jnp.where(sub8 < N_HEADS, raw_r, _log_sigmoid(raw_r))
    f_col = jnp.dot(lower.astype(F32), gcol, preferred_element_type=F32, precision=HIGHEST)
    f_row = jnp.dot(grow, (row <= col).astype(F32), preferred_element_type=F32, precision=HIGHEST)
    r8 = lax.broadcasted_iota(jnp.int32, (2 * N_HEADS, 2 * WIDTH), 0)
    c8 = lax.broadcasted_iota(jnp.int32, (2 * N_HEADS, 2 * WIDTH), 1)
    expand = (c8 // HEAD_DIM == r8).astype(BF16)
    x8 = jnp.where(lane8 < N_HEADS, gcol, f_col)
    x_hi, x_mid, x_lo = _split3(x8)
    both = _dot(x_hi, expand) + _dot(x_mid, expand) + _dot(x_lo, expand)
    li_all = both[:, 0:WIDTH]
    f_all = both[:, WIDTH:2 * WIDTH]
    d_row = jnp.concatenate([grow[h:h + 1, :] - f_row[N_HEADS + h:N_HEADS + h + 1, :] for h in heads], axis=1)
    yield
    t_idx = lax.broadcasted_iota(jnp.int32, (L, WIDTH), 0)
    s_idx = lax.broadcasted_iota(jnp.int32, (L, WIDTH), 1) % L
    ones_bd = ones_ref[...]
    masks = _head_masks()
    cm = li_all - f_all
    for sh in (1, 2, 4, 8, 16, 32):
        cm = jnp.maximum(cm, jnp.where(t_idx >= sh, pltpu.roll(cm, sh, 0), -jnp.inf))
    g = f_all + m_ref[...]
    mt = jnp.maximum(g, f_all + cm)
    wg = jnp.exp(g - mt)
    wd = jnp.exp(jnp.where(s_idx <= t_idx, f_all + d_row - mt, -jnp.inf))
    q = qk_all[:, 0:WIDTH]
    k = qk_all[:, WIDTH:2 * WIDTH]
    v = v_ref[...]
    q_bf = q.astype(BF16)
    v_bf = v.astype(BF16)
    head_rows = [mk.astype(BF16) for mk in masks]
    k_stack = jnp.concatenate([k.astype(BF16) * r for r in head_rows], axis=0)
    v_stack = jnp.concatenate([v_bf * r for r in head_rows], axis=0)
    c_bd = c_ref[...]
    n_row = n_ref[...]
    qk = _dot(q_bf, k_stack, NT) * wd
    q_c = _dot(q_bf, c_bd.astype(BF16))
    q_n = _head_sums(q * n_row, ones_bd)
    yield
    qkv = _dot(qk.astype(BF16), v_stack)
    qk_sum = _head_sums(qk, ones_bd)
    mt_last = mt[L - 1:L]
    kw = k * jnp.exp(f_all[L - 1:L] - f_all + li_all - mt_last)
    c_upd = _dot(kw.astype(BF16), v_bf, TN)
    yield
    wgl = wg[L - 1:L]
    c_ref[...] = wgl * c_bd + c_upd * mask_ref[...]
    n_ref[...] = wgl * n_row + jnp.sum(kw, axis=0, keepdims=True)
    m_ref[...] = mt_last
    den = wg * q_n + qk_sum
    hh = (wg * q_c + qkv) / jnp.maximum(jnp.abs(den), jnp.exp(-mt))
    z = _sigmoid(o_ref[...]) * hh
    mu = _head_sums(z, ones_bd) * (1.0 / HEAD_DIM)
    yield
    zc = z - mu
    var = _head_sums(zc * zc, ones_bd) * (1.0 / HEAD_DIM)
    yield
    y_ref[...] = (zc * lax.rsqrt(var + HEAD_NORM_EPS) * vec_ref[1:2, :] + vec_ref[2:3, :] * cact).astype(y_ref.dtype)


def _mlstm(h3, gcol, grow, b_col, b_row, conv_w, vecs, w_qk, mask_bd, c0, n0, m0, conv0):
    bsz, t, _ = h3.shape
    nc = t // CHUNK
    cb = COL_B // WIDTH
    bb = STREAMS_PER_STEP
    state4 = pl.BlockSpec((bb, N_HEADS, HEAD_DIM, HEAD_DIM), lambda b, c: (b, 0, 0, 0))
    state_n = pl.BlockSpec((bb, 1, WIDTH), lambda b, c: (b, 0, 0))
    state_m = state_n
    state_conv = pl.BlockSpec((bb, 8, WIDTH), lambda b, c: (b, 0, 0))

    def const(shape):
        return pl.BlockSpec(shape, lambda b, c: tuple(0 for _ in shape))

    def col(j):
        return pl.BlockSpec((bb, CHUNK, WIDTH), lambda b, c, j=j: (b, c, cb + j))

    return pl.pallas_call(
        functools.partial(_mlstm_kernel, nc=nc, bb=bb),
        grid=(bsz // bb, nc),
        in_specs=[col(0), col(1), col(2),
                  pl.BlockSpec((bb, CHUNK, 2 * N_HEADS), lambda b, c: (b, c, 0)),
                  pl.BlockSpec((bb, 1, 2 * N_HEADS, CHUNK), lambda b, c: (b, c, 0, 0)),
                  const((2 * N_HEADS, 1)), const((1, 2 * N_HEADS)), const((8, WIDTH)), const((8, WIDTH)),
                  const((WIDTH, 2 * WIDTH)), const((WIDTH, WIDTH)), const((WIDTH, WIDTH)),
                  state4, state_n, state_m, state_conv],
        out_specs=[pl.BlockSpec((bb, CHUNK, WIDTH), lambda b, c: (b, c, 0)), state4, state_n, state_m, state_conv],
        out_shape=[jax.ShapeDtypeStruct((bsz, t, WIDTH), BF16),
                   jax.ShapeDtypeStruct((bsz, N_HEADS, HEAD_DIM, HEAD_DIM), F32),
                   jax.ShapeDtypeStruct((bsz, 1, WIDTH), F32),
                   jax.ShapeDtypeStruct((bsz, 1, WIDTH), F32),
                   jax.ShapeDtypeStruct((bsz, 8, WIDTH), F32)],
        scratch_shapes=[pltpu.VMEM((bb, WIDTH, WIDTH), F32),
                        pltpu.VMEM((bb, 1, WIDTH), F32),
                        pltpu.VMEM((bb, 1, WIDTH), F32),
                        pltpu.VMEM((bb, 8, WIDTH), F32)],
        compiler_params=_cparams("parallel", "arbitrary"),
        name="mlstm",
    )(h3, h3, h3, gcol, grow, b_col, b_row, conv_w, vecs, w_qk, mask_bd.astype(BF16), mask_bd, c0, n0, m0, conv0)


SWA_CHUNKS_PER_STEP = 4


def _swa_kernel(*refs, prev_valid, cps, n_kv):
    q_ref = refs[0]
    k_refs = refs[1:1 + n_kv]
    v_refs = refs[1 + n_kv:1 + 2 * n_kv]
    bias_ref, sink_ref, y_ref, s_ref, p_ref = refs[1 + 2 * n_kv:]
    g = pl.program_id(1)
    L = CHUNK
    span = WINDOW + L
    kcat = jnp.concatenate([r[0] for r in k_refs], axis=0).astype(BF16)
    vcat = jnp.concatenate([r[0] for r in v_refs], axis=0).astype(BF16)
    kv_of = [h // (N_HEADS // C_KV_HEADS) for h in range(N_HEADS)]
    for j in range(cps):
        q = q_ref[0, j * L:(j + 1) * L, :].astype(BF16)
        k_j = kcat[j * L:j * L + span]
        for h in range(N_HEADS):
            s_ref[j * N_HEADS + h] = _dot(_head(q, h), _head(k_j, kv_of[h]), NT)
    for j in range(cps):
        key_pos = lax.broadcasted_iota(jnp.int32, (L, span), 1) + (g * cps + j) * L
        for h in range(N_HEADS):
            s = s_ref[j * N_HEADS + h] * HEAD_DIM ** -0.5 + bias_ref[h]
            if not prev_valid:
                s = jnp.where(key_pos >= WINDOW, s, -jnp.inf)
            sink = sink_ref[0:1, h:h + 1]
            m = jnp.maximum(jnp.max(s, axis=1, keepdims=True), sink)
            p = jnp.exp(s - m)
            inv = 1.0 / (jnp.sum(p, axis=1, keepdims=True) + jnp.exp(sink - m))
            p_ref[j * N_HEADS + h] = (p * inv).astype(BF16)
    for j in range(cps):
        v_j = vcat[j * L:j * L + span]
        o = [_dot(p_ref[j * N_HEADS + h], _head(v_j, kv_of[h])) for h in range(N_HEADS)]
        y_ref[0, j * L:(j + 1) * L, :] = jnp.concatenate(o, axis=1).astype(y_ref.dtype)


def _swa(h3, kf, vf, bias, sinks, prev_valid):
    bsz, t, _ = h3.shape
    nc = t // CHUNK
    cps = min(SWA_CHUNKS_PER_STEP, nc)
    rows = cps * CHUNK
    kv_rows = min(rows, WINDOW)
    n_kv = (WINDOW + rows) // kv_rows
    qb = COL_C // WIDTH
    kv = [pl.BlockSpec((1, kv_rows, KV_WIDTH), lambda b, g, j=j: (b, g * (rows // kv_rows) + j, 0))
          for j in range(n_kv)]

    return pl.pallas_call(
        functools.partial(_swa_kernel, prev_valid=prev_valid, cps=cps, n_kv=n_kv),
        grid=(bsz, nc // cps),
        in_specs=[pl.BlockSpec((1, rows, WIDTH), lambda b, g: (b, g, qb))] + kv + kv + [
                  pl.BlockSpec((N_HEADS, CHUNK, WINDOW + CHUNK), lambda b, g: (0, 0, 0)),
                  pl.BlockSpec((1, N_HEADS), lambda b, g: (0, 0))],
        out_specs=pl.BlockSpec((1, rows, WIDTH), lambda b, g: (b, g, 0)),
        out_shape=jax.ShapeDtypeStruct((bsz, t, WIDTH), BF16),
        scratch_shapes=[pltpu.VMEM((cps * N_HEADS, CHUNK, WINDOW + CHUNK), F32),
                        pltpu.VMEM((cps * N_HEADS, CHUNK, WINDOW + CHUNK), BF16)],
        compiler_params=_cparams("parallel", "parallel"),
        name="swa",
    )(h3, *([kf] * n_kv), *([vf] * n_kv), bias, sinks)


def _rwkv_kernel(pd_ref, mu_ref, vec_ref, wwa_ref, gup_ref, ones_ref, mask_ref, s0_ref, sh0_ref,
                 y_ref, s_out_ref, sh_out_ref, st_ref, carry_ref, *, nc, bb):
    c = pl.program_id(1)

    @pl.when(c == 0)
    def _():
        for bi in range(bb):
            st_ref[bi] = jnp.zeros(st_ref.shape[1:], F32)
            for h in range(N_HEADS):
                st_ref[bi, h * HEAD_DIM:(h + 1) * HEAD_DIM, h * HEAD_DIM:(h + 1) * HEAD_DIM] = s0_ref[bi, h]
            carry_ref[bi, 7:8, :] = sh0_ref[bi]

    _round_robin([_rwkv_chunk(pd_ref.at[bi], mu_ref, vec_ref, wwa_ref, gup_ref, ones_ref, mask_ref,
                              y_ref.at[bi], st_ref.at[bi], carry_ref.at[bi]) for bi in range(bb)])

    @pl.when(c == nc - 1)
    def _():
        for bi in range(bb):
            for h in range(N_HEADS):
                s_out_ref[bi, h] = st_ref[bi, h * HEAD_DIM:(h + 1) * HEAD_DIM, h * HEAD_DIM:(h + 1) * HEAD_DIM]
            sh_out_ref[bi] = carry_ref[bi, 7:8, :]


def _rwkv_chunk(pd_ref, mu_ref, vec_ref, wwa_ref, gup_ref, ones_ref, mask_ref, y_ref, st_ref, carry_ref):
    L = CHUNK
    ones_bd = ones_ref[...]
    masks = _head_masks()

    pd = pd_ref[...]
    rolled = pltpu.roll(pd, 1, 0)
    row8 = lax.broadcasted_iota(jnp.int32, (8, D_COLS), 0)
    top = jnp.where(row8 == 0, carry_ref[7:8, :], rolled[0:8])
    pd_prev = jnp.concatenate([top, rolled[8:]], axis=0)
    carry_ref[...] = pd[L - 8:L]
    mixed = pd + (pd_prev - pd) * mu_ref[...]
    r = mixed[:, 0:WIDTH]
    k_raw = mixed[:, WIDTH:2 * WIDTH]
    v = mixed[:, 2 * WIDTH:3 * WIDTH]
    wa = mixed[:, 3 * WIDTH:3 * WIDTH + 2 * D_DECAY_LORA]
    g_in = mixed[:, 3 * WIDTH + 2 * D_DECAY_LORA:D_COLS]
    lane_wa = lax.broadcasted_iota(jnp.int32, wa.shape, 1)
    wa_act = jnp.where(lane_wa < D_DECAY_LORA, jnp.tanh(wa), wa)
    lora = _dot(wa_act.astype(BF16), wwa_ref[...])
    g_d = _dot(_sigmoid(g_in).astype(BF16), gup_ref[...])
    yield
    z = -(vec_ref[0:1, :] + lora[:, 0:WIDTH])
    w_log = -(jnp.maximum(z, 0.0) + jnp.log1p(jnp.exp(-jnp.abs(z)))) - 0.5
    lw = -jnp.exp(w_log)
    a_d = _sigmoid(vec_ref[1:2, :] + lora[:, WIDTH:2 * WIDTH])
    kk = k_raw * vec_ref[2:3, :]
    kk_norm = _head_sums(kk * kk, ones_bd)
    cw = _cumsum_rows(lw)
    yield
    kk = kk * lax.rsqrt(jnp.maximum(kk_norm, 1e-24))
    k = k_raw * (1.0 + (a_d - 1.0) * vec_ref[3:4, :])
    a = -kk
    b = kk * a_d
    g_in_c = jnp.exp(cw)
    g_inv = jnp.exp(-cw)
    at = a * jnp.exp(cw - lw)
    rt = r * g_in_c
    bt = b * g_inv
    kt = k * g_inv
    x = jnp.concatenate([at, rt], axis=0)
    bt_st = _stack_parts(bt, masks)
    kt_st = _stack_parts(kt, masks)
    gram_b = _mm2(x, bt_st, NT)
    gram_k = _mm2(x, kt_st, NT)
    st = st_ref[...]
    u = _mm2(x, _split2(st), NT)
    yield

    t_idx = lax.broadcasted_iota(jnp.int32, (L, WIDTH), 0)
    i_idx = lax.broadcasted_iota(jnp.int32, (L, WIDTH), 1) % L
    strict = i_idx < t_idx
    incl = i_idx <= t_idx
    same_blk = (i_idx // SUB) == (t_idx // SUB)
    eye = (i_idx == t_idx).astype(F32)
    n_all = jnp.where(strict, gram_b[0:L], 0.0)
    m_all = jnp.where(strict, gram_k[0:L], 0.0)
    rb_all = jnp.where(incl, gram_b[L:2 * L], 0.0)
    rk_all = jnp.where(incl, gram_k[L:2 * L], 0.0)
    n_d = jnp.where(same_blk, n_all, 0.0)
    n_off = jnp.where(same_blk, 0.0, n_all)

    v_st = _stack_parts(v, masks)
    x2 = _mm2(n_d, _stack_parts(n_d, masks))
    rhs = u[0:L] + _mm2(m_all, v_st)
    bonus = _head_sums(r * k * vec_ref[4:5, :], ones_bd) * v
    yield
    x2_st = _stack_parts(x2, masks)
    x4 = _mm2(x2, x2_st)
    t_d = eye + n_d
    t_d = t_d + _mm2(t_d, x2_st)
    yield
    x4_st = _stack_parts(x4, masks)
    x8 = _mm2(x4, x4_st)
    t_d = t_d + _mm2(t_d, x4_st)
    yield
    t_d = t_d + _mm2(t_d, _stack_parts(x8, masks))
    yield
    n1 = _mm2(t_d, _stack_parts(n_off, masks))
    yield
    n1_st = _stack_parts(n1, masks)
    n2 = _mm2(n1, n1_st)
    yield
    a2 = eye + n1 + n2 + _mm2(n2, n1_st)
    yield
    t_full = _mm2(a2, _stack_parts(t_d, masks))
    yield
    sa = _mm2(t_full, _stack_parts(rhs, masks))
    yield
    y = u[L:2 * L] + _mm2(rb_all, _stack_parts(sa, masks)) + _mm2(rk_all, v_st)
    sv_hi, sv_lo = _split2(jnp.concatenate([sa, v], axis=0))
    bk_hi, bk_lo = _split2(jnp.concatenate([bt, kt], axis=0))
    upd = _dot(sv_hi, bk_hi, TN) + _dot(sv_lo, bk_hi, TN) + _dot(sv_hi, bk_lo, TN)
    yield
    st_ref[...] = (st + upd * mask_ref[...]) * g_in_c[L - 1:L]

    mu = _head_sums(y, ones_bd) * (1.0 / HEAD_DIM)
    yield
    yc = y - mu
    var = _head_sums(yc * yc, ones_bd) * (1.0 / HEAD_DIM)
    yield
    o = yc * lax.rsqrt(var + RWKV_GN_EPS) * vec_ref[5:6, :] + vec_ref[6:7, :]
    y_ref[...] = ((o + bonus) * g_d).astype(y_ref.dtype)


def _rwkv(h3, mu, vecs, w_wa, g_up, mask_bd, s0, shift0):
    bsz, t, _ = h3.shape
    nc = t // CHUNK
    bb = STREAMS_PER_STEP
    state = pl.BlockSpec((bb, N_HEADS, HEAD_DIM, HEAD_DIM), lambda g, i: (g, 0, 0, 0))
    shift = pl.BlockSpec((bb, 1, D_COLS), lambda g, i: (g, 0, 0))

    def const(shape):
        return pl.BlockSpec(shape, lambda g, i: tuple(0 for _ in shape))

    return pl.pallas_call(
        functools.partial(_rwkv_kernel, nc=nc, bb=bb),
        grid=(bsz // bb, nc),
        in_specs=[pl.BlockSpec((bb, CHUNK, D_COLS), lambda g, i: (g, i, COL_D // D_COLS)),
                  const((1, D_COLS)), const((8, WIDTH)), const((2 * D_DECAY_LORA, 2 * WIDTH)),
                  const((D_GATE_LORA, WIDTH)), const((WIDTH, WIDTH)), const((WIDTH, WIDTH)), state, shift],
        out_specs=[pl.BlockSpec((bb, CHUNK, WIDTH), lambda g, i: (g, i, 0)), state, shift],
        out_shape=[jax.ShapeDtypeStruct((bsz, t, WIDTH), BF16),
                   jax.ShapeDtypeStruct((bsz, N_HEADS, HEAD_DIM, HEAD_DIM), F32),
                   jax.ShapeDtypeStruct((bsz, 1, D_COLS), F32)],
        scratch_shapes=[pltpu.VMEM((bb, WIDTH, WIDTH), F32), pltpu.VMEM((bb, 8, D_COLS), F32)],
        compiler_params=_cparams("parallel", "arbitrary"),
        name="rwkv7",
    )(h3, mu, vecs, w_wa, g_up, mask_bd.astype(BF16), mask_bd, s0, shift0)


def _merge_kernel(ya_ref, yb_ref, yc_ref, yd_ref, xb_ref, wg_ref, wbr_ref, wo_ref, x_ref,
                  lng_ref, lnb_ref, xo_ref, xob_ref):
    ys = (ya_ref, yb_ref, yc_ref, yd_ref)
    xb = xb_ref[...]
    merged = None
    for n in range(N_BRANCH):
        gate = _dot(xb, wg_ref[:, n * D_MODEL:(n + 1) * D_MODEL])
        term = _sigmoid(gate) * _dot(ys[n][...], wbr_ref[n])
        merged = term if merged is None else merged + term
    out = _dot(merged.astype(BF16), wo_ref[...])
    xn = _layernorm_rows(ALPHA * x_ref[...] + out, lng_ref[...], lnb_ref[...])
    xo_ref[...] = xn
    xob_ref[...] = xn.astype(BF16)


def _merge(ys, xb2, w_gate, wbr, wo, x2, lng, lnb, *, tm=512):
    m = x2.shape[0]
    tm = min(tm, m)
    ysp = pl.BlockSpec((tm, WIDTH), lambda i: (i, 0))
    row = pl.BlockSpec((tm, D_MODEL), lambda i: (i, 0))
    vec = pl.BlockSpec((1, D_MODEL), lambda i: (0, 0))
    resident = dict(pipeline_mode=pl.Buffered(1))
    return pl.pallas_call(
        _merge_kernel,
        grid=(m // tm,),
        in_specs=[ysp, ysp, ysp, ysp, row,
                  pl.BlockSpec((D_MODEL, GATE_COLS), lambda i: (0, 0), **resident),
                  pl.BlockSpec((N_BRANCH, WIDTH, D_MODEL), lambda i: (0, 0, 0), **resident),
                  pl.BlockSpec((D_MODEL, D_MODEL), lambda i: (0, 0), **resident),
                  row, vec, vec],
        out_specs=[row, row],
        out_shape=[jax.ShapeDtypeStruct((m, D_MODEL), F32), jax.ShapeDtypeStruct((m, D_MODEL), BF16)],
        compiler_params=_cparams("parallel"),
        name="merge",
    )(*ys, xb2, w_gate, wbr, wo, x2, lng, lnb)


def _ffn_kernel(*refs, n_steps, gated):
    if gated:
        xb_ref, x_ref, gates_ref, wg_ref, wu_ref, wd_ref, lng_ref, lnb_ref, xo_ref, xob_ref, acc_ref = refs
    else:
        xb_ref, x_ref, wg_ref, wu_ref, wd_ref, lng_ref, lnb_ref, xo_ref, xob_ref, acc_ref = refs
    j = pl.program_id(1)

    @pl.when(j == 0)
    def _():
        acc_ref[...] = jnp.zeros_like(acc_ref)

    xb = xb_ref[...]
    hg = _dot(xb, wg_ref[0])
    hu = _dot(xb, wu_ref[0])
    part = _dot((_silu(hg) * hu).astype(BF16), wd_ref[0])
    if gated:
        gates = gates_ref[...]
        lane = lax.broadcasted_iota(jnp.int32, gates.shape, 1)
        part = jnp.sum(jnp.where(lane == j, gates, 0.0), axis=1, keepdims=True) * part
    acc_ref[...] += part

    @pl.when(j == n_steps - 1)
    def _():
        xn = _layernorm_rows(ALPHA * x_ref[...] + acc_ref[...], lng_ref[...], lnb_ref[...])
        xo_ref[...] = xn
        xob_ref[...] = xn.astype(BF16)


def _ffn(xb, x2, wg, wu, wd, lng, lnb, gates=None, *, tm=512):
    m = x2.shape[0]
    tm = min(tm, m)
    n_steps, _, tf = wg.shape
    row = pl.BlockSpec((tm, D_MODEL), lambda i, j: (i, 0))
    vec = pl.BlockSpec((1, D_MODEL), lambda i, j: (0, 0))
    w_in_spec = pl.BlockSpec((1, D_MODEL, tf), lambda i, j: (j, 0, 0))
    w_out_spec = pl.BlockSpec((1, tf, D_MODEL), lambda i, j: (j, 0, 0))
    in_specs = [row, row]
    args = [xb, x2]
    if gates is not None:
        in_specs.append(pl.BlockSpec((tm, N_EXPERTS), lambda i, j: (i, 0)))
        args.append(gates)
    in_specs += [w_in_spec, w_in_spec, w_out_spec, vec, vec]
    args += [wg, wu, wd, lng, lnb]
    return pl.pallas_call(
        functools.partial(_ffn_kernel, n_steps=n_steps, gated=gates is not None),
        grid=(m // tm, n_steps),
        in_specs=in_specs,
        out_specs=[row, row],
        out_shape=[jax.ShapeDtypeStruct((m, D_MODEL), F32), jax.ShapeDtypeStruct((m, D_MODEL), BF16)],
        scratch_shapes=[pltpu.VMEM((tm, D_MODEL), F32)],
        compiler_params=_cparams("parallel", "arbitrary"),
        name="moe" if gates is not None else "ffn",
    )(*args)


STATE_NAMES = ('swa_k', 'swa_v', 'hgrn', 'mlstm_c', 'mlstm_n', 'mlstm_m', 'mlstm_conv', 'rwkv', 'rwkv_shift')

_D_ORIG = (('r', WIDTH), ('w', D_DECAY_LORA), ('k', WIDTH), ('v', WIDTH), ('a', D_AAA_LORA), ('g', D_GATE_LORA))
_D_KERNEL = ('r', 'k', 'v', 'w', 'a', 'g')


def _d_pieces(arr):
    out, off = {}, 0
    for name, size in _D_ORIG:
        out[name] = arr[..., off:off + size]
        off += size
    return out


def _d_to_kernel_order(arr):
    p = _d_pieces(arr)
    return jnp.concatenate([p[n] for n in _D_KERNEL], axis=-1)


def _d_to_original_order(arr):
    sizes = dict(_D_ORIG)
    p, off = {}, 0
    for name in _D_KERNEL:
        p[name] = arr[..., off:off + sizes[name]]
        off += sizes[name]
    return jnp.concatenate([p[n] for n, _ in _D_ORIG], axis=-1)


def _rel_bucket(rel):
    half = NUM_BUCKETS // 2
    exact = half // 2
    dist = jnp.abs(rel)
    far = exact + (jnp.log(jnp.maximum(dist, 1).astype(F32) / exact)
                   / math.log(MAX_DISTANCE / exact) * (half - exact)).astype(jnp.int32)
    far = jnp.minimum(far, half - 1)
    return jnp.where(rel > 0, half, 0) + jnp.where(dist < exact, dist, far)


def _lower_bounds(lb_raw):
    sm = jax.nn.softmax(lb_raw.astype(F32), axis=0)
    lb = jnp.concatenate([jnp.zeros_like(sm[:1]), jnp.cumsum(sm[1:], axis=0)[:-1]], axis=0)
    return jnp.clip(lb, 0.0, LB_CEIL)


def _block_diag(blocks):
    rows = sum(b.shape[0] for b in blocks)
    cols = sum(b.shape[1] for b in blocks)
    out = jnp.zeros((rows, cols), blocks[0].dtype)
    r = c = 0
    for b in blocks:
        out = out.at[r:r + b.shape[0], c:c + b.shape[1]].set(b)
        r += b.shape[0]
        c += b.shape[1]
    return out


def _pad_rows(a, rows):
    return jnp.concatenate([a, jnp.zeros((rows - a.shape[0],) + a.shape[1:], a.dtype)], axis=0)


def _mixer(x3, xb2, st, prev_valid, lp, l):
    bsz, t, _ = x3.shape
    m = bsz * t
    x2 = x3.reshape(m, D_MODEL)
    h2 = _proj(xb2, lp['w_main'][l])
    gif = _matmul(x2, lp['w_if'][l], tm=1024, tn=2 * N_HEADS, precision=HIGHEST)
    gif_t = _matmul_t(lp['w_if_t'][l], x2, tm=1024, precision=HIGHEST)
    h3 = h2.reshape(bsz, t, MAIN_COLS)

    y_a, s_hgrn = _hgrn(h3, lp['lbp'][l], lp['gn_a'][l][None], lp['mask_bd'], st['hgrn'])

    grow = jnp.transpose(gif_t.reshape(2 * N_HEADS, bsz, t // CHUNK, CHUNK), (1, 2, 0, 3))
    conv0 = jnp.concatenate([jnp.zeros((bsz, 8 - (CONV_W - 1), WIDTH), F32), st['mlstm_conv']], axis=1)
    y_b, mc, mn, mm, conv8 = _mlstm(h3, gif.reshape(bsz, t, 2 * N_HEADS), grow, lp['b_if'][l][:, None],
                                    lp['b_if'][l][None, :], lp['conv_w8'][l], lp['vec_b'][l], lp['w_qk'][l],
                                    lp['mask_bd'], st['mlstm_c'], st['mlstm_n'].reshape(bsz, 1, WIDTH),
                                    jnp.repeat(st['mlstm_m'], HEAD_DIM, axis=-1)[:, None, :], conv0)
    mn = mn.reshape(bsz, N_HEADS, HEAD_DIM)
    mm = mm[:, :, ::HEAD_DIM]
    conv_state = conv8[:, 8 - (CONV_W - 1):]

    k_off = COL_C + WIDTH
    kf = jnp.concatenate([st['swa_k'].reshape(bsz, WINDOW, KV_WIDTH), h3[..., k_off:k_off + KV_WIDTH]], axis=1)
    vf = jnp.concatenate([st['swa_v'].reshape(bsz, WINDOW, KV_WIDTH), h3[..., k_off + KV_WIDTH:MAIN_COLS]], axis=1)
    y_c = _swa(h3, kf, vf, lp['bias'], lp['sinks'][l][None], prev_valid)
    keep = st['keep']
    k_win = kf[:, -keep:].reshape(bsz, keep, C_KV_HEADS, HEAD_DIM)
    v_win = vf[:, -keep:].reshape(bsz, keep, C_KV_HEADS, HEAD_DIM)

    y_d, s_rwkv, shift = _rwkv(h3, lp['mu_d'][l][None], lp['vec_d'][l], lp['w_wa'][l], lp['g_up_d'][l],
                               lp['mask_bd'], st['rwkv'], _d_to_kernel_order(st['rwkv_shift'])[:, None, :])
    shift_state = _d_to_original_order(shift[:, 0, :])

    ys = [y.reshape(m, WIDTH) for y in (y_a, y_b, y_c, y_d)]
    x1, x1b = _merge(ys, xb2, lp['w_gate'][l], lp['w_br'][l], lp['w_o'][l], x2,
                     lp['ln1_g'][l][None], lp['ln1_b'][l][None])
    new_st = {'swa_k': k_win, 'swa_v': v_win, 'hgrn': s_hgrn, 'mlstm_c': mc, 'mlstm_n': mn,
              'mlstm_m': mm[:, 0, :], 'mlstm_conv': conv_state, 'rwkv': s_rwkv, 'rwkv_shift': shift_state}
    return x1, x1b, new_st


def _trunk(x3, states, prev_valid, keep, lp):
    bsz, t, _ = x3.shape
    m = bsz * t
    xb2 = x3.reshape(m, D_MODEL).astype(BF16)
    collected = {name: [] for name in STATE_NAMES}
    for l in range(DEPTH):
        st = {name: states[name][l] for name in STATE_NAMES}
        st['keep'] = keep
        x1, x1b, new_st = _mixer(x3, xb2, st, prev_valid, lp, l)
        j = l // 2
        ln_g, ln_b = lp['ln2_g'][l][None], lp['ln2_b'][l][None]
        if l % 2 == 0:
            x2, xb2 = _ffn(x1b, x1, lp['ffn_wg'][j], lp['ffn_wu'][j], lp['ffn_wd'][j], ln_g, ln_b)
        else:
            logits = _matmul(x1, lp['router_w'][j], tm=1024, tn=N_EXPERTS, precision=HIGHEST) + lp['router_b'][j]
            top_v, top_i = lax.top_k(logits, TOP_K)
            probs = jax.nn.softmax(top_v, axis=-1)
            gates = jnp.einsum('mk,mke->me', probs, jax.nn.one_hot(top_i, N_EXPERTS, dtype=F32))
            x2, xb2 = _ffn(x1b, x1, lp['exp_wg'][j], lp['exp_wu'][j], lp['exp_wd'][j], ln_g, ln_b, gates)
        x3 = x2.reshape(bsz, t, D_MODEL)
        for name in STATE_NAMES:
            collected[name].append(new_st[name])
    return x3, {name: jnp.stack(collected[name]) for name in STATE_NAMES}


def kernel(x_prompt, x_sample, cache_swa_k, cache_swa_v, state_hgrn, state_mlstm_c, state_mlstm_n, state_mlstm_m, state_mlstm_conv, state_rwkv, state_rwkv_shift, w_in, lb_raw, gn_a, conv_w, conv_b, wq_b, wk_b, b_i, b_f, gn_b, skip_b, sinks, rel_bias, mu_d, w0_d, w_up_d, a0_d, a_up_d, g_up_d, k_k_d, k_a_d, r_k_d, gn_w_d, gn_b_d, w_br, w_o, ln1_g, ln1_b, ln2_g, ln2_b, ffn_w_gate, ffn_w_up, ffn_w_down, router_w, router_b, exp_w_gate, exp_w_up, exp_w_down):
    off_if = 4 * WIDTH + 3 * WIDTH
    off_c = off_if + 2 * N_HEADS
    off_d = off_c + WIDTH + 2 * KV_WIDTH
    off_gate = off_d + D_COLS
    w_main = jnp.concatenate([w_in[:, :, :4 * WIDTH], _d_to_kernel_order(w_in[:, :, off_d:off_gate]),
                              w_in[:, :, 4 * WIDTH:off_if], w_in[:, :, off_c:off_d]], axis=-1).astype(BF16)
    w_gate = w_in[:, :, off_gate:].astype(BF16)
    w_if = w_in[:, :, off_if:off_c]

    lb = _lower_bounds(lb_raw)
    lb = lb[jnp.minimum(jnp.arange(DEPTH), lb.shape[0] - 1)]
    lbp = jnp.stack([jnp.log(jnp.maximum(lb, LB_FLOOR)), jnp.log1p(-lb), 1.0 - lb], axis=1)
    lbp = jnp.concatenate([lbp, jnp.zeros((DEPTH, 5, WIDTH), F32)], axis=1)

    w_qk = jnp.stack([jnp.concatenate([_block_diag(list(wq_b[l])), _block_diag(list(wk_b[l])) * HEAD_DIM ** -0.5],
                                      axis=1) for l in range(DEPTH)]).astype(BF16)
    w_wa = jnp.stack([_block_diag([w_up_d[l], a_up_d[l]]) for l in range(DEPTH)]).astype(BF16)
    vec_b = jnp.stack([_pad_rows(jnp.stack([conv_b[l], gn_b[l], skip_b[l]]), 8) for l in range(DEPTH)])
    conv_w8 = jnp.stack([_pad_rows(conv_w[l], 8) for l in range(DEPTH)])
    vec_d = jnp.stack([_pad_rows(jnp.stack([w0_d[l], a0_d[l], k_k_d[l], k_a_d[l], r_k_d[l], gn_w_d[l], gn_b_d[l]]), 8)
                       for l in range(DEPTH)])
    head_of = jnp.arange(WIDTH) // HEAD_DIM
    mask_bd = (head_of[:, None] == head_of[None, :]).astype(F32)

    span = WINDOW + CHUNK
    rel = jnp.arange(span)[None, :] - WINDOW - jnp.arange(CHUNK)[:, None]
    bias = jnp.transpose(rel_bias.astype(F32)[_rel_bucket(rel)], (2, 0, 1))

    n_dense = ffn_w_gate.shape[0]
    ff_steps = D_FF // D_FF_EXPERT
    lp = {
        'w_main': w_main, 'w_gate': w_gate, 'w_if': w_if, 'w_if_t': jnp.swapaxes(w_if, 1, 2), 'lbp': lbp, 'gn_a': gn_a,
        'mask_bd': mask_bd, 'conv_w8': conv_w8, 'vec_b': vec_b, 'w_qk': w_qk,
        'b_if': jnp.concatenate([b_i, b_f], axis=-1), 'sinks': sinks, 'bias': bias,
        'mu_d': _d_to_kernel_order(mu_d), 'vec_d': vec_d, 'w_wa': w_wa, 'g_up_d': g_up_d.astype(BF16),
        'w_br': w_br.astype(BF16), 'w_o': w_o.astype(BF16),
        'ln1_g': ln1_g, 'ln1_b': ln1_b, 'ln2_g': ln2_g, 'ln2_b': ln2_b,
        'ffn_wg': jnp.swapaxes(ffn_w_gate.astype(BF16).reshape(n_dense, D_MODEL, ff_steps, D_FF_EXPERT), 1, 2),
        'ffn_wu': jnp.swapaxes(ffn_w_up.astype(BF16).reshape(n_dense, D_MODEL, ff_steps, D_FF_EXPERT), 1, 2),
        'ffn_wd': ffn_w_down.astype(BF16).reshape(n_dense, ff_steps, D_FF_EXPERT, D_MODEL),
        'router_w': router_w, 'router_b': router_b,
        'exp_wg': exp_w_gate.astype(BF16), 'exp_wu': exp_w_up.astype(BF16), 'exp_wd': exp_w_down.astype(BF16),
    }

    sample_states = {
        'swa_k': cache_swa_k, 'swa_v': cache_swa_v, 'hgrn': state_hgrn, 'mlstm_c': state_mlstm_c,
        'mlstm_n': state_mlstm_n, 'mlstm_m': state_mlstm_m, 'mlstm_conv': state_mlstm_conv,
        'rwkv': state_rwkv, 'rwkv_shift': state_rwkv_shift,
    }
    keep = cache_swa_k.shape[2]
    bp = x_prompt.shape[0]
    prompt_states = {}
    for name in STATE_NAMES:
        arr = sample_states[name]
        rows = (WINDOW,) + arr.shape[3:] if name in ('swa_k', 'swa_v') else arr.shape[2:]
        prompt_states[name] = jnp.zeros((DEPTH, bp) + tuple(rows), arr.dtype)

    y_prompt, pst = _trunk(x_prompt, prompt_states, False, keep, lp)
    y_sample, sst = _trunk(x_sample, sample_states, True, keep, lp)
    return (y_prompt, y_sample) + tuple(pst[n] for n in STATE_NAMES) + tuple(sst[n] for n in STATE_NAMES)
```

```python
import functools
import math

import jax
import jax.numpy as jnp
from jax import lax
from jax.experimental import pallas as pl
from jax.experimental.pallas import tpu as pltpu

F32 = jnp.float32
BF16 = jnp.bfloat16
HIGHEST = lax.Precision.HIGHEST

D_MODEL = 1024
DEPTH = 4
CHUNK = 64
HEAD_DIM = 64
N_HEADS = 4
WIDTH = N_HEADS * HEAD_DIM
C_KV_HEADS = 2
KV_WIDTH = C_KV_HEADS * HEAD_DIM
CONV_W = 4
WINDOW = 128
NUM_BUCKETS = 32
MAX_DISTANCE = 128
D_DECAY_LORA = 64
D_AAA_LORA = 64
D_GATE_LORA = 128
D_COLS = 3 * WIDTH + D_DECAY_LORA + D_AAA_LORA + D_GATE_LORA
N_BRANCH = 4
D_FF = 2816
N_EXPERTS = 8
TOP_K = 2
D_FF_EXPERT = 1408
ALPHA = (2 * DEPTH) ** 0.25
LN_EPS = 1e-5
HEAD_NORM_EPS = 1e-5
RWKV_GN_EPS = 64e-5
LB_FLOOR = 1e-30
LB_CEIL = 1.0 - 1e-6

GATE_COLS = N_BRANCH * D_MODEL
COL_A = 0
COL_D = COL_A + 4 * WIDTH
COL_B = COL_D + D_COLS
COL_C = COL_B + 3 * WIDTH
COL_IF = COL_C + WIDTH + 2 * KV_WIDTH
LANE_TILE = 128
MAIN_COLS = COL_IF + LANE_TILE
SUB = 16
STREAMS_PER_STEP = 4
V7X_VMEM_LIMIT = 48 * 1024 * 1024

NN = (((1,), (0,)), ((), ()))
NT = (((1,), (1,)), ((), ()))
TN = (((0,), (0,)), ((), ()))


def _cparams(*sem):
    return pltpu.CompilerParams(dimension_semantics=sem, vmem_limit_bytes=V7X_VMEM_LIMIT)


def _sigmoid(x):
    return 1.0 / (1.0 + jnp.exp(-x))


def _silu(x):
    return x * _sigmoid(x)


def _log_sigmoid(x):
    return jnp.minimum(x, 0.0) - jnp.log1p(jnp.exp(-jnp.abs(x)))


def _layernorm_rows(z, g, b):
    mu = jnp.mean(z, axis=-1, keepdims=True)
    zc = z - mu
    var = jnp.mean(zc * zc, axis=-1, keepdims=True)
    return zc * lax.rsqrt(var + LN_EPS) * g + b


def _split2(x):
    hi = x.astype(BF16)
    lo = (x - hi.astype(F32)).astype(BF16)
    return hi, lo


def _split3(x):
    hi = x.astype(BF16)
    r1 = x - hi.astype(F32)
    mid = r1.astype(BF16)
    lo = (r1 - mid.astype(F32)).astype(BF16)
    return hi, mid, lo


def _dot(a, b, dims=NN):
    return lax.dot_general(a, b, dims, preferred_element_type=F32)


def _mm2(a, b_parts, dims=NN):
    a_hi, a_lo = _split2(a)
    n = a.shape[0]
    both = _dot(jnp.concatenate([a_hi, a_lo], axis=0), b_parts[0], dims)
    return both[:n] + both[n:] + _dot(a_hi, b_parts[1], dims)


def _cumsum_rows(x):
    n = x.shape[0]
    row = lax.broadcasted_iota(jnp.int32, (n, n), 0)
    col = lax.broadcasted_iota(jnp.int32, (n, n), 1)
    tri = (row >= col).astype(BF16)
    hi, mid, lo = _split3(x)
    return _dot(tri, hi) + _dot(tri, mid) + _dot(tri, lo)


def _head_masks():
    lane_head = lax.broadcasted_iota(jnp.int32, (1, WIDTH), 1) // HEAD_DIM
    return [(lane_head == h) for h in range(N_HEADS)]


def _stack_heads(x, masks):
    return jnp.concatenate([jnp.where(mk, x, 0.0) for mk in masks], axis=0)


def _stack_parts(x, masks):
    parts = _split2(x)
    rows = [mk.astype(BF16) for mk in masks]
    return tuple(jnp.concatenate([p * r for r in rows], axis=0) for p in parts)


def _round_robin(stage_generators):
    live = list(stage_generators)
    while live:
        live = [g for g in live if next(g, _DONE) is not _DONE]


_DONE = object()


def _head_sums(x, ones_bd):
    hi, lo = _split2(x)
    return _dot(hi, ones_bd) + _dot(lo, ones_bd)


def _matmul_kernel(x_ref, w_ref, o_ref):
    o_ref[...] = _dot(x_ref[...], w_ref[...])


def _matmul(x, w, *, tm, tn):
    m, k = x.shape
    n = w.shape[1]
    tm = min(tm, m)
    tn = min(tn, n)
    return pl.pallas_call(
        _matmul_kernel,
        grid=(m // tm, n // tn),
        in_specs=[pl.BlockSpec((tm, k), lambda i, j: (i, 0)),
                  pl.BlockSpec((k, tn), lambda i, j: (0, j))],
        out_specs=pl.BlockSpec((tm, tn), lambda i, j: (i, j)),
        out_shape=jax.ShapeDtypeStruct((m, n), F32),
        compiler_params=_cparams("parallel", "parallel"),
        name="matmul",
    )(x, w)


PROJ_TN = 256


def _proj_kernel(x_ref, w_ref, o_ref):
    x = x_ref[...]
    n = w_ref.shape[1]
    for lo in range(0, n, PROJ_TN):
        hi = min(lo + PROJ_TN, n)
        o_ref[:, lo:hi] = _dot(x, w_ref[:, lo:hi])


def _proj(xb, w, *, tm=512):
    m, k = xb.shape
    n = w.shape[1]
    tm = min(tm, m)
    return pl.pallas_call(
        _proj_kernel,
        grid=(m // tm,),
        in_specs=[pl.BlockSpec((tm, k), lambda i: (i, 0)),
                  pl.BlockSpec((k, n), lambda i: (0, 0), pipeline_mode=pl.Buffered(1))],
        out_specs=pl.BlockSpec((tm, n), lambda i: (i, 0)),
        out_shape=jax.ShapeDtypeStruct((m, n), F32),
        compiler_params=_cparams("parallel"),
        name="proj",
    )(xb, w)


def _head(a, h):
    return a[:, h * HEAD_DIM:(h + 1) * HEAD_DIM]


def _hgrn_kernel(q_ref, f_ref, i_ref, g_ref, lbp_ref, gn_ref, ones_ref, mask_ref, s0_ref, y_ref, s_out_ref,
                 st_ref, phi_ref, plo_ref, *, nc, bb):
    c = pl.program_id(1)

    @pl.when(c == 0)
    def _():
        for bi in range(bb):
            st_ref[bi] = jnp.zeros(st_ref.shape[1:], F32)
            for h in range(N_HEADS):
                st_ref[bi, h * HEAD_DIM:(h + 1) * HEAD_DIM, h * HEAD_DIM:(h + 1) * HEAD_DIM] = s0_ref[bi, h].T

    _round_robin([_hgrn_chunk(q_ref.at[bi], f_ref.at[bi], i_ref.at[bi], g_ref.at[bi], lbp_ref, gn_ref, ones_ref,
                              mask_ref, y_ref.at[bi], st_ref.at[bi], phi_ref.at[bi], plo_ref.at[bi])
                  for bi in range(bb)])

    @pl.when(c == nc - 1)
    def _():
        for bi in range(bb):
            for h in range(N_HEADS):
                s_out_ref[bi, h] = st_ref[bi, h * HEAD_DIM:(h + 1) * HEAD_DIM, h * HEAD_DIM:(h + 1) * HEAD_DIM].T


def _hgrn_chunk(q_ref, f_ref, i_ref, g_ref, lbp_ref, gn_ref, ones_ref, mask_ref, y_ref, st_ref, phi_ref, plo_ref):
    L = CHUNK
    ones_bd = ones_ref[...]
    masks = _head_masks()
    log_lb = lbp_ref[0:1, :]
    log1m_lb = lbp_ref[1:2, :]
    one_m_lb = lbp_ref[2:3, :]
    zf = f_ref[...]
    u = log_lb
    w = log1m_lb + _log_sigmoid(zf)
    logf = jnp.maximum(u, w) + jnp.log1p(jnp.exp(-jnp.abs(u - w)))
    k = one_m_lb * _sigmoid(-zf)
    q = _silu(q_ref[...])
    v = i_ref[...]

    b = _cumsum_rows(logf)
    yield
    b_last = b[L - 1:L]
    qe = q * jnp.exp(b)
    kdec = k * jnp.exp(b_last - b)
    sub_row = lax.broadcasted_iota(jnp.int32, (SUB, WIDTH), 0)

    for s in range(L):
        r0 = (s // SUB) * SUB
        p = q[r0:r0 + SUB] * k[s:s + 1] * jnp.exp(jnp.minimum(b[r0:r0 + SUB] - b[s:s + 1], 0.0))
        hi, lo = _split2(jnp.where(sub_row >= s - r0, p, 0.0))
        phi_ref[s * SUB:(s + 1) * SUB, :] = hi
        plo_ref[s * SUB:(s + 1) * SUB, :] = lo
    yield
    att = _dot(phi_ref[...], ones_bd) + _dot(plo_ref[...], ones_bd)
    st = st_ref[...]
    o_inter = _dot(qe, st, NT)
    upd = _dot(v, kdec, TN)
    scores = []
    for blk in range(1, L // SUB):
        r0 = blk * SUB
        ref_row = b[r0 - 1:r0]
        q_s = q[r0:r0 + SUB] * jnp.exp(b[r0:r0 + SUB] - ref_row)
        k_s = k[:r0] * jnp.exp(ref_row - b[:r0])
        scores.append(_dot(q_s, _stack_heads(k_s, masks), NT))
    yield
    st_ref[...] = st * jnp.exp(b_last) + upd * mask_ref[...]
    o_blocks = []
    for blk in range(L // SUB):
        r0 = blk * SUB
        acc = jnp.zeros((SUB, WIDTH), F32)
        for j in range(SUB):
            s = r0 + j
            acc = acc + att[s * SUB:(s + 1) * SUB] * v[s:s + 1]
        if blk > 0:
            acc = acc + _dot(scores[blk - 1], _stack_heads(v[:r0], masks))
        o_blocks.append(acc)
    yield
    o = jnp.concatenate(o_blocks, axis=0) + o_inter
    ms = _head_sums(o * o, ones_bd) * (1.0 / HEAD_DIM)
    yield
    y_ref[...] = (o * lax.rsqrt(ms + HEAD_NORM_EPS) * gn_ref[...] * _silu(g_ref[...])).astype(y_ref.dtype)


def _hgrn(h3, lbp, gn, mask_bd, s0):
    bsz, t, _ = h3.shape
    nc = t // CHUNK
    cb = COL_A // WIDTH
    bb = STREAMS_PER_STEP

    def col(j):
        return pl.BlockSpec((bb, CHUNK, WIDTH), lambda b, c, j=j: (b, c, cb + j))

    square = pl.BlockSpec((WIDTH, WIDTH), lambda b, c: (0, 0))
    state = pl.BlockSpec((bb, N_HEADS, HEAD_DIM, HEAD_DIM), lambda b, c: (b, 0, 0, 0))
    return pl.pallas_call(
        functools.partial(_hgrn_kernel, nc=nc, bb=bb),
        grid=(bsz // bb, nc),
        in_specs=[col(0), col(1), col(2), col(3),
                  pl.BlockSpec((8, WIDTH), lambda b, c: (0, 0)),
                  pl.BlockSpec((1, WIDTH), lambda b, c: (0, 0)),
                  square, square, state],
        out_specs=[pl.BlockSpec((bb, CHUNK, WIDTH), lambda b, c: (b, c, 0)), state],
        out_shape=[jax.ShapeDtypeStruct((bsz, t, WIDTH), BF16),
                   jax.ShapeDtypeStruct((bsz, N_HEADS, HEAD_DIM, HEAD_DIM), F32)],
        scratch_shapes=[pltpu.VMEM((bb, WIDTH, WIDTH), F32),
                        pltpu.VMEM((bb, CHUNK * SUB, WIDTH), BF16),
                        pltpu.VMEM((bb, CHUNK * SUB, WIDTH), BF16)],
        compiler_params=_cparams("parallel", "arbitrary"),
        name="hgrn2",
    )(h3, h3, h3, h3, lbp, gn, mask_bd.astype(BF16), mask_bd, s0)


def _mlstm_kernel(u_ref, v_ref, o_ref, gcol_ref, grow_ref, bcol_ref, brow_ref, cw_ref, vec_ref, wqk_ref,
                  ones_ref, mask_ref, c0_ref, n0_ref, m0_ref, conv0_ref,
                  y_ref, c_out_ref, n_out_ref, m_out_ref, conv_out_ref,
                  c_ref, n_ref, m_ref, carry_ref, *, nc, bb):
    c = pl.program_id(1)

    @pl.when(c == 0)
    def _():
        for bi in range(bb):
            c_ref[bi] = jnp.zeros(c_ref.shape[1:], F32)
            for h in range(N_HEADS):
                c_ref[bi, h * HEAD_DIM:(h + 1) * HEAD_DIM, h * HEAD_DIM:(h + 1) * HEAD_DIM] = c0_ref[bi, h]
        n_ref[...] = n0_ref[...]
        m_ref[...] = m0_ref[...]
        carry_ref[...] = conv0_ref[...]

    _round_robin([_mlstm_chunk(u_ref.at[bi], v_ref.at[bi], o_ref.at[bi], gcol_ref.at[bi], grow_ref.at[bi, 0],
                               bcol_ref, brow_ref, cw_ref, vec_ref, wqk_ref, ones_ref, mask_ref, y_ref.at[bi],
                               c_ref.at[bi], n_ref.at[bi], m_ref.at[bi], carry_ref.at[bi]) for bi in range(bb)])

    @pl.when(c == nc - 1)
    def _():
        for bi in range(bb):
            for h in range(N_HEADS):
                c_out_ref[bi, h] = c_ref[bi, h * HEAD_DIM:(h + 1) * HEAD_DIM, h * HEAD_DIM:(h + 1) * HEAD_DIM]
        n_out_ref[...] = n_ref[...]
        m_out_ref[...] = m_ref[...]
        conv_out_ref[...] = carry_ref[...]


def _mlstm_chunk(u_ref, v_ref, o_ref, gcol_ref, grow_ref, bcol_ref, brow_ref, cw_ref, vec_ref, wqk_ref,
                 ones_ref, mask_ref, y_ref, c_ref, n_ref, m_ref, carry_ref):
    L = CHUNK
    heads = range(N_HEADS)

    u = u_ref[...]
    carry = carry_ref[...]
    row8 = lax.broadcasted_iota(jnp.int32, (8, WIDTH), 0)
    conv = vec_ref[0:1, :] + cw_ref[CONV_W - 1:CONV_W, :] * u
    for d in range(1, CONV_W):
        rolled = pltpu.roll(u, d, 0)
        top = jnp.where(row8 < d, pltpu.roll(carry, d, 0), rolled[0:8])
        conv = conv + cw_ref[CONV_W - 1 - d:CONV_W - d, :] * jnp.concatenate([top, rolled[8:]], axis=0)
    carry_ref[...] = u[L - 8:L]
    cact = _silu(conv)
    qk_all = _dot(cact.astype(BF16), wqk_ref[...])

    row = lax.broadcasted_iota(jnp.int32, (L, L), 0)
    col = lax.broadcasted_iota(jnp.int32, (L, L), 1)
    lower = row >= col
    raw_c = gcol_ref[:, 0:2 * N_HEADS] + brow_ref[...]
    lane8 = lax.broadcasted_iota(jnp.int32, raw_c.shape, 1)
    gcol = jnp.where(lane8 < N_HEADS, raw_c, _log_sigmoid(raw_c))
    raw_r = grow_ref[...] + bcol_ref[...]
    sub8 = lax.broadcasted_iota(jnp.int32, raw_r.shape, 0)
    grow = jnp.where(sub8 < N_HEADS, raw_r, _log_sigmoid(raw_r))
    f_col = jnp.dot(lower.astype(F32), gcol, preferred_element_type=F32, precision=HIGHEST)
    f_row = jnp.dot(grow, (row <= col).astype(F32), preferred_element_type=F32, precision=HIGHEST)
    r8 = lax.broadcasted_iota(jnp.int32, (2 * N_HEADS, 2 * WIDTH), 0)
    c8 = lax.broadcasted_iota(jnp.int32, (2 * N_HEADS, 2 * WIDTH), 1)
    expand = (c8 // HEAD_DIM == r8).astype(BF16)
    x8 = jnp.where(lane8 < N_HEADS, gcol, f_col)
    x_hi, x_mid, x_lo = _split3(x8)
    both = _dot(x_hi, expand) + _dot(x_mid, expand) + _dot(x_lo, expand)
    li_all = both[:, 0:WIDTH]
    f_all = both[:, WIDTH:2 * WIDTH]
    d_row = jnp.concatenate([grow[h:h + 1, :] - f_row[N_HEADS + h:N_HEADS + h + 1, :] for h in heads], axis=1)
    yield
    t_idx = lax.broadcasted_iota(jnp.int32, (L, WIDTH), 0)
    s_idx = lax.broadcasted_iota(jnp.int32, (L, WIDTH), 1) % L
    ones_bd = ones_ref[...]
    masks = _head_masks()
    cm = li_all - f_all
    for sh in (1, 2, 4, 8, 16, 32):
        cm = jnp.maximum(cm, jnp.where(t_idx >= sh, pltpu.roll(cm, sh, 0), -jnp.inf))
    g = f_all + m_ref[...]
    mt = jnp.maximum(g, f_all + cm)
    wg = jnp.exp(g - mt)
    wd = jnp.exp(jnp.where(s_idx <= t_idx, f_all + d_row - mt, -jnp.inf))
    q = qk_all[:, 0:WIDTH]
    k = qk_all[:, WIDTH:2 * WIDTH]
    v = v_ref[...]
    q_bf = q.astype(BF16)
    v_bf = v.astype(BF16)
    head_rows = [mk.astype(BF16) for mk in masks]
    k_stack = jnp.concatenate([k.astype(BF16) * r for r in head_rows], axis=0)
    v_stack = jnp.concatenate([v_bf * r for r in head_rows], axis=0)
    c_bd = c_ref[...]
    n_row = n_ref[...]
    qk = _dot(q_bf, k_stack, NT) * wd
    q_c = _dot(q_bf, c_bd.astype(BF16))
    q_n = _head_sums(q * n_row, ones_bd)
    yield
    qkv = _dot(qk.astype(BF16), v_stack)
    qk_sum = _head_sums(qk, ones_bd)
    mt_last = mt[L - 1:L]
    kw = k * jnp.exp(f_all[L - 1:L] - f_all + li_all - mt_last)
    c_upd = _dot(kw.astype(BF16), v_bf, TN)
    yield
    wgl = wg[L - 1:L]
    c_ref[...] = wgl * c_bd + c_upd * mask_ref[...]
    n_ref[...] = wgl * n_row + jnp.sum(kw, axis=0, keepdims=True)
    m_ref[...] = mt_last
    den = wg * q_n + qk_sum
    hh = (wg * q_c + qkv) / jnp.maximum(jnp.abs(den), jnp.exp(-mt))
    z = _sigmoid(o_ref[...]) * hh
    mu = _head_sums(z, ones_bd) * (1.0 / HEAD_DIM)
    yield
    zc = z - mu
    var = _head_sums(zc * zc, ones_bd) * (1.0 / HEAD_DIM)
    yield
    y_ref[...] = (zc * lax.rsqrt(var + HEAD_NORM_EPS) * vec_ref[1:2, :] + vec_ref[2:3, :] * cact).astype(y_ref.dtype)


def _mlstm(h3, grow, b_col, b_row, conv_w, vecs, w_qk, mask_bd, c0, n0, m0, conv0):
    bsz, t, _ = h3.shape
    nc = t // CHUNK
    cb = COL_B // WIDTH
    bb = STREAMS_PER_STEP
    state4 = pl.BlockSpec((bb, N_HEADS, HEAD_DIM, HEAD_DIM), lambda b, c: (b, 0, 0, 0))
    state_n = pl.BlockSpec((bb, 1, WIDTH), lambda b, c: (b, 0, 0))
    state_m = state_n
    state_conv = pl.BlockSpec((bb, 8, WIDTH), lambda b, c: (b, 0, 0))

    def const(shape):
        return pl.BlockSpec(shape, lambda b, c: tuple(0 for _ in shape))

    def col(j):
        return pl.BlockSpec((bb, CHUNK, WIDTH), lambda b, c, j=j: (b, c, cb + j))

    return pl.pallas_call(
        functools.partial(_mlstm_kernel, nc=nc, bb=bb),
        grid=(bsz // bb, nc),
        in_specs=[col(0), col(1), col(2),
                  pl.BlockSpec((bb, CHUNK, LANE_TILE), lambda b, c: (b, c, COL_IF // LANE_TILE)),
                  pl.BlockSpec((bb, 1, 2 * N_HEADS, CHUNK), lambda b, c: (b, c, 0, 0)),
                  const((2 * N_HEADS, 1)), const((1, 2 * N_HEADS)), const((8, WIDTH)), const((8, WIDTH)),
                  const((WIDTH, 2 * WIDTH)), const((WIDTH, WIDTH)), const((WIDTH, WIDTH)),
                  state4, state_n, state_m, state_conv],
        out_specs=[pl.BlockSpec((bb, CHUNK, WIDTH), lambda b, c: (b, c, 0)), state4, state_n, state_m, state_conv],
        out_shape=[jax.ShapeDtypeStruct((bsz, t, WIDTH), BF16),
                   jax.ShapeDtypeStruct((bsz, N_HEADS, HEAD_DIM, HEAD_DIM), F32),
                   jax.ShapeDtypeStruct((bsz, 1, WIDTH), F32),
                   jax.ShapeDtypeStruct((bsz, 1, WIDTH), F32),
                   jax.ShapeDtypeStruct((bsz, 8, WIDTH), F32)],
        scratch_shapes=[pltpu.VMEM((bb, WIDTH, WIDTH), F32),
                        pltpu.VMEM((bb, 1, WIDTH), F32),
                        pltpu.VMEM((bb, 1, WIDTH), F32),
                        pltpu.VMEM((bb, 8, WIDTH), F32)],
        compiler_params=_cparams("parallel", "arbitrary"),
        name="mlstm",
    )(h3, h3, h3, h3, grow, b_col, b_row, conv_w, vecs, w_qk, mask_bd.astype(BF16), mask_bd, c0, n0, m0, conv0)


SWA_CHUNKS_PER_STEP = 4


def _swa_kernel(*refs, prev_valid, cps, n_kv):
    q_ref = refs[0]
    k_refs = refs[1:1 + n_kv]
    v_refs = refs[1 + n_kv:1 + 2 * n_kv]
    bias_ref, sink_ref, y_ref, s_ref, p_ref = refs[1 + 2 * n_kv:]
    g = pl.program_id(1)
    L = CHUNK
    span = WINDOW + L
    kcat = jnp.concatenate([r[0] for r in k_refs], axis=0).astype(BF16)
    vcat = jnp.concatenate([r[0] for r in v_refs], axis=0).astype(BF16)
    kv_of = [h // (N_HEADS // C_KV_HEADS) for h in range(N_HEADS)]
    for j in range(cps):
        q = q_ref[0, j * L:(j + 1) * L, :].astype(BF16)
        k_j = kcat[j * L:j * L + span]
        for h in range(N_HEADS):
            s_ref[j * N_HEADS + h] = _dot(_head(q, h), _head(k_j, kv_of[h]), NT)
    for j in range(cps):
        key_pos = lax.broadcasted_iota(jnp.int32, (L, span), 1) + (g * cps + j) * L
        for h in range(N_HEADS):
            s = s_ref[j * N_HEADS + h] * HEAD_DIM ** -0.5 + bias_ref[h]
            if not prev_valid:
                s = jnp.where(key_pos >= WINDOW, s, -jnp.inf)
            sink = sink_ref[0:1, h:h + 1]
            m = jnp.maximum(jnp.max(s, axis=1, keepdims=True), sink)
            p = jnp.exp(s - m)
            inv = 1.0 / (jnp.sum(p, axis=1, keepdims=True) + jnp.exp(sink - m))
            p_ref[j * N_HEADS + h] = (p * inv).astype(BF16)
    for j in range(cps):
        v_j = vcat[j * L:j * L + span]
        o = [_dot(p_ref[j * N_HEADS + h], _head(v_j, kv_of[h])) for h in range(N_HEADS)]
        y_ref[0, j * L:(j + 1) * L, :] = jnp.concatenate(o, axis=1).astype(y_ref.dtype)


def _swa(h3, kf, vf, bias, sinks, prev_valid):
    bsz, t, _ = h3.shape
    nc = t // CHUNK
    cps = min(SWA_CHUNKS_PER_STEP, nc)
    rows = cps * CHUNK
    kv_rows = min(rows, WINDOW)
    n_kv = (WINDOW + rows) // kv_rows
    qb = COL_C // WIDTH
    kv = [pl.BlockSpec((1, kv_rows, KV_WIDTH), lambda b, g, j=j: (b, g * (rows // kv_rows) + j, 0))
          for j in range(n_kv)]

    return pl.pallas_call(
        functools.partial(_swa_kernel, prev_valid=prev_valid, cps=cps, n_kv=n_kv),
        grid=(bsz, nc // cps),
        in_specs=[pl.BlockSpec((1, rows, WIDTH), lambda b, g: (b, g, qb))] + kv + kv + [
                  pl.BlockSpec((N_HEADS, CHUNK, WINDOW + CHUNK), lambda b, g: (0, 0, 0)),
                  pl.BlockSpec((1, N_HEADS), lambda b, g: (0, 0))],
        out_specs=pl.BlockSpec((1, rows, WIDTH), lambda b, g: (b, g, 0)),
        out_shape=jax.ShapeDtypeStruct((bsz, t, WIDTH), BF16),
        scratch_shapes=[pltpu.VMEM((cps * N_HEADS, CHUNK, WINDOW + CHUNK), F32),
                        pltpu.VMEM((cps * N_HEADS, CHUNK, WINDOW + CHUNK), BF16)],
        compiler_params=_cparams("parallel", "parallel"),
        name="swa",
    )(h3, *([kf] * n_kv), *([vf] * n_kv), bias, sinks)


def _rwkv_kernel(pd_ref, mu_ref, vec_ref, wwa_ref, gup_ref, ones_ref, mask_ref, s0_ref, sh0_ref,
                 y_ref, s_out_ref, sh_out_ref, st_ref, carry_ref, *, nc, bb):
    c = pl.program_id(1)

    @pl.when(c == 0)
    def _():
        for bi in range(bb):
            st_ref[bi] = jnp.zeros(st_ref.shape[1:], F32)
            for h in range(N_HEADS):
                st_ref[bi, h * HEAD_DIM:(h + 1) * HEAD_DIM, h * HEAD_DIM:(h + 1) * HEAD_DIM] = s0_ref[bi, h]
            carry_ref[bi, 7:8, :] = sh0_ref[bi]

    _round_robin([_rwkv_chunk(pd_ref.at[bi], mu_ref, vec_ref, wwa_ref, gup_ref, ones_ref, mask_ref,
                              y_ref.at[bi], st_ref.at[bi], carry_ref.at[bi]) for bi in range(bb)])

    @pl.when(c == nc - 1)
    def _():
        for bi in range(bb):
            for h in range(N_HEADS):
                s_out_ref[bi, h] = st_ref[bi, h * HEAD_DIM:(h + 1) * HEAD_DIM, h * HEAD_DIM:(h + 1) * HEAD_DIM]
            sh_out_ref[bi] = carry_ref[bi, 7:8, :]


def _rwkv_chunk(pd_ref, mu_ref, vec_ref, wwa_ref, gup_ref, ones_ref, mask_ref, y_ref, st_ref, carry_ref):
    L = CHUNK
    ones_bd = ones_ref[...]
    masks = _head_masks()

    pd = pd_ref[...]
    rolled = pltpu.roll(pd, 1, 0)
    row8 = lax.broadcasted_iota(jnp.int32, (8, D_COLS), 0)
    top = jnp.where(row8 == 0, carry_ref[7:8, :], rolled[0:8])
    pd_prev = jnp.concatenate([top, rolled[8:]], axis=0)
    carry_ref[...] = pd[L - 8:L]
    mixed = pd + (pd_prev - pd) * mu_ref[...]
    r = mixed[:, 0:WIDTH]
    k_raw = mixed[:, WIDTH:2 * WIDTH]
    v = mixed[:, 2 * WIDTH:3 * WIDTH]
    wa = mixed[:, 3 * WIDTH:3 * WIDTH + 2 * D_DECAY_LORA]
    g_in = mixed[:, 3 * WIDTH + 2 * D_DECAY_LORA:D_COLS]
    lane_wa = lax.broadcasted_iota(jnp.int32, wa.shape, 1)
    wa_act = jnp.where(lane_wa < D_DECAY_LORA, jnp.tanh(wa), wa)
    lora = _dot(wa_act.astype(BF16), wwa_ref[...])
    g_d = _dot(_sigmoid(g_in).astype(BF16), gup_ref[...])
    yield
    z = -(vec_ref[0:1, :] + lora[:, 0:WIDTH])
    w_log = -(jnp.maximum(z, 0.0) + jnp.log1p(jnp.exp(-jnp.abs(z)))) - 0.5
    lw = -jnp.exp(w_log)
    a_d = _sigmoid(vec_ref[1:2, :] + lora[:, WIDTH:2 * WIDTH])
    kk = k_raw * vec_ref[2:3, :]
    kk_norm = _head_sums(kk * kk, ones_bd)
    cw = _cumsum_rows(lw)
    yield
    kk = kk * lax.rsqrt(jnp.maximum(kk_norm, 1e-24))
    k = k_raw * (1.0 + (a_d - 1.0) * vec_ref[3:4, :])
    a = -kk
    b = kk * a_d
    g_in_c = jnp.exp(cw)
    g_inv = jnp.exp(-cw)
    at = a * jnp.exp(cw - lw)
    rt = r * g_in_c
    bt = b * g_inv
    kt = k * g_inv
    x = jnp.concatenate([at, rt], axis=0)
    bt_st = _stack_parts(bt, masks)
    kt_st = _stack_parts(kt, masks)
    gram_b = _mm2(x, bt_st, NT)
    gram_k = _mm2(x, kt_st, NT)
    st = st_ref[...]
    u = _mm2(x, _split2(st), NT)
    yield

    t_idx = lax.broadcasted_iota(jnp.int32, (L, WIDTH), 0)
    i_idx = lax.broadcasted_iota(jnp.int32, (L, WIDTH), 1) % L
    strict = i_idx < t_idx
    incl = i_idx <= t_idx
    same_blk = (i_idx // SUB) == (t_idx // SUB)
    eye = (i_idx == t_idx).astype(F32)
    n_all = jnp.where(strict, gram_b[0:L], 0.0)
    m_all = jnp.where(strict, gram_k[0:L], 0.0)
    rb_all = jnp.where(incl, gram_b[L:2 * L], 0.0)
    rk_all = jnp.where(incl, gram_k[L:2 * L], 0.0)
    n_d = jnp.where(same_blk, n_all, 0.0)
    n_off = jnp.where(same_blk, 0.0, n_all)

    v_st = _stack_parts(v, masks)
    x2 = _mm2(n_d, _stack_parts(n_d, masks))
    rhs = u[0:L] + _mm2(m_all, v_st)
    bonus = _head_sums(r * k * vec_ref[4:5, :], ones_bd) * v
    yield
    x2_st = _stack_parts(x2, masks)
    x4 = _mm2(x2, x2_st)
    t_d = eye + n_d
    t_d = t_d + _mm2(t_d, x2_st)
    yield
    x4_st = _stack_parts(x4, masks)
    x8 = _mm2(x4, x4_st)
    t_d = t_d + _mm2(t_d, x4_st)
    yield
    t_d = t_d + _mm2(t_d, _stack_parts(x8, masks))
    yield
    n1 = _mm2(t_d, _stack_parts(n_off, masks))
    yield
    n1_st = _stack_parts(n1, masks)
    n2 = _mm2(n1, n1_st)
    yield
    a2 = eye + n1 + n2 + _mm2(n2, n1_st)
    yield
    t_full = _mm2(a2, _stack_parts(t_d, masks))
    yield
    sa = _mm2(t_full, _stack_parts(rhs, masks))
    yield
    y = u[L:2 * L] + _mm2(rb_all, _stack_parts(sa, masks)) + _mm2(rk_all, v_st)
    sv_hi, sv_lo = _split2(jnp.concatenate([sa, v], axis=0))
    bk_hi, bk_lo = _split2(jnp.concatenate([bt, kt], axis=0))
    upd = _dot(sv_hi, bk_hi, TN) + _dot(sv_lo, bk_hi, TN) + _dot(sv_hi, bk_lo, TN)
    yield
    st_ref[...] = (st + upd * mask_ref[...]) * g_in_c[L - 1:L]

    mu = _head_sums(y, ones_bd) * (1.0 / HEAD_DIM)
    yield
    yc = y - mu
    var = _head_sums(yc * yc, ones_bd) * (1.0 / HEAD_DIM)
    yield
    o = yc * lax.rsqrt(var + RWKV_GN_EPS) * vec_ref[5:6, :] + vec_ref[6:7, :]
    y_ref[...] = ((o + bonus) * g_d).astype(y_ref.dtype)


def _rwkv(h3, mu, vecs, w_wa, g_up, mask_bd, s0, shift0):
    bsz, t, _ = h3.shape
    nc = t // CHUNK
    bb = STREAMS_PER_STEP
    state = pl.BlockSpec((bb, N_HEADS, HEAD_DIM, HEAD_DIM), lambda g, i: (g, 0, 0, 0))
    shift = pl.BlockSpec((bb, 1, D_COLS), lambda g, i: (g, 0, 0))

    def const(shape):
        return pl.BlockSpec(shape, lambda g, i: tuple(0 for _ in shape))

    return pl.pallas_call(
        functools.partial(_rwkv_kernel, nc=nc, bb=bb),
        grid=(bsz // bb, nc),
        in_specs=[pl.BlockSpec((bb, CHUNK, D_COLS), lambda g, i: (g, i, COL_D // D_COLS)),
                  const((1, D_COLS)), const((8, WIDTH)), const((2 * D_DECAY_LORA, 2 * WIDTH)),
                  const((D_GATE_LORA, WIDTH)), const((WIDTH, WIDTH)), const((WIDTH, WIDTH)), state, shift],
        out_specs=[pl.BlockSpec((bb, CHUNK, WIDTH), lambda g, i: (g, i, 0)), state, shift],
        out_shape=[jax.ShapeDtypeStruct((bsz, t, WIDTH), BF16),
                   jax.ShapeDtypeStruct((bsz, N_HEADS, HEAD_DIM, HEAD_DIM), F32),
                   jax.ShapeDtypeStruct((bsz, 1, D_COLS), F32)],
        scratch_shapes=[pltpu.VMEM((bb, WIDTH, WIDTH), F32), pltpu.VMEM((bb, 8, D_COLS), F32)],
        compiler_params=_cparams("parallel", "arbitrary"),
        name="rwkv7",
    )(h3, mu, vecs, w_wa, g_up, mask_bd.astype(BF16), mask_bd, s0, shift0)


def _merge_kernel(ya_ref, yb_ref, yc_ref, yd_ref, xb_ref, wg_ref, wbr_ref, wo_ref, x_ref,
                  lng_ref, lnb_ref, xo_ref, xob_ref):
    ys = (ya_ref, yb_ref, yc_ref, yd_ref)
    xb = xb_ref[...]
    merged = None
    for n in range(N_BRANCH):
        gate = _dot(xb, wg_ref[:, n * D_MODEL:(n + 1) * D_MODEL])
        term = _sigmoid(gate) * _dot(ys[n][...], wbr_ref[n])
        merged = term if merged is None else merged + term
    out = _dot(merged.astype(BF16), wo_ref[...])
    xn = _layernorm_rows(ALPHA * x_ref[...] + out, lng_ref[...], lnb_ref[...])
    xo_ref[...] = xn
    xob_ref[...] = xn.astype(BF16)


def _merge(ys, xb2, w_gate, wbr, wo, x2, lng, lnb, *, tm=512):
    m = x2.shape[0]
    tm = min(tm, m)
    ysp = pl.BlockSpec((tm, WIDTH), lambda i: (i, 0))
    row = pl.BlockSpec((tm, D_MODEL), lambda i: (i, 0))
    vec = pl.BlockSpec((1, D_MODEL), lambda i: (0, 0))
    resident = dict(pipeline_mode=pl.Buffered(1))
    return pl.pallas_call(
        _merge_kernel,
        grid=(m // tm,),
        in_specs=[ysp, ysp, ysp, ysp, row,
                  pl.BlockSpec((D_MODEL, GATE_COLS), lambda i: (0, 0), **resident),
                  pl.BlockSpec((N_BRANCH, WIDTH, D_MODEL), lambda i: (0, 0, 0), **resident),
                  pl.BlockSpec((D_MODEL, D_MODEL), lambda i: (0, 0), **resident),
                  row, vec, vec],
        out_specs=[row, row],
        out_shape=[jax.ShapeDtypeStruct((m, D_MODEL), F32), jax.ShapeDtypeStruct((m, D_MODEL), BF16)],
        compiler_params=_cparams("parallel"),
        name="merge",
    )(*ys, xb2, w_gate, wbr, wo, x2, lng, lnb)


def _ffn_kernel(*refs, n_steps, gated):
    if gated:
        xb_ref, x_ref, gates_ref, wg_ref, wu_ref, wd_ref, lng_ref, lnb_ref, xo_ref, xob_ref, acc_ref = refs
    else:
        xb_ref, x_ref, wg_ref, wu_ref, wd_ref, lng_ref, lnb_ref, xo_ref, xob_ref, acc_ref = refs
    j = pl.program_id(1)

    @pl.when(j == 0)
    def _():
        acc_ref[...] = jnp.zeros_like(acc_ref)

    xb = xb_ref[...]
    hg = _dot(xb, wg_ref[0])
    hu = _dot(xb, wu_ref[0])
    part = _dot((_silu(hg) * hu).astype(BF16), wd_ref[0])
    if gated:
        gates = gates_ref[...]
        lane = lax.broadcasted_iota(jnp.int32, gates.shape, 1)
        part = jnp.sum(jnp.where(lane == j, gates, 0.0), axis=1, keepdims=True) * part
    acc_ref[...] += part

    @pl.when(j == n_steps - 1)
    def _():
        xn = _layernorm_rows(ALPHA * x_ref[...] + acc_ref[...], lng_ref[...], lnb_ref[...])
        xo_ref[...] = xn
        xob_ref[...] = xn.astype(BF16)


def _ffn(xb, x2, wg, wu, wd, lng, lnb, gates=None, *, tm=512):
    m = x2.shape[0]
    tm = min(tm, m)
    n_steps, _, tf = wg.shape
    row = pl.BlockSpec((tm, D_MODEL), lambda i, j: (i, 0))
    vec = pl.BlockSpec((1, D_MODEL), lambda i, j: (0, 0))
    w_in_spec = pl.BlockSpec((1, D_MODEL, tf), lambda i, j: (j, 0, 0))
    w_out_spec = pl.BlockSpec((1, tf, D_MODEL), lambda i, j: (j, 0, 0))
    in_specs = [row, row]
    args = [xb, x2]
    if gates is not None:
        in_specs.append(pl.BlockSpec((tm, N_EXPERTS), lambda i, j: (i, 0)))
        args.append(gates)
    in_specs += [w_in_spec, w_in_spec, w_out_spec, vec, vec]
    args += [wg, wu, wd, lng, lnb]
    return pl.pallas_call(
        functools.partial(_ffn_kernel, n_steps=n_steps, gated=gates is not None),
        grid=(m // tm, n_steps),
        in_specs=in_specs,
        out_specs=[row, row],
        out_shape=[jax.ShapeDtypeStruct((m, D_MODEL), F32), jax.ShapeDtypeStruct((m, D_MODEL), BF16)],
        scratch_shapes=[pltpu.VMEM((tm, D_MODEL), F32)],
        compiler_params=_cparams("parallel", "arbitrary"),
        name="moe" if gates is not None else "ffn",
    )(*args)


STATE_NAMES = ('swa_k', 'swa_v', 'hgrn', 'mlstm_c', 'mlstm_n', 'mlstm_m', 'mlstm_conv', 'rwkv', 'rwkv_shift')

_D_ORIG = (('r', WIDTH), ('w', D_DECAY_LORA), ('k', WIDTH), ('v', WIDTH), ('a', D_AAA_LORA), ('g', D_GATE_LORA))
_D_KERNEL = ('r', 'k', 'v', 'w', 'a', 'g')


def _d_pieces(arr):
    out, off = {}, 0
    for name, size in _D_ORIG:
        out[name] = arr[..., off:off + size]
        off += size
    return out


def _d_to_kernel_order(arr):
    p = _d_pieces(arr)
    return jnp.concatenate([p[n] for n in _D_KERNEL], axis=-1)


def _d_to_original_order(arr):
    sizes = dict(_D_ORIG)
    p, off = {}, 0
    for name in _D_KERNEL:
        p[name] = arr[..., off:off + sizes[name]]
        off += sizes[name]
    return jnp.concatenate([p[n] for n, _ in _D_ORIG], axis=-1)


def _rel_bucket(rel):
    half = NUM_BUCKETS // 2
    exact = half // 2
    dist = jnp.abs(rel)
    far = exact + (jnp.log(jnp.maximum(dist, 1).astype(F32) / exact)
                   / math.log(MAX_DISTANCE / exact) * (half - exact)).astype(jnp.int32)
    far = jnp.minimum(far, half - 1)
    return jnp.where(rel > 0, half, 0) + jnp.where(dist < exact, dist, far)


def _lower_bounds(lb_raw):
    sm = jax.nn.softmax(lb_raw.astype(F32), axis=0)
    lb = jnp.concatenate([jnp.zeros_like(sm[:1]), jnp.cumsum(sm[1:], axis=0)[:-1]], axis=0)
    return jnp.clip(lb, 0.0, LB_CEIL)


def _block_diag(blocks):
    rows = sum(b.shape[0] for b in blocks)
    cols = sum(b.shape[1] for b in blocks)
    out = jnp.zeros((rows, cols), blocks[0].dtype)
    r = c = 0
    for b in blocks:
        out = out.at[r:r + b.shape[0], c:c + b.shape[1]].set(b)
        r += b.shape[0]
        c += b.shape[1]
    return out


def _pad_rows(a, rows):
    return jnp.concatenate([a, jnp.zeros((rows - a.shape[0],) + a.shape[1:], a.dtype)], axis=0)


def _mixer(x3, xb2, st, prev_valid, lp, l):
    bsz, t, _ = x3.shape
    m = bsz * t
    x2 = x3.reshape(m, D_MODEL)
    h2 = _proj(xb2, lp['w_main'][l])
    h3 = h2.reshape(bsz, t, MAIN_COLS)

    y_a, s_hgrn = _hgrn(h3, lp['lbp'][l], lp['gn_a'][l][None], lp['mask_bd'], st['hgrn'])

    grow = jnp.swapaxes(h3[..., COL_IF:COL_IF + 2 * N_HEADS].reshape(bsz, t // CHUNK, CHUNK, 2 * N_HEADS), 2, 3)
    conv0 = jnp.concatenate([jnp.zeros((bsz, 8 - (CONV_W - 1), WIDTH), F32), st['mlstm_conv']], axis=1)
    y_b, mc, mn, mm, conv8 = _mlstm(h3, grow, lp['b_if'][l][:, None],
                                    lp['b_if'][l][None, :], lp['conv_w8'][l], lp['vec_b'][l], lp['w_qk'][l],
                                    lp['mask_bd'], st['mlstm_c'], st['mlstm_n'].reshape(bsz, 1, WIDTH),
                                    jnp.repeat(st['mlstm_m'], HEAD_DIM, axis=-1)[:, None, :], conv0)
    mn = mn.reshape(bsz, N_HEADS, HEAD_DIM)
    mm = mm[:, :, ::HEAD_DIM]
    conv_state = conv8[:, 8 - (CONV_W - 1):]

    k_off = COL_C + WIDTH
    kf = jnp.concatenate([st['swa_k'].reshape(bsz, WINDOW, KV_WIDTH), h3[..., k_off:k_off + KV_WIDTH]], axis=1)
    vf = jnp.concatenate([st['swa_v'].reshape(bsz, WINDOW, KV_WIDTH), h3[..., k_off + KV_WIDTH:k_off + 2 * KV_WIDTH]], axis=1)
    y_c = _swa(h3, kf, vf, lp['bias'], lp['sinks'][l][None], prev_valid)
    keep = st['keep']
    k_win = kf[:, -keep:].reshape(bsz, keep, C_KV_HEADS, HEAD_DIM)
    v_win = vf[:, -keep:].reshape(bsz, keep, C_KV_HEADS, HEAD_DIM)

    y_d, s_rwkv, shift = _rwkv(h3, lp['mu_d'][l][None], lp['vec_d'][l], lp['w_wa'][l], lp['g_up_d'][l],
                               lp['mask_bd'], st['rwkv'], _d_to_kernel_order(st['rwkv_shift'])[:, None, :])
    shift_state = _d_to_original_order(shift[:, 0, :])

    ys = [y.reshape(m, WIDTH) for y in (y_a, y_b, y_c, y_d)]
    x1, x1b = _merge(ys, xb2, lp['w_gate'][l], lp['w_br'][l], lp['w_o'][l], x2,
                     lp['ln1_g'][l][None], lp['ln1_b'][l][None])
    new_st = {'swa_k': k_win, 'swa_v': v_win, 'hgrn': s_hgrn, 'mlstm_c': mc, 'mlstm_n': mn,
              'mlstm_m': mm[:, 0, :], 'mlstm_conv': conv_state, 'rwkv': s_rwkv, 'rwkv_shift': shift_state}
    return x1, x1b, new_st


def _trunk(x3, states, prev_valid, keep, lp):
    bsz, t, _ = x3.shape
    m = bsz * t
    xb2 = x3.reshape(m, D_MODEL).astype(BF16)
    collected = {name: [] for name in STATE_NAMES}
    for l in range(DEPTH):
        st = {name: states[name][l] for name in STATE_NAMES}
        st['keep'] = keep
        x1, x1b, new_st = _mixer(x3, xb2, st, prev_valid, lp, l)
        j = l // 2
        ln_g, ln_b = lp['ln2_g'][l][None], lp['ln2_b'][l][None]
        if l % 2 == 0:
            x2, xb2 = _ffn(x1b, x1, lp['ffn_wg'][j], lp['ffn_wu'][j], lp['ffn_wd'][j], ln_g, ln_b)
        else:
            logits = _matmul(x1b, lp['router_w'][j], tm=1024, tn=N_EXPERTS) + lp['router_b'][j]
            top_v, top_i = lax.top_k(logits, TOP_K)
            probs = jax.nn.softmax(top_v, axis=-1)
            gates = jnp.einsum('mk,mke->me', probs, jax.nn.one_hot(top_i, N_EXPERTS, dtype=F32))
            x2, xb2 = _ffn(x1b, x1, lp['exp_wg'][j], lp['exp_wu'][j], lp['exp_wd'][j], ln_g, ln_b, gates)
        x3 = x2.reshape(bsz, t, D_MODEL)
        for name in STATE_NAMES:
            collected[name].append(new_st[name])
    return x3, {name: jnp.stack(collected[name]) for name in STATE_NAMES}


def kernel(x_prompt, x_sample, cache_swa_k, cache_swa_v, state_hgrn, state_mlstm_c, state_mlstm_n, state_mlstm_m, state_mlstm_conv, state_rwkv, state_rwkv_shift, w_in, lb_raw, gn_a, conv_w, conv_b, wq_b, wk_b, b_i, b_f, gn_b, skip_b, sinks, rel_bias, mu_d, w0_d, w_up_d, a0_d, a_up_d, g_up_d, k_k_d, k_a_d, r_k_d, gn_w_d, gn_b_d, w_br, w_o, ln1_g, ln1_b, ln2_g, ln2_b, ffn_w_gate, ffn_w_up, ffn_w_down, router_w, router_b, exp_w_gate, exp_w_up, exp_w_down):
    off_if = 4 * WIDTH + 3 * WIDTH
    off_c = off_if + 2 * N_HEADS
    off_d = off_c + WIDTH + 2 * KV_WIDTH
    off_gate = off_d + D_COLS
    w_main = jnp.concatenate([w_in[:, :, :4 * WIDTH], _d_to_kernel_order(w_in[:, :, off_d:off_gate]),
                              w_in[:, :, 4 * WIDTH:off_if], w_in[:, :, off_c:off_d], w_in[:, :, off_if:off_c],
                              jnp.zeros((DEPTH, D_MODEL, LANE_TILE - 2 * N_HEADS), w_in.dtype)], axis=-1).astype(BF16)
    w_gate = w_in[:, :, off_gate:].astype(BF16)

    lb = _lower_bounds(lb_raw)
    lb = lb[jnp.minimum(jnp.arange(DEPTH), lb.shape[0] - 1)]
    lbp = jnp.stack([jnp.log(jnp.maximum(lb, LB_FLOOR)), jnp.log1p(-lb), 1.0 - lb], axis=1)
    lbp = jnp.concatenate([lbp, jnp.zeros((DEPTH, 5, WIDTH), F32)], axis=1)

    w_qk = jnp.stack([jnp.concatenate([_block_diag(list(wq_b[l])), _block_diag(list(wk_b[l])) * HEAD_DIM ** -0.5],
                                      axis=1) for l in range(DEPTH)]).astype(BF16)
    w_wa = jnp.stack([_block_diag([w_up_d[l], a_up_d[l]]) for l in range(DEPTH)]).astype(BF16)
    vec_b = jnp.stack([_pad_rows(jnp.stack([conv_b[l], gn_b[l], skip_b[l]]), 8) for l in range(DEPTH)])
    conv_w8 = jnp.stack([_pad_rows(conv_w[l], 8) for l in range(DEPTH)])
    vec_d = jnp.stack([_pad_rows(jnp.stack([w0_d[l], a0_d[l], k_k_d[l], k_a_d[l], r_k_d[l], gn_w_d[l], gn_b_d[l]]), 8)
                       for l in range(DEPTH)])
    head_of = jnp.arange(WIDTH) // HEAD_DIM
    mask_bd = (head_of[:, None] == head_of[None, :]).astype(F32)

    span = WINDOW + CHUNK
    rel = jnp.arange(span)[None, :] - WINDOW - jnp.arange(CHUNK)[:, None]
    bias = jnp.transpose(rel_bias.astype(F32)[_rel_bucket(rel)], (2, 0, 1))

    n_dense = ffn_w_gate.shape[0]
    ff_steps = D_FF // D_FF_EXPERT
    lp = {
        'w_main': w_main, 'w_gate': w_gate, 'lbp': lbp, 'gn_a': gn_a,
        'mask_bd': mask_bd, 'conv_w8': conv_w8, 'vec_b': vec_b, 'w_qk': w_qk,
        'b_if': jnp.concatenate([b_i, b_f], axis=-1), 'sinks': sinks, 'bias': bias,
        'mu_d': _d_to_kernel_order(mu_d), 'vec_d': vec_d, 'w_wa': w_wa, 'g_up_d': g_up_d.astype(BF16),
        'w_br': w_br.astype(BF16), 'w_o': w_o.astype(BF16),
        'ln1_g': ln1_g, 'ln1_b': ln1_b, 'ln2_g': ln2_g, 'ln2_b': ln2_b,
        'ffn_wg': jnp.swapaxes(ffn_w_gate.astype(BF16).reshape(n_dense, D_MODEL, ff_steps, D_FF_EXPERT), 1, 2),
        'ffn_wu': jnp.swapaxes(ffn_w_up.astype(BF16).reshape(n_dense, D_MODEL, ff_steps, D_FF_EXPERT), 1, 2),
        'ffn_wd': ffn_w_down.astype(BF16).reshape(n_dense, ff_steps, D_FF_EXPERT, D_MODEL),
        'router_w': router_w.astype(BF16), 'router_b': router_b,
        'exp_wg': exp_w_gate.astype(BF16), 'exp_wu': exp_w_up.astype(BF16), 'exp_wd': exp_w_down.astype(BF16),
    }

    sample_states = {
        'swa_k': cache_swa_k, 'swa_v': cache_swa_v, 'hgrn': state_hgrn, 'mlstm_c': state_mlstm_c,
        'mlstm_n': state_mlstm_n, 'mlstm_m': state_mlstm_m, 'mlstm_conv': state_mlstm_conv,
        'rwkv': state_rwkv, 'rwkv_shift': state_rwkv_shift,
    }
    keep = cache_swa_k.shape[2]
    bp = x_prompt.shape[0]
    prompt_states = {}
    for name in STATE_NAMES:
        arr = sample_states[name]
        rows = (WINDOW,) + arr.shape[3:] if name in ('swa_k', 'swa_v') else arr.shape[2:]
        prompt_states[name] = jnp.zeros((DEPTH, bp) + tuple(rows), arr.dtype)

    y_prompt, pst = _trunk(x_prompt, prompt_states, False, keep, lp)
    y_sample, sst = _trunk(x_sample, sample_states, True, keep, lp)
    return (y_prompt, y_sample) + tuple(pst[n] for n in STATE_NAMES) + tuple(sst[n] for n in STATE_NAMES)
```

```python
import functools
import math

import jax
import jax.numpy as jnp
from jax import lax
from jax.experimental import pallas as pl
from jax.experimental.pallas import tpu as pltpu

F32 = jnp.float32
BF16 = jnp.bfloat16
HIGHEST = lax.Precision.HIGHEST

D_MODEL = 1024
DEPTH = 4
CHUNK = 64
HEAD_DIM = 64
N_HEADS = 4
WIDTH = N_HEADS * HEAD_DIM
C_KV_HEADS = 2
KV_WIDTH = C_KV_HEADS * HEAD_DIM
CONV_W = 4
WINDOW = 128
NUM_BUCKETS = 32
MAX_DISTANCE = 128
D_DECAY_LORA = 64
D_AAA_LORA = 64
D_GATE_LORA = 128
D_COLS = 3 * WIDTH + D_DECAY_LORA + D_AAA_LORA + D_GATE_LORA
N_BRANCH = 4
D_FF = 2816
N_EXPERTS = 8
TOP_K = 2
D_FF_EXPERT = 1408
ALPHA = (2 * DEPTH) ** 0.25
LN_EPS = 1e-5
HEAD_NORM_EPS = 1e-5
RWKV_GN_EPS = 64e-5
LB_FLOOR = 1e-30
LB_CEIL = 1.0 - 1e-6

GATE_COLS = N_BRANCH * D_MODEL
COL_A = 0
COL_D = COL_A + 4 * WIDTH
COL_B = COL_D + D_COLS
COL_C = COL_B + 3 * WIDTH
COL_IF = COL_C + WIDTH + 2 * KV_WIDTH
LANE_TILE = 128
MAIN_COLS = COL_IF + LANE_TILE
SUB = 16
STREAMS_PER_STEP = 4
V7X_VMEM_LIMIT = 48 * 1024 * 1024

NN = (((1,), (0,)), ((), ()))
NT = (((1,), (1,)), ((), ()))
TN = (((0,), (0,)), ((), ()))


def _cparams(*sem):
    return pltpu.CompilerParams(dimension_semantics=sem, vmem_limit_bytes=V7X_VMEM_LIMIT)


def _sigmoid(x):
    return 1.0 / (1.0 + jnp.exp(-x))


def _silu(x):
    return x * _sigmoid(x)


def _log_sigmoid(x):
    return jnp.minimum(x, 0.0) - jnp.log1p(jnp.exp(-jnp.abs(x)))


def _layernorm_rows(z, g, b):
    mu = jnp.mean(z, axis=-1, keepdims=True)
    zc = z - mu
    var = jnp.mean(zc * zc, axis=-1, keepdims=True)
    return zc * lax.rsqrt(var + LN_EPS) * g + b


def _split2(x):
    hi = x.astype(BF16)
    lo = (x - hi.astype(F32)).astype(BF16)
    return hi, lo


def _split3(x):
    hi = x.astype(BF16)
    r1 = x - hi.astype(F32)
    mid = r1.astype(BF16)
    lo = (r1 - mid.astype(F32)).astype(BF16)
    return hi, mid, lo


def _dot(a, b, dims=NN):
    return lax.dot_general(a, b, dims, preferred_element_type=F32)


def _mm2(a, b_parts, dims=NN):
    a_hi, a_lo = _split2(a)
    n = a.shape[0]
    both = _dot(jnp.concatenate([a_hi, a_lo], axis=0), b_parts[0], dims)
    return both[:n] + both[n:] + _dot(a_hi, b_parts[1], dims)


def _cumsum_rows(x):
    n = x.shape[0]
    row = lax.broadcasted_iota(jnp.int32, (n, n), 0)
    col = lax.broadcasted_iota(jnp.int32, (n, n), 1)
    tri = (row >= col).astype(BF16)
    hi, mid, lo = _split3(x)
    return _dot(tri, hi) + _dot(tri, mid) + _dot(tri, lo)


def _head_masks():
    lane_head = lax.broadcasted_iota(jnp.int32, (1, WIDTH), 1) // HEAD_DIM
    return [(lane_head == h) for h in range(N_HEADS)]


def _stack_heads(x, masks):
    return jnp.concatenate([jnp.where(mk, x, 0.0) for mk in masks], axis=0)


def _stack_parts(x, masks):
    parts = _split2(x)
    rows = [mk.astype(BF16) for mk in masks]
    return tuple(jnp.concatenate([p * r for r in rows], axis=0) for p in parts)


def _round_robin(stage_generators):
    live = list(stage_generators)
    while live:
        live = [g for g in live if next(g, _DONE) is not _DONE]


_DONE = object()


def _head_sums(x, ones_bd):
    hi, lo = _split2(x)
    return _dot(hi, ones_bd) + _dot(lo, ones_bd)


def _matmul_kernel(x_ref, w_ref, o_ref):
    o_ref[...] = _dot(x_ref[...], w_ref[...])


def _matmul(x, w, *, tm, tn):
    m, k = x.shape
    n = w.shape[1]
    tm = min(tm, m)
    tn = min(tn, n)
    return pl.pallas_call(
        _matmul_kernel,
        grid=(m // tm, n // tn),
        in_specs=[pl.BlockSpec((tm, k), lambda i, j: (i, 0)),
                  pl.BlockSpec((k, tn), lambda i, j: (0, j))],
        out_specs=pl.BlockSpec((tm, tn), lambda i, j: (i, j)),
        out_shape=jax.ShapeDtypeStruct((m, n), F32),
        compiler_params=_cparams("parallel", "parallel"),
        name="matmul",
    )(x, w)


PROJ_TN = 256


def _proj_kernel(x_ref, w_ref, o_ref):
    x = x_ref[...]
    n = w_ref.shape[1]
    for lo in range(0, n, PROJ_TN):
        hi = min(lo + PROJ_TN, n)
        o_ref[:, lo:hi] = _dot(x, w_ref[:, lo:hi])


def _proj(xb, w, *, tm=512):
    m, k = xb.shape
    n = w.shape[1]
    tm = min(tm, m)
    return pl.pallas_call(
        _proj_kernel,
        grid=(m // tm,),
        in_specs=[pl.BlockSpec((tm, k), lambda i: (i, 0)),
                  pl.BlockSpec((k, n), lambda i: (0, 0), pipeline_mode=pl.Buffered(1))],
        out_specs=pl.BlockSpec((tm, n), lambda i: (i, 0)),
        out_shape=jax.ShapeDtypeStruct((m, n), F32),
        compiler_params=_cparams("parallel"),
        name="proj",
    )(xb, w)


def _head(a, h):
    return a[:, h * HEAD_DIM:(h + 1) * HEAD_DIM]


def _hgrn_kernel(q_ref, f_ref, i_ref, g_ref, lbp_ref, gn_ref, ones_ref, mask_ref, s0_ref, y_ref, s_out_ref,
                 st_ref, phi_ref, plo_ref, *, nc, bb):
    c = pl.program_id(1)

    @pl.when(c == 0)
    def _():
        for bi in range(bb):
            st_ref[bi] = jnp.zeros(st_ref.shape[1:], F32)
            for h in range(N_HEADS):
                st_ref[bi, h * HEAD_DIM:(h + 1) * HEAD_DIM, h * HEAD_DIM:(h + 1) * HEAD_DIM] = s0_ref[bi, h].T

    _round_robin([_hgrn_chunk(q_ref.at[bi], f_ref.at[bi], i_ref.at[bi], g_ref.at[bi], lbp_ref, gn_ref, ones_ref,
                              mask_ref, y_ref.at[bi], st_ref.at[bi], phi_ref.at[bi], plo_ref.at[bi])
                  for bi in range(bb)])

    @pl.when(c == nc - 1)
    def _():
        for bi in range(bb):
            for h in range(N_HEADS):
                s_out_ref[bi, h] = st_ref[bi, h * HEAD_DIM:(h + 1) * HEAD_DIM, h * HEAD_DIM:(h + 1) * HEAD_DIM].T


def _hgrn_chunk(q_ref, f_ref, i_ref, g_ref, lbp_ref, gn_ref, ones_ref, mask_ref, y_ref, st_ref, phi_ref, plo_ref):
    L = CHUNK
    ones_bd = ones_ref[...]
    masks = _head_masks()
    log_lb = lbp_ref[0:1, :]
    log1m_lb = lbp_ref[1:2, :]
    one_m_lb = lbp_ref[2:3, :]
    zf = f_ref[...]
    u = log_lb
    w = log1m_lb + _log_sigmoid(zf)
    logf = jnp.maximum(u, w) + jnp.log1p(jnp.exp(-jnp.abs(u - w)))
    k = one_m_lb * _sigmoid(-zf)
    q = _silu(q_ref[...])
    v = i_ref[...]

    b = _cumsum_rows(logf)
    yield
    b_last = b[L - 1:L]
    qe = q * jnp.exp(b)
    kdec = k * jnp.exp(b_last - b)
    sub_row = lax.broadcasted_iota(jnp.int32, (SUB, WIDTH), 0)

    for s in range(L):
        r0 = (s // SUB) * SUB
        p = q[r0:r0 + SUB] * k[s:s + 1] * jnp.exp(jnp.minimum(b[r0:r0 + SUB] - b[s:s + 1], 0.0))
        hi, lo = _split2(jnp.where(sub_row >= s - r0, p, 0.0))
        phi_ref[s * SUB:(s + 1) * SUB, :] = hi
        plo_ref[s * SUB:(s + 1) * SUB, :] = lo
    yield
    att = _dot(phi_ref[...], ones_bd) + _dot(plo_ref[...], ones_bd)
    st = st_ref[...]
    o_inter = _dot(qe, st, NT)
    upd = _dot(v, kdec, TN)
    scores = []
    for blk in range(1, L // SUB):
        r0 = blk * SUB
        ref_row = b[r0 - 1:r0]
        q_s = q[r0:r0 + SUB] * jnp.exp(b[r0:r0 + SUB] - ref_row)
        k_s = k[:r0] * jnp.exp(ref_row - b[:r0])
        scores.append(_dot(q_s, _stack_heads(k_s, masks), NT))
    yield
    st_ref[...] = st * jnp.exp(b_last) + upd * mask_ref[...]
    o_blocks = []
    for blk in range(L // SUB):
        r0 = blk * SUB
        acc = jnp.zeros((SUB, WIDTH), F32)
        for j in range(SUB):
            s = r0 + j
            acc = acc + att[s * SUB:(s + 1) * SUB] * v[s:s + 1]
        if blk > 0:
            acc = acc + _dot(scores[blk - 1], _stack_heads(v[:r0], masks))
        o_blocks.append(acc)
    yield
    o = jnp.concatenate(o_blocks, axis=0) + o_inter
    ms = _head_sums(o * o, ones_bd) * (1.0 / HEAD_DIM)
    yield
    y_ref[...] = (o * lax.rsqrt(ms + HEAD_NORM_EPS) * gn_ref[...] * _silu(g_ref[...])).astype(y_ref.dtype)


def _hgrn(h3, lbp, gn, mask_bd, s0):
    bsz, t, _ = h3.shape
    nc = t // CHUNK
    cb = COL_A // WIDTH
    bb = STREAMS_PER_STEP

    def col(j):
        return pl.BlockSpec((bb, CHUNK, WIDTH), lambda b, c, j=j: (b, c, cb + j))

    square = pl.BlockSpec((WIDTH, WIDTH), lambda b, c: (0, 0))
    state = pl.BlockSpec((bb, N_HEADS, HEAD_DIM, HEAD_DIM), lambda b, c: (b, 0, 0, 0))
    return pl.pallas_call(
        functools.partial(_hgrn_kernel, nc=nc, bb=bb),
        grid=(bsz // bb, nc),
        in_specs=[col(0), col(1), col(2), col(3),
                  pl.BlockSpec((8, WIDTH), lambda b, c: (0, 0)),
                  pl.BlockSpec((1, WIDTH), lambda b, c: (0, 0)),
                  square, square, state],
        out_specs=[pl.BlockSpec((bb, CHUNK, WIDTH), lambda b, c: (b, c, 0)), state],
        out_shape=[jax.ShapeDtypeStruct((bsz, t, WIDTH), BF16),
                   jax.ShapeDtypeStruct((bsz, N_HEADS, HEAD_DIM, HEAD_DIM), F32)],
        scratch_shapes=[pltpu.VMEM((bb, WIDTH, WIDTH), F32),
                        pltpu.VMEM((bb, CHUNK * SUB, WIDTH), BF16),
                        pltpu.VMEM((bb, CHUNK * SUB, WIDTH), BF16)],
        compiler_params=_cparams("parallel", "arbitrary"),
        name="hgrn2",
    )(h3, h3, h3, h3, lbp, gn, mask_bd.astype(BF16), mask_bd, s0)


def _mlstm_kernel(u_ref, v_ref, o_ref, gcol_ref, grow_ref, bcol_ref, brow_ref, cw_ref, vec_ref, wqk_ref,
                  ones_ref, mask_ref, c0_ref, n0_ref, m0_ref, conv0_ref,
                  y_ref, c_out_ref, n_out_ref, m_out_ref, conv_out_ref,
                  c_ref, n_ref, m_ref, carry_ref, *, nc, bb):
    c = pl.program_id(1)

    @pl.when(c == 0)
    def _():
        for bi in range(bb):
            c_ref[bi] = jnp.zeros(c_ref.shape[1:], F32)
            for h in range(N_HEADS):
                c_ref[bi, h * HEAD_DIM:(h + 1) * HEAD_DIM, h * HEAD_DIM:(h + 1) * HEAD_DIM] = c0_ref[bi, h]
        n_ref[...] = n0_ref[...]
        m_ref[...] = m0_ref[...]
        carry_ref[...] = conv0_ref[...]

    _round_robin([_mlstm_chunk(u_ref.at[bi], v_ref.at[bi], o_ref.at[bi], gcol_ref.at[bi], grow_ref.at[bi, 0],
                               bcol_ref, brow_ref, cw_ref, vec_ref, wqk_ref, ones_ref, mask_ref, y_ref.at[bi],
                               c_ref.at[bi], n_ref.at[bi], m_ref.at[bi], carry_ref.at[bi]) for bi in range(bb)])

    @pl.when(c == nc - 1)
    def _():
        for bi in range(bb):
            for h in range(N_HEADS):
                c_out_ref[bi, h] = c_ref[bi, h * HEAD_DIM:(h + 1) * HEAD_DIM, h * HEAD_DIM:(h + 1) * HEAD_DIM]
        n_out_ref[...] = n_ref[...]
        m_out_ref[...] = m_ref[...]
        conv_out_ref[...] = carry_ref[...]


def _mlstm_chunk(u_ref, v_ref, o_ref, gcol_ref, grow_ref, bcol_ref, brow_ref, cw_ref, vec_ref, wqk_ref,
                 ones_ref, mask_ref, y_ref, c_ref, n_ref, m_ref, carry_ref):
    L = CHUNK
    heads = range(N_HEADS)

    u = u_ref[...]
    carry = carry_ref[...]
    row8 = lax.broadcasted_iota(jnp.int32, (8, WIDTH), 0)
    conv = vec_ref[0:1, :] + cw_ref[CONV_W - 1:CONV_W, :] * u
    for d in range(1, CONV_W):
        rolled = pltpu.roll(u, d, 0)
        top = jnp.where(row8 < d, pltpu.roll(carry, d, 0), rolled[0:8])
        conv = conv + cw_ref[CONV_W - 1 - d:CONV_W - d, :] * jnp.concatenate([top, rolled[8:]], axis=0)
    carry_ref[...] = u[L - 8:L]
    cact = _silu(conv)
    qk_all = _dot(cact.astype(BF16), wqk_ref[...])

    row = lax.broadcasted_iota(jnp.int32, (L, L), 0)
    col = lax.broadcasted_iota(jnp.int32, (L, L), 1)
    lower = row >= col
    raw_c = gcol_ref[:, 0:2 * N_HEADS] + brow_ref[...]
    lane8 = lax.broadcasted_iota(jnp.int32, raw_c.shape, 1)
    gcol = jnp.where(lane8 < N_HEADS, raw_c, _log_sigmoid(raw_c))
    raw_r = grow_ref[...] + bcol_ref[...]
    sub8 = lax.broadcasted_iota(jnp.int32, raw_r.shape, 0)
    grow = jnp.where(sub8 < N_HEADS, raw_r, _log_sigmoid(raw_r))
    f_col = jnp.dot(lower.astype(F32), gcol, preferred_element_type=F32, precision=HIGHEST)
    f_row = jnp.dot(grow, (row <= col).astype(F32), preferred_element_type=F32, precision=HIGHEST)
    r8 = lax.broadcasted_iota(jnp.int32, (2 * N_HEADS, 2 * WIDTH), 0)
    c8 = lax.broadcasted_iota(jnp.int32, (2 * N_HEADS, 2 * WIDTH), 1)
    expand = (c8 // HEAD_DIM == r8).astype(BF16)
    x8 = jnp.where(lane8 < N_HEADS, gcol, f_col)
    x_hi, x_mid, x_lo = _split3(x8)
    both = _dot(x_hi, expand) + _dot(x_mid, expand) + _dot(x_lo, expand)
    li_all = both[:, 0:WIDTH]
    f_all = both[:, WIDTH:2 * WIDTH]
    d_row = jnp.concatenate([grow[h:h + 1, :] - f_row[N_HEADS + h:N_HEADS + h + 1, :] for h in heads], axis=1)
    yield
    t_idx = lax.broadcasted_iota(jnp.int32, (L, WIDTH), 0)
    s_idx = lax.broadcasted_iota(jnp.int32, (L, WIDTH), 1) % L
    ones_bd = ones_ref[...]
    masks = _head_masks()
    cm = li_all - f_all
    for sh in (1, 2, 4, 8, 16, 32):
        cm = jnp.maximum(cm, jnp.where(t_idx >= sh, pltpu.roll(cm, sh, 0), -jnp.inf))
    g = f_all + m_ref[...]
    mt = jnp.maximum(g, f_all + cm)
    wg = jnp.exp(g - mt)
    wd = jnp.exp(jnp.where(s_idx <= t_idx, f_all + d_row - mt, -jnp.inf))
    q = qk_all[:, 0:WIDTH]
    k = qk_all[:, WIDTH:2 * WIDTH]
    v = v_ref[...]
    q_bf = q.astype(BF16)
    v_bf = v.astype(BF16)
    head_rows = [mk.astype(BF16) for mk in masks]
    k_stack = jnp.concatenate([k.astype(BF16) * r for r in head_rows], axis=0)
    v_stack = jnp.concatenate([v_bf * r for r in head_rows], axis=0)
    c_bd = c_ref[...]
    n_row = n_ref[...]
    qk = _dot(q_bf, k_stack, NT) * wd
    q_c = _dot(q_bf, c_bd.astype(BF16))
    q_n = _head_sums(q * n_row, ones_bd)
    yield
    qkv = _dot(qk.astype(BF16), v_stack)
    qk_sum = _head_sums(qk, ones_bd)
    mt_last = mt[L - 1:L]
    kw = k * jnp.exp(f_all[L - 1:L] - f_all + li_all - mt_last)
    c_upd = _dot(kw.astype(BF16), v_bf, TN)
    yield
    wgl = wg[L - 1:L]
    c_ref[...] = wgl * c_bd + c_upd * mask_ref[...]
    n_ref[...] = wgl * n_row + jnp.sum(kw, axis=0, keepdims=True)
    m_ref[...] = mt_last
    den = wg * q_n + qk_sum
    hh = (wg * q_c + qkv) / jnp.maximum(jnp.abs(den), jnp.exp(-mt))
    z = _sigmoid(o_ref[...]) * hh
    mu = _head_sums(z, ones_bd) * (1.0 / HEAD_DIM)
    yield
    zc = z - mu
    var = _head_sums(zc * zc, ones_bd) * (1.0 / HEAD_DIM)
    yield
    y_ref[...] = (zc * lax.rsqrt(var + HEAD_NORM_EPS) * vec_ref[1:2, :] + vec_ref[2:3, :] * cact).astype(y_ref.dtype)


def _mlstm(h3, grow, b_col, b_row, conv_w, vecs, w_qk, mask_bd, c0, n0, m0, conv0):
    bsz, t, _ = h3.shape
    nc = t // CHUNK
    cb = COL_B // WIDTH
    bb = STREAMS_PER_STEP
    state4 = pl.BlockSpec((bb, N_HEADS, HEAD_DIM, HEAD_DIM), lambda b, c: (b, 0, 0, 0))
    state_n = pl.BlockSpec((bb, 1, WIDTH), lambda b, c: (b, 0, 0))
    state_m = state_n
    state_conv = pl.BlockSpec((bb, 8, WIDTH), lambda b, c: (b, 0, 0))

    def const(shape):
        return pl.BlockSpec(shape, lambda b, c: tuple(0 for _ in shape))

    def col(j):
        return pl.BlockSpec((bb, CHUNK, WIDTH), lambda b, c, j=j: (b, c, cb + j))

    return pl.pallas_call(
        functools.partial(_mlstm_kernel, nc=nc, bb=bb),
        grid=(bsz // bb, nc),
        in_specs=[col(0), col(1), col(2),
                  pl.BlockSpec((bb, CHUNK, LANE_TILE), lambda b, c: (b, c, COL_IF // LANE_TILE)),
                  pl.BlockSpec((bb, 1, 2 * N_HEADS, CHUNK), lambda b, c: (b, c, 0, 0)),
                  const((2 * N_HEADS, 1)), const((1, 2 * N_HEADS)), const((8, WIDTH)), const((8, WIDTH)),
                  const((WIDTH, 2 * WIDTH)), const((WIDTH, WIDTH)), const((WIDTH, WIDTH)),
                  state4, state_n, state_m, state_conv],
        out_specs=[pl.BlockSpec((bb, CHUNK, WIDTH), lambda b, c: (b, c, 0)), state4, state_n, state_m, state_conv],
        out_shape=[jax.ShapeDtypeStruct((bsz, t, WIDTH), BF16),
                   jax.ShapeDtypeStruct((bsz, N_HEADS, HEAD_DIM, HEAD_DIM), F32),
                   jax.ShapeDtypeStruct((bsz, 1, WIDTH), F32),
                   jax.ShapeDtypeStruct((bsz, 1, WIDTH), F32),
                   jax.ShapeDtypeStruct((bsz, 8, WIDTH), F32)],
        scratch_shapes=[pltpu.VMEM((bb, WIDTH, WIDTH), F32),
                        pltpu.VMEM((bb, 1, WIDTH), F32),
                        pltpu.VMEM((bb, 1, WIDTH), F32),
                        pltpu.VMEM((bb, 8, WIDTH), F32)],
        compiler_params=_cparams("parallel", "arbitrary"),
        name="mlstm",
    )(h3, h3, h3, h3, grow, b_col, b_row, conv_w, vecs, w_qk, mask_bd.astype(BF16), mask_bd, c0, n0, m0, conv0)


SWA_CHUNKS_PER_STEP = 4


def _swa_kernel(*refs, prev_valid, cps, multi_step):
    if multi_step:
        (q_ref, kc_ref, vc_ref, kp_ref, vp_ref, k_ref, v_ref, bias_ref, sink_ref, y_ref, s_ref, p_ref) = refs
    else:
        (q_ref, kc_ref, vc_ref, k_ref, v_ref, bias_ref, sink_ref, y_ref, s_ref, p_ref) = refs
    g = pl.program_id(1)
    L = CHUNK
    span = WINDOW + L
    k_win = kc_ref[0]
    v_win = vc_ref[0]
    if multi_step:
        k_win = jnp.where(g == 0, k_win, kp_ref[0])
        v_win = jnp.where(g == 0, v_win, vp_ref[0])
    kcat = jnp.concatenate([k_win, k_ref[0]], axis=0).astype(BF16)
    vcat = jnp.concatenate([v_win, v_ref[0]], axis=0).astype(BF16)
    kv_of = [h // (N_HEADS // C_KV_HEADS) for h in range(N_HEADS)]
    for j in range(cps):
        q = q_ref[0, j * L:(j + 1) * L, :].astype(BF16)
        k_j = kcat[j * L:j * L + span]
        for h in range(N_HEADS):
            s_ref[j * N_HEADS + h] = _dot(_head(q, h), _head(k_j, kv_of[h]), NT)
    def softmax_tile(j, h):
        key_pos = lax.broadcasted_iota(jnp.int32, (L, span), 1) + (g * cps + j) * L
        s = s_ref[j * N_HEADS + h] * HEAD_DIM ** -0.5 + bias_ref[h]
        if not prev_valid:
            s = jnp.where(key_pos >= WINDOW, s, -jnp.inf)
        sink = sink_ref[0:1, h:h + 1]
        m = jnp.maximum(jnp.max(s, axis=1, keepdims=True), sink)
        yield
        p = jnp.exp(s - m)
        denom = jnp.sum(p, axis=1, keepdims=True) + jnp.exp(sink - m)
        yield
        p_ref[j * N_HEADS + h] = (p * (1.0 / denom)).astype(BF16)

    _round_robin([softmax_tile(j, h) for j in range(cps) for h in range(N_HEADS)])
    for j in range(cps):
        v_j = vcat[j * L:j * L + span]
        o = [_dot(p_ref[j * N_HEADS + h], _head(v_j, kv_of[h])) for h in range(N_HEADS)]
        y_ref[0, j * L:(j + 1) * L, :] = jnp.concatenate(o, axis=1).astype(y_ref.dtype)


def _swa(h3, cache_k, cache_v, bias, sinks, prev_valid):
    bsz, t, _ = h3.shape
    nc = t // CHUNK
    cps = min(SWA_CHUNKS_PER_STEP, nc)
    rows = cps * CHUNK
    steps = nc // cps
    multi_step = steps > 1
    assert nc % cps == 0 and (not multi_step or rows % WINDOW == 0), (t, cps)
    qb = COL_C // WIDTH
    kb = (COL_C + WIDTH) // KV_WIDTH
    cache = pl.BlockSpec((1, WINDOW, KV_WIDTH), lambda b, g: (b, 0, 0))

    def own(col):
        return pl.BlockSpec((1, rows, KV_WIDTH), lambda b, g, col=col: (b, g, col))

    def before(col):
        return pl.BlockSpec((1, WINDOW, KV_WIDTH),
                            lambda b, g, col=col: (b, jnp.maximum(g * (rows // WINDOW) - 1, 0), col))

    kv_specs = [cache, cache] + ([before(kb), before(kb + 1)] if multi_step else []) + [own(kb), own(kb + 1)]
    kv_args = [cache_k, cache_v] + ([h3, h3] if multi_step else []) + [h3, h3]
    return pl.pallas_call(
        functools.partial(_swa_kernel, prev_valid=prev_valid, cps=cps, multi_step=multi_step),
        grid=(bsz, steps),
        in_specs=[pl.BlockSpec((1, rows, WIDTH), lambda b, g: (b, g, qb))] + kv_specs + [
                  pl.BlockSpec((N_HEADS, CHUNK, WINDOW + CHUNK), lambda b, g: (0, 0, 0)),
                  pl.BlockSpec((1, N_HEADS), lambda b, g: (0, 0))],
        out_specs=pl.BlockSpec((1, rows, WIDTH), lambda b, g: (b, g, 0)),
        out_shape=jax.ShapeDtypeStruct((bsz, t, WIDTH), BF16),
        scratch_shapes=[pltpu.VMEM((cps * N_HEADS, CHUNK, WINDOW + CHUNK), F32),
                        pltpu.VMEM((cps * N_HEADS, CHUNK, WINDOW + CHUNK), BF16)],
        compiler_params=_cparams("parallel", "parallel"),
        name="swa",
    )(h3, *kv_args, bias, sinks)


def _rwkv_kernel(pd_ref, mu_ref, vec_ref, wwa_ref, gup_ref, ones_ref, mask_ref, s0_ref, sh0_ref,
                 y_ref, s_out_ref, sh_out_ref, st_ref, carry_ref, *, nc, bb):
    c = pl.program_id(1)

    @pl.when(c == 0)
    def _():
        for bi in range(bb):
            st_ref[bi] = jnp.zeros(st_ref.shape[1:], F32)
            for h in range(N_HEADS):
                st_ref[bi, h * HEAD_DIM:(h + 1) * HEAD_DIM, h * HEAD_DIM:(h + 1) * HEAD_DIM] = s0_ref[bi, h]
            carry_ref[bi, 7:8, :] = sh0_ref[bi]

    _round_robin([_rwkv_chunk(pd_ref.at[bi], mu_ref, vec_ref, wwa_ref, gup_ref, ones_ref, mask_ref,
                              y_ref.at[bi], st_ref.at[bi], carry_ref.at[bi]) for bi in range(bb)])

    @pl.when(c == nc - 1)
    def _():
        for bi in range(bb):
            for h in range(N_HEADS):
                s_out_ref[bi, h] = st_ref[bi, h * HEAD_DIM:(h + 1) * HEAD_DIM, h * HEAD_DIM:(h + 1) * HEAD_DIM]
            sh_out_ref[bi] = carry_ref[bi, 7:8, :]


def _rwkv_chunk(pd_ref, mu_ref, vec_ref, wwa_ref, gup_ref, ones_ref, mask_ref, y_ref, st_ref, carry_ref):
    L = CHUNK
    ones_bd = ones_ref[...]
    masks = _head_masks()

    pd = pd_ref[...]
    rolled = pltpu.roll(pd, 1, 0)
    row8 = lax.broadcasted_iota(jnp.int32, (8, D_COLS), 0)
    top = jnp.where(row8 == 0, carry_ref[7:8, :], rolled[0:8])
    pd_prev = jnp.concatenate([top, rolled[8:]], axis=0)
    carry_ref[...] = pd[L - 8:L]
    mixed = pd + (pd_prev - pd) * mu_ref[...]
    r = mixed[:, 0:WIDTH]
    k_raw = mixed[:, WIDTH:2 * WIDTH]
    v = mixed[:, 2 * WIDTH:3 * WIDTH]
    wa = mixed[:, 3 * WIDTH:3 * WIDTH + 2 * D_DECAY_LORA]
    g_in = mixed[:, 3 * WIDTH + 2 * D_DECAY_LORA:D_COLS]
    lane_wa = lax.broadcasted_iota(jnp.int32, wa.shape, 1)
    wa_act = jnp.where(lane_wa < D_DECAY_LORA, jnp.tanh(wa), wa)
    lora = _dot(wa_act.astype(BF16), wwa_ref[...])
    g_d = _dot(_sigmoid(g_in).astype(BF16), gup_ref[...])
    yield
    z = -(vec_ref[0:1, :] + lora[:, 0:WIDTH])
    w_log = -(jnp.maximum(z, 0.0) + jnp.log1p(jnp.exp(-jnp.abs(z)))) - 0.5
    lw = -jnp.exp(w_log)
    a_d = _sigmoid(vec_ref[1:2, :] + lora[:, WIDTH:2 * WIDTH])
    kk = k_raw * vec_ref[2:3, :]
    kk_norm = _head_sums(kk * kk, ones_bd)
    cw = _cumsum_rows(lw)
    yield
    kk = kk * lax.rsqrt(jnp.maximum(kk_norm, 1e-24))
    k = k_raw * (1.0 + (a_d - 1.0) * vec_ref[3:4, :])
    a = -kk
    b = kk * a_d
    g_in_c = jnp.exp(cw)
    g_inv = jnp.exp(-cw)
    at = a * jnp.exp(cw - lw)
    rt = r * g_in_c
    bt = b * g_inv
    kt = k * g_inv
    x = jnp.concatenate([at, rt], axis=0)
    bt_st = _stack_parts(bt, masks)
    kt_st = _stack_parts(kt, masks)
    gram_b = _mm2(x, bt_st, NT)
    gram_k = _mm2(x, kt_st, NT)
    st = st_ref[...]
    u = _mm2(x, _split2(st), NT)
    yield

    t_idx = lax.broadcasted_iota(jnp.int32, (L, WIDTH), 0)
    i_idx = lax.broadcasted_iota(jnp.int32, (L, WIDTH), 1) % L
    strict = i_idx < t_idx
    incl = i_idx <= t_idx
    same_blk = (i_idx // SUB) == (t_idx // SUB)
    eye = (i_idx == t_idx).astype(F32)
    n_all = jnp.where(strict, gram_b[0:L], 0.0)
    m_all = jnp.where(strict, gram_k[0:L], 0.0)
    rb_all = jnp.where(incl, gram_b[L:2 * L], 0.0)
    rk_all = jnp.where(incl, gram_k[L:2 * L], 0.0)
    n_d = jnp.where(same_blk, n_all, 0.0)
    n_off = jnp.where(same_blk, 0.0, n_all)

    v_st = _stack_parts(v, masks)
    x2 = _mm2(n_d, _stack_parts(n_d, masks))
    rhs = u[0:L] + _mm2(m_all, v_st)
    bonus = _head_sums(r * k * vec_ref[4:5, :], ones_bd) * v
    yield
    x2_st = _stack_parts(x2, masks)
    x4 = _mm2(x2, x2_st)
    t_d = eye + n_d
    t_d = t_d + _mm2(t_d, x2_st)
    yield
    x4_st = _stack_parts(x4, masks)
    x8 = _mm2(x4, x4_st)
    t_d = t_d + _mm2(t_d, x4_st)
    yield
    t_d = t_d + _mm2(t_d, _stack_parts(x8, masks))
    yield
    n1 = _mm2(t_d, _stack_parts(n_off, masks))
    yield
    n1_st = _stack_parts(n1, masks)
    n2 = _mm2(n1, n1_st)
    yield
    a2 = eye + n1 + n2 + _mm2(n2, n1_st)
    yield
    t_full = _mm2(a2, _stack_parts(t_d, masks))
    yield
    sa = _mm2(t_full, _stack_parts(rhs, masks))
    yield
    y = u[L:2 * L] + _mm2(rb_all, _stack_parts(sa, masks)) + _mm2(rk_all, v_st)
    sv_hi, sv_lo = _split2(jnp.concatenate([sa, v], axis=0))
    bk_hi, bk_lo = _split2(jnp.concatenate([bt, kt], axis=0))
    upd = _dot(sv_hi, bk_hi, TN) + _dot(sv_lo, bk_hi, TN) + _dot(sv_hi, bk_lo, TN)
    yield
    st_ref[...] = (st + upd * mask_ref[...]) * g_in_c[L - 1:L]

    mu = _head_sums(y, ones_bd) * (1.0 / HEAD_DIM)
    yield
    yc = y - mu
    var = _head_sums(yc * yc, ones_bd) * (1.0 / HEAD_DIM)
    yield
    o = yc * lax.rsqrt(var + RWKV_GN_EPS) * vec_ref[5:6, :] + vec_ref[6:7, :]
    y_ref[...] = ((o + bonus) * g_d).astype(y_ref.dtype)


def _rwkv(h3, mu, vecs, w_wa, g_up, mask_bd, s0, shift0):
    bsz, t, _ = h3.shape
    nc = t // CHUNK
    bb = STREAMS_PER_STEP
    state = pl.BlockSpec((bb, N_HEADS, HEAD_DIM, HEAD_DIM), lambda g, i: (g, 0, 0, 0))
    shift = pl.BlockSpec((bb, 1, D_COLS), lambda g, i: (g, 0, 0))

    def const(shape):
        return pl.BlockSpec(shape, lambda g, i: tuple(0 for _ in shape))

    return pl.pallas_call(
        functools.partial(_rwkv_kernel, nc=nc, bb=bb),
        grid=(bsz // bb, nc),
        in_specs=[pl.BlockSpec((bb, CHUNK, D_COLS), lambda g, i: (g, i, COL_D // D_COLS)),
                  const((1, D_COLS)), const((8, WIDTH)), const((2 * D_DECAY_LORA, 2 * WIDTH)),
                  const((D_GATE_LORA, WIDTH)), const((WIDTH, WIDTH)), const((WIDTH, WIDTH)), state, shift],
        out_specs=[pl.BlockSpec((bb, CHUNK, WIDTH), lambda g, i: (g, i, 0)), state, shift],
        out_shape=[jax.ShapeDtypeStruct((bsz, t, WIDTH), BF16),
                   jax.ShapeDtypeStruct((bsz, N_HEADS, HEAD_DIM, HEAD_DIM), F32),
                   jax.ShapeDtypeStruct((bsz, 1, D_COLS), F32)],
        scratch_shapes=[pltpu.VMEM((bb, WIDTH, WIDTH), F32), pltpu.VMEM((bb, 8, D_COLS), F32)],
        compiler_params=_cparams("parallel", "arbitrary"),
        name="rwkv7",
    )(h3, mu, vecs, w_wa, g_up, mask_bd.astype(BF16), mask_bd, s0, shift0)


def _merge_kernel(ya_ref, yb_ref, yc_ref, yd_ref, xb_ref, wg_ref, wbr_ref, wo_ref, x_ref,
                  lng_ref, lnb_ref, xo_ref, xob_ref):
    ys = (ya_ref, yb_ref, yc_ref, yd_ref)
    xb = xb_ref[...]
    merged = None
    for n in range(N_BRANCH):
        gate = _dot(xb, wg_ref[:, n * D_MODEL:(n + 1) * D_MODEL])
        term = _sigmoid(gate) * _dot(ys[n][...], wbr_ref[n])
        merged = term if merged is None else merged + term
    out = _dot(merged.astype(BF16), wo_ref[...])
    xn = _layernorm_rows(ALPHA * x_ref[...] + out, lng_ref[...], lnb_ref[...])
    xo_ref[...] = xn
    xob_ref[...] = xn.astype(BF16)


def _merge(ys, xb2, w_gate, wbr, wo, x2, lng, lnb, *, tm=512):
    m = x2.shape[0]
    tm = min(tm, m)
    ysp = pl.BlockSpec((tm, WIDTH), lambda i: (i, 0))
    row = pl.BlockSpec((tm, D_MODEL), lambda i: (i, 0))
    vec = pl.BlockSpec((1, D_MODEL), lambda i: (0, 0))
    resident = dict(pipeline_mode=pl.Buffered(1))
    return pl.pallas_call(
        _merge_kernel,
        grid=(m // tm,),
        in_specs=[ysp, ysp, ysp, ysp, row,
                  pl.BlockSpec((D_MODEL, GATE_COLS), lambda i: (0, 0), **resident),
                  pl.BlockSpec((N_BRANCH, WIDTH, D_MODEL), lambda i: (0, 0, 0), **resident),
                  pl.BlockSpec((D_MODEL, D_MODEL), lambda i: (0, 0), **resident),
                  row, vec, vec],
        out_specs=[row, row],
        out_shape=[jax.ShapeDtypeStruct((m, D_MODEL), F32), jax.ShapeDtypeStruct((m, D_MODEL), BF16)],
        compiler_params=_cparams("parallel"),
        name="merge",
    )(*ys, xb2, w_gate, wbr, wo, x2, lng, lnb)


def _ffn_kernel(*refs, n_steps, gated):
    if gated:
        xb_ref, x_ref, gates_ref, wg_ref, wu_ref, wd_ref, lng_ref, lnb_ref, xo_ref, xob_ref, acc_ref = refs
    else:
        xb_ref, x_ref, wg_ref, wu_ref, wd_ref, lng_ref, lnb_ref, xo_ref, xob_ref, acc_ref = refs
    j = pl.program_id(1)

    @pl.when(j == 0)
    def _():
        acc_ref[...] = jnp.zeros_like(acc_ref)

    xb = xb_ref[...]
    hg = _dot(xb, wg_ref[0])
    hu = _dot(xb, wu_ref[0])
    part = _dot((_silu(hg) * hu).astype(BF16), wd_ref[0])
    if gated:
        gates = gates_ref[...]
        lane = lax.broadcasted_iota(jnp.int32, gates.shape, 1)
        part = jnp.sum(jnp.where(lane == j, gates, 0.0), axis=1, keepdims=True) * part
    acc_ref[...] += part

    @pl.when(j == n_steps - 1)
    def _():
        xn = _layernorm_rows(ALPHA * x_ref[...] + acc_ref[...], lng_ref[...], lnb_ref[...])
        xo_ref[...] = xn
        xob_ref[...] = xn.astype(BF16)


def _ffn(xb, x2, wg, wu, wd, lng, lnb, gates=None, *, tm=512):
    m = x2.shape[0]
    tm = min(tm, m)
    n_steps, _, tf = wg.shape
    row = pl.BlockSpec((tm, D_MODEL), lambda i, j: (i, 0))
    vec = pl.BlockSpec((1, D_MODEL), lambda i, j: (0, 0))
    w_in_spec = pl.BlockSpec((1, D_MODEL, tf), lambda i, j: (j, 0, 0))
    w_out_spec = pl.BlockSpec((1, tf, D_MODEL), lambda i, j: (j, 0, 0))
    in_specs = [row, row]
    args = [xb, x2]
    if gates is not None:
        in_specs.append(pl.BlockSpec((tm, N_EXPERTS), lambda i, j: (i, 0)))
        args.append(gates)
    in_specs += [w_in_spec, w_in_spec, w_out_spec, vec, vec]
    args += [wg, wu, wd, lng, lnb]
    return pl.pallas_call(
        functools.partial(_ffn_kernel, n_steps=n_steps, gated=gates is not None),
        grid=(m // tm, n_steps),
        in_specs=in_specs,
        out_specs=[row, row],
        out_shape=[jax.ShapeDtypeStruct((m, D_MODEL), F32), jax.ShapeDtypeStruct((m, D_MODEL), BF16)],
        scratch_shapes=[pltpu.VMEM((tm, D_MODEL), F32)],
        compiler_params=_cparams("parallel", "arbitrary"),
        name="moe" if gates is not None else "ffn",
    )(*args)


STATE_NAMES = ('swa_k', 'swa_v', 'hgrn', 'mlstm_c', 'mlstm_n', 'mlstm_m', 'mlstm_conv', 'rwkv', 'rwkv_shift')

_D_ORIG = (('r', WIDTH), ('w', D_DECAY_LORA), ('k', WIDTH), ('v', WIDTH), ('a', D_AAA_LORA), ('g', D_GATE_LORA))
_D_KERNEL = ('r', 'k', 'v', 'w', 'a', 'g')


def _d_pieces(arr):
    out, off = {}, 0
    for name, size in _D_ORIG:
        out[name] = arr[..., off:off + size]
        off += size
    return out


def _d_to_kernel_order(arr):
    p = _d_pieces(arr)
    return jnp.concatenate([p[n] for n in _D_KERNEL], axis=-1)


def _d_to_original_order(arr):
    sizes = dict(_D_ORIG)
    p, off = {}, 0
    for name in _D_KERNEL:
        p[name] = arr[..., off:off + sizes[name]]
        off += sizes[name]
    return jnp.concatenate([p[n] for n, _ in _D_ORIG], axis=-1)


def _rel_bucket(rel):
    half = NUM_BUCKETS // 2
    exact = half // 2
    dist = jnp.abs(rel)
    far = exact + (jnp.log(jnp.maximum(dist, 1).astype(F32) / exact)
                   / math.log(MAX_DISTANCE / exact) * (half - exact)).astype(jnp.int32)
    far = jnp.minimum(far, half - 1)
    return jnp.where(rel > 0, half, 0) + jnp.where(dist < exact, dist, far)


def _lower_bounds(lb_raw):
    sm = jax.nn.softmax(lb_raw.astype(F32), axis=0)
    lb = jnp.concatenate([jnp.zeros_like(sm[:1]), jnp.cumsum(sm[1:], axis=0)[:-1]], axis=0)
    return jnp.clip(lb, 0.0, LB_CEIL)


def _block_diag(blocks):
    rows = sum(b.shape[0] for b in blocks)
    cols = sum(b.shape[1] for b in blocks)
    out = jnp.zeros((rows, cols), blocks[0].dtype)
    r = c = 0
    for b in blocks:
        out = out.at[r:r + b.shape[0], c:c + b.shape[1]].set(b)
        r += b.shape[0]
        c += b.shape[1]
    return out


def _pad_rows(a, rows):
    return jnp.concatenate([a, jnp.zeros((rows - a.shape[0],) + a.shape[1:], a.dtype)], axis=0)


def _mixer(x3, xb2, st, prev_valid, lp, l):
    bsz, t, _ = x3.shape
    m = bsz * t
    x2 = x3.reshape(m, D_MODEL)
    h2 = _proj(xb2, lp['w_main'][l])
    h3 = h2.reshape(bsz, t, MAIN_COLS)

    y_a, s_hgrn = _hgrn(h3, lp['lbp'][l], lp['gn_a'][l][None], lp['mask_bd'], st['hgrn'])

    grow = jnp.swapaxes(h3[..., COL_IF:COL_IF + 2 * N_HEADS].reshape(bsz, t // CHUNK, CHUNK, 2 * N_HEADS), 2, 3)
    conv0 = jnp.concatenate([jnp.zeros((bsz, 8 - (CONV_W - 1), WIDTH), F32), st['mlstm_conv']], axis=1)
    y_b, mc, mn, mm, conv8 = _mlstm(h3, grow, lp['b_if'][l][:, None],
                                    lp['b_if'][l][None, :], lp['conv_w8'][l], lp['vec_b'][l], lp['w_qk'][l],
                                    lp['mask_bd'], st['mlstm_c'], st['mlstm_n'].reshape(bsz, 1, WIDTH),
                                    jnp.repeat(st['mlstm_m'], HEAD_DIM, axis=-1)[:, None, :], conv0)
    mn = mn.reshape(bsz, N_HEADS, HEAD_DIM)
    mm = mm[:, :, ::HEAD_DIM]
    conv_state = conv8[:, 8 - (CONV_W - 1):]

    k_off = COL_C + WIDTH
    cache_k = st['swa_k'].reshape(bsz, WINDOW, KV_WIDTH)
    cache_v = st['swa_v'].reshape(bsz, WINDOW, KV_WIDTH)
    y_c = _swa(h3, cache_k, cache_v, lp['bias'], lp['sinks'][l][None], prev_valid)
    keep = st['keep']

    def window(cache, col):
        new = h3[:, max(t - keep, 0):, col:col + KV_WIDTH]
        if t < keep:
            new = jnp.concatenate([cache[:, WINDOW - (keep - t):], new], axis=1)
        return new.reshape(bsz, keep, C_KV_HEADS, HEAD_DIM)

    k_win = window(cache_k, k_off)
    v_win = window(cache_v, k_off + KV_WIDTH)

    y_d, s_rwkv, shift = _rwkv(h3, lp['mu_d'][l][None], lp['vec_d'][l], lp['w_wa'][l], lp['g_up_d'][l],
                               lp['mask_bd'], st['rwkv'], _d_to_kernel_order(st['rwkv_shift'])[:, None, :])
    shift_state = _d_to_original_order(shift[:, 0, :])

    ys = [y.reshape(m, WIDTH) for y in (y_a, y_b, y_c, y_d)]
    x1, x1b = _merge(ys, xb2, lp['w_gate'][l], lp['w_br'][l], lp['w_o'][l], x2,
                     lp['ln1_g'][l][None], lp['ln1_b'][l][None])
    new_st = {'swa_k': k_win, 'swa_v': v_win, 'hgrn': s_hgrn, 'mlstm_c': mc, 'mlstm_n': mn,
              'mlstm_m': mm[:, 0, :], 'mlstm_conv': conv_state, 'rwkv': s_rwkv, 'rwkv_shift': shift_state}
    return x1, x1b, new_st


def _trunk(x3, states, prev_valid, keep, lp):
    bsz, t, _ = x3.shape
    m = bsz * t
    xb2 = x3.reshape(m, D_MODEL).astype(BF16)
    collected = {name: [] for name in STATE_NAMES}
    for l in range(DEPTH):
        st = {name: states[name][l] for name in STATE_NAMES}
        st['keep'] = keep
        x1, x1b, new_st = _mixer(x3, xb2, st, prev_valid, lp, l)
        j = l // 2
        ln_g, ln_b = lp['ln2_g'][l][None], lp['ln2_b'][l][None]
        if l % 2 == 0:
            x2, xb2 = _ffn(x1b, x1, lp['ffn_wg'][j], lp['ffn_wu'][j], lp['ffn_wd'][j], ln_g, ln_b)
        else:
            logits = _matmul(x1b, lp['router_w'][j], tm=1024, tn=N_EXPERTS) + lp['router_b'][j]
            top_v, top_i = lax.top_k(logits, TOP_K)
            probs = jax.nn.softmax(top_v, axis=-1)
            gates = jnp.einsum('mk,mke->me', probs, jax.nn.one_hot(top_i, N_EXPERTS, dtype=F32))
            x2, xb2 = _ffn(x1b, x1, lp['exp_wg'][j], lp['exp_wu'][j], lp['exp_wd'][j], ln_g, ln_b, gates)
        x3 = x2.reshape(bsz, t, D_MODEL)
        for name in STATE_NAMES:
            collected[name].append(new_st[name])
    return x3, {name: jnp.stack(collected[name]) for name in STATE_NAMES}


def kernel(x_prompt, x_sample, cache_swa_k, cache_swa_v, state_hgrn, state_mlstm_c, state_mlstm_n, state_mlstm_m, state_mlstm_conv, state_rwkv, state_rwkv_shift, w_in, lb_raw, gn_a, conv_w, conv_b, wq_b, wk_b, b_i, b_f, gn_b, skip_b, sinks, rel_bias, mu_d, w0_d, w_up_d, a0_d, a_up_d, g_up_d, k_k_d, k_a_d, r_k_d, gn_w_d, gn_b_d, w_br, w_o, ln1_g, ln1_b, ln2_g, ln2_b, ffn_w_gate, ffn_w_up, ffn_w_down, router_w, router_b, exp_w_gate, exp_w_up, exp_w_down):
    off_if = 4 * WIDTH + 3 * WIDTH
    off_c = off_if + 2 * N_HEADS
    off_d = off_c + WIDTH + 2 * KV_WIDTH
    off_gate = off_d + D_COLS
    w_main = jnp.concatenate([w_in[:, :, :4 * WIDTH], _d_to_kernel_order(w_in[:, :, off_d:off_gate]),
                              w_in[:, :, 4 * WIDTH:off_if], w_in[:, :, off_c:off_d], w_in[:, :, off_if:off_c],
                              jnp.zeros((DEPTH, D_MODEL, LANE_TILE - 2 * N_HEADS), w_in.dtype)], axis=-1).astype(BF16)
    w_gate = w_in[:, :, off_gate:].astype(BF16)

    lb = _lower_bounds(lb_raw)
    lb = lb[jnp.minimum(jnp.arange(DEPTH), lb.shape[0] - 1)]
    lbp = jnp.stack([jnp.log(jnp.maximum(lb, LB_FLOOR)), jnp.log1p(-lb), 1.0 - lb], axis=1)
    lbp = jnp.concatenate([lbp, jnp.zeros((DEPTH, 5, WIDTH), F32)], axis=1)

    w_qk = jnp.stack([jnp.concatenate([_block_diag(list(wq_b[l])), _block_diag(list(wk_b[l])) * HEAD_DIM ** -0.5],
                                      axis=1) for l in range(DEPTH)]).astype(BF16)
    w_wa = jnp.stack([_block_diag([w_up_d[l], a_up_d[l]]) for l in range(DEPTH)]).astype(BF16)
    vec_b = jnp.stack([_pad_rows(jnp.stack([conv_b[l], gn_b[l], skip_b[l]]), 8) for l in range(DEPTH)])
    conv_w8 = jnp.stack([_pad_rows(conv_w[l], 8) for l in range(DEPTH)])
    vec_d = jnp.stack([_pad_rows(jnp.stack([w0_d[l], a0_d[l], k_k_d[l], k_a_d[l], r_k_d[l], gn_w_d[l], gn_b_d[l]]), 8)
                       for l in range(DEPTH)])
    head_of = jnp.arange(WIDTH) // HEAD_DIM
    mask_bd = (head_of[:, None] == head_of[None, :]).astype(F32)

    span = WINDOW + CHUNK
    rel = jnp.arange(span)[None, :] - WINDOW - jnp.arange(CHUNK)[:, None]
    bias = jnp.transpose(rel_bias.astype(F32)[_rel_bucket(rel)], (2, 0, 1))

    n_dense = ffn_w_gate.shape[0]
    ff_steps = D_FF // D_FF_EXPERT
    lp = {
        'w_main': w_main, 'w_gate': w_gate, 'lbp': lbp, 'gn_a': gn_a,
        'mask_bd': mask_bd, 'conv_w8': conv_w8, 'vec_b': vec_b, 'w_qk': w_qk,
        'b_if': jnp.concatenate([b_i, b_f], axis=-1), 'sinks': sinks, 'bias': bias,
        'mu_d': _d_to_kernel_order(mu_d), 'vec_d': vec_d, 'w_wa': w_wa, 'g_up_d': g_up_d.astype(BF16),
        'w_br': w_br.astype(BF16), 'w_o': w_o.astype(BF16),
        'ln1_g': ln1_g, 'ln1_b': ln1_b, 'ln2_g': ln2_g, 'ln2_b': ln2_b,
        'ffn_wg': jnp.swapaxes(ffn_w_gate.astype(BF16).reshape(n_dense, D_MODEL, ff_steps, D_FF_EXPERT), 1, 2),
        'ffn_wu': jnp.swapaxes(ffn_w_up.astype(BF16).reshape(n_dense, D_MODEL, ff_steps, D_FF_EXPERT), 1, 2),
        'ffn_wd': ffn_w_down.astype(BF16).reshape(n_dense, ff_steps, D_FF_EXPERT, D_MODEL),
        'router_w': router_w.astype(BF16), 'router_b': router_b,
        'exp_wg': exp_w_gate.astype(BF16), 'exp_wu': exp_w_up.astype(BF16), 'exp_wd': exp_w_down.astype(BF16),
    }

    sample_states = {
        'swa_k': cache_swa_k, 'swa_v': cache_swa_v, 'hgrn': state_hgrn, 'mlstm_c': state_mlstm_c,
        'mlstm_n': state_mlstm_n, 'mlstm_m': state_mlstm_m, 'mlstm_conv': state_mlstm_conv,
        'rwkv': state_rwkv, 'rwkv_shift': state_rwkv_shift,
    }
    keep = cache_swa_k.shape[2]
    bp = x_prompt.shape[0]
    prompt_states = {}
    for name in STATE_NAMES:
        arr = sample_states[name]
        rows = (WINDOW,) + arr.shape[3:] if name in ('swa_k', 'swa_v') else arr.shape[2:]
        prompt_states[name] = jnp.zeros((DEPTH, bp) + tuple(rows), arr.dtype)

    y_prompt, pst = _trunk(x_prompt, prompt_states, False, keep, lp)
    y_sample, sst = _trunk(x_sample, sample_states, True, keep, lp)
    return (y_prompt, y_sample) + tuple(pst[n] for n in STATE_NAMES) + tuple(sst[n] for n in STATE_NAMES)
```

```python
import functools
import math

import jax
import jax.numpy as jnp
from jax import lax
from jax.experimental import pallas as pl
from jax.experimental.pallas import tpu as pltpu

F32 = jnp.float32
BF16 = jnp.bfloat16
HIGHEST = lax.Precision.HIGHEST

D_MODEL = 1024
DEPTH = 4
CHUNK = 64
HEAD_DIM = 64
N_HEADS = 4
WIDTH = N_HEADS * HEAD_DIM
C_KV_HEADS = 2
KV_WIDTH = C_KV_HEADS * HEAD_DIM
CONV_W = 4
WINDOW = 128
NUM_BUCKETS = 32
MAX_DISTANCE = 128
D_DECAY_LORA = 64
D_AAA_LORA = 64
D_GATE_LORA = 128
D_COLS = 3 * WIDTH + D_DECAY_LORA + D_AAA_LORA + D_GATE_LORA
N_BRANCH = 4
D_FF = 2816
N_EXPERTS = 8
TOP_K = 2
D_FF_EXPERT = 1408
ALPHA = (2 * DEPTH) ** 0.25
LN_EPS = 1e-5
HEAD_NORM_EPS = 1e-5
RWKV_GN_EPS = 64e-5
LB_FLOOR = 1e-30
LB_CEIL = 1.0 - 1e-6

GATE_COLS = N_BRANCH * D_MODEL
COL_A = 0
COL_D = COL_A + 4 * WIDTH
COL_B = COL_D + D_COLS
COL_C = COL_B + 3 * WIDTH
COL_IF = COL_C + WIDTH + 2 * KV_WIDTH
LANE_TILE = 128
MAIN_COLS = COL_IF + LANE_TILE
SUB = 16
STREAMS_PER_STEP = 4
V7X_VMEM_LIMIT = 48 * 1024 * 1024

NN = (((1,), (0,)), ((), ()))
NT = (((1,), (1,)), ((), ()))
TN = (((0,), (0,)), ((), ()))


def _cparams(*sem):
    return pltpu.CompilerParams(dimension_semantics=sem, vmem_limit_bytes=V7X_VMEM_LIMIT)


def _sigmoid(x):
    return 1.0 / (1.0 + jnp.exp(-x))


def _silu(x):
    return x * _sigmoid(x)


def _log_sigmoid(x):
    return jnp.minimum(x, 0.0) - jnp.log1p(jnp.exp(-jnp.abs(x)))


def _layernorm_rows(z, g, b):
    mu = jnp.mean(z, axis=-1, keepdims=True)
    zc = z - mu
    var = jnp.mean(zc * zc, axis=-1, keepdims=True)
    return zc * lax.rsqrt(var + LN_EPS) * g + b


def _split2(x):
    hi = x.astype(BF16)
    lo = (x - hi.astype(F32)).astype(BF16)
    return hi, lo


def _split3(x):
    hi = x.astype(BF16)
    r1 = x - hi.astype(F32)
    mid = r1.astype(BF16)
    lo = (r1 - mid.astype(F32)).astype(BF16)
    return hi, mid, lo


def _dot(a, b, dims=NN):
    return lax.dot_general(a, b, dims, preferred_element_type=F32)


def _mm2(a, b_parts, dims=NN):
    a_hi, a_lo = _split2(a)
    n = a.shape[0]
    both = _dot(jnp.concatenate([a_hi, a_lo], axis=0), b_parts[0], dims)
    return both[:n] + both[n:] + _dot(a_hi, b_parts[1], dims)


def _cumsum_rows(x):
    n = x.shape[0]
    row = lax.broadcasted_iota(jnp.int32, (n, n), 0)
    col = lax.broadcasted_iota(jnp.int32, (n, n), 1)
    tri = (row >= col).astype(BF16)
    hi, mid, lo = _split3(x)
    return _dot(tri, hi) + _dot(tri, mid) + _dot(tri, lo)


def _head_masks():
    lane_head = lax.broadcasted_iota(jnp.int32, (1, WIDTH), 1) // HEAD_DIM
    return [(lane_head == h) for h in range(N_HEADS)]


def _stack_heads(x, masks):
    return jnp.concatenate([jnp.where(mk, x, 0.0) for mk in masks], axis=0)


def _stack_parts(x, masks):
    parts = _split2(x)
    rows = [mk.astype(BF16) for mk in masks]
    return tuple(jnp.concatenate([p * r for r in rows], axis=0) for p in parts)


def _round_robin(stage_generators):
    live = list(stage_generators)
    while live:
        live = [g for g in live if next(g, _DONE) is not _DONE]


_DONE = object()


def _head_sums(x, ones_bd):
    hi, lo = _split2(x)
    return _dot(hi, ones_bd) + _dot(lo, ones_bd)


def _matmul_kernel(x_ref, w_ref, o_ref):
    o_ref[...] = _dot(x_ref[...], w_ref[...])


def _matmul(x, w, *, tm, tn):
    m, k = x.shape
    n = w.shape[1]
    tm = min(tm, m)
    tn = min(tn, n)
    return pl.pallas_call(
        _matmul_kernel,
        grid=(m // tm, n // tn),
        in_specs=[pl.BlockSpec((tm, k), lambda i, j: (i, 0)),
                  pl.BlockSpec((k, tn), lambda i, j: (0, j))],
        out_specs=pl.BlockSpec((tm, tn), lambda i, j: (i, j)),
        out_shape=jax.ShapeDtypeStruct((m, n), F32),
        compiler_params=_cparams("parallel", "parallel"),
        name="matmul",
    )(x, w)


PROJ_TN = 256


def _proj_kernel(x_ref, w_ref, o_ref):
    x = x_ref[...]
    n = w_ref.shape[2]
    for lo in range(0, n, PROJ_TN):
        hi = min(lo + PROJ_TN, n)
        o_ref[:, lo:hi] = _dot(x, w_ref[0, :, lo:hi])


def _proj(xb, w, layer, *, tm=512):
    m, k = xb.shape
    n = w.shape[2]
    tm = min(tm, m)
    return pl.pallas_call(
        _proj_kernel,
        grid=(m // tm,),
        in_specs=[pl.BlockSpec((tm, k), lambda i: (i, 0)),
                  pl.BlockSpec((1, k, n), lambda i: (layer, 0, 0), pipeline_mode=pl.Buffered(1))],
        out_specs=pl.BlockSpec((tm, n), lambda i: (i, 0)),
        out_shape=jax.ShapeDtypeStruct((m, n), F32),
        compiler_params=_cparams("parallel"),
        name="proj",
    )(xb, w)


def _head(a, h):
    return a[:, h * HEAD_DIM:(h + 1) * HEAD_DIM]


def _hgrn_kernel(q_ref, f_ref, i_ref, g_ref, lbp_ref, gn_ref, ones_ref, mask_ref, s0_ref, y_ref, s_out_ref,
                 st_ref, phi_ref, plo_ref, *, nc, bb):
    c = pl.program_id(1)

    @pl.when(c == 0)
    def _():
        for bi in range(bb):
            st_ref[bi] = jnp.zeros(st_ref.shape[1:], F32)
            for h in range(N_HEADS):
                st_ref[bi, h * HEAD_DIM:(h + 1) * HEAD_DIM, h * HEAD_DIM:(h + 1) * HEAD_DIM] = s0_ref[bi, h].T

    _round_robin([_hgrn_chunk(q_ref.at[bi], f_ref.at[bi], i_ref.at[bi], g_ref.at[bi], lbp_ref, gn_ref, ones_ref,
                              mask_ref, y_ref.at[bi], st_ref.at[bi], phi_ref.at[bi], plo_ref.at[bi])
                  for bi in range(bb)])

    @pl.when(c == nc - 1)
    def _():
        for bi in range(bb):
            for h in range(N_HEADS):
                s_out_ref[bi, h] = st_ref[bi, h * HEAD_DIM:(h + 1) * HEAD_DIM, h * HEAD_DIM:(h + 1) * HEAD_DIM].T


def _hgrn_chunk(q_ref, f_ref, i_ref, g_ref, lbp_ref, gn_ref, ones_ref, mask_ref, y_ref, st_ref, phi_ref, plo_ref):
    L = CHUNK
    ones_bd = ones_ref[...]
    masks = _head_masks()
    log_lb = lbp_ref[0:1, :]
    log1m_lb = lbp_ref[1:2, :]
    one_m_lb = lbp_ref[2:3, :]
    zf = f_ref[...]
    u = log_lb
    w = log1m_lb + _log_sigmoid(zf)
    logf = jnp.maximum(u, w) + jnp.log1p(jnp.exp(-jnp.abs(u - w)))
    k = one_m_lb * _sigmoid(-zf)
    q = _silu(q_ref[...])
    v = i_ref[...]

    b = _cumsum_rows(logf)
    yield
    b_last = b[L - 1:L]
    qe = q * jnp.exp(b)
    kdec = k * jnp.exp(b_last - b)
    sub_row = lax.broadcasted_iota(jnp.int32, (SUB, WIDTH), 0)

    for s in range(L):
        r0 = (s // SUB) * SUB
        p = q[r0:r0 + SUB] * k[s:s + 1] * jnp.exp(jnp.minimum(b[r0:r0 + SUB] - b[s:s + 1], 0.0))
        hi, lo = _split2(jnp.where(sub_row >= s - r0, p, 0.0))
        phi_ref[s * SUB:(s + 1) * SUB, :] = hi
        plo_ref[s * SUB:(s + 1) * SUB, :] = lo
    yield
    att = _dot(phi_ref[...], ones_bd) + _dot(plo_ref[...], ones_bd)
    st = st_ref[...]
    o_inter = _dot(qe, st, NT)
    upd = _dot(v, kdec, TN)
    scores = []
    for blk in range(1, L // SUB):
        r0 = blk * SUB
        ref_row = b[r0 - 1:r0]
        q_s = q[r0:r0 + SUB] * jnp.exp(b[r0:r0 + SUB] - ref_row)
        k_s = k[:r0] * jnp.exp(ref_row - b[:r0])
        scores.append(_dot(q_s, _stack_heads(k_s, masks), NT))
    yield
    st_ref[...] = st * jnp.exp(b_last) + upd * mask_ref[...]
    o_blocks = []
    for blk in range(L // SUB):
        r0 = blk * SUB
        acc = jnp.zeros((SUB, WIDTH), F32)
        for j in range(SUB):
            s = r0 + j
            acc = acc + att[s * SUB:(s + 1) * SUB] * v[s:s + 1]
        if blk > 0:
            acc = acc + _dot(scores[blk - 1], _stack_heads(v[:r0], masks))
        o_blocks.append(acc)
    yield
    o = jnp.concatenate(o_blocks, axis=0) + o_inter
    ms = _head_sums(o * o, ones_bd) * (1.0 / HEAD_DIM)
    yield
    y_ref[...] = (o * lax.rsqrt(ms + HEAD_NORM_EPS) * gn_ref[...] * _silu(g_ref[...])).astype(y_ref.dtype)


def _hgrn(h3, lbp, gn, mask_bd, s0):
    bsz, t, _ = h3.shape
    nc = t // CHUNK
    cb = COL_A // WIDTH
    bb = STREAMS_PER_STEP

    def col(j):
        return pl.BlockSpec((bb, CHUNK, WIDTH), lambda b, c, j=j: (b, c, cb + j))

    square = pl.BlockSpec((WIDTH, WIDTH), lambda b, c: (0, 0))
    state = pl.BlockSpec((bb, N_HEADS, HEAD_DIM, HEAD_DIM), lambda b, c: (b, 0, 0, 0))
    return pl.pallas_call(
        functools.partial(_hgrn_kernel, nc=nc, bb=bb),
        grid=(bsz // bb, nc),
        in_specs=[col(0), col(1), col(2), col(3),
                  pl.BlockSpec((8, WIDTH), lambda b, c: (0, 0)),
                  pl.BlockSpec((1, WIDTH), lambda b, c: (0, 0)),
                  square, square, state],
        out_specs=[pl.BlockSpec((bb, CHUNK, WIDTH), lambda b, c: (b, c, 0)), state],
        out_shape=[jax.ShapeDtypeStruct((bsz, t, WIDTH), BF16),
                   jax.ShapeDtypeStruct((bsz, N_HEADS, HEAD_DIM, HEAD_DIM), F32)],
        scratch_shapes=[pltpu.VMEM((bb, WIDTH, WIDTH), F32),
                        pltpu.VMEM((bb, CHUNK * SUB, WIDTH), BF16),
                        pltpu.VMEM((bb, CHUNK * SUB, WIDTH), BF16)],
        compiler_params=_cparams("parallel", "arbitrary"),
        name="hgrn2",
    )(h3, h3, h3, h3, lbp, gn, mask_bd.astype(BF16), mask_bd, s0)


def _mlstm_kernel(u_ref, v_ref, o_ref, gcol_ref, xin_ref, wift_ref, bcol_ref, brow_ref, cw_ref, vec_ref, wqk_ref,
                  ones_ref, mask_ref, c0_ref, n0_ref, m0_ref, conv0_ref,
                  y_ref, c_out_ref, n_out_ref, m_out_ref, conv_out_ref,
                  c_ref, n_ref, m_ref, carry_ref, *, nc, bb):
    c = pl.program_id(1)

    @pl.when(c == 0)
    def _():
        for bi in range(bb):
            c_ref[bi] = jnp.zeros(c_ref.shape[1:], F32)
            for h in range(N_HEADS):
                c_ref[bi, h * HEAD_DIM:(h + 1) * HEAD_DIM, h * HEAD_DIM:(h + 1) * HEAD_DIM] = c0_ref[bi, h]
        n_ref[...] = n0_ref[...]
        m_ref[...] = m0_ref[...]
        carry_ref[...] = conv0_ref[...]

    _round_robin([_mlstm_chunk(u_ref.at[bi], v_ref.at[bi], o_ref.at[bi], gcol_ref.at[bi], xin_ref.at[bi],
                               wift_ref, bcol_ref, brow_ref, cw_ref, vec_ref, wqk_ref, ones_ref, mask_ref, y_ref.at[bi],
                               c_ref.at[bi], n_ref.at[bi], m_ref.at[bi], carry_ref.at[bi]) for bi in range(bb)])

    @pl.when(c == nc - 1)
    def _():
        for bi in range(bb):
            for h in range(N_HEADS):
                c_out_ref[bi, h] = c_ref[bi, h * HEAD_DIM:(h + 1) * HEAD_DIM, h * HEAD_DIM:(h + 1) * HEAD_DIM]
        n_out_ref[...] = n_ref[...]
        m_out_ref[...] = m_ref[...]
        conv_out_ref[...] = carry_ref[...]


def _mlstm_chunk(u_ref, v_ref, o_ref, gcol_ref, xin_ref, wift_ref, bcol_ref, brow_ref, cw_ref, vec_ref, wqk_ref,
                 ones_ref, mask_ref, y_ref, c_ref, n_ref, m_ref, carry_ref):
    L = CHUNK
    heads = range(N_HEADS)

    u = u_ref[...]
    carry = carry_ref[...]
    row8 = lax.broadcasted_iota(jnp.int32, (8, WIDTH), 0)
    conv = vec_ref[0:1, :] + cw_ref[CONV_W - 1:CONV_W, :] * u
    for d in range(1, CONV_W):
        rolled = pltpu.roll(u, d, 0)
        top = jnp.where(row8 < d, pltpu.roll(carry, d, 0), rolled[0:8])
        conv = conv + cw_ref[CONV_W - 1 - d:CONV_W - d, :] * jnp.concatenate([top, rolled[8:]], axis=0)
    carry_ref[...] = u[L - 8:L]
    cact = _silu(conv)
    qk_all = _dot(cact.astype(BF16), wqk_ref[...])

    row = lax.broadcasted_iota(jnp.int32, (L, L), 0)
    col = lax.broadcasted_iota(jnp.int32, (L, L), 1)
    lower = row >= col
    raw_c = gcol_ref[:, 0:2 * N_HEADS] + brow_ref[...]
    lane8 = lax.broadcasted_iota(jnp.int32, raw_c.shape, 1)
    gcol = jnp.where(lane8 < N_HEADS, raw_c, _log_sigmoid(raw_c))
    raw_r = _dot(wift_ref[...], xin_ref[...], NT) + bcol_ref[...]
    sub8 = lax.broadcasted_iota(jnp.int32, raw_r.shape, 0)
    grow = jnp.where(sub8 < N_HEADS, raw_r, _log_sigmoid(raw_r))
    f_col = jnp.dot(lower.astype(F32), gcol, preferred_element_type=F32, precision=HIGHEST)
    f_row = jnp.dot(grow, (row <= col).astype(F32), preferred_element_type=F32, precision=HIGHEST)
    r8 = lax.broadcasted_iota(jnp.int32, (2 * N_HEADS, 2 * WIDTH), 0)
    c8 = lax.broadcasted_iota(jnp.int32, (2 * N_HEADS, 2 * WIDTH), 1)
    expand = (c8 // HEAD_DIM == r8).astype(BF16)
    x8 = jnp.where(lane8 < N_HEADS, gcol, f_col)
    x_hi, x_mid, x_lo = _split3(x8)
    both = _dot(x_hi, expand) + _dot(x_mid, expand) + _dot(x_lo, expand)
    li_all = both[:, 0:WIDTH]
    f_all = both[:, WIDTH:2 * WIDTH]
    d_row = jnp.concatenate([grow[h:h + 1, :] - f_row[N_HEADS + h:N_HEADS + h + 1, :] for h in heads], axis=1)
    yield
    t_idx = lax.broadcasted_iota(jnp.int32, (L, WIDTH), 0)
    s_idx = lax.broadcasted_iota(jnp.int32, (L, WIDTH), 1) % L
    ones_bd = ones_ref[...]
    masks = _head_masks()
    cm = li_all - f_all
    for sh in (1, 2, 4, 8, 16, 32):
        cm = jnp.maximum(cm, jnp.where(t_idx >= sh, pltpu.roll(cm, sh, 0), -jnp.inf))
    g = f_all + m_ref[...]
    mt = jnp.maximum(g, f_all + cm)
    wg = jnp.exp(g - mt)
    wd = jnp.exp(jnp.where(s_idx <= t_idx, f_all + d_row - mt, -jnp.inf))
    q = qk_all[:, 0:WIDTH]
    k = qk_all[:, WIDTH:2 * WIDTH]
    v = v_ref[...]
    q_bf = q.astype(BF16)
    v_bf = v.astype(BF16)
    head_rows = [mk.astype(BF16) for mk in masks]
    k_stack = jnp.concatenate([k.astype(BF16) * r for r in head_rows], axis=0)
    v_stack = jnp.concatenate([v_bf * r for r in head_rows], axis=0)
    c_bd = c_ref[...]
    n_row = n_ref[...]
    qk = _dot(q_bf, k_stack, NT) * wd
    q_c = _dot(q_bf, c_bd.astype(BF16))
    q_n = _head_sums(q * n_row, ones_bd)
    yield
    qkv = _dot(qk.astype(BF16), v_stack)
    qk_sum = _head_sums(qk, ones_bd)
    mt_last = mt[L - 1:L]
    kw = k * jnp.exp(f_all[L - 1:L] - f_all + li_all - mt_last)
    c_upd = _dot(kw.astype(BF16), v_bf, TN)
    yield
    wgl = wg[L - 1:L]
    c_ref[...] = wgl * c_bd + c_upd * mask_ref[...]
    n_ref[...] = wgl * n_row + jnp.sum(kw, axis=0, keepdims=True)
    m_ref[...] = mt_last
    den = wg * q_n + qk_sum
    hh = (wg * q_c + qkv) / jnp.maximum(jnp.abs(den), jnp.exp(-mt))
    z = _sigmoid(o_ref[...]) * hh
    mu = _head_sums(z, ones_bd) * (1.0 / HEAD_DIM)
    yield
    zc = z - mu
    var = _head_sums(zc * zc, ones_bd) * (1.0 / HEAD_DIM)
    yield
    y_ref[...] = (zc * lax.rsqrt(var + HEAD_NORM_EPS) * vec_ref[1:2, :] + vec_ref[2:3, :] * cact).astype(y_ref.dtype)


def _mlstm(h3, xb3, w_if_t, b_col, b_row, conv_w, vecs, w_qk, mask_bd, c0, n0, m0, conv0):
    bsz, t, _ = h3.shape
    nc = t // CHUNK
    cb = COL_B // WIDTH
    bb = STREAMS_PER_STEP
    state4 = pl.BlockSpec((bb, N_HEADS, HEAD_DIM, HEAD_DIM), lambda b, c: (b, 0, 0, 0))
    state_n = pl.BlockSpec((bb, 1, WIDTH), lambda b, c: (b, 0, 0))
    state_m = state_n
    state_conv = pl.BlockSpec((bb, 8, WIDTH), lambda b, c: (b, 0, 0))

    def const(shape):
        return pl.BlockSpec(shape, lambda b, c: tuple(0 for _ in shape))

    def col(j):
        return pl.BlockSpec((bb, CHUNK, WIDTH), lambda b, c, j=j: (b, c, cb + j))

    return pl.pallas_call(
        functools.partial(_mlstm_kernel, nc=nc, bb=bb),
        grid=(bsz // bb, nc),
        in_specs=[col(0), col(1), col(2),
                  pl.BlockSpec((bb, CHUNK, LANE_TILE), lambda b, c: (b, c, COL_IF // LANE_TILE)),
                  pl.BlockSpec((bb, CHUNK, D_MODEL), lambda b, c: (b, c, 0)), const((2 * N_HEADS, D_MODEL)),
                  const((2 * N_HEADS, 1)), const((1, 2 * N_HEADS)), const((8, WIDTH)), const((8, WIDTH)),
                  const((WIDTH, 2 * WIDTH)), const((WIDTH, WIDTH)), const((WIDTH, WIDTH)),
                  state4, state_n, state_m, state_conv],
        out_specs=[pl.BlockSpec((bb, CHUNK, WIDTH), lambda b, c: (b, c, 0)), state4, state_n, state_m, state_conv],
        out_shape=[jax.ShapeDtypeStruct((bsz, t, WIDTH), BF16),
                   jax.ShapeDtypeStruct((bsz, N_HEADS, HEAD_DIM, HEAD_DIM), F32),
                   jax.ShapeDtypeStruct((bsz, 1, WIDTH), F32),
                   jax.ShapeDtypeStruct((bsz, 1, WIDTH), F32),
                   jax.ShapeDtypeStruct((bsz, 8, WIDTH), F32)],
        scratch_shapes=[pltpu.VMEM((bb, WIDTH, WIDTH), F32),
                        pltpu.VMEM((bb, 1, WIDTH), F32),
                        pltpu.VMEM((bb, 1, WIDTH), F32),
                        pltpu.VMEM((bb, 8, WIDTH), F32)],
        compiler_params=_cparams("parallel", "arbitrary"),
        name="mlstm",
    )(h3, h3, h3, h3, xb3, w_if_t, b_col, b_row, conv_w, vecs, w_qk, mask_bd.astype(BF16), mask_bd, c0, n0, m0, conv0)


SWA_CHUNKS_PER_STEP = 4


def _swa_kernel(*refs, prev_valid, cps, multi_step):
    if multi_step:
        (q_ref, kc_ref, vc_ref, kp_ref, vp_ref, k_ref, v_ref, bias_ref, sink_ref, y_ref, s_ref, p_ref) = refs
    else:
        (q_ref, kc_ref, vc_ref, k_ref, v_ref, bias_ref, sink_ref, y_ref, s_ref, p_ref) = refs
    g = pl.program_id(1)
    L = CHUNK
    span = WINDOW + L
    k_win = kc_ref[0]
    v_win = vc_ref[0]
    if multi_step:
        k_win = jnp.where(g == 0, k_win, kp_ref[0])
        v_win = jnp.where(g == 0, v_win, vp_ref[0])
    kcat = jnp.concatenate([k_win, k_ref[0]], axis=0).astype(BF16)
    vcat = jnp.concatenate([v_win, v_ref[0]], axis=0).astype(BF16)
    kv_of = [h // (N_HEADS // C_KV_HEADS) for h in range(N_HEADS)]
    for j in range(cps):
        q = q_ref[0, j * L:(j + 1) * L, :].astype(BF16)
        k_j = kcat[j * L:j * L + span]
        for h in range(N_HEADS):
            s_ref[j * N_HEADS + h] = _dot(_head(q, h), _head(k_j, kv_of[h]), NT)
    def softmax_tile(j, h):
        key_pos = lax.broadcasted_iota(jnp.int32, (L, span), 1) + (g * cps + j) * L
        s = s_ref[j * N_HEADS + h] * HEAD_DIM ** -0.5 + bias_ref[h]
        if not prev_valid:
            s = jnp.where(key_pos >= WINDOW, s, -jnp.inf)
        sink = sink_ref[0:1, h:h + 1]
        m = jnp.maximum(jnp.max(s, axis=1, keepdims=True), sink)
        yield
        p = jnp.exp(s - m)
        denom = jnp.sum(p, axis=1, keepdims=True) + jnp.exp(sink - m)
        yield
        p_ref[j * N_HEADS + h] = (p * (1.0 / denom)).astype(BF16)

    _round_robin([softmax_tile(j, h) for j in range(cps) for h in range(N_HEADS)])
    for j in range(cps):
        v_j = vcat[j * L:j * L + span]
        o = [_dot(p_ref[j * N_HEADS + h], _head(v_j, kv_of[h])) for h in range(N_HEADS)]
        y_ref[0, j * L:(j + 1) * L, :] = jnp.concatenate(o, axis=1).astype(y_ref.dtype)


def _swa(h3, cache_k, cache_v, bias, sinks, prev_valid):
    bsz, t, _ = h3.shape
    nc = t // CHUNK
    cps = min(SWA_CHUNKS_PER_STEP, nc)
    rows = cps * CHUNK
    steps = nc // cps
    multi_step = steps > 1
    assert nc % cps == 0 and (not multi_step or rows % WINDOW == 0), (t, cps)
    qb = COL_C // WIDTH
    kb = (COL_C + WIDTH) // KV_WIDTH
    cache = pl.BlockSpec((1, WINDOW, KV_WIDTH), lambda b, g: (b, 0, 0))

    def own(col):
        return pl.BlockSpec((1, rows, KV_WIDTH), lambda b, g, col=col: (b, g, col))

    def before(col):
        return pl.BlockSpec((1, WINDOW, KV_WIDTH),
                            lambda b, g, col=col: (b, jnp.maximum(g * (rows // WINDOW) - 1, 0), col))

    kv_specs = [cache, cache] + ([before(kb), before(kb + 1)] if multi_step else []) + [own(kb), own(kb + 1)]
    kv_args = [cache_k, cache_v] + ([h3, h3] if multi_step else []) + [h3, h3]
    return pl.pallas_call(
        functools.partial(_swa_kernel, prev_valid=prev_valid, cps=cps, multi_step=multi_step),
        grid=(bsz, steps),
        in_specs=[pl.BlockSpec((1, rows, WIDTH), lambda b, g: (b, g, qb))] + kv_specs + [
                  pl.BlockSpec((N_HEADS, CHUNK, WINDOW + CHUNK), lambda b, g: (0, 0, 0)),
                  pl.BlockSpec((1, N_HEADS), lambda b, g: (0, 0))],
        out_specs=pl.BlockSpec((1, rows, WIDTH), lambda b, g: (b, g, 0)),
        out_shape=jax.ShapeDtypeStruct((bsz, t, WIDTH), BF16),
        scratch_shapes=[pltpu.VMEM((cps * N_HEADS, CHUNK, WINDOW + CHUNK), F32),
                        pltpu.VMEM((cps * N_HEADS, CHUNK, WINDOW + CHUNK), BF16)],
        compiler_params=_cparams("parallel", "parallel"),
        name="swa",
    )(h3, *kv_args, bias, sinks)


def _rwkv_kernel(pd_ref, mu_ref, vec_ref, wwa_ref, gup_ref, ones_ref, mask_ref, s0_ref, sh0_ref,
                 y_ref, s_out_ref, sh_out_ref, st_ref, carry_ref, *, nc, bb):
    c = pl.program_id(1)

    @pl.when(c == 0)
    def _():
        for bi in range(bb):
            st_ref[bi] = jnp.zeros(st_ref.shape[1:], F32)
            for h in range(N_HEADS):
                st_ref[bi, h * HEAD_DIM:(h + 1) * HEAD_DIM, h * HEAD_DIM:(h + 1) * HEAD_DIM] = s0_ref[bi, h]
            carry_ref[bi, 7:8, :] = sh0_ref[bi]

    _round_robin([_rwkv_chunk(pd_ref.at[bi], mu_ref, vec_ref, wwa_ref, gup_ref, ones_ref, mask_ref,
                              y_ref.at[bi], st_ref.at[bi], carry_ref.at[bi]) for bi in range(bb)])

    @pl.when(c == nc - 1)
    def _():
        for bi in range(bb):
            for h in range(N_HEADS):
                s_out_ref[bi, h] = st_ref[bi, h * HEAD_DIM:(h + 1) * HEAD_DIM, h * HEAD_DIM:(h + 1) * HEAD_DIM]
            sh_out_ref[bi] = carry_ref[bi, 7:8, :]


def _rwkv_chunk(pd_ref, mu_ref, vec_ref, wwa_ref, gup_ref, ones_ref, mask_ref, y_ref, st_ref, carry_ref):
    L = CHUNK
    ones_bd = ones_ref[...]
    masks = _head_masks()

    pd = pd_ref[...]
    rolled = pltpu.roll(pd, 1, 0)
    row8 = lax.broadcasted_iota(jnp.int32, (8, D_COLS), 0)
    top = jnp.where(row8 == 0, carry_ref[7:8, :], rolled[0:8])
    pd_prev = jnp.concatenate([top, rolled[8:]], axis=0)
    carry_ref[...] = pd[L - 8:L]
    mixed = pd + (pd_prev - pd) * mu_ref[...]
    r = mixed[:, 0:WIDTH]
    k_raw = mixed[:, WIDTH:2 * WIDTH]
    v = mixed[:, 2 * WIDTH:3 * WIDTH]
    wa = mixed[:, 3 * WIDTH:3 * WIDTH + 2 * D_DECAY_LORA]
    g_in = mixed[:, 3 * WIDTH + 2 * D_DECAY_LORA:D_COLS]
    lane_wa = lax.broadcasted_iota(jnp.int32, wa.shape, 1)
    wa_act = jnp.where(lane_wa < D_DECAY_LORA, jnp.tanh(wa), wa)
    lora = _dot(wa_act.astype(BF16), wwa_ref[...])
    g_d = _dot(_sigmoid(g_in).astype(BF16), gup_ref[...])
    yield
    z = -(vec_ref[0:1, :] + lora[:, 0:WIDTH])
    w_log = -(jnp.maximum(z, 0.0) + jnp.log1p(jnp.exp(-jnp.abs(z)))) - 0.5
    lw = -jnp.exp(w_log)
    a_d = _sigmoid(vec_ref[1:2, :] + lora[:, WIDTH:2 * WIDTH])
    kk = k_raw * vec_ref[2:3, :]
    kk_norm = _head_sums(kk * kk, ones_bd)
    cw = _cumsum_rows(lw)
    yield
    kk = kk * lax.rsqrt(jnp.maximum(kk_norm, 1e-24))
    k = k_raw * (1.0 + (a_d - 1.0) * vec_ref[3:4, :])
    a = -kk
    b = kk * a_d
    g_in_c = jnp.exp(cw)
    g_inv = jnp.exp(-cw)
    at = a * jnp.exp(cw - lw)
    rt = r * g_in_c
    bt = b * g_inv
    kt = k * g_inv
    x = jnp.concatenate([at, rt], axis=0)
    bt_st = _stack_parts(bt, masks)
    kt_st = _stack_parts(kt, masks)
    gram_b = _mm2(x, bt_st, NT)
    gram_k = _mm2(x, kt_st, NT)
    st = st_ref[...]
    u = _mm2(x, _split2(st), NT)
    yield

    t_idx = lax.broadcasted_iota(jnp.int32, (L, WIDTH), 0)
    i_idx = lax.broadcasted_iota(jnp.int32, (L, WIDTH), 1) % L
    strict = i_idx < t_idx
    incl = i_idx <= t_idx
    same_blk = (i_idx // SUB) == (t_idx // SUB)
    eye = (i_idx == t_idx).astype(F32)
    n_all = jnp.where(strict, gram_b[0:L], 0.0)
    m_all = jnp.where(strict, gram_k[0:L], 0.0)
    rb_all = jnp.where(incl, gram_b[L:2 * L], 0.0)
    rk_all = jnp.where(incl, gram_k[L:2 * L], 0.0)
    n_d = jnp.where(same_blk, n_all, 0.0)
    n_off = jnp.where(same_blk, 0.0, n_all)

    v_st = _stack_parts(v, masks)
    x2 = _mm2(n_d, _stack_parts(n_d, masks))
    rhs = u[0:L] + _mm2(m_all, v_st)
    bonus = _head_sums(r * k * vec_ref[4:5, :], ones_bd) * v
    yield
    x2_st = _stack_parts(x2, masks)
    x4 = _mm2(x2, x2_st)
    t_d = eye + n_d
    t_d = t_d + _mm2(t_d, x2_st)
    yield
    x4_st = _stack_parts(x4, masks)
    x8 = _mm2(x4, x4_st)
    t_d = t_d + _mm2(t_d, x4_st)
    yield
    t_d = t_d + _mm2(t_d, _stack_parts(x8, masks))
    yield
    n1 = _mm2(t_d, _stack_parts(n_off, masks))
    yield
    n1_st = _stack_parts(n1, masks)
    n2 = _mm2(n1, n1_st)
    yield
    a2 = eye + n1 + n2 + _mm2(n2, n1_st)
    yield
    t_full = _mm2(a2, _stack_parts(t_d, masks))
    yield
    sa = _mm2(t_full, _stack_parts(rhs, masks))
    yield
    y = u[L:2 * L] + _mm2(rb_all, _stack_parts(sa, masks)) + _mm2(rk_all, v_st)
    sv_hi, sv_lo = _split2(jnp.concatenate([sa, v], axis=0))
    bk_hi, bk_lo = _split2(jnp.concatenate([bt, kt], axis=0))
    upd = _dot(sv_hi, bk_hi, TN) + _dot(sv_lo, bk_hi, TN) + _dot(sv_hi, bk_lo, TN)
    yield
    st_ref[...] = (st + upd * mask_ref[...]) * g_in_c[L - 1:L]

    mu = _head_sums(y, ones_bd) * (1.0 / HEAD_DIM)
    yield
    yc = y - mu
    var = _head_sums(yc * yc, ones_bd) * (1.0 / HEAD_DIM)
    yield
    o = yc * lax.rsqrt(var + RWKV_GN_EPS) * vec_ref[5:6, :] + vec_ref[6:7, :]
    y_ref[...] = ((o + bonus) * g_d).astype(y_ref.dtype)


def _rwkv(h3, mu, vecs, w_wa, g_up, mask_bd, s0, shift0):
    bsz, t, _ = h3.shape
    nc = t // CHUNK
    bb = STREAMS_PER_STEP
    state = pl.BlockSpec((bb, N_HEADS, HEAD_DIM, HEAD_DIM), lambda g, i: (g, 0, 0, 0))
    shift = pl.BlockSpec((bb, 1, D_COLS), lambda g, i: (g, 0, 0))

    def const(shape):
        return pl.BlockSpec(shape, lambda g, i: tuple(0 for _ in shape))

    return pl.pallas_call(
        functools.partial(_rwkv_kernel, nc=nc, bb=bb),
        grid=(bsz // bb, nc),
        in_specs=[pl.BlockSpec((bb, CHUNK, D_COLS), lambda g, i: (g, i, COL_D // D_COLS)),
                  const((1, D_COLS)), const((8, WIDTH)), const((2 * D_DECAY_LORA, 2 * WIDTH)),
                  const((D_GATE_LORA, WIDTH)), const((WIDTH, WIDTH)), const((WIDTH, WIDTH)), state, shift],
        out_specs=[pl.BlockSpec((bb, CHUNK, WIDTH), lambda g, i: (g, i, 0)), state, shift],
        out_shape=[jax.ShapeDtypeStruct((bsz, t, WIDTH), BF16),
                   jax.ShapeDtypeStruct((bsz, N_HEADS, HEAD_DIM, HEAD_DIM), F32),
                   jax.ShapeDtypeStruct((bsz, 1, D_COLS), F32)],
        scratch_shapes=[pltpu.VMEM((bb, WIDTH, WIDTH), F32), pltpu.VMEM((bb, 8, D_COLS), F32)],
        compiler_params=_cparams("parallel", "arbitrary"),
        name="rwkv7",
    )(h3, mu, vecs, w_wa, g_up, mask_bd.astype(BF16), mask_bd, s0, shift0)


def _merge_kernel(ya_ref, yb_ref, yc_ref, yd_ref, xb_ref, wg_ref, wbr_ref, wo_ref, x_ref,
                  lng_ref, lnb_ref, xo_ref, xob_ref):
    ys = (ya_ref, yb_ref, yc_ref, yd_ref)
    xb = xb_ref[...]
    merged = None
    for n in range(N_BRANCH):
        gate = _dot(xb, wg_ref[0, :, n * D_MODEL:(n + 1) * D_MODEL])
        term = _sigmoid(gate) * _dot(ys[n][...], wbr_ref[0, n])
        merged = term if merged is None else merged + term
    out = _dot(merged.astype(BF16), wo_ref[0])
    xn = _layernorm_rows(ALPHA * x_ref[...] + out, lng_ref[...], lnb_ref[...])
    xo_ref[...] = xn
    xob_ref[...] = xn.astype(BF16)


def _merge(ys, xb2, w_gate, wbr, wo, layer, x2, lng, lnb, *, tm=512):
    m = x2.shape[0]
    tm = min(tm, m)
    ysp = pl.BlockSpec((tm, WIDTH), lambda i: (i, 0))
    row = pl.BlockSpec((tm, D_MODEL), lambda i: (i, 0))
    vec = pl.BlockSpec((1, D_MODEL), lambda i: (0, 0))
    resident = dict(pipeline_mode=pl.Buffered(1))
    return pl.pallas_call(
        _merge_kernel,
        grid=(m // tm,),
        in_specs=[ysp, ysp, ysp, ysp, row,
                  pl.BlockSpec((1, D_MODEL, GATE_COLS), lambda i: (layer, 0, 0), **resident),
                  pl.BlockSpec((1, N_BRANCH, WIDTH, D_MODEL), lambda i: (layer, 0, 0, 0), **resident),
                  pl.BlockSpec((1, D_MODEL, D_MODEL), lambda i: (layer, 0, 0), **resident),
                  row, vec, vec],
        out_specs=[row, row],
        out_shape=[jax.ShapeDtypeStruct((m, D_MODEL), F32), jax.ShapeDtypeStruct((m, D_MODEL), BF16)],
        compiler_params=_cparams("parallel"),
        name="merge",
    )(*ys, xb2, w_gate, wbr, wo, x2, lng, lnb)


def _ffn_kernel(*refs, n_steps, gated):
    if gated:
        xb_ref, x_ref, gates_ref, wg_ref, wu_ref, wd_ref, lng_ref, lnb_ref, xo_ref, xob_ref, acc_ref = refs
    else:
        xb_ref, x_ref, wg_ref, wu_ref, wd_ref, lng_ref, lnb_ref, xo_ref, xob_ref, acc_ref = refs
    j = pl.program_id(1)

    @pl.when(j == 0)
    def _():
        acc_ref[...] = jnp.zeros_like(acc_ref)

    xb = xb_ref[...]
    hg = _dot(xb, wg_ref[0, 0])
    hu = _dot(xb, wu_ref[0, 0])
    part = _dot((_silu(hg) * hu).astype(BF16), wd_ref[0, 0])
    if gated:
        gates = gates_ref[...]
        lane = lax.broadcasted_iota(jnp.int32, gates.shape, 1)
        part = jnp.sum(jnp.where(lane == j, gates, 0.0), axis=1, keepdims=True) * part
    acc_ref[...] += part

    @pl.when(j == n_steps - 1)
    def _():
        xn = _layernorm_rows(ALPHA * x_ref[...] + acc_ref[...], lng_ref[...], lnb_ref[...])
        xo_ref[...] = xn
        xob_ref[...] = xn.astype(BF16)


def _ffn(xb, x2, wg, wu, wd, layer, lng, lnb, gates=None, *, tm=512):
    m = x2.shape[0]
    tm = min(tm, m)
    _, n_steps, _, tf = wg.shape
    row = pl.BlockSpec((tm, D_MODEL), lambda i, j: (i, 0))
    vec = pl.BlockSpec((1, D_MODEL), lambda i, j: (0, 0))
    w_in_spec = pl.BlockSpec((1, 1, D_MODEL, tf), lambda i, j: (layer, j, 0, 0))
    w_out_spec = pl.BlockSpec((1, 1, tf, D_MODEL), lambda i, j: (layer, j, 0, 0))
    in_specs = [row, row]
    args = [xb, x2]
    if gates is not None:
        in_specs.append(pl.BlockSpec((tm, N_EXPERTS), lambda i, j: (i, 0)))
        args.append(gates)
    in_specs += [w_in_spec, w_in_spec, w_out_spec, vec, vec]
    args += [wg, wu, wd, lng, lnb]
    return pl.pallas_call(
        functools.partial(_ffn_kernel, n_steps=n_steps, gated=gates is not None),
        grid=(m // tm, n_steps),
        in_specs=in_specs,
        out_specs=[row, row],
        out_shape=[jax.ShapeDtypeStruct((m, D_MODEL), F32), jax.ShapeDtypeStruct((m, D_MODEL), BF16)],
        scratch_shapes=[pltpu.VMEM((tm, D_MODEL), F32)],
        compiler_params=_cparams("parallel", "arbitrary"),
        name="moe" if gates is not None else "ffn",
    )(*args)


STATE_NAMES = ('swa_k', 'swa_v', 'hgrn', 'mlstm_c', 'mlstm_n', 'mlstm_m', 'mlstm_conv', 'rwkv', 'rwkv_shift')

_D_ORIG = (('r', WIDTH), ('w', D_DECAY_LORA), ('k', WIDTH), ('v', WIDTH), ('a', D_AAA_LORA), ('g', D_GATE_LORA))
_D_KERNEL = ('r', 'k', 'v', 'w', 'a', 'g')


def _d_pieces(arr):
    out, off = {}, 0
    for name, size in _D_ORIG:
        out[name] = arr[..., off:off + size]
        off += size
    return out


def _d_to_kernel_order(arr):
    p = _d_pieces(arr)
    return jnp.concatenate([p[n] for n in _D_KERNEL], axis=-1)


def _d_to_original_order(arr):
    sizes = dict(_D_ORIG)
    p, off = {}, 0
    for name in _D_KERNEL:
        p[name] = arr[..., off:off + sizes[name]]
        off += sizes[name]
    return jnp.concatenate([p[n] for n, _ in _D_ORIG], axis=-1)


def _rel_bucket(rel):
    half = NUM_BUCKETS // 2
    exact = half // 2
    dist = jnp.abs(rel)
    far = exact + (jnp.log(jnp.maximum(dist, 1).astype(F32) / exact)
                   / math.log(MAX_DISTANCE / exact) * (half - exact)).astype(jnp.int32)
    far = jnp.minimum(far, half - 1)
    return jnp.where(rel > 0, half, 0) + jnp.where(dist < exact, dist, far)


def _lower_bounds(lb_raw):
    sm = jax.nn.softmax(lb_raw.astype(F32), axis=0)
    lb = jnp.concatenate([jnp.zeros_like(sm[:1]), jnp.cumsum(sm[1:], axis=0)[:-1]], axis=0)
    return jnp.clip(lb, 0.0, LB_CEIL)


def _block_diag(blocks):
    rows = sum(b.shape[0] for b in blocks)
    cols = sum(b.shape[1] for b in blocks)
    out = jnp.zeros((rows, cols), blocks[0].dtype)
    r = c = 0
    for b in blocks:
        out = out.at[r:r + b.shape[0], c:c + b.shape[1]].set(b)
        r += b.shape[0]
        c += b.shape[1]
    return out


def _pad_rows(a, rows):
    return jnp.concatenate([a, jnp.zeros((rows - a.shape[0],) + a.shape[1:], a.dtype)], axis=0)


def _mixer(x3, xb2, st, prev_valid, lp, l):
    bsz, t, _ = x3.shape
    m = bsz * t
    x2 = x3.reshape(m, D_MODEL)
    h2 = _proj(xb2, lp['w_main'], l)
    h3 = h2.reshape(bsz, t, MAIN_COLS)

    y_a, s_hgrn = _hgrn(h3, lp['lbp'][l], lp['gn_a'][l][None], lp['mask_bd'], st['hgrn'])

    conv0 = jnp.concatenate([jnp.zeros((bsz, 8 - (CONV_W - 1), WIDTH), F32), st['mlstm_conv']], axis=1)
    y_b, mc, mn, mm, conv8 = _mlstm(h3, xb2.reshape(bsz, t, D_MODEL), lp['w_if_t'][l], lp['b_if'][l][:, None],
                                    lp['b_if'][l][None, :], lp['conv_w8'][l], lp['vec_b'][l], lp['w_qk'][l],
                                    lp['mask_bd'], st['mlstm_c'], st['mlstm_n'].reshape(bsz, 1, WIDTH),
                                    jnp.repeat(st['mlstm_m'], HEAD_DIM, axis=-1)[:, None, :], conv0)
    mn = mn.reshape(bsz, N_HEADS, HEAD_DIM)
    mm = mm[:, :, ::HEAD_DIM]
    conv_state = conv8[:, 8 - (CONV_W - 1):]

    k_off = COL_C + WIDTH
    cache_k = st['swa_k'].reshape(bsz, WINDOW, KV_WIDTH)
    cache_v = st['swa_v'].reshape(bsz, WINDOW, KV_WIDTH)
    y_c = _swa(h3, cache_k, cache_v, lp['bias'], lp['sinks'][l][None], prev_valid)
    keep = st['keep']

    def window(cache, col):
        new = h3[:, max(t - keep, 0):, col:col + KV_WIDTH]
        if t < keep:
            new = jnp.concatenate([cache[:, WINDOW - (keep - t):], new], axis=1)
        return new.reshape(bsz, keep, C_KV_HEADS, HEAD_DIM)

    k_win = window(cache_k, k_off)
    v_win = window(cache_v, k_off + KV_WIDTH)

    y_d, s_rwkv, shift = _rwkv(h3, lp['mu_d'][l][None], lp['vec_d'][l], lp['w_wa'][l], lp['g_up_d'][l],
                               lp['mask_bd'], st['rwkv'], _d_to_kernel_order(st['rwkv_shift'])[:, None, :])
    shift_state = _d_to_original_order(shift[:, 0, :])

    ys = [y.reshape(m, WIDTH) for y in (y_a, y_b, y_c, y_d)]
    x1, x1b = _merge(ys, xb2, lp['w_gate'], lp['w_br'], lp['w_o'], l, x2,
                     lp['ln1_g'][l][None], lp['ln1_b'][l][None])
    new_st = {'swa_k': k_win, 'swa_v': v_win, 'hgrn': s_hgrn, 'mlstm_c': mc, 'mlstm_n': mn,
              'mlstm_m': mm[:, 0, :], 'mlstm_conv': conv_state, 'rwkv': s_rwkv, 'rwkv_shift': shift_state}
    return x1, x1b, new_st


def _trunk(x3, states, prev_valid, keep, lp):
    bsz, t, _ = x3.shape
    m = bsz * t
    xb2 = x3.reshape(m, D_MODEL).astype(BF16)
    collected = {name: [] for name in STATE_NAMES}
    for l in range(DEPTH):
        st = {name: states[name][l] for name in STATE_NAMES}
        st['keep'] = keep
        x1, x1b, new_st = _mixer(x3, xb2, st, prev_valid, lp, l)
        j = l // 2
        ln_g, ln_b = lp['ln2_g'][l][None], lp['ln2_b'][l][None]
        if l % 2 == 0:
            x2, xb2 = _ffn(x1b, x1, lp['ffn_wg'], lp['ffn_wu'], lp['ffn_wd'], j, ln_g, ln_b)
        else:
            logits = _matmul(x1b, lp['router_w'][j], tm=1024, tn=N_EXPERTS) + lp['router_b'][j]
            top_v, top_i = lax.top_k(logits, TOP_K)
            probs = jax.nn.softmax(top_v, axis=-1)
            gates = jnp.einsum('mk,mke->me', probs, jax.nn.one_hot(top_i, N_EXPERTS, dtype=F32))
            x2, xb2 = _ffn(x1b, x1, lp['exp_wg'], lp['exp_wu'], lp['exp_wd'], j, ln_g, ln_b, gates)
        x3 = x2.reshape(bsz, t, D_MODEL)
        for name in STATE_NAMES:
            collected[name].append(new_st[name])
    return x3, {name: jnp.stack(collected[name]) for name in STATE_NAMES}


def kernel(x_prompt, x_sample, cache_swa_k, cache_swa_v, state_hgrn, state_mlstm_c, state_mlstm_n, state_mlstm_m, state_mlstm_conv, state_rwkv, state_rwkv_shift, w_in, lb_raw, gn_a, conv_w, conv_b, wq_b, wk_b, b_i, b_f, gn_b, skip_b, sinks, rel_bias, mu_d, w0_d, w_up_d, a0_d, a_up_d, g_up_d, k_k_d, k_a_d, r_k_d, gn_w_d, gn_b_d, w_br, w_o, ln1_g, ln1_b, ln2_g, ln2_b, ffn_w_gate, ffn_w_up, ffn_w_down, router_w, router_b, exp_w_gate, exp_w_up, exp_w_down):
    off_if = 4 * WIDTH + 3 * WIDTH
    off_c = off_if + 2 * N_HEADS
    off_d = off_c + WIDTH + 2 * KV_WIDTH
    off_gate = off_d + D_COLS
    w_main = jnp.concatenate([w_in[:, :, :4 * WIDTH], _d_to_kernel_order(w_in[:, :, off_d:off_gate]),
                              w_in[:, :, 4 * WIDTH:off_if], w_in[:, :, off_c:off_d], w_in[:, :, off_if:off_c],
                              jnp.zeros((DEPTH, D_MODEL, LANE_TILE - 2 * N_HEADS), w_in.dtype)], axis=-1).astype(BF16)
    w_gate = w_in[:, :, off_gate:].astype(BF16)

    lb = _lower_bounds(lb_raw)
    lb = lb[jnp.minimum(jnp.arange(DEPTH), lb.shape[0] - 1)]
    lbp = jnp.stack([jnp.log(jnp.maximum(lb, LB_FLOOR)), jnp.log1p(-lb), 1.0 - lb], axis=1)
    lbp = jnp.concatenate([lbp, jnp.zeros((DEPTH, 5, WIDTH), F32)], axis=1)

    w_qk = jnp.stack([jnp.concatenate([_block_diag(list(wq_b[l])), _block_diag(list(wk_b[l])) * HEAD_DIM ** -0.5],
                                      axis=1) for l in range(DEPTH)]).astype(BF16)
    w_wa = jnp.stack([_block_diag([w_up_d[l], a_up_d[l]]) for l in range(DEPTH)]).astype(BF16)
    vec_b = jnp.stack([_pad_rows(jnp.stack([conv_b[l], gn_b[l], skip_b[l]]), 8) for l in range(DEPTH)])
    conv_w8 = jnp.stack([_pad_rows(conv_w[l], 8) for l in range(DEPTH)])
    vec_d = jnp.stack([_pad_rows(jnp.stack([w0_d[l], a0_d[l], k_k_d[l], k_a_d[l], r_k_d[l], gn_w_d[l], gn_b_d[l]]), 8)
                       for l in range(DEPTH)])
    head_of = jnp.arange(WIDTH) // HEAD_DIM
    mask_bd = (head_of[:, None] == head_of[None, :]).astype(F32)

    span = WINDOW + CHUNK
    rel = jnp.arange(span)[None, :] - WINDOW - jnp.arange(CHUNK)[:, None]
    one_hot = (_rel_bucket(rel)[..., None] == jnp.arange(NUM_BUCKETS)).astype(F32)
    bias = jnp.einsum('ijb,bh->hij', one_hot, rel_bias.astype(F32), precision=HIGHEST)

    n_dense = ffn_w_gate.shape[0]
    ff_steps = D_FF // D_FF_EXPERT
    lp = {
        'w_main': w_main, 'w_gate': w_gate, 'lbp': lbp, 'gn_a': gn_a,
        'w_if_t': jnp.swapaxes(w_in[:, :, off_if:off_c], 1, 2).astype(BF16),
        'mask_bd': mask_bd, 'conv_w8': conv_w8, 'vec_b': vec_b, 'w_qk': w_qk,
        'b_if': jnp.concatenate([b_i, b_f], axis=-1), 'sinks': sinks, 'bias': bias,
        'mu_d': _d_to_kernel_order(mu_d), 'vec_d': vec_d, 'w_wa': w_wa, 'g_up_d': g_up_d.astype(BF16),
        'w_br': w_br.astype(BF16), 'w_o': w_o.astype(BF16),
        'ln1_g': ln1_g, 'ln1_b': ln1_b, 'ln2_g': ln2_g, 'ln2_b': ln2_b,
        'ffn_wg': jnp.swapaxes(ffn_w_gate.astype(BF16).reshape(n_dense, D_MODEL, ff_steps, D_FF_EXPERT), 1, 2),
        'ffn_wu': jnp.swapaxes(ffn_w_up.astype(BF16).reshape(n_dense, D_MODEL, ff_steps, D_FF_EXPERT), 1, 2),
        'ffn_wd': ffn_w_down.astype(BF16).reshape(n_dense, ff_steps, D_FF_EXPERT, D_MODEL),
        'router_w': router_w.astype(BF16), 'router_b': router_b,
        'exp_wg': exp_w_gate.astype(BF16), 'exp_wu': exp_w_up.astype(BF16), 'exp_wd': exp_w_down.astype(BF16),
    }

    sample_states = {
        'swa_k': cache_swa_k, 'swa_v': cache_swa_v, 'hgrn': state_hgrn, 'mlstm_c': state_mlstm_c,
        'mlstm_n': state_mlstm_n, 'mlstm_m': state_mlstm_m, 'mlstm_conv': state_mlstm_conv,
        'rwkv': state_rwkv, 'rwkv_shift': state_rwkv_shift,
    }
    keep = cache_swa_k.shape[2]
    bp = x_prompt.shape[0]
    prompt_states = {}
    for name in STATE_NAMES:
        arr = sample_states[name]
        rows = (WINDOW,) + arr.shape[3:] if name in ('swa_k', 'swa_v') else arr.shape[2:]
        prompt_states[name] = jnp.zeros((DEPTH, bp) + tuple(rows), arr.dtype)

    y_prompt, pst = _trunk(x_prompt, prompt_states, False, keep, lp)
    y_sample, sst = _trunk(x_sample, sample_states, True, keep, lp)
    return (y_prompt, y_sample) + tuple(pst[n] for n in STATE_NAMES) + tuple(sst[n] for n in STATE_NAMES)
```

```python
import functools
import math

import jax
import jax.numpy as jnp
from jax import lax
from jax.experimental import pallas as pl
from jax.experimental.pallas import tpu as pltpu

F32 = jnp.float32
BF16 = jnp.bfloat16
HIGHEST = lax.Precision.HIGHEST

D_MODEL = 1024
DEPTH = 4
CHUNK = 64
HEAD_DIM = 64
N_HEADS = 4
WIDTH = N_HEADS * HEAD_DIM
C_KV_HEADS = 2
KV_WIDTH = C_KV_HEADS * HEAD_DIM
CONV_W = 4
WINDOW = 128
NUM_BUCKETS = 32
MAX_DISTANCE = 128
D_DECAY_LORA = 64
D_AAA_LORA = 64
D_GATE_LORA = 128
D_COLS = 3 * WIDTH + D_DECAY_LORA + D_AAA_LORA + D_GATE_LORA
N_BRANCH = 4
D_FF = 2816
N_EXPERTS = 8
TOP_K = 2
D_FF_EXPERT = 1408
ALPHA = (2 * DEPTH) ** 0.25
LN_EPS = 1e-5
HEAD_NORM_EPS = 1e-5
RWKV_GN_EPS = 64e-5
LB_FLOOR = 1e-30
LB_CEIL = 1.0 - 1e-6

GATE_COLS = N_BRANCH * D_MODEL
COL_A = 0
COL_D = COL_A + 4 * WIDTH
COL_B = COL_D + D_COLS
COL_C = COL_B + 3 * WIDTH
COL_IF = COL_C + WIDTH + 2 * KV_WIDTH
LANE_TILE = 128
MAIN_COLS = COL_IF + LANE_TILE
SUB = 16
STREAMS_PER_STEP = 4
V7X_VMEM_LIMIT = 48 * 1024 * 1024

NN = (((1,), (0,)), ((), ()))
NT = (((1,), (1,)), ((), ()))
TN = (((0,), (0,)), ((), ()))


def _layer_block(arr, layer):
    tail = arr.shape[1:]
    return pl.BlockSpec((None,) + tail, lambda *_: (layer,) + (0,) * len(tail))


def _cparams(*sem):
    return pltpu.CompilerParams(dimension_semantics=sem, vmem_limit_bytes=V7X_VMEM_LIMIT)


def _sigmoid(x):
    return 1.0 / (1.0 + jnp.exp(-x))


def _silu(x):
    return x * _sigmoid(x)


def _log_sigmoid(x):
    return jnp.minimum(x, 0.0) - jnp.log1p(jnp.exp(-jnp.abs(x)))


def _layernorm_rows(z, g, b):
    mu = jnp.mean(z, axis=-1, keepdims=True)
    zc = z - mu
    var = jnp.mean(zc * zc, axis=-1, keepdims=True)
    return zc * lax.rsqrt(var + LN_EPS) * g + b


def _split2(x):
    hi = x.astype(BF16)
    lo = (x - hi.astype(F32)).astype(BF16)
    return hi, lo


def _split3(x):
    hi = x.astype(BF16)
    r1 = x - hi.astype(F32)
    mid = r1.astype(BF16)
    lo = (r1 - mid.astype(F32)).astype(BF16)
    return hi, mid, lo


def _dot(a, b, dims=NN):
    return lax.dot_general(a, b, dims, preferred_element_type=F32)


def _mm2(a, b_parts, dims=NN):
    a_hi, a_lo = _split2(a)
    n = a.shape[0]
    both = _dot(jnp.concatenate([a_hi, a_lo], axis=0), b_parts[0], dims)
    return both[:n] + both[n:] + _dot(a_hi, b_parts[1], dims)


def _cumsum_rows(x):
    n = x.shape[0]
    row = lax.broadcasted_iota(jnp.int32, (n, n), 0)
    col = lax.broadcasted_iota(jnp.int32, (n, n), 1)
    tri = (row >= col).astype(BF16)
    hi, mid, lo = _split3(x)
    return _dot(tri, hi) + _dot(tri, mid) + _dot(tri, lo)


def _head_masks():
    lane_head = lax.broadcasted_iota(jnp.int32, (1, WIDTH), 1) // HEAD_DIM
    return [(lane_head == h) for h in range(N_HEADS)]


def _stack_heads(x, masks):
    return jnp.concatenate([jnp.where(mk, x, 0.0) for mk in masks], axis=0)


def _stack_parts(x, masks):
    parts = _split2(x)
    rows = [mk.astype(BF16) for mk in masks]
    return tuple(jnp.concatenate([p * r for r in rows], axis=0) for p in parts)


def _round_robin(stage_generators):
    live = list(stage_generators)
    while live:
        live = [g for g in live if next(g, _DONE) is not _DONE]


_DONE = object()


def _head_sums(x, ones_bd):
    hi, lo = _split2(x)
    return _dot(hi, ones_bd) + _dot(lo, ones_bd)


def _matmul_kernel(x_ref, w_ref, o_ref):
    o_ref[...] = _dot(x_ref[...], w_ref[...])


def _matmul(x, w, *, tm, tn):
    m, k = x.shape
    n = w.shape[1]
    tm = min(tm, m)
    tn = min(tn, n)
    return pl.pallas_call(
        _matmul_kernel,
        grid=(m // tm, n // tn),
        in_specs=[pl.BlockSpec((tm, k), lambda i, j: (i, 0)),
                  pl.BlockSpec((k, tn), lambda i, j: (0, j))],
        out_specs=pl.BlockSpec((tm, tn), lambda i, j: (i, j)),
        out_shape=jax.ShapeDtypeStruct((m, n), F32),
        compiler_params=_cparams("parallel", "parallel"),
        name="matmul",
    )(x, w)


PROJ_TN = 256


def _proj_kernel(x_ref, w_ref, o_ref):
    x = x_ref[...]
    n = w_ref.shape[2]
    for lo in range(0, n, PROJ_TN):
        hi = min(lo + PROJ_TN, n)
        o_ref[:, lo:hi] = _dot(x, w_ref[0, :, lo:hi])


def _proj(xb, w, layer, *, tm=512):
    m, k = xb.shape
    n = w.shape[2]
    tm = min(tm, m)
    return pl.pallas_call(
        _proj_kernel,
        grid=(m // tm,),
        in_specs=[pl.BlockSpec((tm, k), lambda i: (i, 0)),
                  pl.BlockSpec((1, k, n), lambda i: (layer, 0, 0), pipeline_mode=pl.Buffered(1))],
        out_specs=pl.BlockSpec((tm, n), lambda i: (i, 0)),
        out_shape=jax.ShapeDtypeStruct((m, n), F32),
        compiler_params=_cparams("parallel"),
        name="proj",
    )(xb, w)


def _head(a, h):
    return a[:, h * HEAD_DIM:(h + 1) * HEAD_DIM]


def _hgrn_kernel(q_ref, f_ref, i_ref, g_ref, lbp_ref, gn_ref, ones_ref, mask_ref, s0_ref, y_ref, s_out_ref,
                 st_ref, phi_ref, plo_ref, *, nc, bb):
    c = pl.program_id(1)

    @pl.when(c == 0)
    def _():
        for bi in range(bb):
            st_ref[bi] = jnp.zeros(st_ref.shape[1:], F32)
            for h in range(N_HEADS):
                st_ref[bi, h * HEAD_DIM:(h + 1) * HEAD_DIM, h * HEAD_DIM:(h + 1) * HEAD_DIM] = s0_ref[bi, h].T

    _round_robin([_hgrn_chunk(q_ref.at[bi], f_ref.at[bi], i_ref.at[bi], g_ref.at[bi], lbp_ref, gn_ref, ones_ref,
                              mask_ref, y_ref.at[bi], st_ref.at[bi], phi_ref.at[bi], plo_ref.at[bi])
                  for bi in range(bb)])

    @pl.when(c == nc - 1)
    def _():
        for bi in range(bb):
            for h in range(N_HEADS):
                s_out_ref[bi, h] = st_ref[bi, h * HEAD_DIM:(h + 1) * HEAD_DIM, h * HEAD_DIM:(h + 1) * HEAD_DIM].T


def _hgrn_chunk(q_ref, f_ref, i_ref, g_ref, lbp_ref, gn_ref, ones_ref, mask_ref, y_ref, st_ref, phi_ref, plo_ref):
    L = CHUNK
    ones_bd = ones_ref[...]
    masks = _head_masks()
    log_lb = lbp_ref[0:1, :]
    log1m_lb = lbp_ref[1:2, :]
    one_m_lb = lbp_ref[2:3, :]
    zf = f_ref[...]
    u = log_lb
    w = log1m_lb + _log_sigmoid(zf)
    logf = jnp.maximum(u, w) + jnp.log1p(jnp.exp(-jnp.abs(u - w)))
    k = one_m_lb * _sigmoid(-zf)
    q = _silu(q_ref[...])
    v = i_ref[...]

    b = _cumsum_rows(logf)
    yield
    b_last = b[L - 1:L]
    qe = q * jnp.exp(b)
    kdec = k * jnp.exp(b_last - b)
    sub_row = lax.broadcasted_iota(jnp.int32, (SUB, WIDTH), 0)

    for s in range(L):
        r0 = (s // SUB) * SUB
        p = q[r0:r0 + SUB] * k[s:s + 1] * jnp.exp(jnp.minimum(b[r0:r0 + SUB] - b[s:s + 1], 0.0))
        hi, lo = _split2(jnp.where(sub_row >= s - r0, p, 0.0))
        phi_ref[s * SUB:(s + 1) * SUB, :] = hi
        plo_ref[s * SUB:(s + 1) * SUB, :] = lo
    yield
    att = _dot(phi_ref[...], ones_bd) + _dot(plo_ref[...], ones_bd)
    st = st_ref[...]
    o_inter = _dot(qe, st, NT)
    upd = _dot(v, kdec, TN)
    scores = []
    for blk in range(1, L // SUB):
        r0 = blk * SUB
        ref_row = b[r0 - 1:r0]
        q_s = q[r0:r0 + SUB] * jnp.exp(b[r0:r0 + SUB] - ref_row)
        k_s = k[:r0] * jnp.exp(ref_row - b[:r0])
        scores.append(_dot(q_s, _stack_heads(k_s, masks), NT))
    yield
    st_ref[...] = st * jnp.exp(b_last) + upd * mask_ref[...]
    o_blocks = []
    for blk in range(L // SUB):
        r0 = blk * SUB
        acc = jnp.zeros((SUB, WIDTH), F32)
        for j in range(SUB):
            s = r0 + j
            acc = acc + att[s * SUB:(s + 1) * SUB] * v[s:s + 1]
        if blk > 0:
            acc = acc + _dot(scores[blk - 1], _stack_heads(v[:r0], masks))
        o_blocks.append(acc)
    yield
    o = jnp.concatenate(o_blocks, axis=0) + o_inter
    ms = _head_sums(o * o, ones_bd) * (1.0 / HEAD_DIM)
    yield
    y_ref[...] = (o * lax.rsqrt(ms + HEAD_NORM_EPS) * gn_ref[...] * _silu(g_ref[...])).astype(y_ref.dtype)


def _hgrn(h3, lbp, gn, layer, mask_bd, s0):
    bsz, t, _ = h3.shape
    nc = t // CHUNK
    cb = COL_A // WIDTH
    bb = STREAMS_PER_STEP

    def col(j):
        return pl.BlockSpec((bb, CHUNK, WIDTH), lambda b, c, j=j: (b, c, cb + j))

    square = pl.BlockSpec((WIDTH, WIDTH), lambda b, c: (0, 0))
    state = pl.BlockSpec((bb, N_HEADS, HEAD_DIM, HEAD_DIM), lambda b, c: (b, 0, 0, 0))
    return pl.pallas_call(
        functools.partial(_hgrn_kernel, nc=nc, bb=bb),
        grid=(bsz // bb, nc),
        in_specs=[col(0), col(1), col(2), col(3),
                  _layer_block(lbp, layer), _layer_block(gn, layer),
                  square, square, state],
        out_specs=[pl.BlockSpec((bb, CHUNK, WIDTH), lambda b, c: (b, c, 0)), state],
        out_shape=[jax.ShapeDtypeStruct((bsz, t, WIDTH), BF16),
                   jax.ShapeDtypeStruct((bsz, N_HEADS, HEAD_DIM, HEAD_DIM), F32)],
        scratch_shapes=[pltpu.VMEM((bb, WIDTH, WIDTH), F32),
                        pltpu.VMEM((bb, CHUNK * SUB, WIDTH), BF16),
                        pltpu.VMEM((bb, CHUNK * SUB, WIDTH), BF16)],
        compiler_params=_cparams("parallel", "arbitrary"),
        name="hgrn2",
    )(h3, h3, h3, h3, lbp, gn, mask_bd.astype(BF16), mask_bd, s0)


def _mlstm_kernel(u_ref, v_ref, o_ref, gcol_ref, xin_ref, wift_ref, bcol_ref, brow_ref, cw_ref, vec_ref, wqk_ref,
                  ones_ref, mask_ref, c0_ref, n0_ref, m0_ref, conv0_ref,
                  y_ref, c_out_ref, n_out_ref, m_out_ref, conv_out_ref,
                  c_ref, n_ref, m_ref, carry_ref, *, nc, bb):
    c = pl.program_id(1)

    @pl.when(c == 0)
    def _():
        for bi in range(bb):
            c_ref[bi] = jnp.zeros(c_ref.shape[1:], F32)
            for h in range(N_HEADS):
                c_ref[bi, h * HEAD_DIM:(h + 1) * HEAD_DIM, h * HEAD_DIM:(h + 1) * HEAD_DIM] = c0_ref[bi, h]
        n_ref[...] = n0_ref[...]
        m_ref[...] = m0_ref[...]
        carry_ref[...] = conv0_ref[...]

    _round_robin([_mlstm_chunk(u_ref.at[bi], v_ref.at[bi], o_ref.at[bi], gcol_ref.at[bi], xin_ref.at[bi],
                               wift_ref, bcol_ref, brow_ref, cw_ref, vec_ref, wqk_ref, ones_ref, mask_ref, y_ref.at[bi],
                               c_ref.at[bi], n_ref.at[bi], m_ref.at[bi], carry_ref.at[bi]) for bi in range(bb)])

    @pl.when(c == nc - 1)
    def _():
        for bi in range(bb):
            for h in range(N_HEADS):
                c_out_ref[bi, h] = c_ref[bi, h * HEAD_DIM:(h + 1) * HEAD_DIM, h * HEAD_DIM:(h + 1) * HEAD_DIM]
        n_out_ref[...] = n_ref[...]
        m_out_ref[...] = m_ref[...]
        conv_out_ref[...] = carry_ref[...]


def _mlstm_chunk(u_ref, v_ref, o_ref, gcol_ref, xin_ref, wift_ref, bcol_ref, brow_ref, cw_ref, vec_ref, wqk_ref,
                 ones_ref, mask_ref, y_ref, c_ref, n_ref, m_ref, carry_ref):
    L = CHUNK
    heads = range(N_HEADS)

    u = u_ref[...]
    carry = carry_ref[...]
    row8 = lax.broadcasted_iota(jnp.int32, (8, WIDTH), 0)
    conv = vec_ref[0:1, :] + cw_ref[CONV_W - 1:CONV_W, :] * u
    for d in range(1, CONV_W):
        rolled = pltpu.roll(u, d, 0)
        top = jnp.where(row8 < d, pltpu.roll(carry, d, 0), rolled[0:8])
        conv = conv + cw_ref[CONV_W - 1 - d:CONV_W - d, :] * jnp.concatenate([top, rolled[8:]], axis=0)
    carry_ref[...] = u[L - 8:L]
    cact = _silu(conv)
    qk_all = _dot(cact.astype(BF16), wqk_ref[...])

    row = lax.broadcasted_iota(jnp.int32, (L, L), 0)
    col = lax.broadcasted_iota(jnp.int32, (L, L), 1)
    lower = row >= col
    raw_c = gcol_ref[:, 0:2 * N_HEADS] + brow_ref[...]
    lane8 = lax.broadcasted_iota(jnp.int32, raw_c.shape, 1)
    gcol = jnp.where(lane8 < N_HEADS, raw_c, _log_sigmoid(raw_c))
    raw_r = _dot(wift_ref[...], xin_ref[...], NT) + bcol_ref[...]
    sub8 = lax.broadcasted_iota(jnp.int32, raw_r.shape, 0)
    grow = jnp.where(sub8 < N_HEADS, raw_r, _log_sigmoid(raw_r))
    f_col = jnp.dot(lower.astype(F32), gcol, preferred_element_type=F32, precision=HIGHEST)
    f_row = jnp.dot(grow, (row <= col).astype(F32), preferred_element_type=F32, precision=HIGHEST)
    r8 = lax.broadcasted_iota(jnp.int32, (2 * N_HEADS, 2 * WIDTH), 0)
    c8 = lax.broadcasted_iota(jnp.int32, (2 * N_HEADS, 2 * WIDTH), 1)
    expand = (c8 // HEAD_DIM == r8).astype(BF16)
    x8 = jnp.where(lane8 < N_HEADS, gcol, f_col)
    x_hi, x_mid, x_lo = _split3(x8)
    both = _dot(x_hi, expand) + _dot(x_mid, expand) + _dot(x_lo, expand)
    li_all = both[:, 0:WIDTH]
    f_all = both[:, WIDTH:2 * WIDTH]
    d_row = jnp.concatenate([grow[h:h + 1, :] - f_row[N_HEADS + h:N_HEADS + h + 1, :] for h in heads], axis=1)
    yield
    t_idx = lax.broadcasted_iota(jnp.int32, (L, WIDTH), 0)
    s_idx = lax.broadcasted_iota(jnp.int32, (L, WIDTH), 1) % L
    ones_bd = ones_ref[...]
    masks = _head_masks()
    cm = li_all - f_all
    for sh in (1, 2, 4, 8, 16, 32):
        cm = jnp.maximum(cm, jnp.where(t_idx >= sh, pltpu.roll(cm, sh, 0), -jnp.inf))
    g = f_all + m_ref[...]
    mt = jnp.maximum(g, f_all + cm)
    wg = jnp.exp(g - mt)
    wd = jnp.exp(jnp.where(s_idx <= t_idx, f_all + d_row - mt, -jnp.inf))
    q = qk_all[:, 0:WIDTH]
    k = qk_all[:, WIDTH:2 * WIDTH]
    v = v_ref[...]
    q_bf = q.astype(BF16)
    v_bf = v.astype(BF16)
    head_rows = [mk.astype(BF16) for mk in masks]
    k_stack = jnp.concatenate([k.astype(BF16) * r for r in head_rows], axis=0)
    v_stack = jnp.concatenate([v_bf * r for r in head_rows], axis=0)
    c_bd = c_ref[...]
    n_row = n_ref[...]
    qk = _dot(q_bf, k_stack, NT) * wd
    q_c = _dot(q_bf, c_bd.astype(BF16))
    q_n = _head_sums(q * n_row, ones_bd)
    yield
    qkv = _dot(qk.astype(BF16), v_stack)
    qk_sum = _head_sums(qk, ones_bd)
    mt_last = mt[L - 1:L]
    kw = k * jnp.exp(f_all[L - 1:L] - f_all + li_all - mt_last)
    c_upd = _dot(kw.astype(BF16), v_bf, TN)
    yield
    wgl = wg[L - 1:L]
    c_ref[...] = wgl * c_bd + c_upd * mask_ref[...]
    n_ref[...] = wgl * n_row + jnp.sum(kw, axis=0, keepdims=True)
    m_ref[...] = mt_last
    den = wg * q_n + qk_sum
    hh = (wg * q_c + qkv) / jnp.maximum(jnp.abs(den), jnp.exp(-mt))
    z = _sigmoid(o_ref[...]) * hh
    mu = _head_sums(z, ones_bd) * (1.0 / HEAD_DIM)
    yield
    zc = z - mu
    var = _head_sums(zc * zc, ones_bd) * (1.0 / HEAD_DIM)
    yield
    y_ref[...] = (zc * lax.rsqrt(var + HEAD_NORM_EPS) * vec_ref[1:2, :] + vec_ref[2:3, :] * cact).astype(y_ref.dtype)


def _mlstm(h3, xb3, w_if_t, b_col, b_row, conv_w, vecs, w_qk, layer, mask_bd, c0, n0, m0, conv0):
    bsz, t, _ = h3.shape
    nc = t // CHUNK
    cb = COL_B // WIDTH
    bb = STREAMS_PER_STEP
    state4 = pl.BlockSpec((bb, N_HEADS, HEAD_DIM, HEAD_DIM), lambda b, c: (b, 0, 0, 0))
    state_n = pl.BlockSpec((bb, 1, WIDTH), lambda b, c: (b, 0, 0))
    state_m = state_n
    state_conv = pl.BlockSpec((bb, 8, WIDTH), lambda b, c: (b, 0, 0))

    def const(shape):
        return pl.BlockSpec(shape, lambda b, c: tuple(0 for _ in shape))

    def col(j):
        return pl.BlockSpec((bb, CHUNK, WIDTH), lambda b, c, j=j: (b, c, cb + j))

    return pl.pallas_call(
        functools.partial(_mlstm_kernel, nc=nc, bb=bb),
        grid=(bsz // bb, nc),
        in_specs=[col(0), col(1), col(2),
                  pl.BlockSpec((bb, CHUNK, LANE_TILE), lambda b, c: (b, c, COL_IF // LANE_TILE)),
                  pl.BlockSpec((bb, CHUNK, D_MODEL), lambda b, c: (b, c, 0)), _layer_block(w_if_t, layer),
                  _layer_block(b_col, layer), _layer_block(b_row, layer), _layer_block(conv_w, layer),
                  _layer_block(vecs, layer), _layer_block(w_qk, layer), const((WIDTH, WIDTH)), const((WIDTH, WIDTH)),
                  state4, state_n, state_m, state_conv],
        out_specs=[pl.BlockSpec((bb, CHUNK, WIDTH), lambda b, c: (b, c, 0)), state4, state_n, state_m, state_conv],
        out_shape=[jax.ShapeDtypeStruct((bsz, t, WIDTH), BF16),
                   jax.ShapeDtypeStruct((bsz, N_HEADS, HEAD_DIM, HEAD_DIM), F32),
                   jax.ShapeDtypeStruct((bsz, 1, WIDTH), F32),
                   jax.ShapeDtypeStruct((bsz, 1, WIDTH), F32),
                   jax.ShapeDtypeStruct((bsz, 8, WIDTH), F32)],
        scratch_shapes=[pltpu.VMEM((bb, WIDTH, WIDTH), F32),
                        pltpu.VMEM((bb, 1, WIDTH), F32),
                        pltpu.VMEM((bb, 1, WIDTH), F32),
                        pltpu.VMEM((bb, 8, WIDTH), F32)],
        compiler_params=_cparams("parallel", "arbitrary"),
        name="mlstm",
    )(h3, h3, h3, h3, xb3, w_if_t, b_col, b_row, conv_w, vecs, w_qk, mask_bd.astype(BF16), mask_bd, c0, n0, m0, conv0)


SWA_CHUNKS_PER_STEP = 4


def _swa_kernel(*refs, prev_valid, cps, multi_step):
    if multi_step:
        (q_ref, kc_ref, vc_ref, kp_ref, vp_ref, k_ref, v_ref, bias_ref, sink_ref, y_ref, s_ref, p_ref) = refs
    else:
        (q_ref, kc_ref, vc_ref, k_ref, v_ref, bias_ref, sink_ref, y_ref, s_ref, p_ref) = refs
    g = pl.program_id(1)
    L = CHUNK
    span = WINDOW + L
    k_win = kc_ref[0]
    v_win = vc_ref[0]
    if multi_step:
        k_win = jnp.where(g == 0, k_win, kp_ref[0])
        v_win = jnp.where(g == 0, v_win, vp_ref[0])
    kcat = jnp.concatenate([k_win, k_ref[0]], axis=0).astype(BF16)
    vcat = jnp.concatenate([v_win, v_ref[0]], axis=0).astype(BF16)
    kv_of = [h // (N_HEADS // C_KV_HEADS) for h in range(N_HEADS)]
    for j in range(cps):
        q = q_ref[0, j * L:(j + 1) * L, :].astype(BF16)
        k_j = kcat[j * L:j * L + span]
        for h in range(N_HEADS):
            s_ref[j * N_HEADS + h] = _dot(_head(q, h), _head(k_j, kv_of[h]), NT)
    def softmax_tile(j, h):
        key_pos = lax.broadcasted_iota(jnp.int32, (L, span), 1) + (g * cps + j) * L
        s = s_ref[j * N_HEADS + h] * HEAD_DIM ** -0.5 + bias_ref[h]
        if not prev_valid:
            s = jnp.where(key_pos >= WINDOW, s, -jnp.inf)
        sink = sink_ref[0:1, h:h + 1]
        m = jnp.maximum(jnp.max(s, axis=1, keepdims=True), sink)
        yield
        p = jnp.exp(s - m)
        denom = jnp.sum(p, axis=1, keepdims=True) + jnp.exp(sink - m)
        yield
        p_ref[j * N_HEADS + h] = (p * (1.0 / denom)).astype(BF16)

    _round_robin([softmax_tile(j, h) for j in range(cps) for h in range(N_HEADS)])
    for j in range(cps):
        v_j = vcat[j * L:j * L + span]
        o = [_dot(p_ref[j * N_HEADS + h], _head(v_j, kv_of[h])) for h in range(N_HEADS)]
        y_ref[0, j * L:(j + 1) * L, :] = jnp.concatenate(o, axis=1).astype(y_ref.dtype)


def _swa(h3, cache_k, cache_v, bias, sinks, layer, prev_valid):
    bsz, t, _ = h3.shape
    nc = t // CHUNK
    cps = min(SWA_CHUNKS_PER_STEP, nc)
    rows = cps * CHUNK
    steps = nc // cps
    multi_step = steps > 1
    assert nc % cps == 0 and (not multi_step or rows % WINDOW == 0), (t, cps)
    qb = COL_C // WIDTH
    kb = (COL_C + WIDTH) // KV_WIDTH
    cache = pl.BlockSpec((1, WINDOW, KV_WIDTH), lambda b, g: (b, 0, 0))

    def own(col):
        return pl.BlockSpec((1, rows, KV_WIDTH), lambda b, g, col=col: (b, g, col))

    def before(col):
        return pl.BlockSpec((1, WINDOW, KV_WIDTH),
                            lambda b, g, col=col: (b, jnp.maximum(g * (rows // WINDOW) - 1, 0), col))

    kv_specs = [cache, cache] + ([before(kb), before(kb + 1)] if multi_step else []) + [own(kb), own(kb + 1)]
    kv_args = [cache_k, cache_v] + ([h3, h3] if multi_step else []) + [h3, h3]
    return pl.pallas_call(
        functools.partial(_swa_kernel, prev_valid=prev_valid, cps=cps, multi_step=multi_step),
        grid=(bsz, steps),
        in_specs=[pl.BlockSpec((1, rows, WIDTH), lambda b, g: (b, g, qb))] + kv_specs + [
                  pl.BlockSpec((N_HEADS, CHUNK, WINDOW + CHUNK), lambda b, g: (0, 0, 0)),
                  _layer_block(sinks, layer)],
        out_specs=pl.BlockSpec((1, rows, WIDTH), lambda b, g: (b, g, 0)),
        out_shape=jax.ShapeDtypeStruct((bsz, t, WIDTH), BF16),
        scratch_shapes=[pltpu.VMEM((cps * N_HEADS, CHUNK, WINDOW + CHUNK), F32),
                        pltpu.VMEM((cps * N_HEADS, CHUNK, WINDOW + CHUNK), BF16)],
        compiler_params=_cparams("parallel", "parallel"),
        name="swa",
    )(h3, *kv_args, bias, sinks)


def _rwkv_kernel(pd_ref, mu_ref, vec_ref, wwa_ref, gup_ref, ones_ref, mask_ref, s0_ref, sh0_ref,
                 y_ref, s_out_ref, sh_out_ref, st_ref, carry_ref, *, nc, bb):
    c = pl.program_id(1)

    @pl.when(c == 0)
    def _():
        for bi in range(bb):
            st_ref[bi] = jnp.zeros(st_ref.shape[1:], F32)
            for h in range(N_HEADS):
                st_ref[bi, h * HEAD_DIM:(h + 1) * HEAD_DIM, h * HEAD_DIM:(h + 1) * HEAD_DIM] = s0_ref[bi, h]
            carry_ref[bi, 7:8, :] = sh0_ref[bi]

    _round_robin([_rwkv_chunk(pd_ref.at[bi], mu_ref, vec_ref, wwa_ref, gup_ref, ones_ref, mask_ref,
                              y_ref.at[bi], st_ref.at[bi], carry_ref.at[bi]) for bi in range(bb)])

    @pl.when(c == nc - 1)
    def _():
        for bi in range(bb):
            for h in range(N_HEADS):
                s_out_ref[bi, h] = st_ref[bi, h * HEAD_DIM:(h + 1) * HEAD_DIM, h * HEAD_DIM:(h + 1) * HEAD_DIM]
            sh_out_ref[bi] = carry_ref[bi, 7:8, :]


def _rwkv_chunk(pd_ref, mu_ref, vec_ref, wwa_ref, gup_ref, ones_ref, mask_ref, y_ref, st_ref, carry_ref):
    L = CHUNK
    ones_bd = ones_ref[...]
    masks = _head_masks()

    pd = pd_ref[...]
    rolled = pltpu.roll(pd, 1, 0)
    row8 = lax.broadcasted_iota(jnp.int32, (8, D_COLS), 0)
    top = jnp.where(row8 == 0, carry_ref[7:8, :], rolled[0:8])
    pd_prev = jnp.concatenate([top, rolled[8:]], axis=0)
    carry_ref[...] = pd[L - 8:L]
    mixed = pd + (pd_prev - pd) * mu_ref[...]
    r = mixed[:, 0:WIDTH]
    k_raw = mixed[:, WIDTH:2 * WIDTH]
    v = mixed[:, 2 * WIDTH:3 * WIDTH]
    wa = mixed[:, 3 * WIDTH:3 * WIDTH + 2 * D_DECAY_LORA]
    g_in = mixed[:, 3 * WIDTH + 2 * D_DECAY_LORA:D_COLS]
    lane_wa = lax.broadcasted_iota(jnp.int32, wa.shape, 1)
    wa_act = jnp.where(lane_wa < D_DECAY_LORA, jnp.tanh(wa), wa)
    lora = _dot(wa_act.astype(BF16), wwa_ref[...])
    g_d = _dot(_sigmoid(g_in).astype(BF16), gup_ref[...])
    yield
    z = -(vec_ref[0:1, :] + lora[:, 0:WIDTH])
    w_log = -(jnp.maximum(z, 0.0) + jnp.log1p(jnp.exp(-jnp.abs(z)))) - 0.5
    lw = -jnp.exp(w_log)
    a_d = _sigmoid(vec_ref[1:2, :] + lora[:, WIDTH:2 * WIDTH])
    kk = k_raw * vec_ref[2:3, :]
    kk_norm = _head_sums(kk * kk, ones_bd)
    cw = _cumsum_rows(lw)
    yield
    kk = kk * lax.rsqrt(jnp.maximum(kk_norm, 1e-24))
    k = k_raw * (1.0 + (a_d - 1.0) * vec_ref[3:4, :])
    a = -kk
    b = kk * a_d
    g_in_c = jnp.exp(cw)
    g_inv = jnp.exp(-cw)
    at = a * jnp.exp(cw - lw)
    rt = r * g_in_c
    bt = b * g_inv
    kt = k * g_inv
    x = jnp.concatenate([at, rt], axis=0)
    bt_st = _stack_parts(bt, masks)
    kt_st = _stack_parts(kt, masks)
    gram_b = _mm2(x, bt_st, NT)
    gram_k = _mm2(x, kt_st, NT)
    st = st_ref[...]
    u = _mm2(x, _split2(st), NT)
    yield

    t_idx = lax.broadcasted_iota(jnp.int32, (L, WIDTH), 0)
    i_idx = lax.broadcasted_iota(jnp.int32, (L, WIDTH), 1) % L
    strict = i_idx < t_idx
    incl = i_idx <= t_idx
    same_blk = (i_idx // SUB) == (t_idx // SUB)
    eye = (i_idx == t_idx).astype(F32)
    n_all = jnp.where(strict, gram_b[0:L], 0.0)
    m_all = jnp.where(strict, gram_k[0:L], 0.0)
    rb_all = jnp.where(incl, gram_b[L:2 * L], 0.0)
    rk_all = jnp.where(incl, gram_k[L:2 * L], 0.0)
    n_d = jnp.where(same_blk, n_all, 0.0)
    n_off = jnp.where(same_blk, 0.0, n_all)

    v_st = _stack_parts(v, masks)
    x2 = _mm2(n_d, _stack_parts(n_d, masks))
    rhs = u[0:L] + _mm2(m_all, v_st)
    bonus = _head_sums(r * k * vec_ref[4:5, :], ones_bd) * v
    yield
    x2_st = _stack_parts(x2, masks)
    x4 = _mm2(x2, x2_st)
    t_d = eye + n_d
    t_d = t_d + _mm2(t_d, x2_st)
    yield
    x4_st = _stack_parts(x4, masks)
    x8 = _mm2(x4, x4_st)
    t_d = t_d + _mm2(t_d, x4_st)
    yield
    t_d = t_d + _mm2(t_d, _stack_parts(x8, masks))
    yield
    n1 = _mm2(t_d, _stack_parts(n_off, masks))
    yield
    n1_st = _stack_parts(n1, masks)
    n2 = _mm2(n1, n1_st)
    yield
    a2 = eye + n1 + n2 + _mm2(n2, n1_st)
    yield
    t_full = _mm2(a2, _stack_parts(t_d, masks))
    yield
    sa = _mm2(t_full, _stack_parts(rhs, masks))
    yield
    y = u[L:2 * L] + _mm2(rb_all, _stack_parts(sa, masks)) + _mm2(rk_all, v_st)
    sv_hi, sv_lo = _split2(jnp.concatenate([sa, v], axis=0))
    bk_hi, bk_lo = _split2(jnp.concatenate([bt, kt], axis=0))
    upd = _dot(sv_hi, bk_hi, TN) + _dot(sv_lo, bk_hi, TN) + _dot(sv_hi, bk_lo, TN)
    yield
    st_ref[...] = (st + upd * mask_ref[...]) * g_in_c[L - 1:L]

    mu = _head_sums(y, ones_bd) * (1.0 / HEAD_DIM)
    yield
    yc = y - mu
    var = _head_sums(yc * yc, ones_bd) * (1.0 / HEAD_DIM)
    yield
    o = yc * lax.rsqrt(var + RWKV_GN_EPS) * vec_ref[5:6, :] + vec_ref[6:7, :]
    y_ref[...] = ((o + bonus) * g_d).astype(y_ref.dtype)


def _rwkv(h3, mu, vecs, w_wa, g_up, layer, mask_bd, s0, shift0):
    bsz, t, _ = h3.shape
    nc = t // CHUNK
    bb = STREAMS_PER_STEP
    state = pl.BlockSpec((bb, N_HEADS, HEAD_DIM, HEAD_DIM), lambda g, i: (g, 0, 0, 0))
    shift = pl.BlockSpec((bb, 1, D_COLS), lambda g, i: (g, 0, 0))

    def const(shape):
        return pl.BlockSpec(shape, lambda g, i: tuple(0 for _ in shape))

    return pl.pallas_call(
        functools.partial(_rwkv_kernel, nc=nc, bb=bb),
        grid=(bsz // bb, nc),
        in_specs=[pl.BlockSpec((bb, CHUNK, D_COLS), lambda g, i: (g, i, COL_D // D_COLS)),
                  _layer_block(mu, layer), _layer_block(vecs, layer), _layer_block(w_wa, layer),
                  _layer_block(g_up, layer), const((WIDTH, WIDTH)), const((WIDTH, WIDTH)), state, shift],
        out_specs=[pl.BlockSpec((bb, CHUNK, WIDTH), lambda g, i: (g, i, 0)), state, shift],
        out_shape=[jax.ShapeDtypeStruct((bsz, t, WIDTH), BF16),
                   jax.ShapeDtypeStruct((bsz, N_HEADS, HEAD_DIM, HEAD_DIM), F32),
                   jax.ShapeDtypeStruct((bsz, 1, D_COLS), F32)],
        scratch_shapes=[pltpu.VMEM((bb, WIDTH, WIDTH), F32), pltpu.VMEM((bb, 8, D_COLS), F32)],
        compiler_params=_cparams("parallel", "arbitrary"),
        name="rwkv7",
    )(h3, mu, vecs, w_wa, g_up, mask_bd.astype(BF16), mask_bd, s0, shift0)


def _merge_kernel(ya_ref, yb_ref, yc_ref, yd_ref, xb_ref, wg_ref, wbr_ref, wo_ref, x_ref,
                  lng_ref, lnb_ref, xo_ref, xob_ref):
    ys = (ya_ref, yb_ref, yc_ref, yd_ref)
    xb = xb_ref[...]
    merged = None
    for n in range(N_BRANCH):
        gate = _dot(xb, wg_ref[0, :, n * D_MODEL:(n + 1) * D_MODEL])
        term = _sigmoid(gate) * _dot(ys[n][...], wbr_ref[0, n])
        merged = term if merged is None else merged + term
    out = _dot(merged.astype(BF16), wo_ref[0])
    xn = _layernorm_rows(ALPHA * x_ref[...] + out, lng_ref[...], lnb_ref[...])
    xo_ref[...] = xn
    xob_ref[...] = xn.astype(BF16)


def _merge(ys, xb2, w_gate, wbr, wo, layer, x2, lng, lnb, *, tm=512):
    m = x2.shape[0]
    tm = min(tm, m)
    ysp = pl.BlockSpec((tm, WIDTH), lambda i: (i, 0))
    row = pl.BlockSpec((tm, D_MODEL), lambda i: (i, 0))
    vec = _layer_block(lng, layer)
    resident = dict(pipeline_mode=pl.Buffered(1))
    return pl.pallas_call(
        _merge_kernel,
        grid=(m // tm,),
        in_specs=[ysp, ysp, ysp, ysp, row,
                  pl.BlockSpec((1, D_MODEL, GATE_COLS), lambda i: (layer, 0, 0), **resident),
                  pl.BlockSpec((1, N_BRANCH, WIDTH, D_MODEL), lambda i: (layer, 0, 0, 0), **resident),
                  pl.BlockSpec((1, D_MODEL, D_MODEL), lambda i: (layer, 0, 0), **resident),
                  row, vec, vec],
        out_specs=[row, row],
        out_shape=[jax.ShapeDtypeStruct((m, D_MODEL), F32), jax.ShapeDtypeStruct((m, D_MODEL), BF16)],
        compiler_params=_cparams("parallel"),
        name="merge",
    )(*ys, xb2, w_gate, wbr, wo, x2, lng, lnb)


def _ffn_kernel(*refs, n_steps, gated):
    if gated:
        xb_ref, x_ref, gates_ref, wg_ref, wu_ref, wd_ref, lng_ref, lnb_ref, xo_ref, xob_ref, acc_ref = refs
    else:
        xb_ref, x_ref, wg_ref, wu_ref, wd_ref, lng_ref, lnb_ref, xo_ref, xob_ref, acc_ref = refs
    j = pl.program_id(1)

    @pl.when(j == 0)
    def _():
        acc_ref[...] = jnp.zeros_like(acc_ref)

    xb = xb_ref[...]
    hg = _dot(xb, wg_ref[0, 0])
    hu = _dot(xb, wu_ref[0, 0])
    part = _dot((_silu(hg) * hu).astype(BF16), wd_ref[0, 0])
    if gated:
        gates = gates_ref[...]
        lane = lax.broadcasted_iota(jnp.int32, gates.shape, 1)
        part = jnp.sum(jnp.where(lane == j, gates, 0.0), axis=1, keepdims=True) * part
    acc_ref[...] += part

    @pl.when(j == n_steps - 1)
    def _():
        xn = _layernorm_rows(ALPHA * x_ref[...] + acc_ref[...], lng_ref[...], lnb_ref[...])
        xo_ref[...] = xn
        xob_ref[...] = xn.astype(BF16)


def _ffn(xb, x2, wg, wu, wd, layer, lng, lnb, ln_layer, gates=None, *, tm=512):
    m = x2.shape[0]
    tm = min(tm, m)
    _, n_steps, _, tf = wg.shape
    row = pl.BlockSpec((tm, D_MODEL), lambda i, j: (i, 0))
    vec = _layer_block(lng, ln_layer)
    w_in_spec = pl.BlockSpec((1, 1, D_MODEL, tf), lambda i, j: (layer, j, 0, 0))
    w_out_spec = pl.BlockSpec((1, 1, tf, D_MODEL), lambda i, j: (layer, j, 0, 0))
    in_specs = [row, row]
    args = [xb, x2]
    if gates is not None:
        in_specs.append(pl.BlockSpec((tm, N_EXPERTS), lambda i, j: (i, 0)))
        args.append(gates)
    in_specs += [w_in_spec, w_in_spec, w_out_spec, vec, vec]
    args += [wg, wu, wd, lng, lnb]
    return pl.pallas_call(
        functools.partial(_ffn_kernel, n_steps=n_steps, gated=gates is not None),
        grid=(m // tm, n_steps),
        in_specs=in_specs,
        out_specs=[row, row],
        out_shape=[jax.ShapeDtypeStruct((m, D_MODEL), F32), jax.ShapeDtypeStruct((m, D_MODEL), BF16)],
        scratch_shapes=[pltpu.VMEM((tm, D_MODEL), F32)],
        compiler_params=_cparams("parallel", "arbitrary"),
        name="moe" if gates is not None else "ffn",
    )(*args)


STATE_NAMES = ('swa_k', 'swa_v', 'hgrn', 'mlstm_c', 'mlstm_n', 'mlstm_m', 'mlstm_conv', 'rwkv', 'rwkv_shift')

_D_ORIG = (('r', WIDTH), ('w', D_DECAY_LORA), ('k', WIDTH), ('v', WIDTH), ('a', D_AAA_LORA), ('g', D_GATE_LORA))
_D_KERNEL = ('r', 'k', 'v', 'w', 'a', 'g')


def _d_pieces(arr):
    out, off = {}, 0
    for name, size in _D_ORIG:
        out[name] = arr[..., off:off + size]
        off += size
    return out


def _d_to_kernel_order(arr):
    p = _d_pieces(arr)
    return jnp.concatenate([p[n] for n in _D_KERNEL], axis=-1)


def _d_to_original_order(arr):
    sizes = dict(_D_ORIG)
    p, off = {}, 0
    for name in _D_KERNEL:
        p[name] = arr[..., off:off + sizes[name]]
        off += sizes[name]
    return jnp.concatenate([p[n] for n, _ in _D_ORIG], axis=-1)


def _rel_bucket(rel):
    half = NUM_BUCKETS // 2
    exact = half // 2
    dist = jnp.abs(rel)
    far = exact + (jnp.log(jnp.maximum(dist, 1).astype(F32) / exact)
                   / math.log(MAX_DISTANCE / exact) * (half - exact)).astype(jnp.int32)
    far = jnp.minimum(far, half - 1)
    return jnp.where(rel > 0, half, 0) + jnp.where(dist < exact, dist, far)


def _lower_bounds(lb_raw):
    sm = jax.nn.softmax(lb_raw.astype(F32), axis=0)
    lb = jnp.concatenate([jnp.zeros_like(sm[:1]), jnp.cumsum(sm[1:], axis=0)[:-1]], axis=0)
    return jnp.clip(lb, 0.0, LB_CEIL)


def _block_diag(blocks):
    rows = sum(b.shape[0] for b in blocks)
    cols = sum(b.shape[1] for b in blocks)
    out = jnp.zeros((rows, cols), blocks[0].dtype)
    r = c = 0
    for b in blocks:
        out = out.at[r:r + b.shape[0], c:c + b.shape[1]].set(b)
        r += b.shape[0]
        c += b.shape[1]
    return out


def _pad_rows(a, rows):
    return jnp.concatenate([a, jnp.zeros((rows - a.shape[0],) + a.shape[1:], a.dtype)], axis=0)


def _mixer(x3, xb2, st, prev_valid, lp, l):
    bsz, t, _ = x3.shape
    m = bsz * t
    x2 = x3.reshape(m, D_MODEL)
    h2 = _proj(xb2, lp['w_main'], l)
    h3 = h2.reshape(bsz, t, MAIN_COLS)

    y_a, s_hgrn = _hgrn(h3, lp['lbp'], lp['gn_a'], l, lp['mask_bd'], st['hgrn'])

    conv0 = jnp.concatenate([jnp.zeros((bsz, 8 - (CONV_W - 1), WIDTH), F32), st['mlstm_conv']], axis=1)
    y_b, mc, mn, mm, conv8 = _mlstm(h3, xb2.reshape(bsz, t, D_MODEL), lp['w_if_t'], lp['b_if_col'],
                                    lp['b_if_row'], lp['conv_w8'], lp['vec_b'], lp['w_qk'], l,
                                    lp['mask_bd'], st['mlstm_c'], st['mlstm_n'].reshape(bsz, 1, WIDTH),
                                    jnp.repeat(st['mlstm_m'], HEAD_DIM, axis=-1)[:, None, :], conv0)
    mn = mn.reshape(bsz, N_HEADS, HEAD_DIM)
    mm = mm[:, :, ::HEAD_DIM]
    conv_state = conv8[:, 8 - (CONV_W - 1):]

    k_off = COL_C + WIDTH
    cache_k = st['swa_k'].reshape(bsz, WINDOW, KV_WIDTH)
    cache_v = st['swa_v'].reshape(bsz, WINDOW, KV_WIDTH)
    y_c = _swa(h3, cache_k, cache_v, lp['bias'], lp['sinks'], l, prev_valid)
    keep = st['keep']

    def window(cache, col):
        new = h3[:, max(t - keep, 0):, col:col + KV_WIDTH]
        if t < keep:
            new = jnp.concatenate([cache[:, WINDOW - (keep - t):], new], axis=1)
        return new.reshape(bsz, keep, C_KV_HEADS, HEAD_DIM)

    k_win = window(cache_k, k_off)
    v_win = window(cache_v, k_off + KV_WIDTH)

    y_d, s_rwkv, shift = _rwkv(h3, lp['mu_d'], lp['vec_d'], lp['w_wa'], lp['g_up_d'], l,
                               lp['mask_bd'], st['rwkv'], _d_to_kernel_order(st['rwkv_shift'])[:, None, :])
    shift_state = _d_to_original_order(shift[:, 0, :])

    ys = [y.reshape(m, WIDTH) for y in (y_a, y_b, y_c, y_d)]
    x1, x1b = _merge(ys, xb2, lp['w_gate'], lp['w_br'], lp['w_o'], l, x2, lp['ln1_g'], lp['ln1_b'])
    new_st = {'swa_k': k_win, 'swa_v': v_win, 'hgrn': s_hgrn, 'mlstm_c': mc, 'mlstm_n': mn,
              'mlstm_m': mm[:, 0, :], 'mlstm_conv': conv_state, 'rwkv': s_rwkv, 'rwkv_shift': shift_state}
    return x1, x1b, new_st


def _trunk(x3, states, prev_valid, keep, lp):
    bsz, t, _ = x3.shape
    m = bsz * t
    xb2 = x3.reshape(m, D_MODEL).astype(BF16)
    collected = {name: [] for name in STATE_NAMES}
    for l in range(DEPTH):
        st = {name: states[name][l] for name in STATE_NAMES}
        st['keep'] = keep
        x1, x1b, new_st = _mixer(x3, xb2, st, prev_valid, lp, l)
        j = l // 2
        ln_g, ln_b = lp['ln2_g'], lp['ln2_b']
        if l % 2 == 0:
            x2, xb2 = _ffn(x1b, x1, lp['ffn_wg'], lp['ffn_wu'], lp['ffn_wd'], j, ln_g, ln_b, l)
        else:
            logits = _matmul(x1b, lp['router_w'][j], tm=1024, tn=N_EXPERTS) + lp['router_b'][j]
            top_v, top_i = lax.top_k(logits, TOP_K)
            probs = jax.nn.softmax(top_v, axis=-1)
            gates = jnp.einsum('mk,mke->me', probs, jax.nn.one_hot(top_i, N_EXPERTS, dtype=F32))
            x2, xb2 = _ffn(x1b, x1, lp['exp_wg'], lp['exp_wu'], lp['exp_wd'], j, ln_g, ln_b, l, gates)
        x3 = x2.reshape(bsz, t, D_MODEL)
        for name in STATE_NAMES:
            collected[name].append(new_st[name])
    return x3, {name: jnp.stack(collected[name]) for name in STATE_NAMES}


def kernel(x_prompt, x_sample, cache_swa_k, cache_swa_v, state_hgrn, state_mlstm_c, state_mlstm_n, state_mlstm_m, state_mlstm_conv, state_rwkv, state_rwkv_shift, w_in, lb_raw, gn_a, conv_w, conv_b, wq_b, wk_b, b_i, b_f, gn_b, skip_b, sinks, rel_bias, mu_d, w0_d, w_up_d, a0_d, a_up_d, g_up_d, k_k_d, k_a_d, r_k_d, gn_w_d, gn_b_d, w_br, w_o, ln1_g, ln1_b, ln2_g, ln2_b, ffn_w_gate, ffn_w_up, ffn_w_down, router_w, router_b, exp_w_gate, exp_w_up, exp_w_down):
    off_if = 4 * WIDTH + 3 * WIDTH
    off_c = off_if + 2 * N_HEADS
    off_d = off_c + WIDTH + 2 * KV_WIDTH
    off_gate = off_d + D_COLS
    w_main = jnp.concatenate([w_in[:, :, :4 * WIDTH], _d_to_kernel_order(w_in[:, :, off_d:off_gate]),
                              w_in[:, :, 4 * WIDTH:off_if], w_in[:, :, off_c:off_d], w_in[:, :, off_if:off_c],
                              jnp.zeros((DEPTH, D_MODEL, LANE_TILE - 2 * N_HEADS), w_in.dtype)], axis=-1).astype(BF16)
    w_gate = w_in[:, :, off_gate:].astype(BF16)

    lb = _lower_bounds(lb_raw)
    lb = lb[jnp.minimum(jnp.arange(DEPTH), lb.shape[0] - 1)]
    lbp = jnp.stack([jnp.log(jnp.maximum(lb, LB_FLOOR)), jnp.log1p(-lb), 1.0 - lb], axis=1)
    lbp = jnp.concatenate([lbp, jnp.zeros((DEPTH, 5, WIDTH), F32)], axis=1)

    w_qk = jnp.stack([jnp.concatenate([_block_diag(list(wq_b[l])), _block_diag(list(wk_b[l])) * HEAD_DIM ** -0.5],
                                      axis=1) for l in range(DEPTH)]).astype(BF16)
    w_wa = jnp.stack([_block_diag([w_up_d[l], a_up_d[l]]) for l in range(DEPTH)]).astype(BF16)
    vec_b = jnp.stack([_pad_rows(jnp.stack([conv_b[l], gn_b[l], skip_b[l]]), 8) for l in range(DEPTH)])
    conv_w8 = jnp.stack([_pad_rows(conv_w[l], 8) for l in range(DEPTH)])
    vec_d = jnp.stack([_pad_rows(jnp.stack([w0_d[l], a0_d[l], k_k_d[l], k_a_d[l], r_k_d[l], gn_w_d[l], gn_b_d[l]]), 8)
                       for l in range(DEPTH)])
    head_of = jnp.arange(WIDTH) // HEAD_DIM
    mask_bd = (head_of[:, None] == head_of[None, :]).astype(F32)

    span = WINDOW + CHUNK
    rel = jnp.arange(span)[None, :] - WINDOW - jnp.arange(CHUNK)[:, None]
    one_hot = (_rel_bucket(rel)[..., None] == jnp.arange(NUM_BUCKETS)).astype(F32)
    bias = jnp.einsum('ijb,bh->hij', one_hot, rel_bias.astype(F32), precision=HIGHEST)

    n_dense = ffn_w_gate.shape[0]
    ff_steps = D_FF // D_FF_EXPERT
    b_if = jnp.concatenate([b_i, b_f], axis=-1)
    lp = {
        'w_main': w_main, 'w_gate': w_gate, 'lbp': lbp, 'gn_a': gn_a[:, None, :],
        'w_if_t': jnp.swapaxes(w_in[:, :, off_if:off_c], 1, 2).astype(BF16),
        'mask_bd': mask_bd, 'conv_w8': conv_w8, 'vec_b': vec_b, 'w_qk': w_qk,
        'b_if_col': b_if[:, :, None], 'b_if_row': b_if[:, None, :], 'sinks': sinks[:, None, :], 'bias': bias,
        'mu_d': _d_to_kernel_order(mu_d)[:, None, :], 'vec_d': vec_d, 'w_wa': w_wa, 'g_up_d': g_up_d.astype(BF16),
        'w_br': w_br.astype(BF16), 'w_o': w_o.astype(BF16),
        'ln1_g': ln1_g[:, None, :], 'ln1_b': ln1_b[:, None, :], 'ln2_g': ln2_g[:, None, :], 'ln2_b': ln2_b[:, None, :],
        'ffn_wg': jnp.swapaxes(ffn_w_gate.astype(BF16).reshape(n_dense, D_MODEL, ff_steps, D_FF_EXPERT), 1, 2),
        'ffn_wu': jnp.swapaxes(ffn_w_up.astype(BF16).reshape(n_dense, D_MODEL, ff_steps, D_FF_EXPERT), 1, 2),
        'ffn_wd': ffn_w_down.astype(BF16).reshape(n_dense, ff_steps, D_FF_EXPERT, D_MODEL),
        'router_w': router_w.astype(BF16), 'router_b': router_b,
        'exp_wg': exp_w_gate.astype(BF16), 'exp_wu': exp_w_up.astype(BF16), 'exp_wd': exp_w_down.astype(BF16),
    }

    sample_states = {
        'swa_k': cache_swa_k, 'swa_v': cache_swa_v, 'hgrn': state_hgrn, 'mlstm_c': state_mlstm_c,
        'mlstm_n': state_mlstm_n, 'mlstm_m': state_mlstm_m, 'mlstm_conv': state_mlstm_conv,
        'rwkv': state_rwkv, 'rwkv_shift': state_rwkv_shift,
    }
    keep = cache_swa_k.shape[2]
    bp = x_prompt.shape[0]
    prompt_states = {}
    for name in STATE_NAMES:
        arr = sample_states[name]
        rows = (WINDOW,) + arr.shape[3:] if name in ('swa_k', 'swa_v') else arr.shape[2:]
        prompt_states[name] = jnp.zeros((DEPTH, bp) + tuple(rows), arr.dtype)

    y_prompt, pst = _trunk(x_prompt, prompt_states, False, keep, lp)
    y_sample, sst = _trunk(x_sample, sample_states, True, keep, lp)
    return (y_prompt, y_sample) + tuple(pst[n] for n in STATE_NAMES) + tuple(sst[n] for n in STATE_NAMES)
```

```python
import functools
import math

import jax
import jax.numpy as jnp
from jax import lax
from jax.experimental import pallas as pl
from jax.experimental.pallas import tpu as pltpu

F32 = jnp.float32
BF16 = jnp.bfloat16
HIGHEST = lax.Precision.HIGHEST

D_MODEL = 1024
DEPTH = 4
CHUNK = 64
HEAD_DIM = 64
N_HEADS = 4
WIDTH = N_HEADS * HEAD_DIM
C_KV_HEADS = 2
KV_WIDTH = C_KV_HEADS * HEAD_DIM
CONV_W = 4
WINDOW = 128
NUM_BUCKETS = 32
MAX_DISTANCE = 128
D_DECAY_LORA = 64
D_AAA_LORA = 64
D_GATE_LORA = 128
D_COLS = 3 * WIDTH + D_DECAY_LORA + D_AAA_LORA + D_GATE_LORA
N_BRANCH = 4
D_FF = 2816
N_EXPERTS = 8
TOP_K = 2
D_FF_EXPERT = 1408
ALPHA = (2 * DEPTH) ** 0.25
LN_EPS = 1e-5
HEAD_NORM_EPS = 1e-5
RWKV_GN_EPS = 64e-5
LB_FLOOR = 1e-30
LB_CEIL = 1.0 - 1e-6

GATE_COLS = N_BRANCH * D_MODEL
COL_A = 0
COL_D = COL_A + 4 * WIDTH
COL_B = COL_D + D_COLS
COL_C = COL_B + 3 * WIDTH
COL_IF = COL_C + WIDTH + 2 * KV_WIDTH
LANE_TILE = 128
MAIN_COLS = COL_IF + LANE_TILE
SUB = 16
STREAMS_PER_STEP = 4
V7X_VMEM_LIMIT = 48 * 1024 * 1024

NN = (((1,), (0,)), ((), ()))
NT = (((1,), (1,)), ((), ()))
TN = (((0,), (0,)), ((), ()))


def _layer_block(arr, layer):
    tail = arr.shape[1:]
    return pl.BlockSpec((None,) + tail, lambda *_: (layer,) + (0,) * len(tail))


def _cparams(*sem):
    return pltpu.CompilerParams(dimension_semantics=sem, vmem_limit_bytes=V7X_VMEM_LIMIT)


def _sigmoid(x):
    return 1.0 / (1.0 + jnp.exp(-x))


def _silu(x):
    return x * _sigmoid(x)


def _log_sigmoid(x):
    return jnp.minimum(x, 0.0) - jnp.log1p(jnp.exp(-jnp.abs(x)))


def _layernorm_rows(z, g, b):
    mu = jnp.mean(z, axis=-1, keepdims=True)
    zc = z - mu
    var = jnp.mean(zc * zc, axis=-1, keepdims=True)
    return zc * lax.rsqrt(var + LN_EPS) * g + b


def _split2(x):
    hi = x.astype(BF16)
    lo = (x - hi.astype(F32)).astype(BF16)
    return hi, lo


def _split3(x):
    hi = x.astype(BF16)
    r1 = x - hi.astype(F32)
    mid = r1.astype(BF16)
    lo = (r1 - mid.astype(F32)).astype(BF16)
    return hi, mid, lo


def _dot(a, b, dims=NN):
    return lax.dot_general(a, b, dims, preferred_element_type=F32)


def _mm2(a, b_parts, dims=NN):
    a_hi, a_lo = _split2(a)
    n = a.shape[0]
    both = _dot(jnp.concatenate([a_hi, a_lo], axis=0), b_parts[0], dims)
    return both[:n] + both[n:] + _dot(a_hi, b_parts[1], dims)


def _cumsum_rows(x):
    n = x.shape[0]
    row = lax.broadcasted_iota(jnp.int32, (n, n), 0)
    col = lax.broadcasted_iota(jnp.int32, (n, n), 1)
    tri = (row >= col).astype(BF16)
    hi, mid, lo = _split3(x)
    return _dot(tri, hi) + _dot(tri, mid) + _dot(tri, lo)


def _head_masks():
    lane_head = lax.broadcasted_iota(jnp.int32, (1, WIDTH), 1) // HEAD_DIM
    return [(lane_head == h) for h in range(N_HEADS)]


def _stack_heads(x, masks):
    return jnp.concatenate([jnp.where(mk, x, 0.0) for mk in masks], axis=0)


def _stack_parts(x, masks):
    parts = _split2(x)
    rows = [mk.astype(BF16) for mk in masks]
    return tuple(jnp.concatenate([p * r for r in rows], axis=0) for p in parts)


def _round_robin(stage_generators):
    live = list(stage_generators)
    while live:
        live = [g for g in live if next(g, _DONE) is not _DONE]


_DONE = object()


def _head_sums(x, ones_bd):
    hi, lo = _split2(x)
    return _dot(hi, ones_bd) + _dot(lo, ones_bd)


def _matmul_kernel(x_ref, w_ref, o_ref):
    o_ref[...] = _dot(x_ref[...], w_ref[...])


def _matmul(x, w, *, tm, tn):
    m, k = x.shape
    n = w.shape[1]
    tm = min(tm, m)
    tn = min(tn, n)
    return pl.pallas_call(
        _matmul_kernel,
        grid=(m // tm, n // tn),
        in_specs=[pl.BlockSpec((tm, k), lambda i, j: (i, 0)),
                  pl.BlockSpec((k, tn), lambda i, j: (0, j))],
        out_specs=pl.BlockSpec((tm, tn), lambda i, j: (i, j)),
        out_shape=jax.ShapeDtypeStruct((m, n), F32),
        compiler_params=_cparams("parallel", "parallel"),
        name="matmul",
    )(x, w)


PROJ_TN = 256


def _proj_kernel(x_ref, w_ref, o_ref):
    x = x_ref[...]
    n = w_ref.shape[2]
    for lo in range(0, n, PROJ_TN):
        hi = min(lo + PROJ_TN, n)
        o_ref[:, lo:hi] = _dot(x, w_ref[0, :, lo:hi])


def _proj(xb, w, layer, *, tm=512):
    m, k = xb.shape
    n = w.shape[2]
    tm = min(tm, m)
    return pl.pallas_call(
        _proj_kernel,
        grid=(m // tm,),
        in_specs=[pl.BlockSpec((tm, k), lambda i: (i, 0)),
                  pl.BlockSpec((1, k, n), lambda i: (layer, 0, 0), pipeline_mode=pl.Buffered(1))],
        out_specs=pl.BlockSpec((tm, n), lambda i: (i, 0)),
        out_shape=jax.ShapeDtypeStruct((m, n), F32),
        compiler_params=_cparams("parallel"),
        name="proj",
    )(xb, w)


def _head(a, h):
    return a[:, h * HEAD_DIM:(h + 1) * HEAD_DIM]


def _hgrn_kernel(q_ref, f_ref, i_ref, g_ref, lbp_ref, gn_ref, ones_ref, mask_ref, s0_ref, y_ref, s_out_ref,
                 st_ref, phi_ref, plo_ref, *, nc, bb):
    c = pl.program_id(1)

    @pl.when(c == 0)
    def _():
        for bi in range(bb):
            st_ref[bi] = jnp.zeros(st_ref.shape[1:], F32)
            for h in range(N_HEADS):
                st_ref[bi, h * HEAD_DIM:(h + 1) * HEAD_DIM, h * HEAD_DIM:(h + 1) * HEAD_DIM] = s0_ref[bi, h].T

    _round_robin([_hgrn_chunk(q_ref.at[bi], f_ref.at[bi], i_ref.at[bi], g_ref.at[bi], lbp_ref, gn_ref, ones_ref,
                              mask_ref, y_ref.at[bi], st_ref.at[bi], phi_ref.at[bi], plo_ref.at[bi])
                  for bi in range(bb)])

    @pl.when(c == nc - 1)
    def _():
        for bi in range(bb):
            for h in range(N_HEADS):
                s_out_ref[bi, h] = st_ref[bi, h * HEAD_DIM:(h + 1) * HEAD_DIM, h * HEAD_DIM:(h + 1) * HEAD_DIM].T


def _hgrn_chunk(q_ref, f_ref, i_ref, g_ref, lbp_ref, gn_ref, ones_ref, mask_ref, y_ref, st_ref, phi_ref, plo_ref):
    L = CHUNK
    ones_bd = ones_ref[...]
    masks = _head_masks()
    log_lb = lbp_ref[0:1, :]
    log1m_lb = lbp_ref[1:2, :]
    one_m_lb = lbp_ref[2:3, :]
    zf = f_ref[...]
    u = log_lb
    w = log1m_lb + _log_sigmoid(zf)
    logf = jnp.maximum(u, w) + jnp.log1p(jnp.exp(-jnp.abs(u - w)))
    k = one_m_lb * _sigmoid(-zf)
    q = _silu(q_ref[...])
    v = i_ref[...]

    b = _cumsum_rows(logf)
    yield
    b_last = b[L - 1:L]
    qe = q * jnp.exp(b)
    kdec = k * jnp.exp(b_last - b)
    sub_row = lax.broadcasted_iota(jnp.int32, (SUB, WIDTH), 0)

    for s in range(L):
        r0 = (s // SUB) * SUB
        p = q[r0:r0 + SUB] * k[s:s + 1] * jnp.exp(jnp.minimum(b[r0:r0 + SUB] - b[s:s + 1], 0.0))
        hi, lo = _split2(jnp.where(sub_row >= s - r0, p, 0.0))
        phi_ref[s * SUB:(s + 1) * SUB, :] = hi
        plo_ref[s * SUB:(s + 1) * SUB, :] = lo
    yield
    att = _dot(phi_ref[...], ones_bd) + _dot(plo_ref[...], ones_bd)
    st = st_ref[...]
    o_inter = _dot(qe, st, NT)
    upd = _dot(v, kdec, TN)
    scores = []
    for blk in range(1, L // SUB):
        r0 = blk * SUB
        ref_row = b[r0 - 1:r0]
        q_s = q[r0:r0 + SUB] * jnp.exp(b[r0:r0 + SUB] - ref_row)
        k_s = k[:r0] * jnp.exp(ref_row - b[:r0])
        scores.append(_dot(q_s, _stack_heads(k_s, masks), NT))
    yield
    st_ref[...] = st * jnp.exp(b_last) + upd * mask_ref[...]
    o_blocks = []
    for blk in range(L // SUB):
        r0 = blk * SUB
        acc = jnp.zeros((SUB, WIDTH), F32)
        for j in range(SUB):
            s = r0 + j
            acc = acc + att[s * SUB:(s + 1) * SUB] * v[s:s + 1]
        if blk > 0:
            acc = acc + _dot(scores[blk - 1], _stack_heads(v[:r0], masks))
        o_blocks.append(acc)
    yield
    o = jnp.concatenate(o_blocks, axis=0) + o_inter
    ms = _head_sums(o * o, ones_bd) * (1.0 / HEAD_DIM)
    yield
    y_ref[...] = (o * lax.rsqrt(ms + HEAD_NORM_EPS) * gn_ref[...] * _silu(g_ref[...])).astype(y_ref.dtype)


def _hgrn(h3, lbp, gn, layer, mask_bd, s0):
    bsz, t, _ = h3.shape
    nc = t // CHUNK
    cb = COL_A // WIDTH
    bb = STREAMS_PER_STEP

    def col(j):
        return pl.BlockSpec((bb, CHUNK, WIDTH), lambda b, c, j=j: (b, c, cb + j))

    square = pl.BlockSpec((WIDTH, WIDTH), lambda b, c: (0, 0))
    state = pl.BlockSpec((bb, N_HEADS, HEAD_DIM, HEAD_DIM), lambda b, c: (b, 0, 0, 0))
    return pl.pallas_call(
        functools.partial(_hgrn_kernel, nc=nc, bb=bb),
        grid=(bsz // bb, nc),
        in_specs=[col(0), col(1), col(2), col(3),
                  _layer_block(lbp, layer), _layer_block(gn, layer),
                  square, square, state],
        out_specs=[pl.BlockSpec((bb, CHUNK, WIDTH), lambda b, c: (b, c, 0)), state],
        out_shape=[jax.ShapeDtypeStruct((bsz, t, WIDTH), BF16),
                   jax.ShapeDtypeStruct((bsz, N_HEADS, HEAD_DIM, HEAD_DIM), F32)],
        scratch_shapes=[pltpu.VMEM((bb, WIDTH, WIDTH), F32),
                        pltpu.VMEM((bb, CHUNK * SUB, WIDTH), BF16),
                        pltpu.VMEM((bb, CHUNK * SUB, WIDTH), BF16)],
        compiler_params=_cparams("parallel", "arbitrary"),
        name="hgrn2",
    )(h3, h3, h3, h3, lbp, gn, mask_bd.astype(BF16), mask_bd, s0)


def _mlstm_kernel(u_ref, v_ref, o_ref, gcol_ref, xin_ref, wift_ref, bcol_ref, brow_ref, cw_ref, vec_ref, wqk_ref,
                  ones_ref, mask_ref, c0_ref, n0_ref, m0_ref, conv0_ref,
                  y_ref, c_out_ref, n_out_ref, m_out_ref, conv_out_ref,
                  c_ref, n_ref, m_ref, carry_ref, *, nc, bb):
    c = pl.program_id(1)

    @pl.when(c == 0)
    def _():
        for bi in range(bb):
            c_ref[bi] = jnp.zeros(c_ref.shape[1:], F32)
            for h in range(N_HEADS):
                c_ref[bi, h * HEAD_DIM:(h + 1) * HEAD_DIM, h * HEAD_DIM:(h + 1) * HEAD_DIM] = c0_ref[bi, h]
        n_ref[...] = n0_ref[...]
        m_ref[...] = m0_ref[...]
        carry_ref[...] = conv0_ref[...]

    _round_robin([_mlstm_chunk(u_ref.at[bi], v_ref.at[bi], o_ref.at[bi], gcol_ref.at[bi], xin_ref.at[bi],
                               wift_ref, bcol_ref, brow_ref, cw_ref, vec_ref, wqk_ref, ones_ref, mask_ref, y_ref.at[bi],
                               c_ref.at[bi], n_ref.at[bi], m_ref.at[bi], carry_ref.at[bi]) for bi in range(bb)])

    @pl.when(c == nc - 1)
    def _():
        for bi in range(bb):
            for h in range(N_HEADS):
                c_out_ref[bi, h] = c_ref[bi, h * HEAD_DIM:(h + 1) * HEAD_DIM, h * HEAD_DIM:(h + 1) * HEAD_DIM]
        n_out_ref[...] = n_ref[...]
        m_out_ref[...] = m_ref[...]
        conv_out_ref[...] = carry_ref[...]


def _mlstm_chunk(u_ref, v_ref, o_ref, gcol_ref, xin_ref, wift_ref, bcol_ref, brow_ref, cw_ref, vec_ref, wqk_ref,
                 ones_ref, mask_ref, y_ref, c_ref, n_ref, m_ref, carry_ref):
    L = CHUNK
    heads = range(N_HEADS)

    u = u_ref[...]
    carry = carry_ref[...]
    row8 = lax.broadcasted_iota(jnp.int32, (8, WIDTH), 0)
    conv = vec_ref[0:1, :] + cw_ref[CONV_W - 1:CONV_W, :] * u
    for d in range(1, CONV_W):
        rolled = pltpu.roll(u, d, 0)
        top = jnp.where(row8 < d, pltpu.roll(carry, d, 0), rolled[0:8])
        conv = conv + cw_ref[CONV_W - 1 - d:CONV_W - d, :] * jnp.concatenate([top, rolled[8:]], axis=0)
    carry_ref[...] = u[L - 8:L]
    cact = _silu(conv)
    qk_all = _dot(cact.astype(BF16), wqk_ref[...])

    row = lax.broadcasted_iota(jnp.int32, (L, L), 0)
    col = lax.broadcasted_iota(jnp.int32, (L, L), 1)
    lower = row >= col
    raw_c = gcol_ref[:, 0:2 * N_HEADS] + brow_ref[...]
    lane8 = lax.broadcasted_iota(jnp.int32, raw_c.shape, 1)
    gcol = jnp.where(lane8 < N_HEADS, raw_c, _log_sigmoid(raw_c))
    raw_r = _dot(wift_ref[...], xin_ref[...], NT) + bcol_ref[...]
    sub8 = lax.broadcasted_iota(jnp.int32, raw_r.shape, 0)
    grow = jnp.where(sub8 < N_HEADS, raw_r, _log_sigmoid(raw_r))
    f_col = jnp.dot(lower.astype(F32), gcol, preferred_element_type=F32, precision=HIGHEST)
    f_row = jnp.dot(grow, (row <= col).astype(F32), preferred_element_type=F32, precision=HIGHEST)
    r8 = lax.broadcasted_iota(jnp.int32, (2 * N_HEADS, 2 * WIDTH), 0)
    c8 = lax.broadcasted_iota(jnp.int32, (2 * N_HEADS, 2 * WIDTH), 1)
    expand = (c8 // HEAD_DIM == r8).astype(BF16)
    x8 = jnp.where(lane8 < N_HEADS, gcol, f_col)
    x_hi, x_mid, x_lo = _split3(x8)
    both = _dot(x_hi, expand) + _dot(x_mid, expand) + _dot(x_lo, expand)
    li_all = both[:, 0:WIDTH]
    f_all = both[:, WIDTH:2 * WIDTH]
    d_row = jnp.concatenate([grow[h:h + 1, :] - f_row[N_HEADS + h:N_HEADS + h + 1, :] for h in heads], axis=1)
    yield
    t_idx = lax.broadcasted_iota(jnp.int32, (L, WIDTH), 0)
    s_idx = lax.broadcasted_iota(jnp.int32, (L, WIDTH), 1) % L
    ones_bd = ones_ref[...]
    masks = _head_masks()
    cm = li_all - f_all
    for sh in (1, 2, 4, 8, 16, 32):
        cm = jnp.maximum(cm, jnp.where(t_idx >= sh, pltpu.roll(cm, sh, 0), -jnp.inf))
    g = f_all + m_ref[...]
    mt = jnp.maximum(g, f_all + cm)
    wg = jnp.exp(g - mt)
    wd = jnp.exp(jnp.where(s_idx <= t_idx, f_all + d_row - mt, -jnp.inf))
    q = qk_all[:, 0:WIDTH]
    k = qk_all[:, WIDTH:2 * WIDTH]
    v = v_ref[...]
    q_bf = q.astype(BF16)
    v_bf = v.astype(BF16)
    head_rows = [mk.astype(BF16) for mk in masks]
    k_stack = jnp.concatenate([k.astype(BF16) * r for r in head_rows], axis=0)
    v_stack = jnp.concatenate([v_bf * r for r in head_rows], axis=0)
    c_bd = c_ref[...]
    n_row = n_ref[...]
    qk = _dot(q_bf, k_stack, NT) * wd
    q_c = _dot(q_bf, c_bd.astype(BF16))
    q_n = _head_sums(q * n_row, ones_bd)
    yield
    qkv = _dot(qk.astype(BF16), v_stack)
    qk_sum = _head_sums(qk, ones_bd)
    mt_last = mt[L - 1:L]
    kw = k * jnp.exp(f_all[L - 1:L] - f_all + li_all - mt_last)
    c_upd = _dot(kw.astype(BF16), v_bf, TN)
    yield
    wgl = wg[L - 1:L]
    c_ref[...] = wgl * c_bd + c_upd * mask_ref[...]
    n_ref[...] = wgl * n_row + jnp.sum(kw, axis=0, keepdims=True)
    m_ref[...] = mt_last
    den = wg * q_n + qk_sum
    hh = (wg * q_c + qkv) / jnp.maximum(jnp.abs(den), jnp.exp(-mt))
    z = _sigmoid(o_ref[...]) * hh
    mu = _head_sums(z, ones_bd) * (1.0 / HEAD_DIM)
    yield
    zc = z - mu
    var = _head_sums(zc * zc, ones_bd) * (1.0 / HEAD_DIM)
    yield
    y_ref[...] = (zc * lax.rsqrt(var + HEAD_NORM_EPS) * vec_ref[1:2, :] + vec_ref[2:3, :] * cact).astype(y_ref.dtype)


def _mlstm(h3, xb3, w_if_t, b_col, b_row, conv_w, vecs, w_qk, layer, mask_bd, c0, n0, m0, conv0):
    bsz, t, _ = h3.shape
    nc = t // CHUNK
    cb = COL_B // WIDTH
    bb = STREAMS_PER_STEP
    state4 = pl.BlockSpec((bb, N_HEADS, HEAD_DIM, HEAD_DIM), lambda b, c: (b, 0, 0, 0))
    state_n = pl.BlockSpec((bb, 1, WIDTH), lambda b, c: (b, 0, 0))
    state_m = state_n
    state_conv = pl.BlockSpec((bb, 8, WIDTH), lambda b, c: (b, 0, 0))

    def const(shape):
        return pl.BlockSpec(shape, lambda b, c: tuple(0 for _ in shape))

    def col(j):
        return pl.BlockSpec((bb, CHUNK, WIDTH), lambda b, c, j=j: (b, c, cb + j))

    return pl.pallas_call(
        functools.partial(_mlstm_kernel, nc=nc, bb=bb),
        grid=(bsz // bb, nc),
        in_specs=[col(0), col(1), col(2),
                  pl.BlockSpec((bb, CHUNK, LANE_TILE), lambda b, c: (b, c, COL_IF // LANE_TILE)),
                  pl.BlockSpec((bb, CHUNK, D_MODEL), lambda b, c: (b, c, 0)), _layer_block(w_if_t, layer),
                  _layer_block(b_col, layer), _layer_block(b_row, layer), _layer_block(conv_w, layer),
                  _layer_block(vecs, layer), _layer_block(w_qk, layer), const((WIDTH, WIDTH)), const((WIDTH, WIDTH)),
                  state4, state_n, state_m, state_conv],
        out_specs=[pl.BlockSpec((bb, CHUNK, WIDTH), lambda b, c: (b, c, 0)), state4, state_n, state_m, state_conv],
        out_shape=[jax.ShapeDtypeStruct((bsz, t, WIDTH), BF16),
                   jax.ShapeDtypeStruct((bsz, N_HEADS, HEAD_DIM, HEAD_DIM), F32),
                   jax.ShapeDtypeStruct((bsz, 1, WIDTH), F32),
                   jax.ShapeDtypeStruct((bsz, 1, WIDTH), F32),
                   jax.ShapeDtypeStruct((bsz, 8, WIDTH), F32)],
        scratch_shapes=[pltpu.VMEM((bb, WIDTH, WIDTH), F32),
                        pltpu.VMEM((bb, 1, WIDTH), F32),
                        pltpu.VMEM((bb, 1, WIDTH), F32),
                        pltpu.VMEM((bb, 8, WIDTH), F32)],
        compiler_params=_cparams("parallel", "arbitrary"),
        name="mlstm",
    )(h3, h3, h3, h3, xb3, w_if_t, b_col, b_row, conv_w, vecs, w_qk, mask_bd.astype(BF16), mask_bd, c0, n0, m0, conv0)


SWA_CHUNKS_PER_STEP = 4


def _swa_kernel(*refs, prev_valid, cps, multi_step):
    if multi_step:
        (q_ref, kc_ref, vc_ref, kp_ref, vp_ref, k_ref, v_ref, bias_ref, sink_ref, y_ref, s_ref, p_ref) = refs
    else:
        (q_ref, kc_ref, vc_ref, k_ref, v_ref, bias_ref, sink_ref, y_ref, s_ref, p_ref) = refs
    g = pl.program_id(1)
    L = CHUNK
    span = WINDOW + L
    k_win = kc_ref[0]
    v_win = vc_ref[0]
    if multi_step:
        k_win = jnp.where(g == 0, k_win, kp_ref[0])
        v_win = jnp.where(g == 0, v_win, vp_ref[0])
    kcat = jnp.concatenate([k_win, k_ref[0]], axis=0).astype(BF16)
    vcat = jnp.concatenate([v_win, v_ref[0]], axis=0).astype(BF16)
    kv_of = [h // (N_HEADS // C_KV_HEADS) for h in range(N_HEADS)]
    for j in range(cps):
        q = q_ref[0, j * L:(j + 1) * L, :].astype(BF16)
        k_j = kcat[j * L:j * L + span]
        for h in range(N_HEADS):
            s_ref[j * N_HEADS + h] = _dot(_head(q, h), _head(k_j, kv_of[h]), NT)
    def softmax_tile(j, h):
        key_pos = lax.broadcasted_iota(jnp.int32, (L, span), 1) + (g * cps + j) * L
        s = s_ref[j * N_HEADS + h] * HEAD_DIM ** -0.5 + bias_ref[h]
        if not prev_valid:
            s = jnp.where(key_pos >= WINDOW, s, -jnp.inf)
        sink = sink_ref[0:1, h:h + 1]
        m = jnp.maximum(jnp.max(s, axis=1, keepdims=True), sink)
        yield
        p = jnp.exp(s - m)
        denom = jnp.sum(p, axis=1, keepdims=True) + jnp.exp(sink - m)
        yield
        p_ref[j * N_HEADS + h] = (p * (1.0 / denom)).astype(BF16)

    _round_robin([softmax_tile(j, h) for j in range(cps) for h in range(N_HEADS)])
    for j in range(cps):
        v_j = vcat[j * L:j * L + span]
        o = [_dot(p_ref[j * N_HEADS + h], _head(v_j, kv_of[h])) for h in range(N_HEADS)]
        y_ref[0, j * L:(j + 1) * L, :] = jnp.concatenate(o, axis=1).astype(y_ref.dtype)


def _swa(h3, cache_k, cache_v, bias, sinks, layer, prev_valid):
    bsz, t, _ = h3.shape
    nc = t // CHUNK
    cps = min(SWA_CHUNKS_PER_STEP, nc)
    rows = cps * CHUNK
    steps = nc // cps
    multi_step = steps > 1
    assert nc % cps == 0 and (not multi_step or rows % WINDOW == 0), (t, cps)
    qb = COL_C // WIDTH
    kb = (COL_C + WIDTH) // KV_WIDTH
    cache = pl.BlockSpec((1, WINDOW, KV_WIDTH), lambda b, g: (b, 0, 0))

    def own(col):
        return pl.BlockSpec((1, rows, KV_WIDTH), lambda b, g, col=col: (b, g, col))

    def before(col):
        return pl.BlockSpec((1, WINDOW, KV_WIDTH),
                            lambda b, g, col=col: (b, jnp.maximum(g * (rows // WINDOW) - 1, 0), col))

    kv_specs = [cache, cache] + ([before(kb), before(kb + 1)] if multi_step else []) + [own(kb), own(kb + 1)]
    kv_args = [cache_k, cache_v] + ([h3, h3] if multi_step else []) + [h3, h3]
    return pl.pallas_call(
        functools.partial(_swa_kernel, prev_valid=prev_valid, cps=cps, multi_step=multi_step),
        grid=(bsz, steps),
        in_specs=[pl.BlockSpec((1, rows, WIDTH), lambda b, g: (b, g, qb))] + kv_specs + [
                  pl.BlockSpec((N_HEADS, CHUNK, WINDOW + CHUNK), lambda b, g: (0, 0, 0)),
                  _layer_block(sinks, layer)],
        out_specs=pl.BlockSpec((1, rows, WIDTH), lambda b, g: (b, g, 0)),
        out_shape=jax.ShapeDtypeStruct((bsz, t, WIDTH), BF16),
        scratch_shapes=[pltpu.VMEM((cps * N_HEADS, CHUNK, WINDOW + CHUNK), F32),
                        pltpu.VMEM((cps * N_HEADS, CHUNK, WINDOW + CHUNK), BF16)],
        compiler_params=_cparams("parallel", "parallel"),
        name="swa",
    )(h3, *kv_args, bias, sinks)


def _rwkv_kernel(pd_ref, mu_ref, vec_ref, wwa_ref, gup_ref, ones_ref, mask_ref, s0_ref, sh0_ref,
                 y_ref, s_out_ref, sh_out_ref, st_ref, carry_ref, *, nc, bb):
    c = pl.program_id(1)

    @pl.when(c == 0)
    def _():
        for bi in range(bb):
            st_ref[bi] = jnp.zeros(st_ref.shape[1:], F32)
            for h in range(N_HEADS):
                st_ref[bi, h * HEAD_DIM:(h + 1) * HEAD_DIM, h * HEAD_DIM:(h + 1) * HEAD_DIM] = s0_ref[bi, h]
            carry_ref[bi, 7:8, :] = sh0_ref[bi]

    _round_robin([_rwkv_chunk(pd_ref.at[bi], mu_ref, vec_ref, wwa_ref, gup_ref, ones_ref, mask_ref,
                              y_ref.at[bi], st_ref.at[bi], carry_ref.at[bi]) for bi in range(bb)])

    @pl.when(c == nc - 1)
    def _():
        for bi in range(bb):
            for h in range(N_HEADS):
                s_out_ref[bi, h] = st_ref[bi, h * HEAD_DIM:(h + 1) * HEAD_DIM, h * HEAD_DIM:(h + 1) * HEAD_DIM]
            sh_out_ref[bi] = carry_ref[bi, 7:8, :]


def _rwkv_chunk(pd_ref, mu_ref, vec_ref, wwa_ref, gup_ref, ones_ref, mask_ref, y_ref, st_ref, carry_ref):
    L = CHUNK
    ones_bd = ones_ref[...]
    masks = _head_masks()

    pd = pd_ref[...]
    rolled = pltpu.roll(pd, 1, 0)
    row8 = lax.broadcasted_iota(jnp.int32, (8, D_COLS), 0)
    top = jnp.where(row8 == 0, carry_ref[7:8, :], rolled[0:8])
    pd_prev = jnp.concatenate([top, rolled[8:]], axis=0)
    carry_ref[...] = pd[L - 8:L]
    mixed = pd + (pd_prev - pd) * mu_ref[...]
    r = mixed[:, 0:WIDTH]
    k_raw = mixed[:, WIDTH:2 * WIDTH]
    v = mixed[:, 2 * WIDTH:3 * WIDTH]
    wa = mixed[:, 3 * WIDTH:3 * WIDTH + 2 * D_DECAY_LORA]
    g_in = mixed[:, 3 * WIDTH + 2 * D_DECAY_LORA:D_COLS]
    lane_wa = lax.broadcasted_iota(jnp.int32, wa.shape, 1)
    wa_act = jnp.where(lane_wa < D_DECAY_LORA, jnp.tanh(wa), wa)
    lora = _dot(wa_act.astype(BF16), wwa_ref[...])
    g_d = _dot(_sigmoid(g_in).astype(BF16), gup_ref[...])
    yield
    z = -(vec_ref[0:1, :] + lora[:, 0:WIDTH])
    w_log = -(jnp.maximum(z, 0.0) + jnp.log1p(jnp.exp(-jnp.abs(z)))) - 0.5
    lw = -jnp.exp(w_log)
    a_d = _sigmoid(vec_ref[1:2, :] + lora[:, WIDTH:2 * WIDTH])
    kk = k_raw * vec_ref[2:3, :]
    kk_norm = _head_sums(kk * kk, ones_bd)
    cw = _cumsum_rows(lw)
    yield
    kk = kk * lax.rsqrt(jnp.maximum(kk_norm, 1e-24))
    k = k_raw * (1.0 + (a_d - 1.0) * vec_ref[3:4, :])
    a = -kk
    b = kk * a_d
    g_in_c = jnp.exp(cw)
    g_inv = jnp.exp(-cw)
    at = a * jnp.exp(cw - lw)
    rt = r * g_in_c
    bt = b * g_inv
    kt = k * g_inv
    x = jnp.concatenate([at, rt], axis=0)
    bt_st = _stack_parts(bt, masks)
    kt_st = _stack_parts(kt, masks)
    gram_b = _mm2(x, bt_st, NT)
    gram_k = _mm2(x, kt_st, NT)
    st = st_ref[...]
    u = _mm2(x, _split2(st), NT)
    yield

    t_idx = lax.broadcasted_iota(jnp.int32, (L, WIDTH), 0)
    i_idx = lax.broadcasted_iota(jnp.int32, (L, WIDTH), 1) % L
    strict = i_idx < t_idx
    incl = i_idx <= t_idx
    same_blk = (i_idx // SUB) == (t_idx // SUB)
    eye = (i_idx == t_idx).astype(F32)
    n_all = jnp.where(strict, gram_b[0:L], 0.0)
    m_all = jnp.where(strict, gram_k[0:L], 0.0)
    rb_all = jnp.where(incl, gram_b[L:2 * L], 0.0)
    rk_all = jnp.where(incl, gram_k[L:2 * L], 0.0)
    n_d = jnp.where(same_blk, n_all, 0.0)
    n_off = jnp.where(same_blk, 0.0, n_all)

    v_st = _stack_parts(v, masks)
    x2 = _mm2(n_d, _stack_parts(n_d, masks))
    mv_rkv = _mm2(jnp.concatenate([m_all, rk_all], axis=0), v_st)
    rhs = u[0:L] + mv_rkv[0:L]
    bonus = _head_sums(r * k * vec_ref[4:5, :], ones_bd) * v
    yield
    t_d = eye + n_d
    both = _mm2(jnp.concatenate([t_d, x2], axis=0), _stack_parts(x2, masks))
    t_d = t_d + both[0:L]
    x4 = both[L:2 * L]
    yield
    both = _mm2(jnp.concatenate([t_d, x4], axis=0), _stack_parts(x4, masks))
    t_d = t_d + both[0:L]
    x8 = both[L:2 * L]
    yield
    t_d = t_d + _mm2(t_d, _stack_parts(x8, masks))
    yield
    n1 = _mm2(t_d, _stack_parts(n_off, masks))
    yield
    n1_st = _stack_parts(n1, masks)
    n2 = _mm2(n1, n1_st)
    yield
    a2 = eye + n1 + n2 + _mm2(n2, n1_st)
    yield
    t_full = _mm2(a2, _stack_parts(t_d, masks))
    yield
    sa = _mm2(t_full, _stack_parts(rhs, masks))
    yield
    y = u[L:2 * L] + _mm2(rb_all, _stack_parts(sa, masks)) + mv_rkv[L:2 * L]
    sv_hi, sv_lo = _split2(jnp.concatenate([sa, v], axis=0))
    bk_hi, bk_lo = _split2(jnp.concatenate([bt, kt], axis=0))
    upd = _dot(sv_hi, bk_hi, TN) + _dot(sv_lo, bk_hi, TN) + _dot(sv_hi, bk_lo, TN)
    yield
    st_ref[...] = (st + upd * mask_ref[...]) * g_in_c[L - 1:L]

    mu = _head_sums(y, ones_bd) * (1.0 / HEAD_DIM)
    yield
    yc = y - mu
    var = _head_sums(yc * yc, ones_bd) * (1.0 / HEAD_DIM)
    yield
    o = yc * lax.rsqrt(var + RWKV_GN_EPS) * vec_ref[5:6, :] + vec_ref[6:7, :]
    y_ref[...] = ((o + bonus) * g_d).astype(y_ref.dtype)


def _rwkv(h3, mu, vecs, w_wa, g_up, layer, mask_bd, s0, shift0):
    bsz, t, _ = h3.shape
    nc = t // CHUNK
    bb = STREAMS_PER_STEP
    state = pl.BlockSpec((bb, N_HEADS, HEAD_DIM, HEAD_DIM), lambda g, i: (g, 0, 0, 0))
    shift = pl.BlockSpec((bb, 1, D_COLS), lambda g, i: (g, 0, 0))

    def const(shape):
        return pl.BlockSpec(shape, lambda g, i: tuple(0 for _ in shape))

    return pl.pallas_call(
        functools.partial(_rwkv_kernel, nc=nc, bb=bb),
        grid=(bsz // bb, nc),
        in_specs=[pl.BlockSpec((bb, CHUNK, D_COLS), lambda g, i: (g, i, COL_D // D_COLS)),
                  _layer_block(mu, layer), _layer_block(vecs, layer), _layer_block(w_wa, layer),
                  _layer_block(g_up, layer), const((WIDTH, WIDTH)), const((WIDTH, WIDTH)), state, shift],
        out_specs=[pl.BlockSpec((bb, CHUNK, WIDTH), lambda g, i: (g, i, 0)), state, shift],
        out_shape=[jax.ShapeDtypeStruct((bsz, t, WIDTH), BF16),
                   jax.ShapeDtypeStruct((bsz, N_HEADS, HEAD_DIM, HEAD_DIM), F32),
                   jax.ShapeDtypeStruct((bsz, 1, D_COLS), F32)],
        scratch_shapes=[pltpu.VMEM((bb, WIDTH, WIDTH), F32), pltpu.VMEM((bb, 8, D_COLS), F32)],
        compiler_params=_cparams("parallel", "arbitrary"),
        name="rwkv7",
    )(h3, mu, vecs, w_wa, g_up, mask_bd.astype(BF16), mask_bd, s0, shift0)


def _merge_kernel(ya_ref, yb_ref, yc_ref, yd_ref, xb_ref, wg_ref, wbr_ref, wo_ref, x_ref,
                  lng_ref, lnb_ref, xo_ref, xob_ref):
    ys = (ya_ref, yb_ref, yc_ref, yd_ref)
    xb = xb_ref[...]
    merged = None
    for n in range(N_BRANCH):
        gate = _dot(xb, wg_ref[0, :, n * D_MODEL:(n + 1) * D_MODEL])
        term = _sigmoid(gate) * _dot(ys[n][...], wbr_ref[0, n])
        merged = term if merged is None else merged + term
    out = _dot(merged.astype(BF16), wo_ref[0])
    xn = _layernorm_rows(ALPHA * x_ref[...] + out, lng_ref[...], lnb_ref[...])
    xo_ref[...] = xn
    xob_ref[...] = xn.astype(BF16)


def _merge(ys, xb2, w_gate, wbr, wo, layer, x2, lng, lnb, *, tm=512):
    m = x2.shape[0]
    tm = min(tm, m)
    ysp = pl.BlockSpec((tm, WIDTH), lambda i: (i, 0))
    row = pl.BlockSpec((tm, D_MODEL), lambda i: (i, 0))
    vec = _layer_block(lng, layer)
    resident = dict(pipeline_mode=pl.Buffered(1))
    return pl.pallas_call(
        _merge_kernel,
        grid=(m // tm,),
        in_specs=[ysp, ysp, ysp, ysp, row,
                  pl.BlockSpec((1, D_MODEL, GATE_COLS), lambda i: (layer, 0, 0), **resident),
                  pl.BlockSpec((1, N_BRANCH, WIDTH, D_MODEL), lambda i: (layer, 0, 0, 0), **resident),
                  pl.BlockSpec((1, D_MODEL, D_MODEL), lambda i: (layer, 0, 0), **resident),
                  row, vec, vec],
        out_specs=[row, row],
        out_shape=[jax.ShapeDtypeStruct((m, D_MODEL), F32), jax.ShapeDtypeStruct((m, D_MODEL), BF16)],
        compiler_params=_cparams("parallel"),
        name="merge",
    )(*ys, xb2, w_gate, wbr, wo, x2, lng, lnb)


def _ffn_kernel(*refs, n_steps, gated):
    if gated:
        xb_ref, x_ref, gates_ref, wg_ref, wu_ref, wd_ref, lng_ref, lnb_ref, xo_ref, xob_ref, acc_ref = refs
    else:
        xb_ref, x_ref, wg_ref, wu_ref, wd_ref, lng_ref, lnb_ref, xo_ref, xob_ref, acc_ref = refs
    j = pl.program_id(1)

    @pl.when(j == 0)
    def _():
        acc_ref[...] = jnp.zeros_like(acc_ref)

    xb = xb_ref[...]
    hg = _dot(xb, wg_ref[0, 0])
    hu = _dot(xb, wu_ref[0, 0])
    part = _dot((_silu(hg) * hu).astype(BF16), wd_ref[0, 0])
    if gated:
        gates = gates_ref[...]
        lane = lax.broadcasted_iota(jnp.int32, gates.shape, 1)
        part = jnp.sum(jnp.where(lane == j, gates, 0.0), axis=1, keepdims=True) * part
    acc_ref[...] += part

    @pl.when(j == n_steps - 1)
    def _():
        xn = _layernorm_rows(ALPHA * x_ref[...] + acc_ref[...], lng_ref[...], lnb_ref[...])
        xo_ref[...] = xn
        xob_ref[...] = xn.astype(BF16)


def _ffn(xb, x2, wg, wu, wd, layer, lng, lnb, ln_layer, gates=None, *, tm=512):
    m = x2.shape[0]
    tm = min(tm, m)
    _, n_steps, _, tf = wg.shape
    row = pl.BlockSpec((tm, D_MODEL), lambda i, j: (i, 0))
    vec = _layer_block(lng, ln_layer)
    w_in_spec = pl.BlockSpec((1, 1, D_MODEL, tf), lambda i, j: (layer, j, 0, 0))
    w_out_spec = pl.BlockSpec((1, 1, tf, D_MODEL), lambda i, j: (layer, j, 0, 0))
    in_specs = [row, row]
    args = [xb, x2]
    if gates is not None:
        in_specs.append(pl.BlockSpec((tm, N_EXPERTS), lambda i, j: (i, 0)))
        args.append(gates)
    in_specs += [w_in_spec, w_in_spec, w_out_spec, vec, vec]
    args += [wg, wu, wd, lng, lnb]
    return pl.pallas_call(
        functools.partial(_ffn_kernel, n_steps=n_steps, gated=gates is not None),
        grid=(m // tm, n_steps),
        in_specs=in_specs,
        out_specs=[row, row],
        out_shape=[jax.ShapeDtypeStruct((m, D_MODEL), F32), jax.ShapeDtypeStruct((m, D_MODEL), BF16)],
        scratch_shapes=[pltpu.VMEM((tm, D_MODEL), F32)],
        compiler_params=_cparams("parallel", "arbitrary"),
        name="moe" if gates is not None else "ffn",
    )(*args)


STATE_NAMES = ('swa_k', 'swa_v', 'hgrn', 'mlstm_c', 'mlstm_n', 'mlstm_m', 'mlstm_conv', 'rwkv', 'rwkv_shift')

_D_ORIG = (('r', WIDTH), ('w', D_DECAY_LORA), ('k', WIDTH), ('v', WIDTH), ('a', D_AAA_LORA), ('g', D_GATE_LORA))
_D_KERNEL = ('r', 'k', 'v', 'w', 'a', 'g')


def _d_pieces(arr):
    out, off = {}, 0
    for name, size in _D_ORIG:
        out[name] = arr[..., off:off + size]
        off += size
    return out


def _d_to_kernel_order(arr):
    p = _d_pieces(arr)
    return jnp.concatenate([p[n] for n in _D_KERNEL], axis=-1)


def _d_to_original_order(arr):
    sizes = dict(_D_ORIG)
    p, off = {}, 0
    for name in _D_KERNEL:
        p[name] = arr[..., off:off + sizes[name]]
        off += sizes[name]
    return jnp.concatenate([p[n] for n, _ in _D_ORIG], axis=-1)


def _rel_bucket(rel):
    half = NUM_BUCKETS // 2
    exact = half // 2
    dist = jnp.abs(rel)
    far = exact + (jnp.log(jnp.maximum(dist, 1).astype(F32) / exact)
                   / math.log(MAX_DISTANCE / exact) * (half - exact)).astype(jnp.int32)
    far = jnp.minimum(far, half - 1)
    return jnp.where(rel > 0, half, 0) + jnp.where(dist < exact, dist, far)


def _lower_bounds(lb_raw):
    sm = jax.nn.softmax(lb_raw.astype(F32), axis=0)
    lb = jnp.concatenate([jnp.zeros_like(sm[:1]), jnp.cumsum(sm[1:], axis=0)[:-1]], axis=0)
    return jnp.clip(lb, 0.0, LB_CEIL)


def _block_diag(blocks):
    rows = sum(b.shape[0] for b in blocks)
    cols = sum(b.shape[1] for b in blocks)
    out = jnp.zeros((rows, cols), blocks[0].dtype)
    r = c = 0
    for b in blocks:
        out = out.at[r:r + b.shape[0], c:c + b.shape[1]].set(b)
        r += b.shape[0]
        c += b.shape[1]
    return out


def _pad_rows(a, rows):
    return jnp.concatenate([a, jnp.zeros((rows - a.shape[0],) + a.shape[1:], a.dtype)], axis=0)


def _mixer(x3, xb2, st, prev_valid, lp, l):
    bsz, t, _ = x3.shape
    m = bsz * t
    x2 = x3.reshape(m, D_MODEL)
    h2 = _proj(xb2, lp['w_main'], l)
    h3 = h2.reshape(bsz, t, MAIN_COLS)

    y_a, s_hgrn = _hgrn(h3, lp['lbp'], lp['gn_a'], l, lp['mask_bd'], st['hgrn'])

    conv0 = jnp.concatenate([jnp.zeros((bsz, 8 - (CONV_W - 1), WIDTH), F32), st['mlstm_conv']], axis=1)
    y_b, mc, mn, mm, conv8 = _mlstm(h3, xb2.reshape(bsz, t, D_MODEL), lp['w_if_t'], lp['b_if_col'],
                                    lp['b_if_row'], lp['conv_w8'], lp['vec_b'], lp['w_qk'], l,
                                    lp['mask_bd'], st['mlstm_c'], st['mlstm_n'].reshape(bsz, 1, WIDTH),
                                    jnp.repeat(st['mlstm_m'], HEAD_DIM, axis=-1)[:, None, :], conv0)
    mn = mn.reshape(bsz, N_HEADS, HEAD_DIM)
    mm = mm[:, :, ::HEAD_DIM]
    conv_state = conv8[:, 8 - (CONV_W - 1):]

    k_off = COL_C + WIDTH
    cache_k = st['swa_k'].reshape(bsz, WINDOW, KV_WIDTH)
    cache_v = st['swa_v'].reshape(bsz, WINDOW, KV_WIDTH)
    y_c = _swa(h3, cache_k, cache_v, lp['bias'], lp['sinks'], l, prev_valid)
    keep = st['keep']

    def window(cache, col):
        new = h3[:, max(t - keep, 0):, col:col + KV_WIDTH]
        if t < keep:
            new = jnp.concatenate([cache[:, WINDOW - (keep - t):], new], axis=1)
        return new.reshape(bsz, keep, C_KV_HEADS, HEAD_DIM)

    k_win = window(cache_k, k_off)
    v_win = window(cache_v, k_off + KV_WIDTH)

    y_d, s_rwkv, shift = _rwkv(h3, lp['mu_d'], lp['vec_d'], lp['w_wa'], lp['g_up_d'], l,
                               lp['mask_bd'], st['rwkv'], _d_to_kernel_order(st['rwkv_shift'])[:, None, :])
    shift_state = _d_to_original_order(shift[:, 0, :])

    ys = [y.reshape(m, WIDTH) for y in (y_a, y_b, y_c, y_d)]
    x1, x1b = _merge(ys, xb2, lp['w_gate'], lp['w_br'], lp['w_o'], l, x2, lp['ln1_g'], lp['ln1_b'])
    new_st = {'swa_k': k_win, 'swa_v': v_win, 'hgrn': s_hgrn, 'mlstm_c': mc, 'mlstm_n': mn,
              'mlstm_m': mm[:, 0, :], 'mlstm_conv': conv_state, 'rwkv': s_rwkv, 'rwkv_shift': shift_state}
    return x1, x1b, new_st


def _trunk(x3, states, prev_valid, keep, lp):
    bsz, t, _ = x3.shape
    m = bsz * t
    xb2 = x3.reshape(m, D_MODEL).astype(BF16)
    collected = {name: [] for name in STATE_NAMES}
    for l in range(DEPTH):
        st = {name: states[name][l] for name in STATE_NAMES}
        st['keep'] = keep
        x1, x1b, new_st = _mixer(x3, xb2, st, prev_valid, lp, l)
        j = l // 2
        ln_g, ln_b = lp['ln2_g'], lp['ln2_b']
        if l % 2 == 0:
            x2, xb2 = _ffn(x1b, x1, lp['ffn_wg'], lp['ffn_wu'], lp['ffn_wd'], j, ln_g, ln_b, l)
        else:
            logits = _matmul(x1b, lp['router_w'][j], tm=1024, tn=N_EXPERTS) + lp['router_b'][j]
            top_v, top_i = lax.top_k(logits, TOP_K)
            probs = jax.nn.softmax(top_v, axis=-1)
            gates = jnp.einsum('mk,mke->me', probs, jax.nn.one_hot(top_i, N_EXPERTS, dtype=F32))
            x2, xb2 = _ffn(x1b, x1, lp['exp_wg'], lp['exp_wu'], lp['exp_wd'], j, ln_g, ln_b, l, gates)
        x3 = x2.reshape(bsz, t, D_MODEL)
        for name in STATE_NAMES:
            collected[name].append(new_st[name])
    return x3, {name: jnp.stack(collected[name]) for name in STATE_NAMES}


def kernel(x_prompt, x_sample, cache_swa_k, cache_swa_v, state_hgrn, state_mlstm_c, state_mlstm_n, state_mlstm_m, state_mlstm_conv, state_rwkv, state_rwkv_shift, w_in, lb_raw, gn_a, conv_w, conv_b, wq_b, wk_b, b_i, b_f, gn_b, skip_b, sinks, rel_bias, mu_d, w0_d, w_up_d, a0_d, a_up_d, g_up_d, k_k_d, k_a_d, r_k_d, gn_w_d, gn_b_d, w_br, w_o, ln1_g, ln1_b, ln2_g, ln2_b, ffn_w_gate, ffn_w_up, ffn_w_down, router_w, router_b, exp_w_gate, exp_w_up, exp_w_down):
    off_if = 4 * WIDTH + 3 * WIDTH
    off_c = off_if + 2 * N_HEADS
    off_d = off_c + WIDTH + 2 * KV_WIDTH
    off_gate = off_d + D_COLS
    w_main = jnp.concatenate([w_in[:, :, :4 * WIDTH], _d_to_kernel_order(w_in[:, :, off_d:off_gate]),
                              w_in[:, :, 4 * WIDTH:off_if], w_in[:, :, off_c:off_d], w_in[:, :, off_if:off_c],
                              jnp.zeros((DEPTH, D_MODEL, LANE_TILE - 2 * N_HEADS), w_in.dtype)], axis=-1).astype(BF16)
    w_gate = w_in[:, :, off_gate:].astype(BF16)

    lb = _lower_bounds(lb_raw)
    lb = lb[jnp.minimum(jnp.arange(DEPTH), lb.shape[0] - 1)]
    lbp = jnp.stack([jnp.log(jnp.maximum(lb, LB_FLOOR)), jnp.log1p(-lb), 1.0 - lb], axis=1)
    lbp = jnp.concatenate([lbp, jnp.zeros((DEPTH, 5, WIDTH), F32)], axis=1)

    w_qk = jnp.stack([jnp.concatenate([_block_diag(list(wq_b[l])), _block_diag(list(wk_b[l])) * HEAD_DIM ** -0.5],
                                      axis=1) for l in range(DEPTH)]).astype(BF16)
    w_wa = jnp.stack([_block_diag([w_up_d[l], a_up_d[l]]) for l in range(DEPTH)]).astype(BF16)
    vec_b = jnp.stack([_pad_rows(jnp.stack([conv_b[l], gn_b[l], skip_b[l]]), 8) for l in range(DEPTH)])
    conv_w8 = jnp.stack([_pad_rows(conv_w[l], 8) for l in range(DEPTH)])
    vec_d = jnp.stack([_pad_rows(jnp.stack([w0_d[l], a0_d[l], k_k_d[l], k_a_d[l], r_k_d[l], gn_w_d[l], gn_b_d[l]]), 8)
                       for l in range(DEPTH)])
    head_of = jnp.arange(WIDTH) // HEAD_DIM
    mask_bd = (head_of[:, None] == head_of[None, :]).astype(F32)

    span = WINDOW + CHUNK
    rel = jnp.arange(span)[None, :] - WINDOW - jnp.arange(CHUNK)[:, None]
    one_hot = (_rel_bucket(rel)[..., None] == jnp.arange(NUM_BUCKETS)).astype(F32)
    bias = jnp.einsum('ijb,bh->hij', one_hot, rel_bias.astype(F32), precision=HIGHEST)

    n_dense = ffn_w_gate.shape[0]
    ff_steps = D_FF // D_FF_EXPERT
    b_if = jnp.concatenate([b_i, b_f], axis=-1)
    lp = {
        'w_main': w_main, 'w_gate': w_gate, 'lbp': lbp, 'gn_a': gn_a[:, None, :],
        'w_if_t': jnp.swapaxes(w_in[:, :, off_if:off_c], 1, 2).astype(BF16),
        'mask_bd': mask_bd, 'conv_w8': conv_w8, 'vec_b': vec_b, 'w_qk': w_qk,
        'b_if_col': b_if[:, :, None], 'b_if_row': b_if[:, None, :], 'sinks': sinks[:, None, :], 'bias': bias,
        'mu_d': _d_to_kernel_order(mu_d)[:, None, :], 'vec_d': vec_d, 'w_wa': w_wa, 'g_up_d': g_up_d.astype(BF16),
        'w_br': w_br.astype(BF16), 'w_o': w_o.astype(BF16),
        'ln1_g': ln1_g[:, None, :], 'ln1_b': ln1_b[:, None, :], 'ln2_g': ln2_g[:, None, :], 'ln2_b': ln2_b[:, None, :],
        'ffn_wg': jnp.swapaxes(ffn_w_gate.astype(BF16).reshape(n_dense, D_MODEL, ff_steps, D_FF_EXPERT), 1, 2),
        'ffn_wu': jnp.swapaxes(ffn_w_up.astype(BF16).reshape(n_dense, D_MODEL, ff_steps, D_FF_EXPERT), 1, 2),
        'ffn_wd': ffn_w_down.astype(BF16).reshape(n_dense, ff_steps, D_FF_EXPERT, D_MODEL),
        'router_w': router_w.astype(BF16), 'router_b': router_b,
        'exp_wg': exp_w_gate.astype(BF16), 'exp_wu': exp_w_up.astype(BF16), 'exp_wd': exp_w_down.astype(BF16),
    }

    sample_states = {
        'swa_k': cache_swa_k, 'swa_v': cache_swa_v, 'hgrn': state_hgrn, 'mlstm_c': state_mlstm_c,
        'mlstm_n': state_mlstm_n, 'mlstm_m': state_mlstm_m, 'mlstm_conv': state_mlstm_conv,
        'rwkv': state_rwkv, 'rwkv_shift': state_rwkv_shift,
    }
    keep = cache_swa_k.shape[2]
    bp = x_prompt.shape[0]
    prompt_states = {}
    for name in STATE_NAMES:
        arr = sample_states[name]
        rows = (WINDOW,) + arr.shape[3:] if name in ('swa_k', 'swa_v') else arr.shape[2:]
        prompt_states[name] = jnp.zeros((DEPTH, bp) + tuple(rows), arr.dtype)

    y_prompt, pst = _trunk(x_prompt, prompt_states, False, keep, lp)
    y_sample, sst = _trunk(x_sample, sample_states, True, keep, lp)
    return (y_prompt, y_sample) + tuple(pst[n] for n in STATE_NAMES) + tuple(sst[n] for n in STATE_NAMES)
```

```python
import functools
import math

import jax
import jax.numpy as jnp
from jax import lax
from jax.experimental import pallas as pl
from jax.experimental.pallas import tpu as pltpu

F32 = jnp.float32
BF16 = jnp.bfloat16
HIGHEST = lax.Precision.HIGHEST

D_MODEL = 1024
DEPTH = 4
CHUNK = 64
HEAD_DIM = 64
N_HEADS = 4
WIDTH = N_HEADS * HEAD_DIM
C_KV_HEADS = 2
KV_WIDTH = C_KV_HEADS * HEAD_DIM
CONV_W = 4
WINDOW = 128
NUM_BUCKETS = 32
MAX_DISTANCE = 128
D_DECAY_LORA = 64
D_AAA_LORA = 64
D_GATE_LORA = 128
D_COLS = 3 * WIDTH + D_DECAY_LORA + D_AAA_LORA + D_GATE_LORA
N_BRANCH = 4
D_FF = 2816
N_EXPERTS = 8
TOP_K = 2
D_FF_EXPERT = 1408
ALPHA = (2 * DEPTH) ** 0.25
LN_EPS = 1e-5
HEAD_NORM_EPS = 1e-5
RWKV_GN_EPS = 64e-5
LB_FLOOR = 1e-30
LB_CEIL = 1.0 - 1e-6

GATE_COLS = N_BRANCH * D_MODEL
COL_A = 0
COL_D = COL_A + 4 * WIDTH
COL_B = COL_D + D_COLS
COL_C = COL_B + 3 * WIDTH
COL_IF = COL_C + WIDTH + 2 * KV_WIDTH
LANE_TILE = 128
MAIN_COLS = COL_IF + LANE_TILE
SUB = 16
STREAMS_PER_STEP = 4
V7X_VMEM_LIMIT = 48 * 1024 * 1024

NN = (((1,), (0,)), ((), ()))
NT = (((1,), (1,)), ((), ()))
TN = (((0,), (0,)), ((), ()))


def _layer_block(arr, layer):
    tail = arr.shape[1:]
    return pl.BlockSpec((None,) + tail, lambda *_: (layer,) + (0,) * len(tail))


def _cparams(*sem):
    return pltpu.CompilerParams(dimension_semantics=sem, vmem_limit_bytes=V7X_VMEM_LIMIT)


def _sigmoid(x):
    return 1.0 / (1.0 + jnp.exp(-x))


def _silu(x):
    return x * _sigmoid(x)


def _log_sigmoid(x):
    return jnp.minimum(x, 0.0) - jnp.log1p(jnp.exp(-jnp.abs(x)))


def _layernorm_rows(z, g, b):
    mu = jnp.mean(z, axis=-1, keepdims=True)
    zc = z - mu
    var = jnp.mean(zc * zc, axis=-1, keepdims=True)
    return zc * lax.rsqrt(var + LN_EPS) * g + b


def _split2(x):
    hi = x.astype(BF16)
    lo = (x - hi.astype(F32)).astype(BF16)
    return hi, lo


def _split3(x):
    hi = x.astype(BF16)
    r1 = x - hi.astype(F32)
    mid = r1.astype(BF16)
    lo = (r1 - mid.astype(F32)).astype(BF16)
    return hi, mid, lo


def _dot(a, b, dims=NN):
    return lax.dot_general(a, b, dims, preferred_element_type=F32)


def _mm2(a, b_parts, dims=NN):
    a_hi, a_lo = _split2(a)
    n = a.shape[0]
    both = _dot(jnp.concatenate([a_hi, a_lo], axis=0), b_parts[0], dims)
    return both[:n] + both[n:] + _dot(a_hi, b_parts[1], dims)


def _cumsum_rows(x):
    n = x.shape[0]
    row = lax.broadcasted_iota(jnp.int32, (n, n), 0)
    col = lax.broadcasted_iota(jnp.int32, (n, n), 1)
    tri = (row >= col).astype(BF16)
    hi, mid, lo = _split3(x)
    return _dot(tri, hi) + _dot(tri, mid) + _dot(tri, lo)


def _head_masks():
    lane_head = lax.broadcasted_iota(jnp.int32, (1, WIDTH), 1) // HEAD_DIM
    return [(lane_head == h) for h in range(N_HEADS)]


def _stack_heads(x, masks):
    return jnp.concatenate([jnp.where(mk, x, 0.0) for mk in masks], axis=0)


def _stack_parts(x, masks):
    parts = _split2(x)
    rows = [mk.astype(BF16) for mk in masks]
    return tuple(jnp.concatenate([p * r for r in rows], axis=0) for p in parts)


def _round_robin(stage_generators):
    live = list(stage_generators)
    while live:
        live = [g for g in live if next(g, _DONE) is not _DONE]


_DONE = object()


def _head_sums(x, ones_bd):
    hi, lo = _split2(x)
    return _dot(hi, ones_bd) + _dot(lo, ones_bd)


def _matmul_kernel(x_ref, w_ref, o_ref):
    o_ref[...] = _dot(x_ref[...], w_ref[...])


def _matmul(x, w, *, tm, tn):
    m, k = x.shape
    n = w.shape[1]
    tm = min(tm, m)
    tn = min(tn, n)
    return pl.pallas_call(
        _matmul_kernel,
        grid=(m // tm, n // tn),
        in_specs=[pl.BlockSpec((tm, k), lambda i, j: (i, 0)),
                  pl.BlockSpec((k, tn), lambda i, j: (0, j))],
        out_specs=pl.BlockSpec((tm, tn), lambda i, j: (i, j)),
        out_shape=jax.ShapeDtypeStruct((m, n), F32),
        compiler_params=_cparams("parallel", "parallel"),
        name="matmul",
    )(x, w)


PROJ_TN = 256


def _proj_kernel(x_ref, w_ref, o_ref):
    x = x_ref[...]
    n = w_ref.shape[2]
    for lo in range(0, n, PROJ_TN):
        hi = min(lo + PROJ_TN, n)
        o_ref[:, lo:hi] = _dot(x, w_ref[0, :, lo:hi])


def _proj(xb, w, layer, *, tm=512):
    m, k = xb.shape
    n = w.shape[2]
    tm = min(tm, m)
    return pl.pallas_call(
        _proj_kernel,
        grid=(m // tm,),
        in_specs=[pl.BlockSpec((tm, k), lambda i: (i, 0)),
                  pl.BlockSpec((1, k, n), lambda i: (layer, 0, 0), pipeline_mode=pl.Buffered(1))],
        out_specs=pl.BlockSpec((tm, n), lambda i: (i, 0)),
        out_shape=jax.ShapeDtypeStruct((m, n), F32),
        compiler_params=_cparams("parallel"),
        name="proj",
    )(xb, w)


def _head(a, h):
    return a[:, h * HEAD_DIM:(h + 1) * HEAD_DIM]


def _hgrn_kernel(q_ref, f_ref, i_ref, g_ref, lbp_ref, gn_ref, ones_ref, mask_ref, s0_ref, y_ref, s_out_ref,
                 st_ref, phi_ref, plo_ref, *, nc, bb):
    c = pl.program_id(1)

    @pl.when(c == 0)
    def _():
        for bi in range(bb):
            st_ref[bi] = jnp.zeros(st_ref.shape[1:], F32)
            for h in range(N_HEADS):
                st_ref[bi, h * HEAD_DIM:(h + 1) * HEAD_DIM, h * HEAD_DIM:(h + 1) * HEAD_DIM] = s0_ref[bi, h].T

    _round_robin([_hgrn_chunk(q_ref.at[bi], f_ref.at[bi], i_ref.at[bi], g_ref.at[bi], lbp_ref, gn_ref, ones_ref,
                              mask_ref, y_ref.at[bi], st_ref.at[bi], phi_ref.at[bi], plo_ref.at[bi])
                  for bi in range(bb)])

    @pl.when(c == nc - 1)
    def _():
        for bi in range(bb):
            for h in range(N_HEADS):
                s_out_ref[bi, h] = st_ref[bi, h * HEAD_DIM:(h + 1) * HEAD_DIM, h * HEAD_DIM:(h + 1) * HEAD_DIM].T


def _hgrn_chunk(q_ref, f_ref, i_ref, g_ref, lbp_ref, gn_ref, ones_ref, mask_ref, y_ref, st_ref, phi_ref, plo_ref):
    L = CHUNK
    ones_bd = ones_ref[...]
    masks = _head_masks()
    log_lb = lbp_ref[0:1, :]
    log1m_lb = lbp_ref[1:2, :]
    one_m_lb = lbp_ref[2:3, :]
    zf = f_ref[...]
    u = log_lb
    w = log1m_lb + _log_sigmoid(zf)
    logf = jnp.maximum(u, w) + jnp.log1p(jnp.exp(-jnp.abs(u - w)))
    k = one_m_lb * _sigmoid(-zf)
    q = _silu(q_ref[...])
    v = i_ref[...]

    b = _cumsum_rows(logf)
    yield
    b_last = b[L - 1:L]
    qe = q * jnp.exp(b)
    kdec = k * jnp.exp(b_last - b)
    sub_row = lax.broadcasted_iota(jnp.int32, (SUB, WIDTH), 0)

    for s in range(L):
        r0 = (s // SUB) * SUB
        p = q[r0:r0 + SUB] * k[s:s + 1] * jnp.exp(jnp.minimum(b[r0:r0 + SUB] - b[s:s + 1], 0.0))
        hi, lo = _split2(jnp.where(sub_row >= s - r0, p, 0.0))
        phi_ref[s * SUB:(s + 1) * SUB, :] = hi
        plo_ref[s * SUB:(s + 1) * SUB, :] = lo
    yield
    att = _dot(phi_ref[...], ones_bd) + _dot(plo_ref[...], ones_bd)
    st = st_ref[...]
    o_inter = _dot(qe, st, NT)
    upd = _dot(v, kdec, TN)
    scores = []
    for blk in range(1, L // SUB):
        r0 = blk * SUB
        ref_row = b[r0 - 1:r0]
        q_s = q[r0:r0 + SUB] * jnp.exp(b[r0:r0 + SUB] - ref_row)
        k_s = k[:r0] * jnp.exp(ref_row - b[:r0])
        scores.append(_dot(q_s, _stack_heads(k_s, masks), NT))
    yield
    st_ref[...] = st * jnp.exp(b_last) + upd * mask_ref[...]
    o_blocks = []
    for blk in range(L // SUB):
        r0 = blk * SUB
        acc = jnp.zeros((SUB, WIDTH), F32)
        for j in range(SUB):
            s = r0 + j
            acc = acc + att[s * SUB:(s + 1) * SUB] * v[s:s + 1]
        if blk > 0:
            acc = acc + _dot(scores[blk - 1], _stack_heads(v[:r0], masks))
        o_blocks.append(acc)
    yield
    o = jnp.concatenate(o_blocks, axis=0) + o_inter
    ms = _head_sums(o * o, ones_bd) * (1.0 / HEAD_DIM)
    yield
    y_ref[...] = (o * lax.rsqrt(ms + HEAD_NORM_EPS) * gn_ref[...] * _silu(g_ref[...])).astype(y_ref.dtype)


def _hgrn(h3, lbp, gn, layer, mask_bd, s0):
    bsz, t, _ = h3.shape
    nc = t // CHUNK
    cb = COL_A // WIDTH
    bb = STREAMS_PER_STEP

    def col(j):
        return pl.BlockSpec((bb, CHUNK, WIDTH), lambda b, c, j=j: (b, c, cb + j))

    square = pl.BlockSpec((WIDTH, WIDTH), lambda b, c: (0, 0))
    state = pl.BlockSpec((bb, N_HEADS, HEAD_DIM, HEAD_DIM), lambda b, c: (b, 0, 0, 0))
    return pl.pallas_call(
        functools.partial(_hgrn_kernel, nc=nc, bb=bb),
        grid=(bsz // bb, nc),
        in_specs=[col(0), col(1), col(2), col(3),
                  _layer_block(lbp, layer), _layer_block(gn, layer),
                  square, square, state],
        out_specs=[pl.BlockSpec((bb, CHUNK, WIDTH), lambda b, c: (b, c, 0)), state],
        out_shape=[jax.ShapeDtypeStruct((bsz, t, WIDTH), BF16),
                   jax.ShapeDtypeStruct((bsz, N_HEADS, HEAD_DIM, HEAD_DIM), F32)],
        scratch_shapes=[pltpu.VMEM((bb, WIDTH, WIDTH), F32),
                        pltpu.VMEM((bb, CHUNK * SUB, WIDTH), BF16),
                        pltpu.VMEM((bb, CHUNK * SUB, WIDTH), BF16)],
        compiler_params=_cparams("parallel", "arbitrary"),
        name="hgrn2",
    )(h3, h3, h3, h3, lbp, gn, mask_bd.astype(BF16), mask_bd, s0)


def _mlstm_kernel(u_ref, v_ref, o_ref, gcol_ref, xin_ref, wift_ref, bcol_ref, brow_ref, cw_ref, vec_ref, wqk_ref,
                  ones_ref, mask_ref, c0_ref, n0_ref, m0_ref, conv0_ref,
                  y_ref, c_out_ref, n_out_ref, m_out_ref, conv_out_ref,
                  c_ref, n_ref, m_ref, carry_ref, *, nc, bb):
    c = pl.program_id(1)

    @pl.when(c == 0)
    def _():
        for bi in range(bb):
            c_ref[bi] = jnp.zeros(c_ref.shape[1:], F32)
            for h in range(N_HEADS):
                c_ref[bi, h * HEAD_DIM:(h + 1) * HEAD_DIM, h * HEAD_DIM:(h + 1) * HEAD_DIM] = c0_ref[bi, h]
        n_ref[...] = n0_ref[...]
        m_ref[...] = m0_ref[...]
        carry_ref[...] = conv0_ref[...]

    _round_robin([_mlstm_chunk(u_ref.at[bi], v_ref.at[bi], o_ref.at[bi], gcol_ref.at[bi], xin_ref.at[bi],
                               wift_ref, bcol_ref, brow_ref, cw_ref, vec_ref, wqk_ref, ones_ref, mask_ref, y_ref.at[bi],
                               c_ref.at[bi], n_ref.at[bi], m_ref.at[bi], carry_ref.at[bi]) for bi in range(bb)])

    @pl.when(c == nc - 1)
    def _():
        for bi in range(bb):
            for h in range(N_HEADS):
                c_out_ref[bi, h] = c_ref[bi, h * HEAD_DIM:(h + 1) * HEAD_DIM, h * HEAD_DIM:(h + 1) * HEAD_DIM]
        n_out_ref[...] = n_ref[...]
        m_out_ref[...] = m_ref[...]
        conv_out_ref[...] = carry_ref[...]


def _mlstm_chunk(u_ref, v_ref, o_ref, gcol_ref, xin_ref, wift_ref, bcol_ref, brow_ref, cw_ref, vec_ref, wqk_ref,
                 ones_ref, mask_ref, y_ref, c_ref, n_ref, m_ref, carry_ref):
    L = CHUNK
    heads = range(N_HEADS)

    u = u_ref[...]
    carry = carry_ref[...]
    row8 = lax.broadcasted_iota(jnp.int32, (8, WIDTH), 0)
    conv = vec_ref[0:1, :] + cw_ref[CONV_W - 1:CONV_W, :] * u
    for d in range(1, CONV_W):
        rolled = pltpu.roll(u, d, 0)
        top = jnp.where(row8 < d, pltpu.roll(carry, d, 0), rolled[0:8])
        conv = conv + cw_ref[CONV_W - 1 - d:CONV_W - d, :] * jnp.concatenate([top, rolled[8:]], axis=0)
    carry_ref[...] = u[L - 8:L]
    cact = _silu(conv)
    qk_all = _dot(cact.astype(BF16), wqk_ref[...])

    row = lax.broadcasted_iota(jnp.int32, (L, L), 0)
    col = lax.broadcasted_iota(jnp.int32, (L, L), 1)
    lower = row >= col
    raw_c = gcol_ref[:, 0:2 * N_HEADS] + brow_ref[...]
    lane8 = lax.broadcasted_iota(jnp.int32, raw_c.shape, 1)
    gcol = jnp.where(lane8 < N_HEADS, raw_c, _log_sigmoid(raw_c))
    raw_r = _dot(wift_ref[...], xin_ref[...], NT) + bcol_ref[...]
    sub8 = lax.broadcasted_iota(jnp.int32, raw_r.shape, 0)
    grow = jnp.where(sub8 < N_HEADS, raw_r, _log_sigmoid(raw_r))
    f_col = jnp.dot(lower.astype(F32), gcol, preferred_element_type=F32, precision=HIGHEST)
    f_row = jnp.dot(grow, (row <= col).astype(F32), preferred_element_type=F32, precision=HIGHEST)
    r8 = lax.broadcasted_iota(jnp.int32, (2 * N_HEADS, 2 * WIDTH), 0)
    c8 = lax.broadcasted_iota(jnp.int32, (2 * N_HEADS, 2 * WIDTH), 1)
    expand = (c8 // HEAD_DIM == r8).astype(BF16)
    x8 = jnp.where(lane8 < N_HEADS, gcol, f_col)
    x_hi, x_mid, x_lo = _split3(x8)
    both = _dot(x_hi, expand) + _dot(x_mid, expand) + _dot(x_lo, expand)
    li_all = both[:, 0:WIDTH]
    f_all = both[:, WIDTH:2 * WIDTH]
    d_row = jnp.concatenate([grow[h:h + 1, :] - f_row[N_HEADS + h:N_HEADS + h + 1, :] for h in heads], axis=1)
    yield
    t_idx = lax.broadcasted_iota(jnp.int32, (L, WIDTH), 0)
    s_idx = lax.broadcasted_iota(jnp.int32, (L, WIDTH), 1) % L
    ones_bd = ones_ref[...]
    masks = _head_masks()
    cm = li_all - f_all
    for sh in (1, 2, 4, 8, 16, 32):
        cm = jnp.maximum(cm, jnp.where(t_idx >= sh, pltpu.roll(cm, sh, 0), -jnp.inf))
    g = f_all + m_ref[...]
    mt = jnp.maximum(g, f_all + cm)
    wg = jnp.exp(g - mt)
    wd = jnp.exp(jnp.where(s_idx <= t_idx, f_all + d_row - mt, -jnp.inf))
    q = qk_all[:, 0:WIDTH]
    k = qk_all[:, WIDTH:2 * WIDTH]
    v = v_ref[...]
    q_bf = q.astype(BF16)
    v_bf = v.astype(BF16)
    head_rows = [mk.astype(BF16) for mk in masks]
    k_stack = jnp.concatenate([k.astype(BF16) * r for r in head_rows], axis=0)
    v_stack = jnp.concatenate([v_bf * r for r in head_rows], axis=0)
    c_bd = c_ref[...]
    n_row = n_ref[...]
    qk = _dot(q_bf, k_stack, NT) * wd
    q_c = _dot(q_bf, c_bd.astype(BF16))
    q_n = _head_sums(q * n_row, ones_bd)
    yield
    qkv = _dot(qk.astype(BF16), v_stack)
    qk_sum = _head_sums(qk, ones_bd)
    mt_last = mt[L - 1:L]
    kw = k * jnp.exp(f_all[L - 1:L] - f_all + li_all - mt_last)
    c_upd = _dot(kw.astype(BF16), v_bf, TN)
    yield
    wgl = wg[L - 1:L]
    c_ref[...] = wgl * c_bd + c_upd * mask_ref[...]
    n_ref[...] = wgl * n_row + jnp.sum(kw, axis=0, keepdims=True)
    m_ref[...] = mt_last
    den = wg * q_n + qk_sum
    hh = (wg * q_c + qkv) / jnp.maximum(jnp.abs(den), jnp.exp(-mt))
    z = _sigmoid(o_ref[...]) * hh
    mu = _head_sums(z, ones_bd) * (1.0 / HEAD_DIM)
    yield
    zc = z - mu
    var = _head_sums(zc * zc, ones_bd) * (1.0 / HEAD_DIM)
    yield
    y_ref[...] = (zc * lax.rsqrt(var + HEAD_NORM_EPS) * vec_ref[1:2, :] + vec_ref[2:3, :] * cact).astype(y_ref.dtype)


def _mlstm(h3, xb3, w_if_t, b_col, b_row, conv_w, vecs, w_qk, layer, mask_bd, c0, n0, m0, conv0):
    bsz, t, _ = h3.shape
    nc = t // CHUNK
    cb = COL_B // WIDTH
    bb = STREAMS_PER_STEP
    state4 = pl.BlockSpec((bb, N_HEADS, HEAD_DIM, HEAD_DIM), lambda b, c: (b, 0, 0, 0))
    state_n = pl.BlockSpec((bb, 1, WIDTH), lambda b, c: (b, 0, 0))
    state_m = state_n
    state_conv = pl.BlockSpec((bb, 8, WIDTH), lambda b, c: (b, 0, 0))

    def const(shape):
        return pl.BlockSpec(shape, lambda b, c: tuple(0 for _ in shape))

    def col(j):
        return pl.BlockSpec((bb, CHUNK, WIDTH), lambda b, c, j=j: (b, c, cb + j))

    return pl.pallas_call(
        functools.partial(_mlstm_kernel, nc=nc, bb=bb),
        grid=(bsz // bb, nc),
        in_specs=[col(0), col(1), col(2),
                  pl.BlockSpec((bb, CHUNK, LANE_TILE), lambda b, c: (b, c, COL_IF // LANE_TILE)),
                  pl.BlockSpec((bb, CHUNK, D_MODEL), lambda b, c: (b, c, 0)), _layer_block(w_if_t, layer),
                  _layer_block(b_col, layer), _layer_block(b_row, layer), _layer_block(conv_w, layer),
                  _layer_block(vecs, layer), _layer_block(w_qk, layer), const((WIDTH, WIDTH)), const((WIDTH, WIDTH)),
                  state4, state_n, state_m, state_conv],
        out_specs=[pl.BlockSpec((bb, CHUNK, WIDTH), lambda b, c: (b, c, 0)), state4, state_n, state_m, state_conv],
        out_shape=[jax.ShapeDtypeStruct((bsz, t, WIDTH), BF16),
                   jax.ShapeDtypeStruct((bsz, N_HEADS, HEAD_DIM, HEAD_DIM), F32),
                   jax.ShapeDtypeStruct((bsz, 1, WIDTH), F32),
                   jax.ShapeDtypeStruct((bsz, 1, WIDTH), F32),
                   jax.ShapeDtypeStruct((bsz, 8, WIDTH), F32)],
        scratch_shapes=[pltpu.VMEM((bb, WIDTH, WIDTH), F32),
                        pltpu.VMEM((bb, 1, WIDTH), F32),
                        pltpu.VMEM((bb, 1, WIDTH), F32),
                        pltpu.VMEM((bb, 8, WIDTH), F32)],
        compiler_params=_cparams("parallel", "arbitrary"),
        name="mlstm",
    )(h3, h3, h3, h3, xb3, w_if_t, b_col, b_row, conv_w, vecs, w_qk, mask_bd.astype(BF16), mask_bd, c0, n0, m0, conv0)


SWA_CHUNKS_PER_STEP = 8


def _swa_kernel(*refs, prev_valid, cps, multi_step):
    if multi_step:
        (q_ref, kc_ref, vc_ref, kp_ref, vp_ref, k_ref, v_ref, bias_ref, sink_ref, y_ref, s_ref, p_ref) = refs
    else:
        (q_ref, kc_ref, vc_ref, k_ref, v_ref, bias_ref, sink_ref, y_ref, s_ref, p_ref) = refs
    g = pl.program_id(1)
    L = CHUNK
    span = WINDOW + L
    k_win = kc_ref[0]
    v_win = vc_ref[0]
    if multi_step:
        k_win = jnp.where(g == 0, k_win, kp_ref[0])
        v_win = jnp.where(g == 0, v_win, vp_ref[0])
    kcat = jnp.concatenate([k_win, k_ref[0]], axis=0).astype(BF16)
    vcat = jnp.concatenate([v_win, v_ref[0]], axis=0).astype(BF16)
    kv_of = [h // (N_HEADS // C_KV_HEADS) for h in range(N_HEADS)]
    heads = range(N_HEADS)

    def chunk(j):
        for _ in range(j):
            yield
        q = q_ref[0, j * L:(j + 1) * L, :].astype(BF16)
        k_j = kcat[j * L:j * L + span]
        for h in heads:
            s_ref[j * N_HEADS + h] = _dot(_head(q, h), _head(k_j, kv_of[h]), NT)
        yield
        key_pos = lax.broadcasted_iota(jnp.int32, (L, span), 1) + (g * cps + j) * L
        s = [s_ref[j * N_HEADS + h] * HEAD_DIM ** -0.5 + bias_ref[h] for h in heads]
        if not prev_valid:
            s = [jnp.where(key_pos >= WINDOW, s_h, -jnp.inf) for s_h in s]
        sink = [sink_ref[0:1, h:h + 1] for h in heads]
        m = [jnp.maximum(jnp.max(s[h], axis=1, keepdims=True), sink[h]) for h in heads]
        yield
        p = [jnp.exp(s[h] - m[h]) for h in heads]
        denom = [jnp.sum(p[h], axis=1, keepdims=True) + jnp.exp(sink[h] - m[h]) for h in heads]
        yield
        for h in heads:
            p_ref[j * N_HEADS + h] = (p[h] * (1.0 / denom[h])).astype(BF16)
        yield
        v_j = vcat[j * L:j * L + span]
        o = [_dot(p_ref[j * N_HEADS + h], _head(v_j, kv_of[h])) for h in heads]
        y_ref[0, j * L:(j + 1) * L, :] = jnp.concatenate(o, axis=1).astype(y_ref.dtype)

    _round_robin([chunk(j) for j in range(cps)])


def _swa(h3, cache_k, cache_v, bias, sinks, layer, prev_valid):
    bsz, t, _ = h3.shape
    nc = t // CHUNK
    cps = min(SWA_CHUNKS_PER_STEP, nc)
    rows = cps * CHUNK
    steps = nc // cps
    multi_step = steps > 1
    assert nc % cps == 0 and (not multi_step or rows % WINDOW == 0), (t, cps)
    qb = COL_C // WIDTH
    kb = (COL_C + WIDTH) // KV_WIDTH
    cache = pl.BlockSpec((1, WINDOW, KV_WIDTH), lambda b, g: (b, 0, 0))

    def own(col):
        return pl.BlockSpec((1, rows, KV_WIDTH), lambda b, g, col=col: (b, g, col))

    def before(col):
        return pl.BlockSpec((1, WINDOW, KV_WIDTH),
                            lambda b, g, col=col: (b, jnp.maximum(g * (rows // WINDOW) - 1, 0), col))

    kv_specs = [cache, cache] + ([before(kb), before(kb + 1)] if multi_step else []) + [own(kb), own(kb + 1)]
    kv_args = [cache_k, cache_v] + ([h3, h3] if multi_step else []) + [h3, h3]
    return pl.pallas_call(
        functools.partial(_swa_kernel, prev_valid=prev_valid, cps=cps, multi_step=multi_step),
        grid=(bsz, steps),
        in_specs=[pl.BlockSpec((1, rows, WIDTH), lambda b, g: (b, g, qb))] + kv_specs + [
                  pl.BlockSpec((N_HEADS, CHUNK, WINDOW + CHUNK), lambda b, g: (0, 0, 0)),
                  _layer_block(sinks, layer)],
        out_specs=pl.BlockSpec((1, rows, WIDTH), lambda b, g: (b, g, 0)),
        out_shape=jax.ShapeDtypeStruct((bsz, t, WIDTH), BF16),
        scratch_shapes=[pltpu.VMEM((cps * N_HEADS, CHUNK, WINDOW + CHUNK), F32),
                        pltpu.VMEM((cps * N_HEADS, CHUNK, WINDOW + CHUNK), BF16)],
        compiler_params=_cparams("parallel", "parallel"),
        name="swa",
    )(h3, *kv_args, bias, sinks)


def _rwkv_kernel(pd_ref, mu_ref, vec_ref, wwa_ref, gup_ref, ones_ref, mask_ref, s0_ref, sh0_ref,
                 y_ref, s_out_ref, sh_out_ref, st_ref, carry_ref, *, nc, bb):
    c = pl.program_id(1)

    @pl.when(c == 0)
    def _():
        for bi in range(bb):
            st_ref[bi] = jnp.zeros(st_ref.shape[1:], F32)
            for h in range(N_HEADS):
                st_ref[bi, h * HEAD_DIM:(h + 1) * HEAD_DIM, h * HEAD_DIM:(h + 1) * HEAD_DIM] = s0_ref[bi, h]
            carry_ref[bi, 7:8, :] = sh0_ref[bi]

    _round_robin([_rwkv_chunk(pd_ref.at[bi], mu_ref, vec_ref, wwa_ref, gup_ref, ones_ref, mask_ref,
                              y_ref.at[bi], st_ref.at[bi], carry_ref.at[bi]) for bi in range(bb)])

    @pl.when(c == nc - 1)
    def _():
        for bi in range(bb):
            for h in range(N_HEADS):
                s_out_ref[bi, h] = st_ref[bi, h * HEAD_DIM:(h + 1) * HEAD_DIM, h * HEAD_DIM:(h + 1) * HEAD_DIM]
            sh_out_ref[bi] = carry_ref[bi, 7:8, :]


def _rwkv_chunk(pd_ref, mu_ref, vec_ref, wwa_ref, gup_ref, ones_ref, mask_ref, y_ref, st_ref, carry_ref):
    L = CHUNK
    ones_bd = ones_ref[...]
    masks = _head_masks()

    pd = pd_ref[...]
    rolled = pltpu.roll(pd, 1, 0)
    row8 = lax.broadcasted_iota(jnp.int32, (8, D_COLS), 0)
    top = jnp.where(row8 == 0, carry_ref[7:8, :], rolled[0:8])
    pd_prev = jnp.concatenate([top, rolled[8:]], axis=0)
    carry_ref[...] = pd[L - 8:L]
    mixed = pd + (pd_prev - pd) * mu_ref[...]
    r = mixed[:, 0:WIDTH]
    k_raw = mixed[:, WIDTH:2 * WIDTH]
    v = mixed[:, 2 * WIDTH:3 * WIDTH]
    wa = mixed[:, 3 * WIDTH:3 * WIDTH + 2 * D_DECAY_LORA]
    g_in = mixed[:, 3 * WIDTH + 2 * D_DECAY_LORA:D_COLS]
    lane_wa = lax.broadcasted_iota(jnp.int32, wa.shape, 1)
    wa_act = jnp.where(lane_wa < D_DECAY_LORA, jnp.tanh(wa), wa)
    lora = _dot(wa_act.astype(BF16), wwa_ref[...])
    g_d = _dot(_sigmoid(g_in).astype(BF16), gup_ref[...])
    yield
    z = -(vec_ref[0:1, :] + lora[:, 0:WIDTH])
    w_log = -(jnp.maximum(z, 0.0) + jnp.log1p(jnp.exp(-jnp.abs(z)))) - 0.5
    lw = -jnp.exp(w_log)
    a_d = _sigmoid(vec_ref[1:2, :] + lora[:, WIDTH:2 * WIDTH])
    kk = k_raw * vec_ref[2:3, :]
    kk_norm = _head_sums(kk * kk, ones_bd)
    cw = _cumsum_rows(lw)
    yield
    kk = kk * lax.rsqrt(jnp.maximum(kk_norm, 1e-24))
    k = k_raw * (1.0 + (a_d - 1.0) * vec_ref[3:4, :])
    a = -kk
    b = kk * a_d
    g_in_c = jnp.exp(cw)
    g_inv = jnp.exp(-cw)
    at = a * jnp.exp(cw - lw)
    rt = r * g_in_c
    bt = b * g_inv
    kt = k * g_inv
    x = jnp.concatenate([at, rt], axis=0)
    bt_st = _stack_parts(bt, masks)
    kt_st = _stack_parts(kt, masks)
    gram_b = _mm2(x, bt_st, NT)
    gram_k = _mm2(x, kt_st, NT)
    st = st_ref[...]
    u = _mm2(x, _split2(st), NT)
    yield

    t_idx = lax.broadcasted_iota(jnp.int32, (L, WIDTH), 0)
    i_idx = lax.broadcasted_iota(jnp.int32, (L, WIDTH), 1) % L
    strict = i_idx < t_idx
    incl = i_idx <= t_idx
    same_blk = (i_idx // SUB) == (t_idx // SUB)
    eye = (i_idx == t_idx).astype(F32)
    n_all = jnp.where(strict, gram_b[0:L], 0.0)
    m_all = jnp.where(strict, gram_k[0:L], 0.0)
    rb_all = jnp.where(incl, gram_b[L:2 * L], 0.0)
    rk_all = jnp.where(incl, gram_k[L:2 * L], 0.0)
    n_d = jnp.where(same_blk, n_all, 0.0)
    n_off = jnp.where(same_blk, 0.0, n_all)

    v_st = _stack_parts(v, masks)
    x2 = _mm2(n_d, _stack_parts(n_d, masks))
    mv_rkv = _mm2(jnp.concatenate([m_all, rk_all], axis=0), v_st)
    rhs = u[0:L] + mv_rkv[0:L]
    bonus = _head_sums(r * k * vec_ref[4:5, :], ones_bd) * v
    yield
    t_d = eye + n_d
    both = _mm2(jnp.concatenate([t_d, x2], axis=0), _stack_parts(x2, masks))
    t_d = t_d + both[0:L]
    x4 = both[L:2 * L]
    yield
    both = _mm2(jnp.concatenate([t_d, x4], axis=0), _stack_parts(x4, masks))
    t_d = t_d + both[0:L]
    x8 = both[L:2 * L]
    yield
    t_d = t_d + _mm2(t_d, _stack_parts(x8, masks))
    yield
    n1 = _mm2(t_d, _stack_parts(n_off, masks))
    yield
    n1_st = _stack_parts(n1, masks)
    n2 = _mm2(n1, n1_st)
    yield
    a2 = eye + n1 + n2 + _mm2(n2, n1_st)
    yield
    t_full = _mm2(a2, _stack_parts(t_d, masks))
    yield
    sa = _mm2(t_full, _stack_parts(rhs, masks))
    yield
    y = u[L:2 * L] + _mm2(rb_all, _stack_parts(sa, masks)) + mv_rkv[L:2 * L]
    sv_hi, sv_lo = _split2(jnp.concatenate([sa, v], axis=0))
    bk_hi, bk_lo = _split2(jnp.concatenate([bt, kt], axis=0))
    upd = _dot(sv_hi, bk_hi, TN) + _dot(sv_lo, bk_hi, TN) + _dot(sv_hi, bk_lo, TN)
    yield
    st_ref[...] = (st + upd * mask_ref[...]) * g_in_c[L - 1:L]

    mu = _head_sums(y, ones_bd) * (1.0 / HEAD_DIM)
    yield
    yc = y - mu
    var = _head_sums(yc * yc, ones_bd) * (1.0 / HEAD_DIM)
    yield
    o = yc * lax.rsqrt(var + RWKV_GN_EPS) * vec_ref[5:6, :] + vec_ref[6:7, :]
    y_ref[...] = ((o + bonus) * g_d).astype(y_ref.dtype)


def _rwkv(h3, mu, vecs, w_wa, g_up, layer, mask_bd, s0, shift0):
    bsz, t, _ = h3.shape
    nc = t // CHUNK
    bb = STREAMS_PER_STEP
    state = pl.BlockSpec((bb, N_HEADS, HEAD_DIM, HEAD_DIM), lambda g, i: (g, 0, 0, 0))
    shift = pl.BlockSpec((bb, 1, D_COLS), lambda g, i: (g, 0, 0))

    def const(shape):
        return pl.BlockSpec(shape, lambda g, i: tuple(0 for _ in shape))

    return pl.pallas_call(
        functools.partial(_rwkv_kernel, nc=nc, bb=bb),
        grid=(bsz // bb, nc),
        in_specs=[pl.BlockSpec((bb, CHUNK, D_COLS), lambda g, i: (g, i, COL_D // D_COLS)),
                  _layer_block(mu, layer), _layer_block(vecs, layer), _layer_block(w_wa, layer),
                  _layer_block(g_up, layer), const((WIDTH, WIDTH)), const((WIDTH, WIDTH)), state, shift],
        out_specs=[pl.BlockSpec((bb, CHUNK, WIDTH), lambda g, i: (g, i, 0)), state, shift],
        out_shape=[jax.ShapeDtypeStruct((bsz, t, WIDTH), BF16),
                   jax.ShapeDtypeStruct((bsz, N_HEADS, HEAD_DIM, HEAD_DIM), F32),
                   jax.ShapeDtypeStruct((bsz, 1, D_COLS), F32)],
        scratch_shapes=[pltpu.VMEM((bb, WIDTH, WIDTH), F32), pltpu.VMEM((bb, 8, D_COLS), F32)],
        compiler_params=_cparams("parallel", "arbitrary"),
        name="rwkv7",
    )(h3, mu, vecs, w_wa, g_up, mask_bd.astype(BF16), mask_bd, s0, shift0)


def _merge_kernel(ya_ref, yb_ref, yc_ref, yd_ref, xb_ref, wg_ref, wbr_ref, wo_ref, x_ref,
                  lng_ref, lnb_ref, xo_ref, xob_ref):
    ys = (ya_ref, yb_ref, yc_ref, yd_ref)
    xb = xb_ref[...]
    merged = None
    for n in range(N_BRANCH):
        gate = _dot(xb, wg_ref[0, :, n * D_MODEL:(n + 1) * D_MODEL])
        term = _sigmoid(gate) * _dot(ys[n][...], wbr_ref[0, n])
        merged = term if merged is None else merged + term
    out = _dot(merged.astype(BF16), wo_ref[0])
    xn = _layernorm_rows(ALPHA * x_ref[...] + out, lng_ref[...], lnb_ref[...])
    xo_ref[...] = xn
    xob_ref[...] = xn.astype(BF16)


def _merge(ys, xb2, w_gate, wbr, wo, layer, x2, lng, lnb, *, tm=512):
    m = x2.shape[0]
    tm = min(tm, m)
    ysp = pl.BlockSpec((tm, WIDTH), lambda i: (i, 0))
    row = pl.BlockSpec((tm, D_MODEL), lambda i: (i, 0))
    vec = _layer_block(lng, layer)
    resident = dict(pipeline_mode=pl.Buffered(1))
    return pl.pallas_call(
        _merge_kernel,
        grid=(m // tm,),
        in_specs=[ysp, ysp, ysp, ysp, row,
                  pl.BlockSpec((1, D_MODEL, GATE_COLS), lambda i: (layer, 0, 0), **resident),
                  pl.BlockSpec((1, N_BRANCH, WIDTH, D_MODEL), lambda i: (layer, 0, 0, 0), **resident),
                  pl.BlockSpec((1, D_MODEL, D_MODEL), lambda i: (layer, 0, 0), **resident),
                  row, vec, vec],
        out_specs=[row, row],
        out_shape=[jax.ShapeDtypeStruct((m, D_MODEL), F32), jax.ShapeDtypeStruct((m, D_MODEL), BF16)],
        compiler_params=_cparams("parallel"),
        name="merge",
    )(*ys, xb2, w_gate, wbr, wo, x2, lng, lnb)


def _ffn_kernel(*refs, n_steps, gated):
    if gated:
        xb_ref, x_ref, gates_ref, wg_ref, wu_ref, wd_ref, lng_ref, lnb_ref, xo_ref, xob_ref, acc_ref = refs
    else:
        xb_ref, x_ref, wg_ref, wu_ref, wd_ref, lng_ref, lnb_ref, xo_ref, xob_ref, acc_ref = refs
    j = pl.program_id(1)

    @pl.when(j == 0)
    def _():
        acc_ref[...] = jnp.zeros_like(acc_ref)

    xb = xb_ref[...]
    hg = _dot(xb, wg_ref[0, 0])
    hu = _dot(xb, wu_ref[0, 0])
    part = _dot((_silu(hg) * hu).astype(BF16), wd_ref[0, 0])
    if gated:
        gates = gates_ref[...]
        lane = lax.broadcasted_iota(jnp.int32, gates.shape, 1)
        part = jnp.sum(jnp.where(lane == j, gates, 0.0), axis=1, keepdims=True) * part
    acc_ref[...] += part

    @pl.when(j == n_steps - 1)
    def _():
        xn = _layernorm_rows(ALPHA * x_ref[...] + acc_ref[...], lng_ref[...], lnb_ref[...])
        xo_ref[...] = xn
        xob_ref[...] = xn.astype(BF16)


def _ffn(xb, x2, wg, wu, wd, layer, lng, lnb, ln_layer, gates=None, *, tm=512):
    m = x2.shape[0]
    tm = min(tm, m)
    _, n_steps, _, tf = wg.shape
    row = pl.BlockSpec((tm, D_MODEL), lambda i, j: (i, 0))
    vec = _layer_block(lng, ln_layer)
    w_in_spec = pl.BlockSpec((1, 1, D_MODEL, tf), lambda i, j: (layer, j, 0, 0))
    w_out_spec = pl.BlockSpec((1, 1, tf, D_MODEL), lambda i, j: (layer, j, 0, 0))
    in_specs = [row, row]
    args = [xb, x2]
    if gates is not None:
        in_specs.append(pl.BlockSpec((tm, N_EXPERTS), lambda i, j: (i, 0)))
        args.append(gates)
    in_specs += [w_in_spec, w_in_spec, w_out_spec, vec, vec]
    args += [wg, wu, wd, lng, lnb]
    return pl.pallas_call(
        functools.partial(_ffn_kernel, n_steps=n_steps, gated=gates is not None),
        grid=(m // tm, n_steps),
        in_specs=in_specs,
        out_specs=[row, row],
        out_shape=[jax.ShapeDtypeStruct((m, D_MODEL), F32), jax.ShapeDtypeStruct((m, D_MODEL), BF16)],
        scratch_shapes=[pltpu.VMEM((tm, D_MODEL), F32)],
        compiler_params=_cparams("parallel", "arbitrary"),
        name="moe" if gates is not None else "ffn",
    )(*args)


STATE_NAMES = ('swa_k', 'swa_v', 'hgrn', 'mlstm_c', 'mlstm_n', 'mlstm_m', 'mlstm_conv', 'rwkv', 'rwkv_shift')

_D_ORIG = (('r', WIDTH), ('w', D_DECAY_LORA), ('k', WIDTH), ('v', WIDTH), ('a', D_AAA_LORA), ('g', D_GATE_LORA))
_D_KERNEL = ('r', 'k', 'v', 'w', 'a', 'g')


def _d_pieces(arr):
    out, off = {}, 0
    for name, size in _D_ORIG:
        out[name] = arr[..., off:off + size]
        off += size
    return out


def _d_to_kernel_order(arr):
    p = _d_pieces(arr)
    return jnp.concatenate([p[n] for n in _D_KERNEL], axis=-1)


def _d_to_original_order(arr):
    sizes = dict(_D_ORIG)
    p, off = {}, 0
    for name in _D_KERNEL:
        p[name] = arr[..., off:off + sizes[name]]
        off += sizes[name]
    return jnp.concatenate([p[n] for n, _ in _D_ORIG], axis=-1)


def _rel_bucket(rel):
    half = NUM_BUCKETS // 2
    exact = half // 2
    dist = jnp.abs(rel)
    far = exact + (jnp.log(jnp.maximum(dist, 1).astype(F32) / exact)
                   / math.log(MAX_DISTANCE / exact) * (half - exact)).astype(jnp.int32)
    far = jnp.minimum(far, half - 1)
    return jnp.where(rel > 0, half, 0) + jnp.where(dist < exact, dist, far)


def _lower_bounds(lb_raw):
    sm = jax.nn.softmax(lb_raw.astype(F32), axis=0)
    lb = jnp.concatenate([jnp.zeros_like(sm[:1]), jnp.cumsum(sm[1:], axis=0)[:-1]], axis=0)
    return jnp.clip(lb, 0.0, LB_CEIL)


def _block_diag(blocks):
    rows = sum(b.shape[0] for b in blocks)
    cols = sum(b.shape[1] for b in blocks)
    out = jnp.zeros((rows, cols), blocks[0].dtype)
    r = c = 0
    for b in blocks:
        out = out.at[r:r + b.shape[0], c:c + b.shape[1]].set(b)
        r += b.shape[0]
        c += b.shape[1]
    return out


def _pad_rows(a, rows):
    return jnp.concatenate([a, jnp.zeros((rows - a.shape[0],) + a.shape[1:], a.dtype)], axis=0)


def _mixer(x3, xb2, st, prev_valid, lp, l):
    bsz, t, _ = x3.shape
    m = bsz * t
    x2 = x3.reshape(m, D_MODEL)
    h2 = _proj(xb2, lp['w_main'], l)
    h3 = h2.reshape(bsz, t, MAIN_COLS)

    y_a, s_hgrn = _hgrn(h3, lp['lbp'], lp['gn_a'], l, lp['mask_bd'], st['hgrn'])

    conv0 = jnp.concatenate([jnp.zeros((bsz, 8 - (CONV_W - 1), WIDTH), F32), st['mlstm_conv']], axis=1)
    y_b, mc, mn, mm, conv8 = _mlstm(h3, xb2.reshape(bsz, t, D_MODEL), lp['w_if_t'], lp['b_if_col'],
                                    lp['b_if_row'], lp['conv_w8'], lp['vec_b'], lp['w_qk'], l,
                                    lp['mask_bd'], st['mlstm_c'], st['mlstm_n'].reshape(bsz, 1, WIDTH),
                                    jnp.repeat(st['mlstm_m'], HEAD_DIM, axis=-1)[:, None, :], conv0)
    mn = mn.reshape(bsz, N_HEADS, HEAD_DIM)
    mm = mm[:, :, ::HEAD_DIM]
    conv_state = conv8[:, 8 - (CONV_W - 1):]

    k_off = COL_C + WIDTH
    cache_k = st['swa_k'].reshape(bsz, WINDOW, KV_WIDTH)
    cache_v = st['swa_v'].reshape(bsz, WINDOW, KV_WIDTH)
    y_c = _swa(h3, cache_k, cache_v, lp['bias'], lp['sinks'], l, prev_valid)
    keep = st['keep']

    def window(cache, col):
        new = h3[:, max(t - keep, 0):, col:col + KV_WIDTH]
        if t < keep:
            new = jnp.concatenate([cache[:, WINDOW - (keep - t):], new], axis=1)
        return new.reshape(bsz, keep, C_KV_HEADS, HEAD_DIM)

    k_win = window(cache_k, k_off)
    v_win = window(cache_v, k_off + KV_WIDTH)

    y_d, s_rwkv, shift = _rwkv(h3, lp['mu_d'], lp['vec_d'], lp['w_wa'], lp['g_up_d'], l,
                               lp['mask_bd'], st['rwkv'], _d_to_kernel_order(st['rwkv_shift'])[:, None, :])
    shift_state = _d_to_original_order(shift[:, 0, :])

    ys = [y.reshape(m, WIDTH) for y in (y_a, y_b, y_c, y_d)]
    x1, x1b = _merge(ys, xb2, lp['w_gate'], lp['w_br'], lp['w_o'], l, x2, lp['ln1_g'], lp['ln1_b'])
    new_st = {'swa_k': k_win, 'swa_v': v_win, 'hgrn': s_hgrn, 'mlstm_c': mc, 'mlstm_n': mn,
              'mlstm_m': mm[:, 0, :], 'mlstm_conv': conv_state, 'rwkv': s_rwkv, 'rwkv_shift': shift_state}
    return x1, x1b, new_st


def _trunk(x3, states, prev_valid, keep, lp):
    bsz, t, _ = x3.shape
    m = bsz * t
    xb2 = x3.reshape(m, D_MODEL).astype(BF16)
    collected = {name: [] for name in STATE_NAMES}
    for l in range(DEPTH):
        st = {name: states[name][l] for name in STATE_NAMES}
        st['keep'] = keep
        x1, x1b, new_st = _mixer(x3, xb2, st, prev_valid, lp, l)
        j = l // 2
        ln_g, ln_b = lp['ln2_g'], lp['ln2_b']
        if l % 2 == 0:
            x2, xb2 = _ffn(x1b, x1, lp['ffn_wg'], lp['ffn_wu'], lp['ffn_wd'], j, ln_g, ln_b, l)
        else:
            logits = _matmul(x1b, lp['router_w'][j], tm=1024, tn=N_EXPERTS) + lp['router_b'][j]
            top_v, top_i = lax.top_k(logits, TOP_K)
            probs = jax.nn.softmax(top_v, axis=-1)
            gates = jnp.einsum('mk,mke->me', probs, jax.nn.one_hot(top_i, N_EXPERTS, dtype=F32))
            x2, xb2 = _ffn(x1b, x1, lp['exp_wg'], lp['exp_wu'], lp['exp_wd'], j, ln_g, ln_b, l, gates)
        x3 = x2.reshape(bsz, t, D_MODEL)
        for name in STATE_NAMES:
            collected[name].append(new_st[name])
    return x3, {name: jnp.stack(collected[name]) for name in STATE_NAMES}


def kernel(x_prompt, x_sample, cache_swa_k, cache_swa_v, state_hgrn, state_mlstm_c, state_mlstm_n, state_mlstm_m, state_mlstm_conv, state_rwkv, state_rwkv_shift, w_in, lb_raw, gn_a, conv_w, conv_b, wq_b, wk_b, b_i, b_f, gn_b, skip_b, sinks, rel_bias, mu_d, w0_d, w_up_d, a0_d, a_up_d, g_up_d, k_k_d, k_a_d, r_k_d, gn_w_d, gn_b_d, w_br, w_o, ln1_g, ln1_b, ln2_g, ln2_b, ffn_w_gate, ffn_w_up, ffn_w_down, router_w, router_b, exp_w_gate, exp_w_up, exp_w_down):
    off_if = 4 * WIDTH + 3 * WIDTH
    off_c = off_if + 2 * N_HEADS
    off_d = off_c + WIDTH + 2 * KV_WIDTH
    off_gate = off_d + D_COLS
    w_main = jnp.concatenate([w_in[:, :, :4 * WIDTH], _d_to_kernel_order(w_in[:, :, off_d:off_gate]),
                              w_in[:, :, 4 * WIDTH:off_if], w_in[:, :, off_c:off_d], w_in[:, :, off_if:off_c],
                              jnp.zeros((DEPTH, D_MODEL, LANE_TILE - 2 * N_HEADS), w_in.dtype)], axis=-1).astype(BF16)
    w_gate = w_in[:, :, off_gate:].astype(BF16)

    lb = _lower_bounds(lb_raw)
    lb = lb[jnp.minimum(jnp.arange(DEPTH), lb.shape[0] - 1)]
    lbp = jnp.stack([jnp.log(jnp.maximum(lb, LB_FLOOR)), jnp.log1p(-lb), 1.0 - lb], axis=1)
    lbp = jnp.concatenate([lbp, jnp.zeros((DEPTH, 5, WIDTH), F32)], axis=1)

    w_qk = jnp.stack([jnp.concatenate([_block_diag(list(wq_b[l])), _block_diag(list(wk_b[l])) * HEAD_DIM ** -0.5],
                                      axis=1) for l in range(DEPTH)]).astype(BF16)
    w_wa = jnp.stack([_block_diag([w_up_d[l], a_up_d[l]]) for l in range(DEPTH)]).astype(BF16)
    vec_b = jnp.stack([_pad_rows(jnp.stack([conv_b[l], gn_b[l], skip_b[l]]), 8) for l in range(DEPTH)])
    conv_w8 = jnp.stack([_pad_rows(conv_w[l], 8) for l in range(DEPTH)])
    vec_d = jnp.stack([_pad_rows(jnp.stack([w0_d[l], a0_d[l], k_k_d[l], k_a_d[l], r_k_d[l], gn_w_d[l], gn_b_d[l]]), 8)
                       for l in range(DEPTH)])
    head_of = jnp.arange(WIDTH) // HEAD_DIM
    mask_bd = (head_of[:, None] == head_of[None, :]).astype(F32)

    span = WINDOW + CHUNK
    rel = jnp.arange(span)[None, :] - WINDOW - jnp.arange(CHUNK)[:, None]
    one_hot = (_rel_bucket(rel)[..., None] == jnp.arange(NUM_BUCKETS)).astype(F32)
    bias = jnp.einsum('ijb,bh->hij', one_hot, rel_bias.astype(F32), precision=HIGHEST)

    n_dense = ffn_w_gate.shape[0]
    ff_steps = D_FF // D_FF_EXPERT
    b_if = jnp.concatenate([b_i, b_f], axis=-1)
    lp = {
        'w_main': w_main, 'w_gate': w_gate, 'lbp': lbp, 'gn_a': gn_a[:, None, :],
        'w_if_t': jnp.swapaxes(w_in[:, :, off_if:off_c], 1, 2).astype(BF16),
        'mask_bd': mask_bd, 'conv_w8': conv_w8, 'vec_b': vec_b, 'w_qk': w_qk,
        'b_if_col': b_if[:, :, None], 'b_if_row': b_if[:, None, :], 'sinks': sinks[:, None, :], 'bias': bias,
        'mu_d': _d_to_kernel_order(mu_d)[:, None, :], 'vec_d': vec_d, 'w_wa': w_wa, 'g_up_d': g_up_d.astype(BF16),
        'w_br': w_br.astype(BF16), 'w_o': w_o.astype(BF16),
        'ln1_g': ln1_g[:, None, :], 'ln1_b': ln1_b[:, None, :], 'ln2_g': ln2_g[:, None, :], 'ln2_b': ln2_b[:, None, :],
        'ffn_wg': jnp.swapaxes(ffn_w_gate.astype(BF16).reshape(n_dense, D_MODEL, ff_steps, D_FF_EXPERT), 1, 2),
        'ffn_wu': jnp.swapaxes(ffn_w_up.astype(BF16).reshape(n_dense, D_MODEL, ff_steps, D_FF_EXPERT), 1, 2),
        'ffn_wd': ffn_w_down.astype(BF16).reshape(n_dense, ff_steps, D_FF_EXPERT, D_MODEL),
        'router_w': router_w.astype(BF16), 'router_b': router_b,
        'exp_wg': exp_w_gate.astype(BF16), 'exp_wu': exp_w_up.astype(BF16), 'exp_wd': exp_w_down.astype(BF16),
    }

    sample_states = {
        'swa_k': cache_swa_k, 'swa_v': cache_swa_v, 'hgrn': state_hgrn, 'mlstm_c': state_mlstm_c,
        'mlstm_n': state_mlstm_n, 'mlstm_m': state_mlstm_m, 'mlstm_conv': state_mlstm_conv,
        'rwkv': state_rwkv, 'rwkv_shift': state_rwkv_shift,
    }
    keep = cache_swa_k.shape[2]
    bp = x_prompt.shape[0]
    prompt_states = {}
    for name in STATE_NAMES:
        arr = sample_states[name]
        rows = (WINDOW,) + arr.shape[3:] if name in ('swa_k', 'swa_v') else arr.shape[2:]
        prompt_states[name] = jnp.zeros((DEPTH, bp) + tuple(rows), arr.dtype)

    y_prompt, pst = _trunk(x_prompt, prompt_states, False, keep, lp)
    y_sample, sst = _trunk(x_sample, sample_states, True, keep, lp)
    return (y_prompt, y_sample) + tuple(pst[n] for n in STATE_NAMES) + tuple(sst[n] for n in STATE_NAMES)
```

```python
import functools
import math

import jax
import jax.numpy as jnp
from jax import lax
from jax.experimental import pallas as pl
from jax.experimental.pallas import tpu as pltpu

F32 = jnp.float32
BF16 = jnp.bfloat16
HIGHEST = lax.Precision.HIGHEST

D_MODEL = 1024
DEPTH = 4
CHUNK = 64
HEAD_DIM = 64
N_HEADS = 4
WIDTH = N_HEADS * HEAD_DIM
C_KV_HEADS = 2
KV_WIDTH = C_KV_HEADS * HEAD_DIM
CONV_W = 4
WINDOW = 128
NUM_BUCKETS = 32
MAX_DISTANCE = 128
D_DECAY_LORA = 64
D_AAA_LORA = 64
D_GATE_LORA = 128
D_COLS = 3 * WIDTH + D_DECAY_LORA + D_AAA_LORA + D_GATE_LORA
N_BRANCH = 4
D_FF = 2816
N_EXPERTS = 8
TOP_K = 2
D_FF_EXPERT = 1408
ALPHA = (2 * DEPTH) ** 0.25
LN_EPS = 1e-5
HEAD_NORM_EPS = 1e-5
RWKV_GN_EPS = 64e-5
LB_FLOOR = 1e-30
LB_CEIL = 1.0 - 1e-6

GATE_COLS = N_BRANCH * D_MODEL
COL_A = 0
COL_D = COL_A + 4 * WIDTH
COL_B = COL_D + D_COLS
COL_C = COL_B + 3 * WIDTH
COL_IF = COL_C + WIDTH + 2 * KV_WIDTH
LANE_TILE = 128
MAIN_COLS = COL_IF + LANE_TILE
SUB = 16
STREAMS_PER_STEP = 4
V7X_VMEM_LIMIT = 48 * 1024 * 1024

NN = (((1,), (0,)), ((), ()))
NT = (((1,), (1,)), ((), ()))
TN = (((0,), (0,)), ((), ()))


def _layer_block(arr, layer):
    tail = arr.shape[1:]
    return pl.BlockSpec((None,) + tail, lambda *_: (layer,) + (0,) * len(tail))


def _cparams(*sem):
    return pltpu.CompilerParams(dimension_semantics=sem, vmem_limit_bytes=V7X_VMEM_LIMIT)


def _sigmoid(x):
    return 1.0 / (1.0 + jnp.exp(-x))


def _silu(x):
    return x * _sigmoid(x)


def _log_sigmoid(x):
    return jnp.minimum(x, 0.0) - jnp.log1p(jnp.exp(-jnp.abs(x)))


def _layernorm_rows(z, g, b):
    mu = jnp.mean(z, axis=-1, keepdims=True)
    zc = z - mu
    var = jnp.mean(zc * zc, axis=-1, keepdims=True)
    return zc * lax.rsqrt(var + LN_EPS) * g + b


def _split2(x):
    hi = x.astype(BF16)
    lo = (x - hi.astype(F32)).astype(BF16)
    return hi, lo


def _split3(x):
    hi = x.astype(BF16)
    r1 = x - hi.astype(F32)
    mid = r1.astype(BF16)
    lo = (r1 - mid.astype(F32)).astype(BF16)
    return hi, mid, lo


def _dot(a, b, dims=NN):
    return lax.dot_general(a, b, dims, preferred_element_type=F32)


def _mm2(a, b_parts, dims=NN):
    a_hi, a_lo = _split2(a)
    n = a.shape[0]
    both = _dot(jnp.concatenate([a_hi, a_lo], axis=0), b_parts[0], dims)
    return both[:n] + both[n:] + _dot(a_hi, b_parts[1], dims)


def _cumsum_rows(x):
    n = x.shape[0]
    row = lax.broadcasted_iota(jnp.int32, (n, n), 0)
    col = lax.broadcasted_iota(jnp.int32, (n, n), 1)
    tri = (row >= col).astype(BF16)
    hi, mid, lo = _split3(x)
    return _dot(tri, hi) + _dot(tri, mid) + _dot(tri, lo)


def _head_masks():
    lane_head = lax.broadcasted_iota(jnp.int32, (1, WIDTH), 1) // HEAD_DIM
    return [(lane_head == h) for h in range(N_HEADS)]


def _stack_heads(x, masks):
    return jnp.concatenate([jnp.where(mk, x, 0.0) for mk in masks], axis=0)


def _stack_parts(x, masks):
    parts = _split2(x)
    rows = [mk.astype(BF16) for mk in masks]
    return tuple(jnp.concatenate([p * r for r in rows], axis=0) for p in parts)


def _round_robin(stage_generators):
    live = list(stage_generators)
    while live:
        live = [g for g in live if next(g, _DONE) is not _DONE]


_DONE = object()


def _head_sums(x, ones_bd):
    hi, lo = _split2(x)
    return _dot(hi, ones_bd) + _dot(lo, ones_bd)


def _matmul_kernel(x_ref, w_ref, o_ref):
    o_ref[...] = _dot(x_ref[...], w_ref[...])


def _matmul(x, w, *, tm, tn):
    m, k = x.shape
    n = w.shape[1]
    tm = min(tm, m)
    tn = min(tn, n)
    return pl.pallas_call(
        _matmul_kernel,
        grid=(m // tm, n // tn),
        in_specs=[pl.BlockSpec((tm, k), lambda i, j: (i, 0)),
                  pl.BlockSpec((k, tn), lambda i, j: (0, j))],
        out_specs=pl.BlockSpec((tm, tn), lambda i, j: (i, j)),
        out_shape=jax.ShapeDtypeStruct((m, n), F32),
        compiler_params=_cparams("parallel", "parallel"),
        name="matmul",
    )(x, w)


PROJ_TN = 256


def _proj_kernel(x_ref, w_ref, o_ref):
    x = x_ref[...]
    n = w_ref.shape[2]
    for lo in range(0, n, PROJ_TN):
        hi = min(lo + PROJ_TN, n)
        o_ref[:, lo:hi] = _dot(x, w_ref[0, :, lo:hi])


def _proj(xb, w, layer, *, tm=512):
    m, k = xb.shape
    n = w.shape[2]
    tm = min(tm, m)
    return pl.pallas_call(
        _proj_kernel,
        grid=(m // tm,),
        in_specs=[pl.BlockSpec((tm, k), lambda i: (i, 0)),
                  pl.BlockSpec((1, k, n), lambda i: (layer, 0, 0), pipeline_mode=pl.Buffered(1))],
        out_specs=pl.BlockSpec((tm, n), lambda i: (i, 0)),
        out_shape=jax.ShapeDtypeStruct((m, n), F32),
        compiler_params=_cparams("parallel"),
        name="proj",
    )(xb, w)


def _head(a, h):
    return a[:, h * HEAD_DIM:(h + 1) * HEAD_DIM]


def _hgrn_kernel(q_ref, f_ref, i_ref, g_ref, lbp_ref, gn_ref, ones_ref, mask_ref, s0_ref, y_ref, s_out_ref,
                 st_ref, phi_ref, plo_ref, *, nc, bb):
    c = pl.program_id(1)

    @pl.when(c == 0)
    def _():
        for bi in range(bb):
            st_ref[bi] = jnp.zeros(st_ref.shape[1:], F32)
            for h in range(N_HEADS):
                st_ref[bi, h * HEAD_DIM:(h + 1) * HEAD_DIM, h * HEAD_DIM:(h + 1) * HEAD_DIM] = s0_ref[bi, h].T

    _round_robin([_hgrn_chunk(q_ref.at[bi], f_ref.at[bi], i_ref.at[bi], g_ref.at[bi], lbp_ref, gn_ref, ones_ref,
                              mask_ref, y_ref.at[bi], st_ref.at[bi], phi_ref.at[bi], plo_ref.at[bi])
                  for bi in range(bb)])

    @pl.when(c == nc - 1)
    def _():
        for bi in range(bb):
            for h in range(N_HEADS):
                s_out_ref[bi, h] = st_ref[bi, h * HEAD_DIM:(h + 1) * HEAD_DIM, h * HEAD_DIM:(h + 1) * HEAD_DIM].T


def _hgrn_chunk(q_ref, f_ref, i_ref, g_ref, lbp_ref, gn_ref, ones_ref, mask_ref, y_ref, st_ref, phi_ref, plo_ref):
    L = CHUNK
    ones_bd = ones_ref[...]
    masks = _head_masks()
    log_lb = lbp_ref[0:1, :]
    log1m_lb = lbp_ref[1:2, :]
    one_m_lb = lbp_ref[2:3, :]
    zf = f_ref[...]
    u = log_lb
    w = log1m_lb + _log_sigmoid(zf)
    logf = jnp.maximum(u, w) + jnp.log1p(jnp.exp(-jnp.abs(u - w)))
    k = one_m_lb * _sigmoid(-zf)
    q = _silu(q_ref[...])
    v = i_ref[...]

    b = _cumsum_rows(logf)
    yield
    b_last = b[L - 1:L]
    qe = q * jnp.exp(b)
    kdec = k * jnp.exp(b_last - b)
    sub_row = lax.broadcasted_iota(jnp.int32, (SUB, WIDTH), 0)

    for s in range(L):
        r0 = (s // SUB) * SUB
        p = q[r0:r0 + SUB] * k[s:s + 1] * jnp.exp(jnp.minimum(b[r0:r0 + SUB] - b[s:s + 1], 0.0))
        hi, lo = _split2(jnp.where(sub_row >= s - r0, p, 0.0))
        phi_ref[s * SUB:(s + 1) * SUB, :] = hi
        plo_ref[s * SUB:(s + 1) * SUB, :] = lo
    yield
    att = _dot(phi_ref[...], ones_bd) + _dot(plo_ref[...], ones_bd)
    st = st_ref[...]
    o_inter = _dot(qe, st, NT)
    upd = _dot(v, kdec, TN)
    scores = []
    for blk in range(1, L // SUB):
        r0 = blk * SUB
        ref_row = b[r0 - 1:r0]
        q_s = q[r0:r0 + SUB] * jnp.exp(b[r0:r0 + SUB] - ref_row)
        k_s = k[:r0] * jnp.exp(ref_row - b[:r0])
        scores.append(_dot(q_s, _stack_heads(k_s, masks), NT))
    yield
    st_ref[...] = st * jnp.exp(b_last) + upd * mask_ref[...]
    o_blocks = []
    for blk in range(L // SUB):
        r0 = blk * SUB
        acc = jnp.zeros((SUB, WIDTH), F32)
        for j in range(SUB):
            s = r0 + j
            acc = acc + att[s * SUB:(s + 1) * SUB] * v[s:s + 1]
        if blk > 0:
            acc = acc + _dot(scores[blk - 1], _stack_heads(v[:r0], masks))
        o_blocks.append(acc)
    yield
    o = jnp.concatenate(o_blocks, axis=0) + o_inter
    ms = _head_sums(o * o, ones_bd) * (1.0 / HEAD_DIM)
    yield
    y_ref[...] = (o * lax.rsqrt(ms + HEAD_NORM_EPS) * gn_ref[...] * _silu(g_ref[...])).astype(y_ref.dtype)


def _hgrn(h3, lbp, gn, layer, mask_bd, s0):
    bsz, t, _ = h3.shape
    nc = t // CHUNK
    cb = COL_A // WIDTH
    bb = STREAMS_PER_STEP

    def col(j):
        return pl.BlockSpec((bb, CHUNK, WIDTH), lambda b, c, j=j: (b, c, cb + j))

    square = pl.BlockSpec((WIDTH, WIDTH), lambda b, c: (0, 0))
    state = pl.BlockSpec((bb, N_HEADS, HEAD_DIM, HEAD_DIM), lambda b, c: (b, 0, 0, 0))
    return pl.pallas_call(
        functools.partial(_hgrn_kernel, nc=nc, bb=bb),
        grid=(bsz // bb, nc),
        in_specs=[col(0), col(1), col(2), col(3),
                  _layer_block(lbp, layer), _layer_block(gn, layer),
                  square, square, state],
        out_specs=[pl.BlockSpec((bb, CHUNK, WIDTH), lambda b, c: (b, c, 0)), state],
        out_shape=[jax.ShapeDtypeStruct((bsz, t, WIDTH), BF16),
                   jax.ShapeDtypeStruct((bsz, N_HEADS, HEAD_DIM, HEAD_DIM), F32)],
        scratch_shapes=[pltpu.VMEM((bb, WIDTH, WIDTH), F32),
                        pltpu.VMEM((bb, CHUNK * SUB, WIDTH), BF16),
                        pltpu.VMEM((bb, CHUNK * SUB, WIDTH), BF16)],
        compiler_params=_cparams("parallel", "arbitrary"),
        name="hgrn2",
    )(h3, h3, h3, h3, lbp, gn, mask_bd.astype(BF16), mask_bd, s0)


def _mlstm_kernel(u_ref, v_ref, o_ref, gcol_ref, xin_ref, wift_ref, bcol_ref, brow_ref, cw_ref, vec_ref, wqk_ref,
                  ones_ref, mask_ref, c0_ref, n0_ref, m0_ref, conv0_ref,
                  y_ref, c_out_ref, n_out_ref, m_out_ref, conv_out_ref,
                  c_ref, n_ref, m_ref, carry_ref, *, nc, bb):
    c = pl.program_id(1)

    @pl.when(c == 0)
    def _():
        for bi in range(bb):
            c_ref[bi] = jnp.zeros(c_ref.shape[1:], F32)
            for h in range(N_HEADS):
                c_ref[bi, h * HEAD_DIM:(h + 1) * HEAD_DIM, h * HEAD_DIM:(h + 1) * HEAD_DIM] = c0_ref[bi, h]
        n_ref[...] = n0_ref[...]
        m_ref[...] = m0_ref[...]
        carry_ref[...] = conv0_ref[...]

    _round_robin([_mlstm_chunk(u_ref.at[bi], v_ref.at[bi], o_ref.at[bi], gcol_ref.at[bi], xin_ref.at[bi],
                               wift_ref, bcol_ref, brow_ref, cw_ref, vec_ref, wqk_ref, ones_ref, mask_ref, y_ref.at[bi],
                               c_ref.at[bi], n_ref.at[bi], m_ref.at[bi], carry_ref.at[bi]) for bi in range(bb)])

    @pl.when(c == nc - 1)
    def _():
        for bi in range(bb):
            for h in range(N_HEADS):
                c_out_ref[bi, h] = c_ref[bi, h * HEAD_DIM:(h + 1) * HEAD_DIM, h * HEAD_DIM:(h + 1) * HEAD_DIM]
        n_out_ref[...] = n_ref[...]
        m_out_ref[...] = m_ref[...]
        conv_out_ref[...] = carry_ref[...]


def _mlstm_chunk(u_ref, v_ref, o_ref, gcol_ref, xin_ref, wift_ref, bcol_ref, brow_ref, cw_ref, vec_ref, wqk_ref,
                 ones_ref, mask_ref, y_ref, c_ref, n_ref, m_ref, carry_ref):
    L = CHUNK
    heads = range(N_HEADS)

    u = u_ref[...]
    carry = carry_ref[...]
    row8 = lax.broadcasted_iota(jnp.int32, (8, WIDTH), 0)
    conv = vec_ref[0:1, :] + cw_ref[CONV_W - 1:CONV_W, :] * u
    for d in range(1, CONV_W):
        rolled = pltpu.roll(u, d, 0)
        top = jnp.where(row8 < d, pltpu.roll(carry, d, 0), rolled[0:8])
        conv = conv + cw_ref[CONV_W - 1 - d:CONV_W - d, :] * jnp.concatenate([top, rolled[8:]], axis=0)
    carry_ref[...] = u[L - 8:L]
    cact = _silu(conv)
    qk_all = _dot(cact.astype(BF16), wqk_ref[...])

    row = lax.broadcasted_iota(jnp.int32, (L, L), 0)
    col = lax.broadcasted_iota(jnp.int32, (L, L), 1)
    lower = row >= col
    raw_c = gcol_ref[:, 0:2 * N_HEADS] + brow_ref[...]
    lane8 = lax.broadcasted_iota(jnp.int32, raw_c.shape, 1)
    gcol = jnp.where(lane8 < N_HEADS, raw_c, _log_sigmoid(raw_c))
    raw_r = _dot(wift_ref[...], xin_ref[...], NT) + bcol_ref[...]
    sub8 = lax.broadcasted_iota(jnp.int32, raw_r.shape, 0)
    grow = jnp.where(sub8 < N_HEADS, raw_r, _log_sigmoid(raw_r))
    f_col = jnp.dot(lower.astype(F32), gcol, preferred_element_type=F32, precision=HIGHEST)
    f_row = jnp.dot(grow, (row <= col).astype(F32), preferred_element_type=F32, precision=HIGHEST)
    r8 = lax.broadcasted_iota(jnp.int32, (2 * N_HEADS, 2 * WIDTH), 0)
    c8 = lax.broadcasted_iota(jnp.int32, (2 * N_HEADS, 2 * WIDTH), 1)
    expand = (c8 // HEAD_DIM == r8).astype(BF16)
    x8 = jnp.where(lane8 < N_HEADS, gcol, f_col)
    x_hi, x_mid, x_lo = _split3(x8)
    both = _dot(x_hi, expand) + _dot(x_mid, expand) + _dot(x_lo, expand)
    li_all = both[:, 0:WIDTH]
    f_all = both[:, WIDTH:2 * WIDTH]
    d_row = jnp.concatenate([grow[h:h + 1, :] - f_row[N_HEADS + h:N_HEADS + h + 1, :] for h in heads], axis=1)
    yield
    t_idx = lax.broadcasted_iota(jnp.int32, (L, WIDTH), 0)
    s_idx = lax.broadcasted_iota(jnp.int32, (L, WIDTH), 1) % L
    ones_bd = ones_ref[...]
    masks = _head_masks()
    cm = li_all - f_all
    for sh in (1, 2, 4, 8, 16, 32):
        cm = jnp.maximum(cm, jnp.where(t_idx >= sh, pltpu.roll(cm, sh, 0), -jnp.inf))
    g = f_all + m_ref[...]
    mt = jnp.maximum(g, f_all + cm)
    wg = jnp.exp(g - mt)
    wd = jnp.exp(jnp.where(s_idx <= t_idx, f_all + d_row - mt, -jnp.inf))
    q = qk_all[:, 0:WIDTH]
    k = qk_all[:, WIDTH:2 * WIDTH]
    v = v_ref[...]
    q_bf = q.astype(BF16)
    v_bf = v.astype(BF16)
    head_rows = [mk.astype(BF16) for mk in masks]
    k_stack = jnp.concatenate([k.astype(BF16) * r for r in head_rows], axis=0)
    v_stack = jnp.concatenate([v_bf * r for r in head_rows], axis=0)
    c_bd = c_ref[...]
    n_row = n_ref[...]
    qk = _dot(q_bf, k_stack, NT) * wd
    q_c = _dot(q_bf, c_bd.astype(BF16))
    q_n = _head_sums(q * n_row, ones_bd)
    yield
    qkv = _dot(qk.astype(BF16), v_stack)
    qk_sum = _head_sums(qk, ones_bd)
    mt_last = mt[L - 1:L]
    kw = k * jnp.exp(f_all[L - 1:L] - f_all + li_all - mt_last)
    c_upd = _dot(kw.astype(BF16), v_bf, TN)
    yield
    wgl = wg[L - 1:L]
    c_ref[...] = wgl * c_bd + c_upd * mask_ref[...]
    n_ref[...] = wgl * n_row + jnp.sum(kw, axis=0, keepdims=True)
    m_ref[...] = mt_last
    den = wg * q_n + qk_sum
    hh = (wg * q_c + qkv) / jnp.maximum(jnp.abs(den), jnp.exp(-mt))
    z = _sigmoid(o_ref[...]) * hh
    mu = _head_sums(z, ones_bd) * (1.0 / HEAD_DIM)
    yield
    zc = z - mu
    var = _head_sums(zc * zc, ones_bd) * (1.0 / HEAD_DIM)
    yield
    y_ref[...] = (zc * lax.rsqrt(var + HEAD_NORM_EPS) * vec_ref[1:2, :] + vec_ref[2:3, :] * cact).astype(y_ref.dtype)


def _mlstm(h3, xb3, w_if_t, b_col, b_row, conv_w, vecs, w_qk, layer, mask_bd, c0, n0, m0, conv0):
    bsz, t, _ = h3.shape
    nc = t // CHUNK
    cb = COL_B // WIDTH
    bb = STREAMS_PER_STEP
    state4 = pl.BlockSpec((bb, N_HEADS, HEAD_DIM, HEAD_DIM), lambda b, c: (b, 0, 0, 0))
    state_n = pl.BlockSpec((bb, 1, WIDTH), lambda b, c: (b, 0, 0))
    state_m = state_n
    state_conv = pl.BlockSpec((bb, 8, WIDTH), lambda b, c: (b, 0, 0))

    def const(shape):
        return pl.BlockSpec(shape, lambda b, c: tuple(0 for _ in shape))

    def col(j):
        return pl.BlockSpec((bb, CHUNK, WIDTH), lambda b, c, j=j: (b, c, cb + j))

    return pl.pallas_call(
        functools.partial(_mlstm_kernel, nc=nc, bb=bb),
        grid=(bsz // bb, nc),
        in_specs=[col(0), col(1), col(2),
                  pl.BlockSpec((bb, CHUNK, LANE_TILE), lambda b, c: (b, c, COL_IF // LANE_TILE)),
                  pl.BlockSpec((bb, CHUNK, D_MODEL), lambda b, c: (b, c, 0)), _layer_block(w_if_t, layer),
                  _layer_block(b_col, layer), _layer_block(b_row, layer), _layer_block(conv_w, layer),
                  _layer_block(vecs, layer), _layer_block(w_qk, layer), const((WIDTH, WIDTH)), const((WIDTH, WIDTH)),
                  state4, state_n, state_m, state_conv],
        out_specs=[pl.BlockSpec((bb, CHUNK, WIDTH), lambda b, c: (b, c, 0)), state4, state_n, state_m, state_conv],
        out_shape=[jax.ShapeDtypeStruct((bsz, t, WIDTH), BF16),
                   jax.ShapeDtypeStruct((bsz, N_HEADS, HEAD_DIM, HEAD_DIM), F32),
                   jax.ShapeDtypeStruct((bsz, 1, WIDTH), F32),
                   jax.ShapeDtypeStruct((bsz, 1, WIDTH), F32),
                   jax.ShapeDtypeStruct((bsz, 8, WIDTH), F32)],
        scratch_shapes=[pltpu.VMEM((bb, WIDTH, WIDTH), F32),
                        pltpu.VMEM((bb, 1, WIDTH), F32),
                        pltpu.VMEM((bb, 1, WIDTH), F32),
                        pltpu.VMEM((bb, 8, WIDTH), F32)],
        compiler_params=_cparams("parallel", "arbitrary"),
        name="mlstm",
    )(h3, h3, h3, h3, xb3, w_if_t, b_col, b_row, conv_w, vecs, w_qk, mask_bd.astype(BF16), mask_bd, c0, n0, m0, conv0)


SWA_CHUNKS_PER_STEP = 8


def _swa_kernel(*refs, prev_valid, cps, multi_step):
    if multi_step:
        (q_ref, kc_ref, vc_ref, kp_ref, vp_ref, k_ref, v_ref, bias_ref, sink_ref, y_ref, s_ref, p_ref) = refs
    else:
        (q_ref, kc_ref, vc_ref, k_ref, v_ref, bias_ref, sink_ref, y_ref, s_ref, p_ref) = refs
    g = pl.program_id(1)
    L = CHUNK
    span = WINDOW + L
    k_win = kc_ref[0]
    v_win = vc_ref[0]
    if multi_step:
        k_win = jnp.where(g == 0, k_win, kp_ref[0])
        v_win = jnp.where(g == 0, v_win, vp_ref[0])
    kcat = jnp.concatenate([k_win, k_ref[0]], axis=0).astype(BF16)
    vcat = jnp.concatenate([v_win, v_ref[0]], axis=0).astype(BF16)
    kv_of = [h // (N_HEADS // C_KV_HEADS) for h in range(N_HEADS)]
    heads = range(N_HEADS)

    def chunk(j):
        for _ in range(j):
            yield
        q = q_ref[0, j * L:(j + 1) * L, :].astype(BF16)
        k_j = kcat[j * L:j * L + span]
        for h in heads:
            s_ref[j * N_HEADS + h] = _dot(_head(q, h), _head(k_j, kv_of[h]), NT)
        yield
        key_pos = lax.broadcasted_iota(jnp.int32, (L, span), 1) + (g * cps + j) * L
        s = [s_ref[j * N_HEADS + h] * HEAD_DIM ** -0.5 + bias_ref[h] for h in heads]
        if not prev_valid:
            s = [jnp.where(key_pos >= WINDOW, s_h, -jnp.inf) for s_h in s]
        sink = [sink_ref[0:1, h:h + 1] for h in heads]
        m = [jnp.maximum(jnp.max(s[h], axis=1, keepdims=True), sink[h]) for h in heads]
        yield
        p = [jnp.exp(s[h] - m[h]) for h in heads]
        denom = [jnp.sum(p[h], axis=1, keepdims=True) + jnp.exp(sink[h] - m[h]) for h in heads]
        yield
        for h in heads:
            p_ref[j * N_HEADS + h] = (p[h] * (1.0 / denom[h])).astype(BF16)
        yield
        v_j = vcat[j * L:j * L + span]
        o = [_dot(p_ref[j * N_HEADS + h], _head(v_j, kv_of[h])) for h in heads]
        y_ref[0, j * L:(j + 1) * L, :] = jnp.concatenate(o, axis=1).astype(y_ref.dtype)

    _round_robin([chunk(j) for j in range(cps)])


def _swa(h3, cache_k, cache_v, bias, sinks, layer, prev_valid):
    bsz, t, _ = h3.shape
    nc = t // CHUNK
    cps = min(SWA_CHUNKS_PER_STEP, nc)
    rows = cps * CHUNK
    steps = nc // cps
    multi_step = steps > 1
    assert nc % cps == 0 and (not multi_step or rows % WINDOW == 0), (t, cps)
    qb = COL_C // WIDTH
    kb = (COL_C + WIDTH) // KV_WIDTH
    cache = pl.BlockSpec((1, WINDOW, KV_WIDTH), lambda b, g: (b, 0, 0))

    def own(col):
        return pl.BlockSpec((1, rows, KV_WIDTH), lambda b, g, col=col: (b, g, col))

    def before(col):
        return pl.BlockSpec((1, WINDOW, KV_WIDTH),
                            lambda b, g, col=col: (b, jnp.maximum(g * (rows // WINDOW) - 1, 0), col))

    kv_specs = [cache, cache] + ([before(kb), before(kb + 1)] if multi_step else []) + [own(kb), own(kb + 1)]
    kv_args = [cache_k, cache_v] + ([h3, h3] if multi_step else []) + [h3, h3]
    return pl.pallas_call(
        functools.partial(_swa_kernel, prev_valid=prev_valid, cps=cps, multi_step=multi_step),
        grid=(bsz, steps),
        in_specs=[pl.BlockSpec((1, rows, WIDTH), lambda b, g: (b, g, qb))] + kv_specs + [
                  pl.BlockSpec((N_HEADS, CHUNK, WINDOW + CHUNK), lambda b, g: (0, 0, 0)),
                  _layer_block(sinks, layer)],
        out_specs=pl.BlockSpec((1, rows, WIDTH), lambda b, g: (b, g, 0)),
        out_shape=jax.ShapeDtypeStruct((bsz, t, WIDTH), BF16),
        scratch_shapes=[pltpu.VMEM((cps * N_HEADS, CHUNK, WINDOW + CHUNK), F32),
                        pltpu.VMEM((cps * N_HEADS, CHUNK, WINDOW + CHUNK), BF16)],
        compiler_params=_cparams("parallel", "parallel"),
        name="swa",
    )(h3, *kv_args, bias, sinks)


def _rwkv_kernel(pd_ref, mu_ref, vec_ref, wwa_ref, gup_ref, ones_ref, mask_ref, s0_ref, sh0_ref,
                 y_ref, s_out_ref, sh_out_ref, st_ref, carry_ref, *, nc, bb):
    c = pl.program_id(1)

    @pl.when(c == 0)
    def _():
        for bi in range(bb):
            st_ref[bi] = jnp.zeros(st_ref.shape[1:], F32)
            for h in range(N_HEADS):
                st_ref[bi, h * HEAD_DIM:(h + 1) * HEAD_DIM, h * HEAD_DIM:(h + 1) * HEAD_DIM] = s0_ref[bi, h]
            carry_ref[bi, 7:8, :] = sh0_ref[bi]

    _round_robin([_rwkv_chunk(pd_ref.at[bi], mu_ref, vec_ref, wwa_ref, gup_ref, ones_ref, mask_ref,
                              y_ref.at[bi], st_ref.at[bi], carry_ref.at[bi]) for bi in range(bb)])

    @pl.when(c == nc - 1)
    def _():
        for bi in range(bb):
            for h in range(N_HEADS):
                s_out_ref[bi, h] = st_ref[bi, h * HEAD_DIM:(h + 1) * HEAD_DIM, h * HEAD_DIM:(h + 1) * HEAD_DIM]
            sh_out_ref[bi] = carry_ref[bi, 7:8, :]


def _rwkv_chunk(pd_ref, mu_ref, vec_ref, wwa_ref, gup_ref, ones_ref, mask_ref, y_ref, st_ref, carry_ref):
    L = CHUNK
    ones_bd = ones_ref[...]
    masks = _head_masks()

    pd = pd_ref[...]
    rolled = pltpu.roll(pd, 1, 0)
    row8 = lax.broadcasted_iota(jnp.int32, (8, D_COLS), 0)
    top = jnp.where(row8 == 0, carry_ref[7:8, :], rolled[0:8])
    pd_prev = jnp.concatenate([top, rolled[8:]], axis=0)
    carry_ref[...] = pd[L - 8:L]
    mixed = pd + (pd_prev - pd) * mu_ref[...]
    r = mixed[:, 0:WIDTH]
    k_raw = mixed[:, WIDTH:2 * WIDTH]
    v = mixed[:, 2 * WIDTH:3 * WIDTH]
    wa = mixed[:, 3 * WIDTH:3 * WIDTH + 2 * D_DECAY_LORA]
    g_in = mixed[:, 3 * WIDTH + 2 * D_DECAY_LORA:D_COLS]
    lane_wa = lax.broadcasted_iota(jnp.int32, wa.shape, 1)
    wa_act = jnp.where(lane_wa < D_DECAY_LORA, jnp.tanh(wa), wa)
    lora = _dot(wa_act.astype(BF16), wwa_ref[...])
    g_d = _dot(_sigmoid(g_in).astype(BF16), gup_ref[...])
    yield
    z = -(vec_ref[0:1, :] + lora[:, 0:WIDTH])
    w_log = -(jnp.maximum(z, 0.0) + jnp.log1p(jnp.exp(-jnp.abs(z)))) - 0.5
    lw = -jnp.exp(w_log)
    a_d = _sigmoid(vec_ref[1:2, :] + lora[:, WIDTH:2 * WIDTH])
    kk = k_raw * vec_ref[2:3, :]
    kk_norm = _head_sums(kk * kk, ones_bd)
    cw = _cumsum_rows(lw)
    yield
    kk = kk * lax.rsqrt(jnp.maximum(kk_norm, 1e-24))
    k = k_raw * (1.0 + (a_d - 1.0) * vec_ref[3:4, :])
    a = -kk
    b = kk * a_d
    g_in_c = jnp.exp(cw)
    g_inv = jnp.exp(-cw)
    at = a * jnp.exp(cw - lw)
    rt = r * g_in_c
    bt = b * g_inv
    kt = k * g_inv
    x = jnp.concatenate([at, rt], axis=0)
    bt_st = _stack_parts(bt, masks)
    kt_st = _stack_parts(kt, masks)
    gram_b = _mm2(x, bt_st, NT)
    gram_k = _mm2(x, kt_st, NT)
    st = st_ref[...]
    u = _mm2(x, _split2(st), NT)
    yield

    t_idx = lax.broadcasted_iota(jnp.int32, (L, WIDTH), 0)
    i_idx = lax.broadcasted_iota(jnp.int32, (L, WIDTH), 1) % L
    strict = i_idx < t_idx
    incl = i_idx <= t_idx
    same_blk = (i_idx // SUB) == (t_idx // SUB)
    eye = (i_idx == t_idx).astype(F32)
    n_all = jnp.where(strict, gram_b[0:L], 0.0)
    m_all = jnp.where(strict, gram_k[0:L], 0.0)
    rb_all = jnp.where(incl, gram_b[L:2 * L], 0.0)
    rk_all = jnp.where(incl, gram_k[L:2 * L], 0.0)
    n_d = jnp.where(same_blk, n_all, 0.0)
    n_off = jnp.where(same_blk, 0.0, n_all)

    v_st = _stack_parts(v, masks)
    x2 = _mm2(n_d, _stack_parts(n_d, masks))
    mv_rkv = _mm2(jnp.concatenate([m_all, rk_all], axis=0), v_st)
    rhs = u[0:L] + mv_rkv[0:L]
    bonus = _head_sums(r * k * vec_ref[4:5, :], ones_bd) * v
    yield
    t_d = eye + n_d
    both = _mm2(jnp.concatenate([t_d, x2], axis=0), _stack_parts(x2, masks))
    t_d = t_d + both[0:L]
    x4 = both[L:2 * L]
    yield
    both = _mm2(jnp.concatenate([t_d, x4], axis=0), _stack_parts(x4, masks))
    t_d = t_d + both[0:L]
    x8 = both[L:2 * L]
    yield
    t_d = t_d + _mm2(t_d, _stack_parts(x8, masks))
    yield
    n1 = _mm2(t_d, _stack_parts(n_off, masks))
    yield
    n1_st = _stack_parts(n1, masks)
    n2 = _mm2(n1, n1_st)
    yield
    a2 = eye + n1 + n2 + _mm2(n2, n1_st)
    yield
    t_full = _mm2(a2, _stack_parts(t_d, masks))
    yield
    sa = _mm2(t_full, _stack_parts(rhs, masks))
    yield
    y = u[L:2 * L] + _mm2(rb_all, _stack_parts(sa, masks)) + mv_rkv[L:2 * L]
    sv_hi, sv_lo = _split2(jnp.concatenate([sa, v], axis=0))
    bk_hi, bk_lo = _split2(jnp.concatenate([bt, kt], axis=0))
    upd = _dot(sv_hi, bk_hi, TN) + _dot(sv_lo, bk_hi, TN) + _dot(sv_hi, bk_lo, TN)
    yield
    st_ref[...] = (st + upd * mask_ref[...]) * g_in_c[L - 1:L]

    mu = _head_sums(y, ones_bd) * (1.0 / HEAD_DIM)
    yield
    yc = y - mu
    var = _head_sums(yc * yc, ones_bd) * (1.0 / HEAD_DIM)
    yield
    o = yc * lax.rsqrt(var + RWKV_GN_EPS) * vec_ref[5:6, :] + vec_ref[6:7, :]
    y_ref[...] = ((o + bonus) * g_d).astype(y_ref.dtype)


def _rwkv(h3, mu, vecs, w_wa, g_up, layer, mask_bd, s0, shift0):
    bsz, t, _ = h3.shape
    nc = t // CHUNK
    bb = STREAMS_PER_STEP
    state = pl.BlockSpec((bb, N_HEADS, HEAD_DIM, HEAD_DIM), lambda g, i: (g, 0, 0, 0))
    shift = pl.BlockSpec((bb, 1, D_COLS), lambda g, i: (g, 0, 0))

    def const(shape):
        return pl.BlockSpec(shape, lambda g, i: tuple(0 for _ in shape))

    return pl.pallas_call(
        functools.partial(_rwkv_kernel, nc=nc, bb=bb),
        grid=(bsz // bb, nc),
        in_specs=[pl.BlockSpec((bb, CHUNK, D_COLS), lambda g, i: (g, i, COL_D // D_COLS)),
                  _layer_block(mu, layer), _layer_block(vecs, layer), _layer_block(w_wa, layer),
                  _layer_block(g_up, layer), const((WIDTH, WIDTH)), const((WIDTH, WIDTH)), state, shift],
        out_specs=[pl.BlockSpec((bb, CHUNK, WIDTH), lambda g, i: (g, i, 0)), state, shift],
        out_shape=[jax.ShapeDtypeStruct((bsz, t, WIDTH), BF16),
                   jax.ShapeDtypeStruct((bsz, N_HEADS, HEAD_DIM, HEAD_DIM), F32),
                   jax.ShapeDtypeStruct((bsz, 1, D_COLS), F32)],
        scratch_shapes=[pltpu.VMEM((bb, WIDTH, WIDTH), F32), pltpu.VMEM((bb, 8, D_COLS), F32)],
        compiler_params=_cparams("parallel", "arbitrary"),
        name="rwkv7",
    )(h3, mu, vecs, w_wa, g_up, mask_bd.astype(BF16), mask_bd, s0, shift0)


def _merge_kernel(ya_ref, yb_ref, yc_ref, yd_ref, xb_ref, wg_ref, wbr_ref, wo_ref, x_ref,
                  lng_ref, lnb_ref, xo_ref, xob_ref):
    ys = (ya_ref, yb_ref, yc_ref, yd_ref)
    xb = xb_ref[...]
    merged = None
    for n in range(N_BRANCH):
        gate = _dot(xb, wg_ref[0, :, n * D_MODEL:(n + 1) * D_MODEL])
        term = _sigmoid(gate) * _dot(ys[n][...], wbr_ref[0, n])
        merged = term if merged is None else merged + term
    out = _dot(merged.astype(BF16), wo_ref[0])
    xn = _layernorm_rows(ALPHA * x_ref[...] + out, lng_ref[...], lnb_ref[...])
    xo_ref[...] = xn
    xob_ref[...] = xn.astype(BF16)


def _merge(ys, xb2, w_gate, wbr, wo, layer, x2, lng, lnb, *, tm=512):
    m = x2.shape[0]
    tm = min(tm, m)
    ysp = pl.BlockSpec((tm, WIDTH), lambda i: (i, 0))
    row = pl.BlockSpec((tm, D_MODEL), lambda i: (i, 0))
    vec = _layer_block(lng, layer)
    resident = dict(pipeline_mode=pl.Buffered(1))
    return pl.pallas_call(
        _merge_kernel,
        grid=(m // tm,),
        in_specs=[ysp, ysp, ysp, ysp, row,
                  pl.BlockSpec((1, D_MODEL, GATE_COLS), lambda i: (layer, 0, 0), **resident),
                  pl.BlockSpec((1, N_BRANCH, WIDTH, D_MODEL), lambda i: (layer, 0, 0, 0), **resident),
                  pl.BlockSpec((1, D_MODEL, D_MODEL), lambda i: (layer, 0, 0), **resident),
                  row, vec, vec],
        out_specs=[row, row],
        out_shape=[jax.ShapeDtypeStruct((m, D_MODEL), F32), jax.ShapeDtypeStruct((m, D_MODEL), BF16)],
        compiler_params=_cparams("parallel"),
        name="merge",
    )(*ys, xb2, w_gate, wbr, wo, x2, lng, lnb)


def _ffn_kernel(*refs, n_steps, gated):
    if gated:
        (xb_ref, x_ref, rw_ref, rb_ref, wg_ref, wu_ref, wd_ref, lng_ref, lnb_ref, xo_ref, xob_ref,
         acc_ref, gates_ref) = refs
    else:
        xb_ref, x_ref, wg_ref, wu_ref, wd_ref, lng_ref, lnb_ref, xo_ref, xob_ref, acc_ref = refs
    j = pl.program_id(1)

    @pl.when(j == 0)
    def _():
        acc_ref[...] = jnp.zeros_like(acc_ref)
        if gated:
            logits = _dot(xb_ref[...], rw_ref[...]) + rb_ref[...]
            lane = lax.broadcasted_iota(jnp.int32, logits.shape, 1).astype(F32)
            v1 = jnp.max(logits, axis=1, keepdims=True)
            i1 = jnp.min(jnp.where(logits == v1, lane, float(N_EXPERTS)), axis=1, keepdims=True)
            rest = jnp.where(lane == i1, -jnp.inf, logits)
            v2 = jnp.max(rest, axis=1, keepdims=True)
            i2 = jnp.min(jnp.where(rest == v2, lane, float(N_EXPERTS)), axis=1, keepdims=True)
            e2 = jnp.exp(v2 - v1)
            total = 1.0 + e2
            gates_ref[...] = jnp.where(lane == i1, 1.0 / total, 0.0) + jnp.where(lane == i2, e2 / total, 0.0)

    xb = xb_ref[...]
    hg = _dot(xb, wg_ref[0, 0])
    hu = _dot(xb, wu_ref[0, 0])
    part = _dot((_silu(hg) * hu).astype(BF16), wd_ref[0, 0])
    if gated:
        gates = gates_ref[...]
        lane = lax.broadcasted_iota(jnp.int32, gates.shape, 1)
        part = jnp.sum(jnp.where(lane == j, gates, 0.0), axis=1, keepdims=True) * part
    acc_ref[...] += part

    @pl.when(j == n_steps - 1)
    def _():
        xn = _layernorm_rows(ALPHA * x_ref[...] + acc_ref[...], lng_ref[...], lnb_ref[...])
        xo_ref[...] = xn
        xob_ref[...] = xn.astype(BF16)


def _ffn(xb, x2, wg, wu, wd, layer, lng, lnb, ln_layer, router=None, *, tm=512):
    m = x2.shape[0]
    tm = min(tm, m)
    _, n_steps, _, tf = wg.shape
    row = pl.BlockSpec((tm, D_MODEL), lambda i, j: (i, 0))
    vec = _layer_block(lng, ln_layer)
    w_in_spec = pl.BlockSpec((1, 1, D_MODEL, tf), lambda i, j: (layer, j, 0, 0))
    w_out_spec = pl.BlockSpec((1, 1, tf, D_MODEL), lambda i, j: (layer, j, 0, 0))
    in_specs = [row, row]
    args = [xb, x2]
    scratch = [pltpu.VMEM((tm, D_MODEL), F32)]
    if router is not None:
        in_specs += [_layer_block(router[0], layer), _layer_block(router[1], layer)]
        args += list(router)
        scratch.append(pltpu.VMEM((tm, N_EXPERTS), F32))
    in_specs += [w_in_spec, w_in_spec, w_out_spec, vec, vec]
    args += [wg, wu, wd, lng, lnb]
    return pl.pallas_call(
        functools.partial(_ffn_kernel, n_steps=n_steps, gated=router is not None),
        grid=(m // tm, n_steps),
        in_specs=in_specs,
        out_specs=[row, row],
        out_shape=[jax.ShapeDtypeStruct((m, D_MODEL), F32), jax.ShapeDtypeStruct((m, D_MODEL), BF16)],
        scratch_shapes=scratch,
        compiler_params=_cparams("parallel", "arbitrary"),
        name="moe" if router is not None else "ffn",
    )(*args)


STATE_NAMES = ('swa_k', 'swa_v', 'hgrn', 'mlstm_c', 'mlstm_n', 'mlstm_m', 'mlstm_conv', 'rwkv', 'rwkv_shift')

_D_ORIG = (('r', WIDTH), ('w', D_DECAY_LORA), ('k', WIDTH), ('v', WIDTH), ('a', D_AAA_LORA), ('g', D_GATE_LORA))
_D_KERNEL = ('r', 'k', 'v', 'w', 'a', 'g')


def _d_pieces(arr):
    out, off = {}, 0
    for name, size in _D_ORIG:
        out[name] = arr[..., off:off + size]
        off += size
    return out


def _d_to_kernel_order(arr):
    p = _d_pieces(arr)
    return jnp.concatenate([p[n] for n in _D_KERNEL], axis=-1)


def _d_to_original_order(arr):
    sizes = dict(_D_ORIG)
    p, off = {}, 0
    for name in _D_KERNEL:
        p[name] = arr[..., off:off + sizes[name]]
        off += sizes[name]
    return jnp.concatenate([p[n] for n, _ in _D_ORIG], axis=-1)


def _rel_bucket(rel):
    half = NUM_BUCKETS // 2
    exact = half // 2
    dist = jnp.abs(rel)
    far = exact + (jnp.log(jnp.maximum(dist, 1).astype(F32) / exact)
                   / math.log(MAX_DISTANCE / exact) * (half - exact)).astype(jnp.int32)
    far = jnp.minimum(far, half - 1)
    return jnp.where(rel > 0, half, 0) + jnp.where(dist < exact, dist, far)


def _lower_bounds(lb_raw):
    sm = jax.nn.softmax(lb_raw.astype(F32), axis=0)
    lb = jnp.concatenate([jnp.zeros_like(sm[:1]), jnp.cumsum(sm[1:], axis=0)[:-1]], axis=0)
    return jnp.clip(lb, 0.0, LB_CEIL)


def _block_diag(blocks):
    rows = sum(b.shape[0] for b in blocks)
    cols = sum(b.shape[1] for b in blocks)
    out = jnp.zeros((rows, cols), blocks[0].dtype)
    r = c = 0
    for b in blocks:
        out = out.at[r:r + b.shape[0], c:c + b.shape[1]].set(b)
        r += b.shape[0]
        c += b.shape[1]
    return out


def _pad_rows(a, rows):
    return jnp.concatenate([a, jnp.zeros((rows - a.shape[0],) + a.shape[1:], a.dtype)], axis=0)


def _mixer(x3, xb2, st, prev_valid, lp, l):
    bsz, t, _ = x3.shape
    m = bsz * t
    x2 = x3.reshape(m, D_MODEL)
    h2 = _proj(xb2, lp['w_main'], l)
    h3 = h2.reshape(bsz, t, MAIN_COLS)

    y_a, s_hgrn = _hgrn(h3, lp['lbp'], lp['gn_a'], l, lp['mask_bd'], st['hgrn'])

    conv0 = jnp.concatenate([jnp.zeros((bsz, 8 - (CONV_W - 1), WIDTH), F32), st['mlstm_conv']], axis=1)
    y_b, mc, mn, mm, conv8 = _mlstm(h3, xb2.reshape(bsz, t, D_MODEL), lp['w_if_t'], lp['b_if_col'],
                                    lp['b_if_row'], lp['conv_w8'], lp['vec_b'], lp['w_qk'], l,
                                    lp['mask_bd'], st['mlstm_c'], st['mlstm_n'].reshape(bsz, 1, WIDTH),
                                    jnp.repeat(st['mlstm_m'], HEAD_DIM, axis=-1)[:, None, :], conv0)
    mn = mn.reshape(bsz, N_HEADS, HEAD_DIM)
    mm = mm[:, :, ::HEAD_DIM]
    conv_state = conv8[:, 8 - (CONV_W - 1):]

    k_off = COL_C + WIDTH
    cache_k = st['swa_k'].reshape(bsz, WINDOW, KV_WIDTH)
    cache_v = st['swa_v'].reshape(bsz, WINDOW, KV_WIDTH)
    y_c = _swa(h3, cache_k, cache_v, lp['bias'], lp['sinks'], l, prev_valid)
    keep = st['keep']

    def window(cache, col):
        new = h3[:, max(t - keep, 0):, col:col + KV_WIDTH]
        if t < keep:
            new = jnp.concatenate([cache[:, WINDOW - (keep - t):], new], axis=1)
        return new.reshape(bsz, keep, C_KV_HEADS, HEAD_DIM)

    k_win = window(cache_k, k_off)
    v_win = window(cache_v, k_off + KV_WIDTH)

    y_d, s_rwkv, shift = _rwkv(h3, lp['mu_d'], lp['vec_d'], lp['w_wa'], lp['g_up_d'], l,
                               lp['mask_bd'], st['rwkv'], _d_to_kernel_order(st['rwkv_shift'])[:, None, :])
    shift_state = _d_to_original_order(shift[:, 0, :])

    ys = [y.reshape(m, WIDTH) for y in (y_a, y_b, y_c, y_d)]
    x1, x1b = _merge(ys, xb2, lp['w_gate'], lp['w_br'], lp['w_o'], l, x2, lp['ln1_g'], lp['ln1_b'])
    new_st = {'swa_k': k_win, 'swa_v': v_win, 'hgrn': s_hgrn, 'mlstm_c': mc, 'mlstm_n': mn,
              'mlstm_m': mm[:, 0, :], 'mlstm_conv': conv_state, 'rwkv': s_rwkv, 'rwkv_shift': shift_state}
    return x1, x1b, new_st


def _trunk(x3, states, prev_valid, keep, lp):
    bsz, t, _ = x3.shape
    m = bsz * t
    xb2 = x3.reshape(m, D_MODEL).astype(BF16)
    collected = {name: [] for name in STATE_NAMES}
    for l in range(DEPTH):
        st = {name: states[name][l] for name in STATE_NAMES}
        st['keep'] = keep
        x1, x1b, new_st = _mixer(x3, xb2, st, prev_valid, lp, l)
        j = l // 2
        ln_g, ln_b = lp['ln2_g'], lp['ln2_b']
        if l % 2 == 0:
            x2, xb2 = _ffn(x1b, x1, lp['ffn_wg'], lp['ffn_wu'], lp['ffn_wd'], j, ln_g, ln_b, l)
        else:
            x2, xb2 = _ffn(x1b, x1, lp['exp_wg'], lp['exp_wu'], lp['exp_wd'], j, ln_g, ln_b, l,
                           (lp['router_w'], lp['router_b']))
        x3 = x2.reshape(bsz, t, D_MODEL)
        for name in STATE_NAMES:
            collected[name].append(new_st[name])
    return x3, {name: jnp.stack(collected[name]) for name in STATE_NAMES}


def kernel(x_prompt, x_sample, cache_swa_k, cache_swa_v, state_hgrn, state_mlstm_c, state_mlstm_n, state_mlstm_m, state_mlstm_conv, state_rwkv, state_rwkv_shift, w_in, lb_raw, gn_a, conv_w, conv_b, wq_b, wk_b, b_i, b_f, gn_b, skip_b, sinks, rel_bias, mu_d, w0_d, w_up_d, a0_d, a_up_d, g_up_d, k_k_d, k_a_d, r_k_d, gn_w_d, gn_b_d, w_br, w_o, ln1_g, ln1_b, ln2_g, ln2_b, ffn_w_gate, ffn_w_up, ffn_w_down, router_w, router_b, exp_w_gate, exp_w_up, exp_w_down):
    off_if = 4 * WIDTH + 3 * WIDTH
    off_c = off_if + 2 * N_HEADS
    off_d = off_c + WIDTH + 2 * KV_WIDTH
    off_gate = off_d + D_COLS
    w_main = jnp.concatenate([w_in[:, :, :4 * WIDTH], _d_to_kernel_order(w_in[:, :, off_d:off_gate]),
                              w_in[:, :, 4 * WIDTH:off_if], w_in[:, :, off_c:off_d], w_in[:, :, off_if:off_c],
                              jnp.zeros((DEPTH, D_MODEL, LANE_TILE - 2 * N_HEADS), w_in.dtype)], axis=-1).astype(BF16)
    w_gate = w_in[:, :, off_gate:].astype(BF16)

    lb = _lower_bounds(lb_raw)
    lb = lb[jnp.minimum(jnp.arange(DEPTH), lb.shape[0] - 1)]
    lbp = jnp.stack([jnp.log(jnp.maximum(lb, LB_FLOOR)), jnp.log1p(-lb), 1.0 - lb], axis=1)
    lbp = jnp.concatenate([lbp, jnp.zeros((DEPTH, 5, WIDTH), F32)], axis=1)

    w_qk = jnp.stack([jnp.concatenate([_block_diag(list(wq_b[l])), _block_diag(list(wk_b[l])) * HEAD_DIM ** -0.5],
                                      axis=1) for l in range(DEPTH)]).astype(BF16)
    w_wa = jnp.stack([_block_diag([w_up_d[l], a_up_d[l]]) for l in range(DEPTH)]).astype(BF16)
    vec_b = jnp.stack([_pad_rows(jnp.stack([conv_b[l], gn_b[l], skip_b[l]]), 8) for l in range(DEPTH)])
    conv_w8 = jnp.stack([_pad_rows(conv_w[l], 8) for l in range(DEPTH)])
    vec_d = jnp.stack([_pad_rows(jnp.stack([w0_d[l], a0_d[l], k_k_d[l], k_a_d[l], r_k_d[l], gn_w_d[l], gn_b_d[l]]), 8)
                       for l in range(DEPTH)])
    head_of = jnp.arange(WIDTH) // HEAD_DIM
    mask_bd = (head_of[:, None] == head_of[None, :]).astype(F32)

    span = WINDOW + CHUNK
    rel = jnp.arange(span)[None, :] - WINDOW - jnp.arange(CHUNK)[:, None]
    one_hot = (_rel_bucket(rel)[..., None] == jnp.arange(NUM_BUCKETS)).astype(F32)
    bias = jnp.einsum('ijb,bh->hij', one_hot, rel_bias.astype(F32), precision=HIGHEST)

    n_dense = ffn_w_gate.shape[0]
    ff_steps = D_FF // D_FF_EXPERT
    b_if = jnp.concatenate([b_i, b_f], axis=-1)
    lp = {
        'w_main': w_main, 'w_gate': w_gate, 'lbp': lbp, 'gn_a': gn_a[:, None, :],
        'w_if_t': jnp.swapaxes(w_in[:, :, off_if:off_c], 1, 2).astype(BF16),
        'mask_bd': mask_bd, 'conv_w8': conv_w8, 'vec_b': vec_b, 'w_qk': w_qk,
        'b_if_col': b_if[:, :, None], 'b_if_row': b_if[:, None, :], 'sinks': sinks[:, None, :], 'bias': bias,
        'mu_d': _d_to_kernel_order(mu_d)[:, None, :], 'vec_d': vec_d, 'w_wa': w_wa, 'g_up_d': g_up_d.astype(BF16),
        'w_br': w_br.astype(BF16), 'w_o': w_o.astype(BF16),
        'ln1_g': ln1_g[:, None, :], 'ln1_b': ln1_b[:, None, :], 'ln2_g': ln2_g[:, None, :], 'ln2_b': ln2_b[:, None, :],
        'ffn_wg': jnp.swapaxes(ffn_w_gate.astype(BF16).reshape(n_dense, D_MODEL, ff_steps, D_FF_EXPERT), 1, 2),
        'ffn_wu': jnp.swapaxes(ffn_w_up.astype(BF16).reshape(n_dense, D_MODEL, ff_steps, D_FF_EXPERT), 1, 2),
        'ffn_wd': ffn_w_down.astype(BF16).reshape(n_dense, ff_steps, D_FF_EXPERT, D_MODEL),
        'router_w': router_w.astype(BF16), 'router_b': router_b[:, None, :],
        'exp_wg': exp_w_gate.astype(BF16), 'exp_wu': exp_w_up.astype(BF16), 'exp_wd': exp_w_down.astype(BF16),
    }

    sample_states = {
        'swa_k': cache_swa_k, 'swa_v': cache_swa_v, 'hgrn': state_hgrn, 'mlstm_c': state_mlstm_c,
        'mlstm_n': state_mlstm_n, 'mlstm_m': state_mlstm_m, 'mlstm_conv': state_mlstm_conv,
        'rwkv': state_rwkv, 'rwkv_shift': state_rwkv_shift,
    }
    keep = cache_swa_k.shape[2]
    bp = x_prompt.shape[0]
    prompt_states = {}
    for name in STATE_NAMES:
        arr = sample_states[name]
        rows = (WINDOW,) + arr.shape[3:] if name in ('swa_k', 'swa_v') else arr.shape[2:]
        prompt_states[name] = jnp.zeros((DEPTH, bp) + tuple(rows), arr.dtype)

    y_prompt, pst = _trunk(x_prompt, prompt_states, False, keep, lp)
    y_sample, sst = _trunk(x_sample, sample_states, True, keep, lp)
    return (y_prompt, y_sample) + tuple(pst[n] for n in STATE_NAMES) + tuple(sst[n] for n in STATE_NAMES)
```
